```python
import jax
import jax.numpy as jnp
from jax import lax
import numpy as np

D_MODEL = 1024
BATCH = 16
SEQ = 4096
DEPTH = 4

N_MIXERS = 4
BLOCK_Q = 128
LN_EPS = 1e-5
RMS_EPS = 1e-6
DN_ALPHA = (2 * DEPTH) ** 0.25
DN_BETA = (8 * DEPTH) ** -0.25
ROPE_THETA = 500000.0
ROPE_FRACTION = 4

FOX_HEADS = 8
FOX_HEAD_DIM = D_MODEL // FOX_HEADS

DSA_HEADS = 8
DSA_HEAD_DIM = D_MODEL // DSA_HEADS
DSA_ROPE_DIM = DSA_HEAD_DIM // ROPE_FRACTION
DSA_NOPE_DIM = DSA_HEAD_DIM - DSA_ROPE_DIM
DSA_KV_RANK = 128
IDX_HEADS = 8
IDX_DIM = 64
IDX_ROPE_DIM = IDX_DIM // ROPE_FRACTION
TOPK_MAX = 256

RWKV_HEAD_DIM = 64
RWKV_HEADS = D_MODEL // RWKV_HEAD_DIM
RWKV_DECAY_LORA = 64
RWKV_AAA_LORA = 64
RWKV_GN_EPS = 64e-5

RET_HEADS = 4
RET_HEAD_DIM = D_MODEL // RET_HEADS
RET_CHUNK = 128
RET_THETA = 10000.0

kernel_name = 'hybrid_fox_dsa_rwkv7_retnet_trunk'


def _layer_norm(x, g, b, eps):
    xf = x.astype(jnp.float32)
    xc = xf - jnp.mean(xf, -1, keepdims=True)
    var = jnp.mean(xc * xc, -1, keepdims=True)
    return (xc * lax.rsqrt(var + eps) * g.astype(jnp.float32) + b.astype(jnp.float32)).astype(x.dtype)


def _rms_norm(x, g, eps):
    xf = x.astype(jnp.float32)
    return (xf * lax.rsqrt(jnp.mean(xf * xf, -1, keepdims=True) + eps) * g.astype(jnp.float32)).astype(x.dtype)


def _split(a, widths):
    idx = [int(v) for v in np.cumsum(widths)[:-1]]
    return jnp.split(a, idx, axis=-1)


def _rope_tables(seq_len, rot_dim, theta):
    inv = 1.0 / (theta ** (jnp.arange(0, rot_dim, 2, dtype=jnp.float32) / rot_dim))
    ang = jnp.arange(seq_len, dtype=jnp.float32)[:, None] * inv[None, :]
    return jnp.cos(ang), jnp.sin(ang)


def _apply_rope(x, cos, sin):
    half = cos.shape[-1]
    bshape = (cos.shape[0],) + (1,) * (x.ndim - 3) + (half,)
    c = cos.reshape(bshape).astype(x.dtype)
    s = sin.reshape(bshape).astype(x.dtype)
    x1, x2, rest = x[..., :half], x[..., half:2 * half], x[..., 2 * half:]
    return jnp.concatenate([x1 * c - x2 * s, x2 * c + x1 * s, rest], axis=-1)


def _to_blocks(a):
    B, S = a.shape[0], a.shape[1]
    a = a.reshape((B, S // BLOCK_Q, BLOCK_Q) + a.shape[2:])
    return jnp.moveaxis(a, 1, 0)


def _from_blocks(a):
    a = jnp.moveaxis(a, 0, 1)
    return a.reshape((a.shape[0], a.shape[1] * a.shape[2]) + a.shape[3:])


def _fox_mixer(x, w_in, b_f, w_out):
    B, S, D = x.shape
    H, dh = FOX_HEADS, FOX_HEAD_DIM
    q, k, v, f_logit, gate = _split(x @ w_in, [H * dh, H * dh, H * dh, H, D])
    q = q.reshape(B, S, H, dh)
    k = k.reshape(B, S, H, dh)
    v = v.reshape(B, S, H, dh)
    log_f = jax.nn.log_sigmoid((f_logit + b_f).astype(jnp.float32))
    cum = jnp.cumsum(log_f, axis=1)
    cum_t = jnp.transpose(cum, (0, 2, 1))
    kpos = jnp.arange(S)
    starts = jnp.arange(S // BLOCK_Q) * BLOCK_Q
    scale = dh ** -0.5

    def block(args):
        q_b, c_b, t0 = args
        s = jnp.einsum('bqhd,bshd->bhqs', q_b, k).astype(jnp.float32) * scale
        s = s + jnp.transpose(c_b, (0, 2, 1))[..., None] - cum_t[:, :, None, :]
        qpos = t0 + jnp.arange(BLOCK_Q)
        s = jnp.where(kpos[None, :] <= qpos[:, None], s, -jnp.inf)
        p = jax.nn.softmax(s, axis=-1).astype(v.dtype)
        return jnp.einsum('bhqs,bshd->bqhd', p, v)

    o = _from_blocks(lax.map(block, (_to_blocks(q), _to_blocks(cum), starts)))
    return (jax.nn.silu(gate) * o.reshape(B, S, D)) @ w_out


def _dsa_mixer(x, cos_a, sin_a, cos_i, sin_i, w_in, kv_norm_g, w_uk, w_uv, w_out):
    B, S, D = x.shape
    H, dh, dr, dc = DSA_HEADS, DSA_HEAD_DIM, DSA_ROPE_DIM, DSA_KV_RANK
    HI, di = IDX_HEADS, IDX_DIM
    q, ckv, kr, qi, ki, wi, gate = _split(x @ w_in, [H * dh, dc, dr, HI * di, di, HI, D])
    q = _apply_rope(q.reshape(B, S, H, dh), cos_a, sin_a)
    q_rope, q_nope = q[..., :dr], q[..., dr:]
    ckv = _rms_norm(ckv, kv_norm_g, RMS_EPS)
    kr = _apply_rope(kr, cos_a, sin_a)
    q_lat = jnp.einsum('bshn,hcn->bshc', q_nope, w_uk)
    q_full = jnp.concatenate([q_lat, q_rope], axis=-1)
    kv_lat = jnp.concatenate([ckv, kr], axis=-1)
    qi = _apply_rope(qi.reshape(B, S, HI, di), cos_i, sin_i)
    ki = _apply_rope(ki, cos_i, sin_i)
    wi = wi.astype(jnp.float32) * HI ** -0.5
    k_sel = min(TOPK_MAX, S // 4)
    kpos = jnp.arange(S)
    starts = jnp.arange(S // BLOCK_Q) * BLOCK_Q

    def block(args):
        qi_b, wi_b, qf_b, t0 = args
        qpos = t0 + jnp.arange(BLOCK_Q)
        isc = jnp.einsum('bqhd,bsd->bqhs', qi_b, ki).astype(jnp.float32) * di ** -0.5
        isc = jnp.einsum('bqhs,bqh->bqs', jax.nn.relu(isc), wi_b)
        isc = jnp.where(kpos[None, None, :] <= qpos[None, :, None], isc, -jnp.inf)
        _, idx = lax.top_k(isc, k_sel)
        valid = idx <= qpos[None, :, None]
        sel = jax.vmap(lambda kv_b, i_b: kv_b[i_b])(kv_lat, idx)
        logits = jnp.einsum('bqhc,bqkc->bhqk', qf_b, sel).astype(jnp.float32) * dh ** -0.5
        logits = jnp.where(valid[:, None], logits, -jnp.inf)
        p = jax.nn.softmax(logits, axis=-1).astype(sel.dtype)
        o_lat = jnp.einsum('bhqk,bqkc->bqhc', p, sel[..., :dc])
        return jnp.einsum('bqhc,hcd->bqhd', o_lat, w_uv)

    o = _from_blocks(lax.map(block, (_to_blocks(qi), _to_blocks(wi), _to_blocks(q_full), starts)))
    return (jax.nn.silu(gate) * o.reshape(B, S, D)) @ w_out


def _rwkv7_mixer(x, mu, w_in, w0, w_lora_a, w_lora_b, a0, a_lora_a, a_lora_b, k_k, k_a, r_k, gn_g, gn_b, w_out):
    B, S, D = x.shape
    H, N = RWKV_HEADS, RWKV_HEAD_DIM
    f32 = jnp.float32
    xx = jnp.pad(x, ((0, 0), (1, 0), (0, 0)))[:, :S] - x
    w_r, w_k, w_v, w_g = jnp.split(w_in, 4, axis=-1)
    r = (x + xx * mu[0]) @ w_r
    xw = x + xx * mu[1]
    k = (x + xx * mu[2]) @ w_k
    v = (x + xx * mu[3]) @ w_v
    xa = x + xx * mu[4]
    g = (x + xx * mu[5]) @ w_g
    w_log = -jax.nn.softplus(-(w0 + jnp.tanh(xw @ w_lora_a) @ w_lora_b)) - 0.5
    decay = jnp.exp(-jnp.exp(w_log.astype(f32)))
    a = jax.nn.sigmoid(a0 + (xa @ a_lora_a) @ a_lora_b)
    kk = (k * k_k).reshape(B, S, H, N).astype(f32)
    kk = kk * lax.rsqrt(jnp.sum(kk * kk, -1, keepdims=True) + 1e-12)
    k = k * (1 + (a - 1) * k_a)
    r = r.reshape(B, S, H, N)
    k = k.reshape(B, S, H, N)
    v = v.reshape(B, S, H, N)
    a = a.reshape(B, S, H, N)
    decay = decay.reshape(B, S, H, N)

    def tm(t):
        return jnp.moveaxis(t.astype(f32), 1, 0)

    def step(state, inp):
        r_t, w_t, k_t, v_t, kk_t, a_t = inp
        sa = jnp.einsum('bhvk,bhk->bhv', state, -kk_t)
        state = (state * w_t[:, :, None, :] + sa[..., None] * (kk_t * a_t)[:, :, None, :]
                 + v_t[..., None] * k_t[:, :, None, :])
        return state, jnp.einsum('bhvk,bhk->bhv', state, r_t)

    state0 = jnp.zeros((B, H, N, N), f32)
    _, ys = lax.scan(step, state0, (tm(r), tm(decay), tm(k), tm(v), tm(kk), tm(a)))
    y = jnp.moveaxis(ys, 0, 1)
    y = _layer_norm(y, gn_g.reshape(H, N), gn_b.reshape(H, N), RWKV_GN_EPS)
    bonus = jnp.sum(r.astype(f32) * k.astype(f32) * r_k, -1, keepdims=True) * v.astype(f32)
    y = (y + bonus).reshape(B, S, D).astype(x.dtype)
    return (jax.nn.silu(g) * y) @ w_out


def _retnet_mixer(x, cos_r, sin_r, w_in, gn_g, w_out):
    B, S, D = x.shape
    H, dk, C = RET_HEADS, RET_HEAD_DIM, RET_CHUNK
    f32 = jnp.float32
    q, k, v, g = jnp.split(x @ w_in, 4, axis=-1)
    q = _apply_rope(q.reshape(B, S, H, dk), cos_r, sin_r)
    k = _apply_rope(k.reshape(B, S, H, dk), cos_r, sin_r) * dk ** -0.5
    v = v.reshape(B, S, H, dk)
    log_g = jnp.log1p(-(2.0 ** (-5.0 - jnp.arange(H, dtype=f32))))
    pos = jnp.arange(C, dtype=f32)
    diff = pos[:, None] - pos[None, :]
    d_mask = jnp.where(diff[None] >= 0, jnp.exp(jnp.maximum(diff, 0.0)[None] * log_g[:, None, None]), 0.0)
    xi = jnp.exp((pos[None, :] + 1.0) * log_g[:, None])
    zeta = jnp.exp((C - 1.0 - pos[None, :]) * log_g[:, None])
    g_c = jnp.exp(C * log_g)
    nc = S // C

    def chunks(t):
        return jnp.transpose(t.reshape(B, nc, C, H, dk), (1, 0, 3, 2, 4)).astype(f32)

    def step(R, inp):
        q_c, k_c, v_c = inp
        inner = jnp.einsum('bhnd,bhmd->bhnm', q_c, k_c) * d_mask[None]
        o = (jnp.einsum('bhnm,bhme->bhne', inner, v_c)
             + jnp.einsum('bhnd,bhde->bhne', q_c, R) * xi[None, :, :, None])
        R = R * g_c[None, :, None, None] + jnp.einsum('bhmd,bhme->bhde', k_c * zeta[None, :, :, None], v_c)
        return R, o

    R0 = jnp.zeros((B, H, dk, dk), f32)
    _, o = lax.scan(step, R0, (chunks(q), chunks(k), chunks(v)))
    o = jnp.transpose(o, (1, 0, 3, 2, 4)).reshape(B, S, H, dk)
    o = _rms_norm(o, gn_g.reshape(H, dk), RMS_EPS).reshape(B, S, D).astype(x.dtype)
    return (jax.nn.silu(g) * o) @ w_out


def setup_inputs(seed: int = 0) -> dict:
    key = jax.random.key(seed)
    keys = iter(jax.random.split(key, 40))
    D = D_MODEL
    f32 = jnp.float32

    def nrm(shape, scale):
        return jax.random.normal(next(keys), shape, f32) * scale

    def unif(shape, lo, hi):
        return jax.random.uniform(next(keys), shape, f32, lo, hi)

    w_s = D ** -0.5
    o_s = D ** -0.5 * DN_BETA
    fox_width = 3 * FOX_HEADS * FOX_HEAD_DIM + FOX_HEADS + D
    dsa_width = (DSA_HEADS * DSA_HEAD_DIM + DSA_KV_RANK + DSA_ROPE_DIM
                 + IDX_HEADS * IDX_DIM + IDX_DIM + IDX_HEADS + D)
    return {
        'x': nrm((BATCH, SEQ, D), 1.0),
        'ln_g': 1.0 + nrm((DEPTH, D), 0.02),
        'ln_b': nrm((DEPTH, D), 0.02),
        'fox_w_in': nrm((D, fox_width), w_s),
        'fox_b_f': unif((FOX_HEADS,), 1.0, 5.0),
        'fox_w_out': nrm((D, D), o_s),
        'dsa_w_in': nrm((D, dsa_width), w_s),
        'dsa_kv_norm_g': 1.0 + nrm((DSA_KV_RANK,), 0.02),
        'dsa_w_uk': nrm((DSA_HEADS, DSA_KV_RANK, DSA_NOPE_DIM), DSA_KV_RANK ** -0.5),
        'dsa_w_uv': nrm((DSA_HEADS, DSA_KV_RANK, DSA_HEAD_DIM), DSA_KV_RANK ** -0.5),
        'dsa_w_out': nrm((D, D), o_s),
        'rwkv_mu': unif((6, D), 0.0, 1.0),
        'rwkv_w_in': nrm((D, 4 * D), w_s),
        'rwkv_w0': unif((D,), -6.0, -1.0),
        'rwkv_w_lora_a': nrm((D, RWKV_DECAY_LORA), w_s),
        'rwkv_w_lora_b': nrm((RWKV_DECAY_LORA, D), 0.5 * RWKV_DECAY_LORA ** -0.5),
        'rwkv_a0': nrm((D,), 0.5),
        'rwkv_a_lora_a': nrm((D, RWKV_AAA_LORA), w_s),
        'rwkv_a_lora_b': nrm((RWKV_AAA_LORA, D), 0.5 * RWKV_AAA_LORA ** -0.5),
        'rwkv_k_k': 0.85 + nrm((D,), 0.05),
        'rwkv_k_a': 1.0 + nrm((D,), 0.05),
        'rwkv_r_k': nrm((RWKV_HEADS, RWKV_HEAD_DIM), 0.1),
        'rwkv_gn_g': 1.0 + nrm((D,), 0.02),
        'rwkv_gn_b': nrm((D,), 0.02),
        'rwkv_w_out': nrm((D, D), o_s),
        'ret_w_in': nrm((D, 4 * D), w_s),
        'ret_gn_g': 1.0 + nrm((D,), 0.02),
        'ret_w_out': nrm((D, D), o_s),
    }


def reference(x, ln_g, ln_b, fox_w_in, fox_b_f, fox_w_out, dsa_w_in, dsa_kv_norm_g, dsa_w_uk, dsa_w_uv,
              dsa_w_out, rwkv_mu, rwkv_w_in, rwkv_w0, rwkv_w_lora_a, rwkv_w_lora_b, rwkv_a0, rwkv_a_lora_a,
              rwkv_a_lora_b, rwkv_k_k, rwkv_k_a, rwkv_r_k, rwkv_gn_g, rwkv_gn_b, rwkv_w_out, ret_w_in,
              ret_gn_g, ret_w_out):
    S = x.shape[1]
    cos_a, sin_a = _rope_tables(S, DSA_ROPE_DIM, ROPE_THETA)
    cos_i, sin_i = _rope_tables(S, IDX_ROPE_DIM, ROPE_THETA)
    cos_r, sin_r = _rope_tables(S, RET_HEAD_DIM, RET_THETA)
    mixers = (
        lambda h: _fox_mixer(h, fox_w_in, fox_b_f, fox_w_out),
        lambda h: _dsa_mixer(h, cos_a, sin_a, cos_i, sin_i, dsa_w_in, dsa_kv_norm_g, dsa_w_uk, dsa_w_uv, dsa_w_out),
        lambda h: _rwkv7_mixer(h, rwkv_mu, rwkv_w_in, rwkv_w0, rwkv_w_lora_a, rwkv_w_lora_b, rwkv_a0,
                               rwkv_a_lora_a, rwkv_a_lora_b, rwkv_k_k, rwkv_k_a, rwkv_r_k, rwkv_gn_g,
                               rwkv_gn_b, rwkv_w_out),
        lambda h: _retnet_mixer(h, cos_r, sin_r, ret_w_in, ret_gn_g, ret_w_out),
    )
    for i in range(DEPTH):
        x = _layer_norm(DN_ALPHA * x + mixers[i % N_MIXERS](x), ln_g[i], ln_b[i], LN_EPS)
    return x
```

```python
import functools
import math

import jax
import jax.numpy as jnp
import numpy as np
from jax import lax
from jax.experimental import pallas as pl
from jax.experimental.pallas import tpu as pltpu

F32 = jnp.float32
BF16 = jnp.bfloat16

D_MODEL = 1024
DEPTH = 4
LN_EPS = 1e-5
RMS_EPS = 1e-6
DN_ALPHA = (2 * DEPTH) ** 0.25
ROPE_THETA = 500000.0

FOX_HEADS = 8
FOX_HEAD_DIM = 128

RET_HEADS = 4
RET_HEAD_DIM = 256
RET_THETA = 10000.0

LANES = 128
VMEM_LIMIT = 48 * 1024 * 1024
NEG_BIG = -1e30


def _cparams(*sem):
    return pltpu.CompilerParams(dimension_semantics=sem, vmem_limit_bytes=VMEM_LIMIT)


def _dot(a, b):
    return jnp.dot(a, b, preferred_element_type=F32)


def _dot_nt(a, b):
    return lax.dot_general(a, b, (((1,), (1,)), ((), ())), preferred_element_type=F32)


def _dot_tn(a, b):
    return lax.dot_general(a, b, (((0,), (0,)), ((), ())), preferred_element_type=F32)


def _split2(x):
    hi = x.astype(BF16)
    lo = (x - hi.astype(F32)).astype(BF16)
    return hi, lo


def _split3(x):
    p1 = x.astype(BF16)
    r1 = x - p1.astype(F32)
    p2 = r1.astype(BF16)
    p3 = (r1 - p2.astype(F32)).astype(BF16)
    return p1, p2, p3


def _sigmoid(x):
    return 1.0 / (1.0 + jnp.exp(-x))


def _pick_tile(n, pref):
    t = min(n, pref)
    while n % t:
        t //= 2
    return t


def _mm_kernel(a_ref, w_ref, o_ref):
    o_ref[...] = _dot(a_ref[...].astype(BF16), w_ref[...]).astype(o_ref.dtype)


def _matmul(a, w, out_dtype=BF16, tm=1024, tn=1024):
    M, K = a.shape
    N = w.shape[1]
    tm = _pick_tile(M, tm)
    if N % tn:
        tn = N
    return pl.pallas_call(
        _mm_kernel,
        grid=(M // tm, N // tn),
        in_specs=[pl.BlockSpec((tm, K), lambda i, j: (i, 0)),
                  pl.BlockSpec((K, tn), lambda i, j: (0, j))],
        out_specs=pl.BlockSpec((tm, tn), lambda i, j: (i, j)),
        out_shape=jax.ShapeDtypeStruct((M, N), out_dtype),
        compiler_params=_cparams("parallel", "arbitrary"),
        name="proj_matmul",
    )(a, w)


def _outproj_ln_kernel(g_ref, o_ref, x_ref, w_ref, lg_ref, lb_ref, xo_ref, xb_ref):
    gate = g_ref[...].astype(F32)
    h = (gate * _sigmoid(gate) * o_ref[...].astype(F32)).astype(BF16)
    z = DN_ALPHA * x_ref[...] + _dot(h, w_ref[...])
    zc = z - jnp.mean(z, axis=-1, keepdims=True)
    var = jnp.mean(zc * zc, axis=-1, keepdims=True)
    out = zc * lax.rsqrt(var + LN_EPS) * lg_ref[...] + lb_ref[...]
    xo_ref[...] = out
    xb_ref[...] = out.astype(BF16)


def _outproj_ln(gate_arr, gate_col, o, x, w_out, ln_g, ln_b, tm=512):
    M, D = x.shape
    tm = _pick_tile(M, tm)
    return pl.pallas_call(
        _outproj_ln_kernel,
        grid=(M // tm,),
        in_specs=[pl.BlockSpec((tm, D), lambda i: (i, gate_col)),
                  pl.BlockSpec((tm, D), lambda i: (i, 0)),
                  pl.BlockSpec((tm, D), lambda i: (i, 0)),
                  pl.BlockSpec((D, D), lambda i: (0, 0)),
                  pl.BlockSpec((1, D), lambda i: (0, 0)),
                  pl.BlockSpec((1, D), lambda i: (0, 0))],
        out_specs=[pl.BlockSpec((tm, D), lambda i: (i, 0)),
                   pl.BlockSpec((tm, D), lambda i: (i, 0))],
        out_shape=[jax.ShapeDtypeStruct((M, D), F32), jax.ShapeDtypeStruct((M, D), BF16)],
        compiler_params=_cparams("parallel"),
        name="outproj_layernorm",
    )(gate_arr, o, x, w_out.astype(BF16), ln_g.reshape(1, D), ln_b.reshape(1, D))


def _fox_cum_kernel(x_ref, wh_ref, wl_ref, bf_ref, tril_ref, cum_ref, carry_sc):
    @pl.when(pl.program_id(1) == 0)
    def _():
        carry_sc[...] = jnp.zeros_like(carry_sc)

    x_hi, x_lo = _split2(x_ref[...])
    z = _dot(x_hi, wh_ref[...]) + _dot(x_lo, wh_ref[...]) + _dot(x_hi, wl_ref[...]) + bf_ref[...]
    logf = jnp.minimum(z, 0.0) - jnp.log(1.0 + jnp.exp(-jnp.abs(z)))
    p1, p2, p3 = _split3(logf)
    tril = tril_ref[...]
    c = _dot(tril, p1) + _dot(tril, p2) + _dot(tril, p3) + carry_sc[...]
    cum_ref[...] = c
    carry_sc[...] = c[c.shape[0] - 1:, :]


def _fox_cum(x3, w_f, b_f, ts=512):
    B, S, D = x3.shape
    H = w_f.shape[1]
    ts = _pick_tile(S, ts)
    w_pad = jnp.zeros((D, LANES), F32).at[:, :H].set(w_f)
    w_hi, w_lo = _split2(w_pad)
    b_pad = jnp.zeros((1, LANES), F32).at[0, :H].set(b_f)
    tril = jnp.tril(jnp.ones((ts, ts), BF16))
    return pl.pallas_call(
        _fox_cum_kernel,
        grid=(B, S // ts),
        in_specs=[pl.BlockSpec((None, ts, D), lambda b, j: (b, j, 0)),
                  pl.BlockSpec((D, LANES), lambda b, j: (0, 0)),
                  pl.BlockSpec((D, LANES), lambda b, j: (0, 0)),
                  pl.BlockSpec((1, LANES), lambda b, j: (0, 0)),
                  pl.BlockSpec((ts, ts), lambda b, j: (0, 0))],
        out_specs=pl.BlockSpec((None, ts, LANES), lambda b, j: (b, j, 0)),
        out_shape=jax.ShapeDtypeStruct((B, S, LANES), F32),
        scratch_shapes=[pltpu.VMEM((1, LANES), F32)],
        compiler_params=_cparams("parallel", "arbitrary"),
        name="fox_decay_cumsum",
    )(x3, w_hi, w_lo, b_pad, tril)


def _fox_attn_kernel(q_ref, k_ref, v_ref, crow_ref, cref_ref, o_ref, m_sc, l_sc, acc_sc, *, tq):
    i = pl.program_id(2)
    q = q_ref[...]
    cref = cref_ref[...]
    m_sc[...] = jnp.full_like(m_sc, NEG_BIG)
    l_sc[...] = jnp.zeros_like(l_sc)
    acc_sc[...] = jnp.zeros_like(acc_sc)

    def step(j, masked):
        off = pl.multiple_of(j * tq, tq)
        s = _dot_nt(q, k_ref[pl.ds(off, tq), :])
        s = s - (crow_ref[:, pl.ds(off, tq)] - cref)
        if masked:
            row = lax.broadcasted_iota(jnp.int32, (tq, tq), 0)
            col = lax.broadcasted_iota(jnp.int32, (tq, tq), 1)
            s = jnp.where(col <= row, s, NEG_BIG)
        m_old = m_sc[...]
        m_new = jnp.maximum(m_old, jnp.max(s, axis=-1, keepdims=True))
        alpha = jnp.exp(m_old - m_new)
        p = jnp.exp(s - m_new)
        l_sc[...] = alpha * l_sc[...] + jnp.sum(p, axis=-1, keepdims=True)
        acc_sc[...] = alpha * acc_sc[...] + _dot(p.astype(BF16), v_ref[pl.ds(off, tq), :])
        m_sc[...] = m_new

    def body(j, c):
        step(j, False)
        return c

    lax.fori_loop(0, i, body, 0)
    step(i, True)
    o_ref[...] = (acc_sc[...] / l_sc[...]).astype(o_ref.dtype)


def _fox_attention(proj3, cum_row, cref, tq=512):
    B, S, _ = proj3.shape
    H, dh = FOX_HEADS, FOX_HEAD_DIM
    tq = _pick_tile(S, tq)
    return pl.pallas_call(
        functools.partial(_fox_attn_kernel, tq=tq),
        grid=(B, H, S // tq),
        in_specs=[pl.BlockSpec((None, tq, dh), lambda b, h, i: (b, i, h)),
                  pl.BlockSpec((None, S, dh), lambda b, h, i: (b, 0, H + h)),
                  pl.BlockSpec((None, S, dh), lambda b, h, i: (b, 0, 2 * H + h)),
                  pl.BlockSpec((None, None, 1, S), lambda b, h, i: (b, h, 0, 0)),
                  pl.BlockSpec((None, None, None, 1, 1), lambda b, h, i: (b, h, i, 0, 0))],
        out_specs=pl.BlockSpec((None, tq, dh), lambda b, h, i: (b, i, h)),
        out_shape=jax.ShapeDtypeStruct((B, S, H * dh), BF16),
        scratch_shapes=[pltpu.VMEM((tq, 1), F32), pltpu.VMEM((tq, 1), F32),
                        pltpu.VMEM((tq, dh), F32)],
        compiler_params=_cparams("parallel", "parallel", "arbitrary"),
        name="fox_attention",
    )(proj3, proj3, proj3, cum_row, cref)


def _fox_layer(x, xb, w_in, b_f, w_out, ln_g, ln_b, B, S):
    D = D_MODEL
    H, dh = FOX_HEADS, FOX_HEAD_DIM
    scale = dh ** -0.5
    w_q, w_k, w_v, w_f, w_g = jnp.split(w_in, [H * dh, 2 * H * dh, 3 * H * dh, 3 * H * dh + H], axis=1)
    w_main = jnp.concatenate([w_q * scale, w_k, w_v, w_g], axis=1).astype(BF16)
    proj = _matmul(xb, w_main)
    cum = _fox_cum(x.reshape(B, S, D), w_f, b_f)
    tq = _pick_tile(S, 512)
    cum_row = jnp.transpose(cum[:, :, :H], (0, 2, 1)).reshape(B, H, 1, S)
    cref = cum_row[:, :, 0, ::tq].reshape(B, H, S // tq, 1, 1)
    o = _fox_attention(proj.reshape(B, S, 4 * D), cum_row, cref, tq=tq)
    return _outproj_ln(proj, 3, o.reshape(B * S, D), x, w_out, ln_g, ln_b)


def _ret_kernel(q_ref, k_ref, v_ref, cos_ref, sin_ref, dm_ref, xi_ref, zeta_ref, gc_ref, gn_ref,
                o_ref, r_sc):
    @pl.when(pl.program_id(2) == 0)
    def _():
        r_sc[...] = jnp.zeros_like(r_sc)

    half = RET_HEAD_DIM // 2
    cos = cos_ref[...]
    sin = sin_ref[...]

    def rope(x):
        x1, x2 = x[:, :half], x[:, half:]
        return jnp.concatenate([x1 * cos - x2 * sin, x2 * cos + x1 * sin], axis=-1)

    q = rope(q_ref[...].astype(F32))
    k = rope(k_ref[...].astype(F32)) * (RET_HEAD_DIM ** -0.5)
    v = v_ref[...]
    qb = q.astype(BF16)
    inner = (_dot_nt(qb, k.astype(BF16)) * dm_ref[...]).astype(BF16)
    r_old = r_sc[...]
    o = _dot(inner, v) + _dot(qb, r_old.astype(BF16)) * xi_ref[...]
    kz = (k * zeta_ref[...]).astype(BF16)
    r_sc[...] = r_old * gc_ref[...] + _dot_tn(kz, v)
    o = o * lax.rsqrt(jnp.mean(o * o, axis=-1, keepdims=True) + RMS_EPS) * gn_ref[...]
    o_ref[...] = o.astype(o_ref.dtype)


def _ret_layer(x, xb, w_in, gn_g, w_out, ln_g, ln_b, B, S, chunk=512):
    D = D_MODEL
    H, dk = RET_HEADS, RET_HEAD_DIM
    C = _pick_tile(S, chunk)
    proj = _matmul(xb, w_in.astype(BF16))
    inv = 1.0 / (RET_THETA ** (jnp.arange(0, dk, 2, dtype=F32) / dk))
    ang = jnp.arange(S, dtype=F32)[:, None] * inv[None, :]
    cos, sin = jnp.cos(ang), jnp.sin(ang)
    log_g = jnp.log1p(-(2.0 ** (-5.0 - jnp.arange(H, dtype=F32))))
    pos = jnp.arange(C, dtype=F32)
    diff = pos[:, None] - pos[None, :]
    d_mask = jnp.where(diff[None] >= 0, jnp.exp(jnp.maximum(diff, 0.0)[None] * log_g[:, None, None]), 0.0)
    xi = jnp.broadcast_to(jnp.exp((pos[None, :] + 1.0) * log_g[:, None])[:, :, None], (H, C, dk))
    zeta = jnp.broadcast_to(jnp.exp((C - 1.0 - pos[None, :]) * log_g[:, None])[:, :, None], (H, C, dk))
    g_c = jnp.broadcast_to(jnp.exp(C * log_g)[:, None, None], (H, 1, dk))
    p3 = proj.reshape(B, S, 4 * D)
    o = pl.pallas_call(
        _ret_kernel,
        grid=(B, H, S // C),
        in_specs=[pl.BlockSpec((None, C, dk), lambda b, h, c: (b, c, h)),
                  pl.BlockSpec((None, C, dk), lambda b, h, c: (b, c, H + h)),
                  pl.BlockSpec((None, C, dk), lambda b, h, c: (b, c, 2 * H + h)),
                  pl.BlockSpec((C, dk // 2), lambda b, h, c: (c, 0)),
                  pl.BlockSpec((C, dk // 2), lambda b, h, c: (c, 0)),
                  pl.BlockSpec((None, C, C), lambda b, h, c: (h, 0, 0)),
                  pl.BlockSpec((None, C, dk), lambda b, h, c: (h, 0, 0)),
                  pl.BlockSpec((None, C, dk), lambda b, h, c: (h, 0, 0)),
                  pl.BlockSpec((None, 1, dk), lambda b, h, c: (h, 0, 0)),
                  pl.BlockSpec((1, dk), lambda b, h, c: (0, h))],
        out_specs=pl.BlockSpec((None, C, dk), lambda b, h, c: (b, c, h)),
        out_shape=jax.ShapeDtypeStruct((B, S, D), BF16),
        scratch_shapes=[pltpu.VMEM((dk, dk), F32)],
        compiler_params=_cparams("parallel", "parallel", "arbitrary"),
        name="retnet_retention",
    )(p3, p3, p3, cos, sin, d_mask, xi, zeta, g_c, gn_g.reshape(1, D))
    return _outproj_ln(proj, 3, o.reshape(B * S, D), x, w_out, ln_g, ln_b)


DSA_HEADS = 8
DSA_HEAD_DIM = 128
DSA_ROPE_DIM = 32
DSA_KV_RANK = 128
IDX_HEADS = 8
IDX_DIM = 64
IDX_ROPE_DIM = 16
TOPK_MAX = 256
INT_MIN = -2 ** 31


def _rope_perm(width, groups):
    p = np.zeros((width, width), np.float32)
    for start, half in groups:
        for j in range(half):
            p[start + half + j, start + j] = 1.0
            p[start + j, start + half + j] = 1.0
    return p


def _rope_cs(S, width, groups, theta_dims):
    c = jnp.ones((S, width), F32)
    sg = jnp.zeros((S, width), F32)
    pos = jnp.arange(S, dtype=F32)[:, None]
    for (start, half), rot_dim in zip(groups, theta_dims):
        inv = 1.0 / (ROPE_THETA ** (jnp.arange(0, rot_dim, 2, dtype=F32) / rot_dim))
        ang = pos * inv[None, :]
        cos, sin = jnp.cos(ang), jnp.sin(ang)
        c = c.at[:, start:start + half].set(cos).at[:, start + half:start + 2 * half].set(cos)
        sg = sg.at[:, start:start + half].set(-sin).at[:, start + half:start + 2 * half].set(sin)
    return c, sg


def _dsa_prep_kernel(q_ref, qi_ref, ckv_ref, misc_ref, cq_ref, sq_ref, ci_ref, si_ref, cm_ref, sm_ref,
                     pq_ref, pi_ref, pm_ref, selk_ref, selw_ref, wuk_ref, kvg_ref,
                     qf_ref, kvl_ref, qir_ref, kid_ref, wi_ref):
    H, dh = DSA_HEADS, DSA_HEAD_DIM
    lane = lax.broadcasted_iota(jnp.int32, (1, LANES), 1)
    rope_lanes = lane < DSA_ROPE_DIM
    cq, sq = cq_ref[...], sq_ref[...]
    scale = dh ** -0.5
    for h in range(H):
        qh = q_ref[:, h * dh:(h + 1) * dh]
        qr = qh.astype(F32) * cq + _dot(qh, pq_ref[...]) * sq
        q_lat = _dot(qr.astype(BF16), wuk_ref[h])
        qf_ref[h, :, :dh] = (q_lat * scale).astype(BF16)
        qf_ref[h, :, dh:] = jnp.where(rope_lanes, qr * scale, 0.0).astype(BF16)
    ci, si = ci_ref[...], si_ref[...]
    for g in range(IDX_HEADS * IDX_DIM // LANES):
        qg = qi_ref[:, g * LANES:(g + 1) * LANES]
        qr = qg.astype(F32) * ci + _dot(qg, pi_ref[...]) * si
        qir_ref[:, g * LANES:(g + 1) * LANES] = (qr * (IDX_DIM ** -0.5)).astype(BF16)
    ckv = ckv_ref[...].astype(F32)
    ckv = ckv * lax.rsqrt(jnp.mean(ckv * ckv, axis=-1, keepdims=True) + RMS_EPS) * kvg_ref[...]
    misc = misc_ref[...]
    mr = (misc.astype(F32) * cm_ref[...] + _dot(misc, pm_ref[...]) * sm_ref[...])
    kvl_ref[:, :DSA_KV_RANK] = ckv.astype(BF16)
    kvl_ref[:, DSA_KV_RANK:] = jnp.where(rope_lanes, mr, 0.0).astype(BF16)
    kid_ref[...] = _dot(mr.astype(BF16), selk_ref[...]).astype(BF16)
    wi_ref[...] = _dot(misc, selw_ref[...]) * (IDX_HEADS ** -0.5)


def _sort_key(x):
    b = pltpu.bitcast(x, jnp.int32)
    return jnp.where(b < 0, b ^ 0x7FFFFFFF, b)


def _dsa_main_kernel(qf_ref, kvl_ref, qi_ref, kid_ref, wi_ref, wuv_ref, o_ref,
                     key_sc, m_sc, l_sc, acc_sc, *, tq, tk, k_sel, seq):
    H = DSA_HEADS
    i = pl.program_id(1)
    q0 = i * tq
    nj = (q0 + tq + tk - 1) // tk
    row = q0 + lax.broadcasted_iota(jnp.int32, (tq, tk), 0)
    col0 = lax.broadcasted_iota(jnp.int32, (tq, tk), 1)
    lane = lax.broadcasted_iota(jnp.int32, (1, LANES), 1)
    neg_inf_key = _sort_key(jnp.full((1, 1), -jnp.inf, F32))

    wi = wi_ref[...]
    qi_heads = []
    for h in range(IDX_HEADS):
        g = qi_ref[:, (h // 2) * LANES:(h // 2 + 1) * LANES]
        keep = (lane >= IDX_DIM) if (h % 2) else (lane < IDX_DIM)
        qi_heads.append(jnp.where(keep, g, jnp.zeros_like(g)))

    def score_tile(j, c):
        off = pl.multiple_of(j * tk, tk)
        ki = kid_ref[pl.ds(off, tk), :]
        isc = jnp.zeros((tq, tk), F32)
        for h in range(IDX_HEADS):
            s = _dot_nt(qi_heads[h], ki)
            isc = isc + jnp.maximum(s, 0.0) * wi[:, h:h + 1]
        isc = jnp.where(col0 + off <= row, isc + 0.0, -jnp.inf)
        key_sc[:, pl.ds(off, tk)] = _sort_key(isc)
        return c

    lax.fori_loop(0, nj, score_tile, 0)

    def count(pred_fn):
        def body(j, acc):
            off = pl.multiple_of(j * tk, tk)
            kt = key_sc[:, pl.ds(off, tk)]
            hit = jnp.where(pred_fn(kt, off), 1, 0)
            for c in range(tk // LANES):
                acc = acc + hit[:, c * LANES:(c + 1) * LANES]
            return acc
        acc = lax.fori_loop(0, nj, body, jnp.zeros((tq, LANES), jnp.int32))
        return jnp.sum(acc, axis=1, keepdims=True)

    cnt0 = count(lambda kt, off: kt >= 0)
    nonneg = cnt0 >= k_sel
    base0 = jnp.where(nonneg, 0, INT_MIN)
    cnt_b0 = jnp.where(nonneg, cnt0, jnp.int32(tq * 0) + nj * tk)

    def bit_step(t, carry):
        base, cnt_b = carry
        cand = base + lax.shift_left(jnp.int32(1), 30 - t)
        c = count(lambda kt, off: kt >= cand)
        ok = c >= k_sel
        return jnp.where(ok, cand, base), jnp.where(ok, c, cnt_b)

    thr, n_ge = lax.fori_loop(0, 31, bit_step, (base0, cnt_b0))
    n_gt = count(lambda kt, off: kt > thr)
    need = k_sel - n_gt
    excess = ((n_ge - n_gt) > need) & (thr > neg_inf_key)
    any_excess = jnp.max(jnp.where(excess, 1, 0)) > 0

    def tie_cut():
        def step(t, lo):
            cand = lo + lax.shift_left(jnp.int32(1), int(math.log2(seq)) - t)
            c = count(lambda kt, off: (kt == thr) & (col0 + off < cand))
            return jnp.where(c < need, cand, lo)
        lo = lax.fori_loop(0, int(math.log2(seq)) + 1, step, jnp.zeros((tq, 1), jnp.int32))
        return jnp.where(excess, lo, seq)

    cut = lax.cond(any_excess, tie_cut, lambda: jnp.full((tq, 1), seq, jnp.int32))

    m_sc[...] = jnp.full_like(m_sc, NEG_BIG)
    l_sc[...] = jnp.zeros_like(l_sc)
    acc_sc[...] = jnp.zeros_like(acc_sc)
    qf = qf_ref[...].reshape(H * tq, 2 * DSA_KV_RANK)

    def attn_tile(j, c):
        off = pl.multiple_of(j * tk, tk)
        kt = key_sc[:, pl.ds(off, tk)]
        col = col0 + off
        bias = jnp.where(kt > thr, 0.0, jnp.where(kt == thr, jnp.where(col <= cut, 0.0, NEG_BIG), NEG_BIG))
        bias = jnp.where(col <= row, bias, NEG_BIG)
        kv = kvl_ref[pl.ds(off, tk), :]
        s = _dot_nt(qf, kv).reshape(H, tq, tk) + bias[None]
        m_old = m_sc[...]
        m_new = jnp.maximum(m_old, jnp.max(s, axis=-1, keepdims=True))
        alpha = jnp.exp(m_old - m_new)
        p = jnp.exp(s - m_new)
        l_sc[...] = alpha * l_sc[...] + jnp.sum(p, axis=-1, keepdims=True)
        pv = _dot(p.reshape(H * tq, tk).astype(BF16), kv[:, :DSA_KV_RANK])
        acc_sc[...] = alpha * acc_sc[...] + pv.reshape(H, tq, DSA_KV_RANK)
        m_sc[...] = m_new
        return c

    lax.fori_loop(0, nj, attn_tile, 0)
    o_lat = acc_sc[...] / l_sc[...]
    for h in range(H):
        o_ref[:, h * DSA_HEAD_DIM:(h + 1) * DSA_HEAD_DIM] = _dot(o_lat[h].astype(BF16), wuv_ref[h]).astype(o_ref.dtype)


def _dsa_layer(x, xb, w_in, kv_norm_g, w_uk, w_uv, w_out, ln_g, ln_b, B, S, tq=128, tk=256):
    D = D_MODEL
    H, dh, dr, dc = DSA_HEADS, DSA_HEAD_DIM, DSA_ROPE_DIM, DSA_KV_RANK
    HI, di = IDX_HEADS, IDX_DIM
    w_q, w_ckv, w_kr, w_qi, w_ki, w_wi, w_g = jnp.split(
        w_in, np.cumsum([H * dh, dc, dr, HI * di, di, HI]).tolist(), axis=1)
    w_misc = jnp.concatenate([w_kr, w_ki, w_wi, jnp.zeros((D, LANES - dr - di - HI), F32)], axis=1)
    w_main = jnp.concatenate([w_q, w_g, w_qi, w_ckv, w_misc], axis=1).astype(BF16)
    n_main = w_main.shape[1]
    proj = _matmul(xb, w_main, tn=n_main // 2)
    c_q, c_qi, c_ckv, c_misc = 0, 2 * D // LANES, (2 * D + HI * di) // LANES, (2 * D + HI * di + dc) // LANES

    q_groups = [(0, dr // 2)]
    i_groups = [(0, IDX_ROPE_DIM // 2), (di, IDX_ROPE_DIM // 2)]
    m_groups = [(0, dr // 2), (dr, IDX_ROPE_DIM // 2)]
    cq, sq = _rope_cs(S, LANES, q_groups, [dr])
    ci, si = _rope_cs(S, LANES, i_groups, [IDX_ROPE_DIM, IDX_ROPE_DIM])
    cm, sm = _rope_cs(S, LANES, m_groups, [dr, IDX_ROPE_DIM])
    pq = jnp.asarray(_rope_perm(LANES, q_groups), BF16)
    pi = jnp.asarray(_rope_perm(LANES, i_groups), BF16)
    pm = jnp.asarray(_rope_perm(LANES, m_groups), BF16)
    selk = np.zeros((LANES, LANES), np.float32)
    for j in range(di):
        selk[dr + j, j] = 1.0
        selk[dr + j, di + j] = 1.0
    selw = np.zeros((LANES, LANES), np.float32)
    for j in range(HI):
        selw[dr + di + j, j] = 1.0
    wuk = jnp.concatenate([jnp.zeros((H, dr, dc), F32), jnp.transpose(w_uk, (0, 2, 1))], axis=1).astype(BF16)

    ts = _pick_tile(S, 256)
    p3 = proj.reshape(B, S, n_main)
    tab = lambda: pl.BlockSpec((ts, LANES), lambda b, j: (j, 0))
    mat = lambda: pl.BlockSpec((LANES, LANES), lambda b, j: (0, 0))
    qf, kvl, qir, kid, wi = pl.pallas_call(
        _dsa_prep_kernel,
        grid=(B, S // ts),
        in_specs=[pl.BlockSpec((None, ts, H * dh), lambda b, j: (b, j, 0)),
                  pl.BlockSpec((None, ts, HI * di), lambda b, j: (b, j, c_qi * LANES // (HI * di))),
                  pl.BlockSpec((None, ts, dc), lambda b, j: (b, j, c_ckv)),
                  pl.BlockSpec((None, ts, LANES), lambda b, j: (b, j, c_misc)),
                  tab(), tab(), tab(), tab(), tab(), tab(),
                  mat(), mat(), mat(), mat(), mat(),
                  pl.BlockSpec((H, LANES, dc), lambda b, j: (0, 0, 0)),
                  pl.BlockSpec((1, dc), lambda b, j: (0, 0))],
        out_specs=[pl.BlockSpec((None, H, ts, 2 * dc), lambda b, j: (b, 0, j, 0)),
                   pl.BlockSpec((None, ts, 2 * dc), lambda b, j: (b, j, 0)),
                   pl.BlockSpec((None, ts, HI * di), lambda b, j: (b, j, 0)),
                   pl.BlockSpec((None, ts, LANES), lambda b, j: (b, j, 0)),
                   pl.BlockSpec((None, ts, LANES), lambda b, j: (b, j, 0))],
        out_shape=[jax.ShapeDtypeStruct((B, H, S, 2 * dc), BF16),
                   jax.ShapeDtypeStruct((B, S, 2 * dc), BF16),
                   jax.ShapeDtypeStruct((B, S, HI * di), BF16),
                   jax.ShapeDtypeStruct((B, S, LANES), BF16),
                   jax.ShapeDtypeStruct((B, S, LANES), F32)],
        compiler_params=_cparams("parallel", "parallel"),
        name="dsa_prep",
    )(p3, p3, p3, p3, cq, sq, ci, si, cm, sm, pq, pi, pm,
      jnp.asarray(selk, BF16), jnp.asarray(selw, BF16), wuk, kv_norm_g.reshape(1, dc))

    tq = _pick_tile(S, tq)
    tk = _pick_tile(S, tk)
    k_sel = min(TOPK_MAX, S // 4)
    o = pl.pallas_call(
        functools.partial(_dsa_main_kernel, tq=tq, tk=tk, k_sel=k_sel, seq=S),
        grid=(B, S // tq),
        in_specs=[pl.BlockSpec((None, H, tq, 2 * dc), lambda b, i: (b, 0, i, 0)),
                  pl.BlockSpec((None, S, 2 * dc), lambda b, i: (b, 0, 0)),
                  pl.BlockSpec((None, tq, HI * di), lambda b, i: (b, i, 0)),
                  pl.BlockSpec((None, S, LANES), lambda b, i: (b, 0, 0)),
                  pl.BlockSpec((None, tq, LANES), lambda b, i: (b, i, 0)),
                  pl.BlockSpec((H, dc, dh), lambda b, i: (0, 0, 0))],
        out_specs=pl.BlockSpec((None, tq, H * dh), lambda b, i: (b, i, 0)),
        out_shape=jax.ShapeDtypeStruct((B, S, H * dh), BF16),
        scratch_shapes=[pltpu.VMEM((tq, S), jnp.int32),
                        pltpu.VMEM((H, tq, 1), F32), pltpu.VMEM((H, tq, 1), F32),
                        pltpu.VMEM((H, tq, dc), F32)],
        compiler_params=_cparams("parallel", "arbitrary"),
        name="dsa_select_attention",
    )(qf, kvl, qir, kid, wi, w_uv.astype(BF16))
    return _outproj_ln(proj, 1, o.reshape(B * S, D), x, w_out, ln_g, ln_b)


RWKV_HEADS = 16
RWKV_HEAD_DIM = 64
RWKV_GN_EPS = 64e-5
RWKV_CHUNK = 64
RWKV_SUB = 16


def _group_sum(x, gmat):
    outs = []
    for c in range(x.shape[1] // LANES):
        hi, lo = _split2(x[:, c * LANES:(c + 1) * LANES])
        outs.append(_dot(hi, gmat) + _dot(lo, gmat))
    return outs[0] if len(outs) == 1 else jnp.concatenate(outs, axis=1)


def _softplus(y):
    return jnp.maximum(y, 0.0) + jnp.log(1.0 + jnp.exp(-jnp.abs(y)))


def _rwkv_proj_kernel(x_ref, xprev_ref, mu_ref, wr_ref, wk_ref, wv_ref, wg_ref, wla_ref, wlb_ref,
                      ala_ref, alb_ref, w0_ref, a0_ref, kk_ref, ka_ref, rk_ref, gmat_ref,
                      r_ref, k_ref, v_ref, g_ref, kap_ref, b_ref, lw_ref, bonus_ref):
    x = x_ref[...]
    ts = x.shape[0]
    prev = jnp.where(pl.program_id(1) == 0, 0.0, xprev_ref[7:8, :])
    rowid = lax.broadcasted_iota(jnp.int32, (ts, 1), 0)
    xx = jnp.where(rowid == 0, prev, pltpu.roll(x, 1, 0)) - x

    def mixed(i):
        return (x + xx * mu_ref[i:i + 1, :]).astype(BF16)

    r = _dot(mixed(0), wr_ref[...])
    k = _dot(mixed(2), wk_ref[...])
    v = _dot(mixed(3), wv_ref[...])
    g_ref[...] = _dot(mixed(5), wg_ref[...]).astype(g_ref.dtype)
    lora_w = _dot(jnp.tanh(_dot(mixed(1), wla_ref[...])).astype(BF16), wlb_ref[...])
    lora_a = _dot(_dot(mixed(4), ala_ref[...]).astype(BF16), alb_ref[...])
    w_log = -_softplus(-(w0_ref[...] + lora_w)) - 0.5
    lw_ref[...] = -jnp.exp(w_log)
    a = _sigmoid(a0_ref[...] + lora_a)
    gmat = gmat_ref[...]
    kk = k * kk_ref[...]
    kap = kk * lax.rsqrt(_group_sum(kk * kk, gmat) + 1e-12)
    k2 = k * (1.0 + (a - 1.0) * ka_ref[...])
    bonus_ref[...] = _group_sum(r * k2 * rk_ref[...], gmat) * v
    r_ref[...] = r.astype(r_ref.dtype)
    k_ref[...] = k2.astype(k_ref.dtype)
    v_ref[...] = v.astype(v_ref.dtype)
    kap_ref[...] = kap.astype(kap_ref.dtype)
    b_ref[...] = (kap * a).astype(b_ref.dtype)


def _bd(x, left):
    z = jnp.zeros_like(x)
    return jnp.concatenate([jnp.where(left, x, z), jnp.where(left, z, x)], axis=0)


def _unbd(x_bd):
    c = x_bd.shape[0] // 2
    return x_bd[:c] + x_bd[c:]


def _rwkv_chunk_pair(lw, r, k, v, kap, b, tril, masks):
    C = RWKV_CHUNK
    left, strict, lower, same_sub, eye = masks
    p1, p2, p3 = _split3(lw)
    L = _dot(tril, p1) + _dot(tril, p2) + _dot(tril, p3)
    Lc = L[C - 1:C, :]
    e_l, e_lx, e_nl, e_r = jnp.exp(L), jnp.exp(L - lw), jnp.exp(-L), jnp.exp(Lc - L)
    at = _bd(-kap * e_lx, left).astype(BF16)
    rt = _bd(r * e_l, left)
    bt = _bd(b * e_nl, left).astype(BF16)
    kt = _bd(k * e_nl, left).astype(BF16)
    bh = _bd(b * e_r, left).astype(BF16)
    kh = _bd(k * e_r, left).astype(BF16)
    vb = _bd(v, left).astype(BF16)

    a1 = _dot_nt(jnp.concatenate([at, rt.astype(BF16)], axis=0), jnp.concatenate([bt, kt], axis=0))
    n = jnp.where(strict, a1[:2 * C, :2 * C], 0.0)
    ak = jnp.where(strict, a1[:2 * C, 2 * C:], 0.0).astype(BF16)
    rb = jnp.where(lower, a1[2 * C:, :2 * C], 0.0).astype(BF16)
    rk = jnp.where(lower, a1[2 * C:, 2 * C:], 0.0).astype(BF16)

    nd = jnp.where(same_sub, n, 0.0)
    no = (n - nd).astype(BF16)
    ndb = nd.astype(BF16)
    n2 = _dot(ndb, ndb)
    n2b = n2.astype(BF16)
    n4 = _dot(n2b, n2b)
    n4b = n4.astype(BF16)
    n8 = _dot(n4b, n4b)
    td = _dot(_dot((eye + nd).astype(BF16), (eye + n2).astype(BF16)).astype(BF16),
              _dot((eye + n4).astype(BF16), (eye + n8).astype(BF16)).astype(BF16))
    tdb = td.astype(BF16)
    x1 = _dot(tdb, no)
    x1b = x1.astype(BF16)
    x2 = _dot(x1b, x1b)
    t = _dot(_dot((eye + x1).astype(BF16), (eye + x2).astype(BF16)).astype(BF16), tdb).astype(BF16)

    akv = _dot(ak, vb)
    pq = _dot(t, jnp.concatenate([at, akv.astype(BF16)], axis=1)).astype(BF16)
    z = jnp.concatenate([pq, jnp.concatenate([jnp.zeros_like(vb), vb], axis=1)], axis=0)
    ry = _dot(jnp.concatenate([rb, rk], axis=1), z)
    rp = rt + ry[:, :2 * C]
    yl = ry[:, 2 * C:]
    mg = _dot_tn(z, jnp.concatenate([bh, kh], axis=0))
    return _unbd(rp), _unbd(yl), _unbd(mg[:2 * C]), _unbd(mg[2 * C:]), jnp.exp(Lc)


def _rwkv_chunk_kernel(lw_ref, r_ref, k_ref, v_ref, kap_ref, b_ref, tril_ref,
                       rp_ref, yl_ref, mm_ref, gg_ref, gam_ref, *, pairs):
    C = RWKV_CHUNK
    lane = lax.broadcasted_iota(jnp.int32, (1, LANES), 1)
    left = lane < RWKV_HEAD_DIM
    ri = lax.broadcasted_iota(jnp.int32, (2 * C, 2 * C), 0)
    ci = lax.broadcasted_iota(jnp.int32, (2 * C, 2 * C), 1)
    same_head = (ri // C) == (ci // C)
    strict = same_head & ((ri % C) > (ci % C))
    lower = same_head & ((ri % C) >= (ci % C))
    same_sub = (ri // RWKV_SUB) == (ci // RWKV_SUB)
    eye = jnp.where(ri == ci, 1.0, 0.0).astype(F32)
    masks = (left, strict, lower, same_sub, eye)
    tril = tril_ref[...]
    for p in range(pairs):
        sl = slice(p * LANES, (p + 1) * LANES)
        rp, yl, mm, gg, gam = _rwkv_chunk_pair(
            lw_ref[:, sl], r_ref[:, sl].astype(F32), k_ref[:, sl].astype(F32), v_ref[:, sl].astype(F32),
            kap_ref[:, sl].astype(F32), b_ref[:, sl].astype(F32), tril, masks)
        rp_ref[:, sl] = rp.astype(rp_ref.dtype)
        yl_ref[:, sl] = yl
        mm_ref[:, sl] = mm.astype(mm_ref.dtype)
        gg_ref[:, sl] = gg
        gam_ref[:, sl] = gam


def _rwkv_seq_kernel(rp_ref, yl_ref, mm_ref, gg_ref, gam_ref, y_ref, s_sc, *, pairs, cb):
    C = RWKV_CHUNK
    lane = lax.broadcasted_iota(jnp.int32, (1, LANES), 1)
    left = lane < RWKV_HEAD_DIM

    @pl.when(pl.program_id(2) == 0)
    def _():
        s_sc[...] = jnp.zeros_like(s_sc)

    states = [s_sc[p] for p in range(pairs)]
    for c in range(cb):
        rows = slice(c * C, (c + 1) * C)
        for p in range(pairs):
            sl = slice(p * LANES, (p + 1) * LANES)
            s = states[p]
            sb = s.astype(BF16)
            y_ref[rows, sl] = _dot_nt(rp_ref[rows, sl], sb) + yl_ref[rows, sl]
            mm = _bd(mm_ref[c, :, sl], left)
            gg = _bd(gg_ref[c, :, sl], left)
            states[p] = s * gam_ref[c, :, sl] + _dot(sb, mm) + gg
    for p in range(pairs):
        s_sc[p] = states[p]


def _rwkv_post_kernel(y_ref, bonus_ref, gmat_ref, gg_ref, gb_ref, o_ref):
    y = y_ref[...]
    gmat = gmat_ref[...]
    inv_n = 1.0 / RWKV_HEAD_DIM
    yc = y - _group_sum(y, gmat) * inv_n
    var = _group_sum(yc * yc, gmat) * inv_n
    yn = yc * lax.rsqrt(var + RWKV_GN_EPS) * gg_ref[...] + gb_ref[...]
    o_ref[...] = (yn + bonus_ref[...]).astype(o_ref.dtype)


def _rwkv_layer(x, xb, mu, w_in, w0, w_lora_a, w_lora_b, a0, a_lora_a, a_lora_b, k_k, k_a, r_k,
                gn_g, gn_b, w_out, ln_g, ln_b, B, S, ts=256, pairs=4):
    D = D_MODEL
    C = RWKV_CHUNK
    nc = S // C
    ts = _pick_tile(S, ts)
    w_r, w_k, w_v, w_g = [w.astype(BF16) for w in jnp.split(w_in, 4, axis=1)]
    gmat = jnp.asarray(np.kron(np.eye(2, dtype=np.float32), np.ones((RWKV_HEAD_DIM, RWKV_HEAD_DIM), np.float32)), BF16)
    row = lambda a: a.reshape(1, D)
    x3 = x.reshape(B, S, D)
    full = lambda shape: pl.BlockSpec(shape, lambda b, j: (0,) * len(shape))
    tile = lambda: pl.BlockSpec((None, ts, D), lambda b, j: (b, j, 0))
    lr = w_lora_a.shape[1]
    outs = pl.pallas_call(
        _rwkv_proj_kernel,
        grid=(B, S // ts),
        in_specs=[tile(),
                  pl.BlockSpec((None, 8, D), lambda b, j: (b, jnp.maximum(j * (ts // 8) - 1, 0), 0)),
                  full((6, D)), full((D, D)), full((D, D)), full((D, D)), full((D, D)),
                  full((D, lr)), full((lr, D)), full((D, lr)), full((lr, D)),
                  full((1, D)), full((1, D)), full((1, D)), full((1, D)), full((1, D)),
                  full((LANES, LANES))],
        out_specs=[tile() for _ in range(8)],
        out_shape=[jax.ShapeDtypeStruct((B, S, D), dt) for dt in (BF16, BF16, BF16, BF16, BF16, BF16, F32, F32)],
        compiler_params=_cparams("parallel", "arbitrary"),
        name="rwkv_projections",
    )(x3, x3, mu, w_r, w_k, w_v, w_g, w_lora_a.astype(BF16), w_lora_b.astype(BF16),
      a_lora_a.astype(BF16), a_lora_b.astype(BF16), row(w0), row(a0), row(k_k), row(k_a), row(r_k), gmat)
    r, k2, v, g, kap, bvec, lw, bonus = outs

    tril = jnp.tril(jnp.ones((C, C), BF16))
    pw = pairs * LANES
    cblk = lambda: pl.BlockSpec((None, C, pw), lambda b, c, q: (b, c, q))
    sblk = lambda: pl.BlockSpec((None, None, C, pw), lambda b, c, q: (b, c, 0, q))
    rp, yl, mm, gg, gam = pl.pallas_call(
        functools.partial(_rwkv_chunk_kernel, pairs=pairs),
        grid=(B, nc, D // pw),
        in_specs=[cblk() for _ in range(6)] + [pl.BlockSpec((C, C), lambda b, c, q: (0, 0))],
        out_specs=[cblk(), cblk(), sblk(), sblk(),
                   pl.BlockSpec((None, None, 1, pw), lambda b, c, q: (b, c, 0, q))],
        out_shape=[jax.ShapeDtypeStruct((B, S, D), BF16), jax.ShapeDtypeStruct((B, S, D), F32),
                   jax.ShapeDtypeStruct((B, nc, C, D), BF16), jax.ShapeDtypeStruct((B, nc, C, D), F32),
                   jax.ShapeDtypeStruct((B, nc, 1, D), F32)],
        compiler_params=_cparams("parallel", "parallel", "parallel"),
        name="rwkv_chunk_summaries",
    )(lw, r, k2, v, kap, bvec, tril)

    cb = _pick_tile(nc, 8)
    y = pl.pallas_call(
        functools.partial(_rwkv_seq_kernel, pairs=pairs, cb=cb),
        grid=(B, D // pw, nc // cb),
        in_specs=[pl.BlockSpec((None, cb * C, pw), lambda b, q, j: (b, j, q)),
                  pl.BlockSpec((None, cb * C, pw), lambda b, q, j: (b, j, q)),
                  pl.BlockSpec((None, cb, C, pw), lambda b, q, j: (b, j, 0, q)),
                  pl.BlockSpec((None, cb, C, pw), lambda b, q, j: (b, j, 0, q)),
                  pl.BlockSpec((None, cb, 1, pw), lambda b, q, j: (b, j, 0, q))],
        out_specs=pl.BlockSpec((None, cb * C, pw), lambda b, q, j: (b, j, q)),
        out_shape=jax.ShapeDtypeStruct((B, S, D), F32),
        scratch_shapes=[pltpu.VMEM((pairs, 2 * C, LANES), F32)],
        compiler_params=_cparams("parallel", "parallel", "arbitrary"),
        name="rwkv_state_scan",
    )(rp, yl, mm, gg, gam)

    tp = _pick_tile(S, 512)
    o = pl.pallas_call(
        _rwkv_post_kernel,
        grid=(B, S // tp),
        in_specs=[pl.BlockSpec((None, tp, D), lambda b, j: (b, j, 0)),
                  pl.BlockSpec((None, tp, D), lambda b, j: (b, j, 0)),
                  pl.BlockSpec((LANES, LANES), lambda b, j: (0, 0)),
                  pl.BlockSpec((1, D), lambda b, j: (0, 0)),
                  pl.BlockSpec((1, D), lambda b, j: (0, 0))],
        out_specs=pl.BlockSpec((None, tp, D), lambda b, j: (b, j, 0)),
        out_shape=jax.ShapeDtypeStruct((B, S, D), BF16),
        compiler_params=_cparams("parallel", "parallel"),
        name="rwkv_groupnorm_bonus",
    )(y, bonus, gmat, row(gn_g), row(gn_b))
    return _outproj_ln(g.reshape(B * S, D), 0, o.reshape(B * S, D), x, w_out, ln_g, ln_b)


def kernel(x, ln_g, ln_b, fox_w_in, fox_b_f, fox_w_out, dsa_w_in, dsa_kv_norm_g, dsa_w_uk, dsa_w_uv, dsa_w_out, rwkv_mu, rwkv_w_in, rwkv_w0, rwkv_w_lora_a, rwkv_w_lora_b, rwkv_a0, rwkv_a_lora_a, rwkv_a_lora_b, rwkv_k_k, rwkv_k_a, rwkv_r_k, rwkv_gn_g, rwkv_gn_b, rwkv_w_out, ret_w_in, ret_gn_g, ret_w_out):
    B, S, D = x.shape
    h = x.reshape(B * S, D)
    hb = h.astype(BF16)
    h, hb = _fox_layer(h, hb, fox_w_in, fox_b_f, fox_w_out, ln_g[0], ln_b[0], B, S)
    h, hb = _dsa_layer(h, hb, dsa_w_in, dsa_kv_norm_g, dsa_w_uk, dsa_w_uv, dsa_w_out, ln_g[1], ln_b[1], B, S)
    h, hb = _rwkv_layer(h, hb, rwkv_mu, rwkv_w_in, rwkv_w0, rwkv_w_lora_a, rwkv_w_lora_b, rwkv_a0,
                        rwkv_a_lora_a, rwkv_a_lora_b, rwkv_k_k, rwkv_k_a, rwkv_r_k, rwkv_gn_g, rwkv_gn_b,
                        rwkv_w_out, ln_g[2], ln_b[2], B, S)
    h, hb = _ret_layer(h, hb, ret_w_in, ret_gn_g, ret_w_out, ln_g[3], ln_b[3], B, S)
    return h.reshape(B, S, D)
```

```python
import functools
import math

import jax
import jax.numpy as jnp
import numpy as np
from jax import lax
from jax.experimental import pallas as pl
from jax.experimental.pallas import tpu as pltpu

F32 = jnp.float32
BF16 = jnp.bfloat16

D_MODEL = 1024
DEPTH = 4
LN_EPS = 1e-5
RMS_EPS = 1e-6
DN_ALPHA = (2 * DEPTH) ** 0.25
ROPE_THETA = 500000.0

FOX_HEADS = 8
FOX_HEAD_DIM = 128

RET_HEADS = 4
RET_HEAD_DIM = 256
RET_THETA = 10000.0

LANES = 128
VMEM_LIMIT = 48 * 1024 * 1024
NEG_BIG = -1e30
LOG2E = 1.4426950408889634


def _cparams(*sem):
    return pltpu.CompilerParams(dimension_semantics=sem, vmem_limit_bytes=VMEM_LIMIT)


def _dot(a, b):
    return jnp.dot(a, b, preferred_element_type=F32)


def _dot_nt(a, b):
    return lax.dot_general(a, b, (((1,), (1,)), ((), ())), preferred_element_type=F32)


def _dot_tn(a, b):
    return lax.dot_general(a, b, (((0,), (0,)), ((), ())), preferred_element_type=F32)


def _split2(x):
    hi = x.astype(BF16)
    lo = (x - hi.astype(F32)).astype(BF16)
    return hi, lo


def _split3(x):
    p1 = x.astype(BF16)
    r1 = x - p1.astype(F32)
    p2 = r1.astype(BF16)
    p3 = (r1 - p2.astype(F32)).astype(BF16)
    return p1, p2, p3


def _sigmoid(x):
    return 1.0 / (1.0 + jnp.exp(-x))


def _round_robin(gens):
    results = [None] * len(gens)
    live = list(range(len(gens)))
    while live:
        still = []
        for i in live:
            try:
                out = next(gens[i])
            except StopIteration:
                continue
            if out is not None:
                results[i] = out
            still.append(i)
        live = still
    return results


def _pick_tile(n, pref):
    t = min(n, pref)
    while n % t:
        t //= 2
    return t


def _mm_kernel(a_ref, w_ref, o_ref):
    o_ref[...] = _dot(a_ref[...].astype(BF16), w_ref[...]).astype(o_ref.dtype)


def _matmul(a, w, out_dtype=BF16, tm=1024, tn=1024):
    M, K = a.shape
    N = w.shape[1]
    tm = _pick_tile(M, tm)
    if N % tn:
        tn = N
    return pl.pallas_call(
        _mm_kernel,
        grid=(M // tm, N // tn),
        in_specs=[pl.BlockSpec((tm, K), lambda i, j: (i, 0)),
                  pl.BlockSpec((K, tn), lambda i, j: (0, j))],
        out_specs=pl.BlockSpec((tm, tn), lambda i, j: (i, j)),
        out_shape=jax.ShapeDtypeStruct((M, N), out_dtype),
        compiler_params=_cparams("parallel", "arbitrary"),
        name="proj_matmul",
    )(a, w)


def _outproj_ln_kernel(g_ref, o_ref, x_ref, w_ref, lg_ref, lb_ref, xo_ref, xb_ref):
    gate = g_ref[...].astype(F32)
    h = (gate * _sigmoid(gate) * o_ref[...].astype(F32)).astype(BF16)
    z = DN_ALPHA * x_ref[...] + _dot(h, w_ref[...])
    zc = z - jnp.mean(z, axis=-1, keepdims=True)
    var = jnp.mean(zc * zc, axis=-1, keepdims=True)
    out = zc * lax.rsqrt(var + LN_EPS) * lg_ref[...] + lb_ref[...]
    xo_ref[...] = out
    xb_ref[...] = out.astype(BF16)


def _outproj_ln(gate_arr, gate_col, o, x, w_out, ln_g, ln_b, tm=512):
    M, D = x.shape
    tm = _pick_tile(M, tm)
    return pl.pallas_call(
        _outproj_ln_kernel,
        grid=(M // tm,),
        in_specs=[pl.BlockSpec((tm, D), lambda i: (i, gate_col)),
                  pl.BlockSpec((tm, D), lambda i: (i, 0)),
                  pl.BlockSpec((tm, D), lambda i: (i, 0)),
                  pl.BlockSpec((D, D), lambda i: (0, 0)),
                  pl.BlockSpec((1, D), lambda i: (0, 0)),
                  pl.BlockSpec((1, D), lambda i: (0, 0))],
        out_specs=[pl.BlockSpec((tm, D), lambda i: (i, 0)),
                   pl.BlockSpec((tm, D), lambda i: (i, 0))],
        out_shape=[jax.ShapeDtypeStruct((M, D), F32), jax.ShapeDtypeStruct((M, D), BF16)],
        compiler_params=_cparams("parallel"),
        name="outproj_layernorm",
    )(gate_arr, o, x, w_out.astype(BF16), ln_g.reshape(1, D), ln_b.reshape(1, D))


FOX_BIAS_PIECES = 3


def _fox_cum_kernel(x_ref, wh_ref, wl_ref, bf_ref, tril_ref, place_ref, pc_ref, carry_sc):
    @pl.when(pl.program_id(1) == 0)
    def _():
        carry_sc[...] = jnp.zeros_like(carry_sc)

    x_hi, x_lo = _split2(x_ref[...])
    z = _dot(x_hi, wh_ref[...]) + _dot(x_lo, wh_ref[...]) + _dot(x_hi, wl_ref[...]) + bf_ref[...]
    logf = jnp.minimum(z, 0.0) - jnp.log(1.0 + jnp.exp(-jnp.abs(z)))
    p1, p2, p3 = _split3(logf)
    tril = tril_ref[...]
    c = _dot(tril, p1) + _dot(tril, p2) + _dot(tril, p3) + carry_sc[...]
    carry_sc[...] = c[c.shape[0] - 1:, :]
    pieces = _split3(c * (-LOG2E))
    pc_ref[...] = sum(_dot(pieces[p], place_ref[p]) for p in range(FOX_BIAS_PIECES)).astype(pc_ref.dtype)


def _fox_cum(x3, w_f, b_f, ts=512):
    B, S, D = x3.shape
    H = w_f.shape[1]
    ts = _pick_tile(S, ts)
    w_pad = jnp.zeros((D, LANES), F32).at[:, :H].set(w_f)
    w_hi, w_lo = _split2(w_pad)
    b_pad = jnp.zeros((1, LANES), F32).at[0, :H].set(b_f)
    tril = jnp.tril(jnp.ones((ts, ts), BF16))
    place = np.zeros((FOX_BIAS_PIECES, LANES, LANES), np.float32)
    for p in range(FOX_BIAS_PIECES):
        for h in range(H):
            place[p, h, FOX_BIAS_PIECES * h + p] = 1.0
    return pl.pallas_call(
        _fox_cum_kernel,
        grid=(B, S // ts),
        in_specs=[pl.BlockSpec((None, ts, D), lambda b, j: (b, j, 0)),
                  pl.BlockSpec((D, LANES), lambda b, j: (0, 0)),
                  pl.BlockSpec((D, LANES), lambda b, j: (0, 0)),
                  pl.BlockSpec((1, LANES), lambda b, j: (0, 0)),
                  pl.BlockSpec((ts, ts), lambda b, j: (0, 0)),
                  pl.BlockSpec((FOX_BIAS_PIECES, LANES, LANES), lambda b, j: (0, 0, 0))],
        out_specs=pl.BlockSpec((None, ts, LANES), lambda b, j: (b, j, 0)),
        out_shape=jax.ShapeDtypeStruct((B, S, LANES), BF16),
        scratch_shapes=[pltpu.VMEM((1, LANES), F32)],
        compiler_params=_cparams("parallel", "arbitrary"),
        name="fox_decay_cumsum",
    )(x3, w_hi, w_lo, b_pad, tril, jnp.asarray(place, BF16))


ONES_ROWS = 16


def _fox_attn_kernel(q_ref, k_ref, v_ref, pc_ref, o_ref, kaug_sc, vt_sc, m_sc, acc_sc, *, tq, nsub, seq, unroll):
    dh = FOX_HEAD_DIM
    h = pl.program_id(1)
    g = pl.program_id(2)

    @pl.when(g == 0)
    def _():
        kaug_sc[:, :dh] = k_ref[...]
        kaug_sc[:, dh:] = pc_ref[...]
        for c in range(seq // tq):
            rows = slice(c * tq, (c + 1) * tq)
            vt_sc[:dh, rows] = v_ref[rows, :].astype(F32).T.astype(BF16)
        vt_sc[dh:, :] = jnp.ones((ONES_ROWS, seq), BF16)

    lane = lax.broadcasted_iota(jnp.int32, (tq, LANES), 1)
    bias_lanes = (lane >= FOX_BIAS_PIECES * h) & (lane < FOX_BIAS_PIECES * (h + 1))
    ones_h = jnp.where(bias_lanes, 1.0, 0.0).astype(BF16)
    q_aug = [jnp.concatenate([q_ref[a * tq:(a + 1) * tq, :], ones_h], axis=1) for a in range(nsub)]
    m_sc[...] = jnp.full_like(m_sc, NEG_BIG)
    acc_sc[...] = jnp.zeros_like(acc_sc)
    causal = (lax.broadcasted_iota(jnp.int32, (tq, tq), 0) <= lax.broadcasted_iota(jnp.int32, (tq, tq), 1))
    first = g * nsub

    def chain(a, tiles, diag_last):
        offs = [pl.multiple_of(j * tq, tq) for j in tiles]
        scores = []
        for off in offs:
            scores.append(_dot_nt(kaug_sc[pl.ds(off, tq), :], q_aug[a]))
            yield None
        for n, (off, s) in enumerate(zip(offs, scores)):
            if diag_last and n == len(offs) - 1:
                s = jnp.where(causal, s, NEG_BIG)
            m_old = m_sc[a]
            m_new = jnp.maximum(m_old, jnp.max(s, axis=0, keepdims=True))
            alpha = jnp.exp2(m_old - m_new)
            pv = _dot(vt_sc[:, pl.ds(off, tq)], jnp.exp2(s - m_new).astype(BF16))
            yield None
            acc_sc[a] = alpha * acc_sc[a] + pv
            m_sc[a] = m_new
        yield None

    def body(jj, c):
        _round_robin([chain(a, [jj * unroll + u for u in range(unroll)], False) for a in range(nsub)])
        return c

    lax.fori_loop(0, first // unroll, body, 0)
    _round_robin([chain(a, [first + t for t in range(a + 1)], True) for a in range(nsub)])
    for a in range(nsub):
        acc = acc_sc[a]
        o_t = acc[:dh] / acc[dh:dh + 1]
        o_ref[a * tq:(a + 1) * tq, :] = o_t.T.astype(o_ref.dtype)


def _fox_attention(proj3, pieces, tq=256, nsub=4):
    B, S, _ = proj3.shape
    H, dh = FOX_HEADS, FOX_HEAD_DIM
    tq = _pick_tile(S, tq)
    nsub = _pick_tile(S // tq, nsub)
    tg = tq * nsub
    return pl.pallas_call(
        functools.partial(_fox_attn_kernel, tq=tq, nsub=nsub, seq=S, unroll=2 if nsub % 2 == 0 else 1),
        grid=(B, H, S // tg),
        in_specs=[pl.BlockSpec((None, tg, dh), lambda b, h, g: (b, g, h)),
                  pl.BlockSpec((None, S, dh), lambda b, h, g: (b, 0, H + h)),
                  pl.BlockSpec((None, S, dh), lambda b, h, g: (b, 0, 2 * H + h)),
                  pl.BlockSpec((None, S, LANES), lambda b, h, g: (b, 0, 0))],
        out_specs=pl.BlockSpec((None, tg, dh), lambda b, h, g: (b, g, h)),
        out_shape=jax.ShapeDtypeStruct((B, S, H * dh), BF16),
        scratch_shapes=[pltpu.VMEM((S, dh + LANES), BF16),
                        pltpu.VMEM((dh + ONES_ROWS, S), BF16),
                        pltpu.VMEM((nsub, 1, tq), F32),
                        pltpu.VMEM((nsub, dh + ONES_ROWS, tq), F32)],
        compiler_params=_cparams("parallel", "parallel", "arbitrary"),
        name="fox_attention",
    )(proj3, proj3, proj3, pieces)


def _fox_layer(x, xb, w_in, b_f, w_out, ln_g, ln_b, B, S):
    D = D_MODEL
    H, dh = FOX_HEADS, FOX_HEAD_DIM
    scale = dh ** -0.5 * LOG2E
    w_q, w_k, w_v, w_f, w_g = jnp.split(w_in, [H * dh, 2 * H * dh, 3 * H * dh, 3 * H * dh + H], axis=1)
    w_main = jnp.concatenate([w_q * scale, w_k, w_v, w_g], axis=1).astype(BF16)
    proj = _matmul(xb, w_main)
    pieces = _fox_cum(x.reshape(B, S, D), w_f, b_f)
    o = _fox_attention(proj.reshape(B, S, 4 * D), pieces)
    return _outproj_ln(proj, 3, o.reshape(B * S, D), x, w_out, ln_g, ln_b)


def _ret_kernel(q_ref, k_ref, v_ref, cos_ref, sin_ref, dm_ref, xi_ref, zeta_ref, gc_ref, gn_ref,
                o_ref, r_sc):
    @pl.when(pl.program_id(2) == 0)
    def _():
        r_sc[...] = jnp.zeros_like(r_sc)

    half = RET_HEAD_DIM // 2
    cos = cos_ref[...]
    sin = sin_ref[...]

    def rope(x):
        x1, x2 = x[:, :half], x[:, half:]
        return jnp.concatenate([x1 * cos - x2 * sin, x2 * cos + x1 * sin], axis=-1)

    q = rope(q_ref[...].astype(F32))
    k = rope(k_ref[...].astype(F32)) * (RET_HEAD_DIM ** -0.5)
    v = v_ref[...]
    qb = q.astype(BF16)
    inner = (_dot_nt(qb, k.astype(BF16)) * dm_ref[...]).astype(BF16)
    r_old = r_sc[...]
    o = _dot(inner, v) + _dot(qb, r_old.astype(BF16)) * xi_ref[...]
    kz = (k * zeta_ref[...]).astype(BF16)
    r_sc[...] = r_old * gc_ref[...] + _dot_tn(kz, v)
    o = o * lax.rsqrt(jnp.mean(o * o, axis=-1, keepdims=True) + RMS_EPS) * gn_ref[...]
    o_ref[...] = o.astype(o_ref.dtype)


def _ret_layer(x, xb, w_in, gn_g, w_out, ln_g, ln_b, B, S, chunk=512):
    D = D_MODEL
    H, dk = RET_HEADS, RET_HEAD_DIM
    C = _pick_tile(S, chunk)
    proj = _matmul(xb, w_in.astype(BF16))
    inv = 1.0 / (RET_THETA ** (jnp.arange(0, dk, 2, dtype=F32) / dk))
    ang = jnp.arange(S, dtype=F32)[:, None] * inv[None, :]
    cos, sin = jnp.cos(ang), jnp.sin(ang)
    log_g = jnp.log1p(-(2.0 ** (-5.0 - jnp.arange(H, dtype=F32))))
    pos = jnp.arange(C, dtype=F32)
    diff = pos[:, None] - pos[None, :]
    d_mask = jnp.where(diff[None] >= 0, jnp.exp(jnp.maximum(diff, 0.0)[None] * log_g[:, None, None]), 0.0)
    xi = jnp.broadcast_to(jnp.exp((pos[None, :] + 1.0) * log_g[:, None])[:, :, None], (H, C, dk))
    zeta = jnp.broadcast_to(jnp.exp((C - 1.0 - pos[None, :]) * log_g[:, None])[:, :, None], (H, C, dk))
    g_c = jnp.broadcast_to(jnp.exp(C * log_g)[:, None, None], (H, 1, dk))
    p3 = proj.reshape(B, S, 4 * D)
    o = pl.pallas_call(
        _ret_kernel,
        grid=(B, H, S // C),
        in_specs=[pl.BlockSpec((None, C, dk), lambda b, h, c: (b, c, h)),
                  pl.BlockSpec((None, C, dk), lambda b, h, c: (b, c, H + h)),
                  pl.BlockSpec((None, C, dk), lambda b, h, c: (b, c, 2 * H + h)),
                  pl.BlockSpec((C, dk // 2), lambda b, h, c: (c, 0)),
                  pl.BlockSpec((C, dk // 2), lambda b, h, c: (c, 0)),
                  pl.BlockSpec((None, C, C), lambda b, h, c: (h, 0, 0)),
                  pl.BlockSpec((None, C, dk), lambda b, h, c: (h, 0, 0)),
                  pl.BlockSpec((None, C, dk), lambda b, h, c: (h, 0, 0)),
                  pl.BlockSpec((None, 1, dk), lambda b, h, c: (h, 0, 0)),
                  pl.BlockSpec((1, dk), lambda b, h, c: (0, h))],
        out_specs=pl.BlockSpec((None, C, dk), lambda b, h, c: (b, c, h)),
        out_shape=jax.ShapeDtypeStruct((B, S, D), BF16),
        scratch_shapes=[pltpu.VMEM((dk, dk), F32)],
        compiler_params=_cparams("parallel", "parallel", "arbitrary"),
        name="retnet_retention",
    )(p3, p3, p3, cos, sin, d_mask, xi, zeta, g_c, gn_g.reshape(1, D))
    return _outproj_ln(proj, 3, o.reshape(B * S, D), x, w_out, ln_g, ln_b)


DSA_HEADS = 8
DSA_HEAD_DIM = 128
DSA_ROPE_DIM = 32
DSA_KV_RANK = 128
IDX_HEADS = 8
IDX_DIM = 64
IDX_ROPE_DIM = 16
TOPK_MAX = 256
INT_MIN = -2 ** 31
DSA_ONES_ROWS = 16


def _rope_perm(width, groups):
    p = np.zeros((width, width), np.float32)
    for start, half in groups:
        for j in range(half):
            p[start + half + j, start + j] = 1.0
            p[start + j, start + half + j] = 1.0
    return p


def _rope_cs(S, width, groups, theta_dims):
    c = jnp.ones((S, width), F32)
    sg = jnp.zeros((S, width), F32)
    pos = jnp.arange(S, dtype=F32)[:, None]
    for (start, half), rot_dim in zip(groups, theta_dims):
        inv = 1.0 / (ROPE_THETA ** (jnp.arange(0, rot_dim, 2, dtype=F32) / rot_dim))
        ang = pos * inv[None, :]
        cos, sin = jnp.cos(ang), jnp.sin(ang)
        c = c.at[:, start:start + half].set(cos).at[:, start + half:start + 2 * half].set(cos)
        sg = sg.at[:, start:start + half].set(-sin).at[:, start + half:start + 2 * half].set(sin)
    return c, sg


def _dsa_prep_kernel(q_ref, qi_ref, ckv_ref, misc_ref, cq_ref, sq_ref, ci_ref, si_ref, cm_ref, sm_ref,
                     pq_ref, pi_ref, pm_ref, selk_ref, selw_ref, wuk_ref, kvg_ref,
                     qf_ref, kvl_ref, kvt_ref, qir_ref, kid_ref, wit_ref):
    H, dh = DSA_HEADS, DSA_HEAD_DIM
    lane = lax.broadcasted_iota(jnp.int32, (1, LANES), 1)
    rope_lanes = lane < DSA_ROPE_DIM
    cq, sq = cq_ref[...], sq_ref[...]
    scale = dh ** -0.5 * LOG2E
    for h in range(H):
        qh = q_ref[:, h * dh:(h + 1) * dh]
        qr = qh.astype(F32) * cq + _dot(qh, pq_ref[...]) * sq
        q_lat = _dot(qr.astype(BF16), wuk_ref[h])
        qf_ref[h, :, :dh] = (q_lat * scale).astype(BF16)
        qf_ref[h, :, dh:] = jnp.where(rope_lanes, qr * scale, 0.0).astype(BF16)
    ci, si = ci_ref[...], si_ref[...]
    for g in range(IDX_HEADS * IDX_DIM // LANES):
        qg = qi_ref[:, g * LANES:(g + 1) * LANES]
        qr = qg.astype(F32) * ci + _dot(qg, pi_ref[...]) * si
        qir_ref[:, g * LANES:(g + 1) * LANES] = (qr * (IDX_DIM ** -0.5)).astype(BF16)
    ckv = ckv_ref[...].astype(F32)
    ckv = ckv * lax.rsqrt(jnp.mean(ckv * ckv, axis=-1, keepdims=True) + RMS_EPS) * kvg_ref[...]
    misc = misc_ref[...]
    mr = (misc.astype(F32) * cm_ref[...] + _dot(misc, pm_ref[...]) * sm_ref[...])
    kvl_ref[:, :DSA_KV_RANK] = ckv.astype(BF16)
    kvl_ref[:, DSA_KV_RANK:] = jnp.where(rope_lanes, mr, 0.0).astype(BF16)
    kvt_ref[:DSA_KV_RANK, :] = ckv.T.astype(BF16)
    kvt_ref[DSA_KV_RANK:, :] = jnp.ones((DSA_ONES_ROWS, ckv.shape[0]), BF16)
    kid_ref[...] = _dot(mr.astype(BF16), selk_ref[...]).astype(BF16)
    wi = _dot(misc, selw_ref[...]) * (IDX_HEADS ** -0.5)
    wit_ref[...] = wi.T[:IDX_HEADS, :]


def _sort_key(x):
    b = pltpu.bitcast(x, jnp.int32)
    return jnp.where(b < 0, b ^ 0x7FFFFFFF, b)


def _dsa_main_kernel(qf_ref, kvl_ref, kvt_ref, qi_ref, kid_ref, wit_ref, wuv_ref, o_ref,
                     key_sc, m_sc, acc_sc, *, tq, tk, k_sel, seq):
    H = DSA_HEADS
    i = pl.program_id(1)
    q0 = i * tq
    nj = (q0 + tq + tk - 1) // tk
    qpos = q0 + lax.broadcasted_iota(jnp.int32, (tk, tq), 1)
    kpos0 = lax.broadcasted_iota(jnp.int32, (tk, tq), 0)
    lane = lax.broadcasted_iota(jnp.int32, (1, LANES), 1)
    neg_inf_key = _sort_key(jnp.full((1, 1), -jnp.inf, F32))

    wit = wit_ref[...]
    qi_heads = []
    for h in range(IDX_HEADS):
        g = qi_ref[:, (h // 2) * LANES:(h // 2 + 1) * LANES]
        keep = (lane >= IDX_DIM) if (h % 2) else (lane < IDX_DIM)
        qi_heads.append(jnp.where(keep, g, jnp.zeros_like(g)))

    def score_tile(j, c):
        off = pl.multiple_of(j * tk, tk)
        ki = kid_ref[pl.ds(off, tk), :]
        scores = [_dot_nt(ki, qi_heads[h]) for h in range(IDX_HEADS)]
        isc = jnp.zeros((tk, tq), F32)
        for h in range(IDX_HEADS):
            isc = isc + jnp.maximum(scores[h], 0.0) * wit[h:h + 1, :]
        isc = jnp.where(kpos0 + off <= qpos, isc + 0.0, -jnp.inf)
        key_sc[pl.ds(off, tk), :] = _sort_key(isc)
        return c

    lax.fori_loop(0, nj, score_tile, 0)

    def count(pred_fn):
        def body(j, acc):
            off = pl.multiple_of(j * tk, tk)
            kt = key_sc[pl.ds(off, tk), :]
            hit = jnp.where(pred_fn(kt, off), 1, 0)
            return acc + jnp.sum(hit.reshape(tk // 8, 8, tq), axis=0)
        acc = lax.fori_loop(0, nj, body, jnp.zeros((8, tq), jnp.int32))
        return jnp.sum(acc, axis=0, keepdims=True)

    cnt0 = count(lambda kt, off: kt >= 0)
    nonneg = cnt0 >= k_sel
    base0 = jnp.where(nonneg, 0, INT_MIN)
    cnt_b0 = jnp.where(nonneg, cnt0, nj * tk)

    def bit_step(t, carry):
        base, cnt_b = carry
        cand = base + lax.shift_left(jnp.int32(1), 30 - t)
        c = count(lambda kt, off: kt >= cand)
        ok = c >= k_sel
        return jnp.where(ok, cand, base), jnp.where(ok, c, cnt_b)

    thr, n_ge = lax.fori_loop(0, 31, bit_step, (base0, cnt_b0))
    n_gt = count(lambda kt, off: kt > thr)
    need = k_sel - n_gt
    excess = ((n_ge - n_gt) > need) & (thr > neg_inf_key)
    any_excess = jnp.max(jnp.where(excess, 1, 0)) > 0

    def tie_cut():
        def step(t, lo):
            cand = lo + lax.shift_left(jnp.int32(1), int(math.log2(seq)) - t)
            c = count(lambda kt, off: (kt == thr) & (kpos0 + off < cand))
            return jnp.where(c < need, cand, lo)
        lo = lax.fori_loop(0, int(math.log2(seq)) + 1, step, jnp.zeros((1, tq), jnp.int32))
        return jnp.where(excess, lo, seq)

    cut = lax.cond(any_excess, tie_cut, lambda: jnp.full((1, tq), seq, jnp.int32))

    m_sc[...] = jnp.full_like(m_sc, NEG_BIG)
    acc_sc[...] = jnp.zeros_like(acc_sc)

    def attn_tile(j, c):
        off = pl.multiple_of(j * tk, tk)
        kt = key_sc[pl.ds(off, tk), :]
        kpos = kpos0 + off
        bias = jnp.where(kt > thr, 0.0, jnp.where(kt == thr, jnp.where(kpos <= cut, 0.0, NEG_BIG), NEG_BIG))
        bias = jnp.where(kpos <= qpos, bias, NEG_BIG)
        kv = kvl_ref[pl.ds(off, tk), :]
        kvt = kvt_ref[:, pl.ds(off, tk)]

        def head_step(h):
            s = _dot_nt(kv, qf_ref[h]) + bias
            yield None
            m_old = m_sc[h]
            m_new = jnp.maximum(m_old, jnp.max(s, axis=0, keepdims=True))
            alpha = jnp.exp2(m_old - m_new)
            pv = _dot(kvt, jnp.exp2(s - m_new).astype(BF16))
            yield None
            acc_sc[h] = alpha * acc_sc[h] + pv
            m_sc[h] = m_new
            yield None

        _round_robin([head_step(h) for h in range(H)])
        return c

    lax.fori_loop(0, nj, attn_tile, 0)
    for h in range(H):
        acc = acc_sc[h]
        o_lat_t = (acc[:DSA_KV_RANK] / acc[DSA_KV_RANK:DSA_KV_RANK + 1]).astype(BF16)
        o_ref[:, h * DSA_HEAD_DIM:(h + 1) * DSA_HEAD_DIM] = _dot_tn(o_lat_t, wuv_ref[h]).astype(o_ref.dtype)


def _dsa_layer(x, xb, w_in, kv_norm_g, w_uk, w_uv, w_out, ln_g, ln_b, B, S, tq=256, tk=256):
    D = D_MODEL
    H, dh, dr, dc = DSA_HEADS, DSA_HEAD_DIM, DSA_ROPE_DIM, DSA_KV_RANK
    HI, di = IDX_HEADS, IDX_DIM
    w_q, w_ckv, w_kr, w_qi, w_ki, w_wi, w_g = jnp.split(
        w_in, np.cumsum([H * dh, dc, dr, HI * di, di, HI]).tolist(), axis=1)
    w_misc = jnp.concatenate([w_kr, w_ki, w_wi, jnp.zeros((D, LANES - dr - di - HI), F32)], axis=1)
    w_main = jnp.concatenate([w_q, w_g, w_qi, w_ckv, w_misc], axis=1).astype(BF16)
    n_main = w_main.shape[1]
    proj = _matmul(xb, w_main, tn=n_main // 2)
    c_q, c_qi, c_ckv, c_misc = 0, 2 * D // LANES, (2 * D + HI * di) // LANES, (2 * D + HI * di + dc) // LANES

    q_groups = [(0, dr // 2)]
    i_groups = [(0, IDX_ROPE_DIM // 2), (di, IDX_ROPE_DIM // 2)]
    m_groups = [(0, dr // 2), (dr, IDX_ROPE_DIM // 2)]
    cq, sq = _rope_cs(S, LANES, q_groups, [dr])
    ci, si = _rope_cs(S, LANES, i_groups, [IDX_ROPE_DIM, IDX_ROPE_DIM])
    cm, sm = _rope_cs(S, LANES, m_groups, [dr, IDX_ROPE_DIM])
    pq = jnp.asarray(_rope_perm(LANES, q_groups), BF16)
    pi = jnp.asarray(_rope_perm(LANES, i_groups), BF16)
    pm = jnp.asarray(_rope_perm(LANES, m_groups), BF16)
    selk = np.zeros((LANES, LANES), np.float32)
    for j in range(di):
        selk[dr + j, j] = 1.0
        selk[dr + j, di + j] = 1.0
    selw = np.zeros((LANES, LANES), np.float32)
    for j in range(HI):
        selw[dr + di + j, j] = 1.0
    wuk = jnp.concatenate([jnp.zeros((H, dr, dc), F32), jnp.transpose(w_uk, (0, 2, 1))], axis=1).astype(BF16)

    ts = _pick_tile(S, 256)
    p3 = proj.reshape(B, S, n_main)
    tab = lambda: pl.BlockSpec((ts, LANES), lambda b, j: (j, 0))
    mat = lambda: pl.BlockSpec((LANES, LANES), lambda b, j: (0, 0))
    qf, kvl, kvt, qir, kid, wit = pl.pallas_call(
        _dsa_prep_kernel,
        grid=(B, S // ts),
        in_specs=[pl.BlockSpec((None, ts, H * dh), lambda b, j: (b, j, 0)),
                  pl.BlockSpec((None, ts, HI * di), lambda b, j: (b, j, c_qi * LANES // (HI * di))),
                  pl.BlockSpec((None, ts, dc), lambda b, j: (b, j, c_ckv)),
                  pl.BlockSpec((None, ts, LANES), lambda b, j: (b, j, c_misc)),
                  tab(), tab(), tab(), tab(), tab(), tab(),
                  mat(), mat(), mat(), mat(), mat(),
                  pl.BlockSpec((H, LANES, dc), lambda b, j: (0, 0, 0)),
                  pl.BlockSpec((1, dc), lambda b, j: (0, 0))],
        out_specs=[pl.BlockSpec((None, H, ts, 2 * dc), lambda b, j: (b, 0, j, 0)),
                   pl.BlockSpec((None, ts, 2 * dc), lambda b, j: (b, j, 0)),
                   pl.BlockSpec((None, dc + DSA_ONES_ROWS, ts), lambda b, j: (b, 0, j)),
                   pl.BlockSpec((None, ts, HI * di), lambda b, j: (b, j, 0)),
                   pl.BlockSpec((None, ts, LANES), lambda b, j: (b, j, 0)),
                   pl.BlockSpec((None, HI, ts), lambda b, j: (b, 0, j))],
        out_shape=[jax.ShapeDtypeStruct((B, H, S, 2 * dc), BF16),
                   jax.ShapeDtypeStruct((B, S, 2 * dc), BF16),
                   jax.ShapeDtypeStruct((B, dc + DSA_ONES_ROWS, S), BF16),
                   jax.ShapeDtypeStruct((B, S, HI * di), BF16),
                   jax.ShapeDtypeStruct((B, S, LANES), BF16),
                   jax.ShapeDtypeStruct((B, HI, S), F32)],
        compiler_params=_cparams("parallel", "parallel"),
        name="dsa_prep",
    )(p3, p3, p3, p3, cq, sq, ci, si, cm, sm, pq, pi, pm,
      jnp.asarray(selk, BF16), jnp.asarray(selw, BF16), wuk, kv_norm_g.reshape(1, dc))

    tq = _pick_tile(S, tq)
    tk = _pick_tile(S, tk)
    k_sel = min(TOPK_MAX, S // 4)
    o = pl.pallas_call(
        functools.partial(_dsa_main_kernel, tq=tq, tk=tk, k_sel=k_sel, seq=S),
        grid=(B, S // tq),
        in_specs=[pl.BlockSpec((None, H, tq, 2 * dc), lambda b, i: (b, 0, i, 0)),
                  pl.BlockSpec((None, S, 2 * dc), lambda b, i: (b, 0, 0)),
                  pl.BlockSpec((None, dc + DSA_ONES_ROWS, S), lambda b, i: (b, 0, 0)),
                  pl.BlockSpec((None, tq, HI * di), lambda b, i: (b, i, 0)),
                  pl.BlockSpec((None, S, LANES), lambda b, i: (b, 0, 0)),
                  pl.BlockSpec((None, HI, tq), lambda b, i: (b, 0, i)),
                  pl.BlockSpec((H, dc, dh), lambda b, i: (0, 0, 0))],
        out_specs=pl.BlockSpec((None, tq, H * dh), lambda b, i: (b, i, 0)),
        out_shape=jax.ShapeDtypeStruct((B, S, H * dh), BF16),
        scratch_shapes=[pltpu.VMEM((S, tq), jnp.int32),
                        pltpu.VMEM((H, 1, tq), F32),
                        pltpu.VMEM((H, dc + DSA_ONES_ROWS, tq), F32)],
        compiler_params=_cparams("parallel", "arbitrary"),
        name="dsa_select_attention",
    )(qf, kvl, kvt, qir, kid, wit, w_uv.astype(BF16))
    return _outproj_ln(proj, 1, o.reshape(B * S, D), x, w_out, ln_g, ln_b)


RWKV_HEADS = 16
RWKV_HEAD_DIM = 64
RWKV_GN_EPS = 64e-5
RWKV_CHUNK = 64
RWKV_SUB = 16


def _group_sum(x, gmat):
    outs = []
    for c in range(x.shape[1] // LANES):
        hi, lo = _split2(x[:, c * LANES:(c + 1) * LANES])
        outs.append(_dot(hi, gmat) + _dot(lo, gmat))
    return outs[0] if len(outs) == 1 else jnp.concatenate(outs, axis=1)


def _softplus(y):
    return jnp.maximum(y, 0.0) + jnp.log(1.0 + jnp.exp(-jnp.abs(y)))


def _rwkv_proj_kernel(x_ref, xprev_ref, mu_ref, wr_ref, wk_ref, wv_ref, wg_ref, wla_ref, wlb_ref,
                      ala_ref, alb_ref, w0_ref, a0_ref, kk_ref, ka_ref, rk_ref, gmat_ref,
                      r_ref, k_ref, v_ref, g_ref, kap_ref, b_ref, lw_ref, bonus_ref):
    x = x_ref[...]
    ts = x.shape[0]
    prev = jnp.where(pl.program_id(1) == 0, 0.0, xprev_ref[7:8, :])
    rowid = lax.broadcasted_iota(jnp.int32, (ts, 1), 0)
    xx = jnp.where(rowid == 0, prev, pltpu.roll(x, 1, 0)) - x

    def mixed(i):
        return (x + xx * mu_ref[i:i + 1, :]).astype(BF16)

    r = _dot(mixed(0), wr_ref[...])
    k = _dot(mixed(2), wk_ref[...])
    v = _dot(mixed(3), wv_ref[...])
    g_ref[...] = _dot(mixed(5), wg_ref[...]).astype(g_ref.dtype)
    lora_w = _dot(jnp.tanh(_dot(mixed(1), wla_ref[...])).astype(BF16), wlb_ref[...])
    lora_a = _dot(_dot(mixed(4), ala_ref[...]).astype(BF16), alb_ref[...])
    w_log = -_softplus(-(w0_ref[...] + lora_w)) - 0.5
    lw_ref[...] = -jnp.exp(w_log)
    a = _sigmoid(a0_ref[...] + lora_a)
    gmat = gmat_ref[...]
    kk = k * kk_ref[...]
    kap = kk * lax.rsqrt(_group_sum(kk * kk, gmat) + 1e-12)
    k2 = k * (1.0 + (a - 1.0) * ka_ref[...])
    bonus_ref[...] = _group_sum(r * k2 * rk_ref[...], gmat) * v
    r_ref[...] = r.astype(r_ref.dtype)
    k_ref[...] = k2.astype(k_ref.dtype)
    v_ref[...] = v.astype(v_ref.dtype)
    kap_ref[...] = kap.astype(kap_ref.dtype)
    b_ref[...] = (kap * a).astype(b_ref.dtype)


def _bd(x, left):
    z = jnp.zeros_like(x)
    return jnp.concatenate([jnp.where(left, x, z), jnp.where(left, z, x)], axis=0)


def _unbd(x_bd):
    c = x_bd.shape[0] // 2
    return x_bd[:c] + x_bd[c:]


def _rwkv_chunk_pair(lw, r, k, v, kap, b, tril, masks):
    C = RWKV_CHUNK
    left, strict, lower, same_sub, eye = masks
    p1, p2, p3 = _split3(lw)
    L = _dot(tril, p1) + _dot(tril, p2) + _dot(tril, p3)
    yield None
    Lc = L[C - 1:C, :]
    e_l, e_lx, e_nl, e_r = jnp.exp(L), jnp.exp(L - lw), jnp.exp(-L), jnp.exp(Lc - L)
    at = _bd(-kap * e_lx, left).astype(BF16)
    rt = _bd(r * e_l, left)
    bt = _bd(b * e_nl, left).astype(BF16)
    kt = _bd(k * e_nl, left).astype(BF16)
    bh = _bd(b * e_r, left).astype(BF16)
    kh = _bd(k * e_r, left).astype(BF16)
    vb = _bd(v, left).astype(BF16)

    a1 = _dot_nt(jnp.concatenate([at, rt.astype(BF16)], axis=0), jnp.concatenate([bt, kt], axis=0))
    yield None
    n = jnp.where(strict, a1[:2 * C, :2 * C], 0.0)
    ak = jnp.where(strict, a1[:2 * C, 2 * C:], 0.0).astype(BF16)
    rb = jnp.where(lower, a1[2 * C:, :2 * C], 0.0).astype(BF16)
    rk = jnp.where(lower, a1[2 * C:, 2 * C:], 0.0).astype(BF16)

    nd = jnp.where(same_sub, n, 0.0)
    no = (n - nd).astype(BF16)
    ndb = nd.astype(BF16)
    n2 = _dot(ndb, ndb)
    akv = _dot(ak, vb)
    yield None
    n2b = n2.astype(BF16)
    n4 = _dot(n2b, n2b)
    t01 = _dot((eye + nd).astype(BF16), (eye + n2).astype(BF16))
    yield None
    n4b = n4.astype(BF16)
    n8 = _dot(n4b, n4b)
    yield None
    t23 = _dot((eye + n4).astype(BF16), (eye + n8).astype(BF16))
    yield None
    tdb = _dot(t01.astype(BF16), t23.astype(BF16)).astype(BF16)
    yield None
    x1 = _dot(tdb, no)
    yield None
    x1b = x1.astype(BF16)
    x2 = _dot(x1b, x1b)
    yield None
    tx = _dot((eye + x1).astype(BF16), (eye + x2).astype(BF16))
    yield None
    t = _dot(tx.astype(BF16), tdb).astype(BF16)
    yield None
    pq = _dot(t, jnp.concatenate([at, akv.astype(BF16)], axis=1)).astype(BF16)
    yield None
    z = jnp.concatenate([pq, jnp.concatenate([jnp.zeros_like(vb), vb], axis=1)], axis=0)
    ry = _dot(jnp.concatenate([rb, rk], axis=1), z)
    mg = _dot_tn(z, jnp.concatenate([bh, kh], axis=0))
    yield None
    rp = rt + ry[:, :2 * C]
    yl = ry[:, 2 * C:]
    yield _unbd(rp), _unbd(yl), _unbd(mg[:2 * C]), _unbd(mg[2 * C:]), jnp.exp(Lc)


def _rwkv_chunk_kernel(lw_ref, r_ref, k_ref, v_ref, kap_ref, b_ref, tril_ref,
                       rp_ref, yl_ref, mm_ref, gg_ref, gam_ref, *, pairs):
    C = RWKV_CHUNK
    lane = lax.broadcasted_iota(jnp.int32, (1, LANES), 1)
    left = lane < RWKV_HEAD_DIM
    ri = lax.broadcasted_iota(jnp.int32, (2 * C, 2 * C), 0)
    ci = lax.broadcasted_iota(jnp.int32, (2 * C, 2 * C), 1)
    same_head = (ri // C) == (ci // C)
    strict = same_head & ((ri % C) > (ci % C))
    lower = same_head & ((ri % C) >= (ci % C))
    same_sub = (ri // RWKV_SUB) == (ci // RWKV_SUB)
    eye = jnp.where(ri == ci, 1.0, 0.0).astype(F32)
    masks = (left, strict, lower, same_sub, eye)
    tril = tril_ref[...]
    slices = [slice(p * LANES, (p + 1) * LANES) for p in range(pairs)]
    results = _round_robin([
        _rwkv_chunk_pair(lw_ref[:, sl], r_ref[:, sl].astype(F32), k_ref[:, sl].astype(F32),
                         v_ref[:, sl].astype(F32), kap_ref[:, sl].astype(F32), b_ref[:, sl].astype(F32),
                         tril, masks)
        for sl in slices])
    for sl, (rp, yl, mm, gg, gam) in zip(slices, results):
        rp_ref[:, sl] = rp.astype(rp_ref.dtype)
        yl_ref[:, sl] = yl
        mm_ref[:, sl] = mm.astype(mm_ref.dtype)
        gg_ref[:, sl] = gg
        gam_ref[:, sl] = gam


def _rwkv_seq_kernel(rp_ref, yl_ref, mm_ref, gg_ref, gam_ref, y_ref, s_sc, *, pairs, cb):
    C = RWKV_CHUNK
    lane = lax.broadcasted_iota(jnp.int32, (1, LANES), 1)
    left = lane < RWKV_HEAD_DIM

    @pl.when(pl.program_id(2) == 0)
    def _():
        s_sc[...] = jnp.zeros_like(s_sc)

    states = [s_sc[p] for p in range(pairs)]
    for c in range(cb):
        rows = slice(c * C, (c + 1) * C)
        for p in range(pairs):
            sl = slice(p * LANES, (p + 1) * LANES)
            s = states[p]
            sb = s.astype(BF16)
            y_ref[rows, sl] = _dot_nt(rp_ref[rows, sl], sb) + yl_ref[rows, sl]
            mm = _bd(mm_ref[c, :, sl], left)
            gg = _bd(gg_ref[c, :, sl], left)
            states[p] = s * gam_ref[c, :, sl] + _dot(sb, mm) + gg
    for p in range(pairs):
        s_sc[p] = states[p]


def _rwkv_post_kernel(y_ref, bonus_ref, gmat_ref, gg_ref, gb_ref, o_ref):
    y = y_ref[...]
    gmat = gmat_ref[...]
    inv_n = 1.0 / RWKV_HEAD_DIM
    yc = y - _group_sum(y, gmat) * inv_n
    var = _group_sum(yc * yc, gmat) * inv_n
    yn = yc * lax.rsqrt(var + RWKV_GN_EPS) * gg_ref[...] + gb_ref[...]
    o_ref[...] = (yn + bonus_ref[...]).astype(o_ref.dtype)


def _rwkv_layer(x, xb, mu, w_in, w0, w_lora_a, w_lora_b, a0, a_lora_a, a_lora_b, k_k, k_a, r_k,
                gn_g, gn_b, w_out, ln_g, ln_b, B, S, ts=256, pairs=8, seq_pairs=4):
    D = D_MODEL
    C = RWKV_CHUNK
    nc = S // C
    ts = _pick_tile(S, ts)
    w_r, w_k, w_v, w_g = [w.astype(BF16) for w in jnp.split(w_in, 4, axis=1)]
    gmat = jnp.asarray(np.kron(np.eye(2, dtype=np.float32), np.ones((RWKV_HEAD_DIM, RWKV_HEAD_DIM), np.float32)), BF16)
    row = lambda a: a.reshape(1, D)
    x3 = x.reshape(B, S, D)
    full = lambda shape: pl.BlockSpec(shape, lambda b, j: (0,) * len(shape))
    tile = lambda: pl.BlockSpec((None, ts, D), lambda b, j: (b, j, 0))
    lr = w_lora_a.shape[1]
    outs = pl.pallas_call(
        _rwkv_proj_kernel,
        grid=(B, S // ts),
        in_specs=[tile(),
                  pl.BlockSpec((None, 8, D), lambda b, j: (b, jnp.maximum(j * (ts // 8) - 1, 0), 0)),
                  full((6, D)), full((D, D)), full((D, D)), full((D, D)), full((D, D)),
                  full((D, lr)), full((lr, D)), full((D, lr)), full((lr, D)),
                  full((1, D)), full((1, D)), full((1, D)), full((1, D)), full((1, D)),
                  full((LANES, LANES))],
        out_specs=[tile() for _ in range(8)],
        out_shape=[jax.ShapeDtypeStruct((B, S, D), dt) for dt in (BF16, BF16, BF16, BF16, BF16, BF16, F32, F32)],
        compiler_params=_cparams("parallel", "arbitrary"),
        name="rwkv_projections",
    )(x3, x3, mu, w_r, w_k, w_v, w_g, w_lora_a.astype(BF16), w_lora_b.astype(BF16),
      a_lora_a.astype(BF16), a_lora_b.astype(BF16), row(w0), row(a0), row(k_k), row(k_a), row(r_k), gmat)
    r, k2, v, g, kap, bvec, lw, bonus = outs

    tril = jnp.tril(jnp.ones((C, C), BF16))
    pw = pairs * LANES
    cblk = lambda: pl.BlockSpec((None, C, pw), lambda b, c, q: (b, c, q))
    sblk = lambda: pl.BlockSpec((None, None, C, pw), lambda b, c, q: (b, c, 0, q))
    rp, yl, mm, gg, gam = pl.pallas_call(
        functools.partial(_rwkv_chunk_kernel, pairs=pairs),
        grid=(B, nc, D // pw),
        in_specs=[cblk() for _ in range(6)] + [pl.BlockSpec((C, C), lambda b, c, q: (0, 0))],
        out_specs=[cblk(), cblk(), sblk(), sblk(),
                   pl.BlockSpec((None, None, 1, pw), lambda b, c, q: (b, c, 0, q))],
        out_shape=[jax.ShapeDtypeStruct((B, S, D), BF16), jax.ShapeDtypeStruct((B, S, D), F32),
                   jax.ShapeDtypeStruct((B, nc, C, D), BF16), jax.ShapeDtypeStruct((B, nc, C, D), F32),
                   jax.ShapeDtypeStruct((B, nc, 1, D), F32)],
        compiler_params=_cparams("parallel", "parallel", "parallel"),
        name="rwkv_chunk_summaries",
    )(lw, r, k2, v, kap, bvec, tril)

    cb = _pick_tile(nc, 8)
    pairs = seq_pairs
    pw = pairs * LANES
    y = pl.pallas_call(
        functools.partial(_rwkv_seq_kernel, pairs=pairs, cb=cb),
        grid=(B, D // pw, nc // cb),
        in_specs=[pl.BlockSpec((None, cb * C, pw), lambda b, q, j: (b, j, q)),
                  pl.BlockSpec((None, cb * C, pw), lambda b, q, j: (b, j, q)),
                  pl.BlockSpec((None, cb, C, pw), lambda b, q, j: (b, j, 0, q)),
                  pl.BlockSpec((None, cb, C, pw), lambda b, q, j: (b, j, 0, q)),
                  pl.BlockSpec((None, cb, 1, pw), lambda b, q, j: (b, j, 0, q))],
        out_specs=pl.BlockSpec((None, cb * C, pw), lambda b, q, j: (b, j, q)),
        out_shape=jax.ShapeDtypeStruct((B, S, D), F32),
        scratch_shapes=[pltpu.VMEM((pairs, 2 * C, LANES), F32)],
        compiler_params=_cparams("parallel", "parallel", "arbitrary"),
        name="rwkv_state_scan",
    )(rp, yl, mm, gg, gam)

    tp = _pick_tile(S, 512)
    o = pl.pallas_call(
        _rwkv_post_kernel,
        grid=(B, S // tp),
        in_specs=[pl.BlockSpec((None, tp, D), lambda b, j: (b, j, 0)),
                  pl.BlockSpec((None, tp, D), lambda b, j: (b, j, 0)),
                  pl.BlockSpec((LANES, LANES), lambda b, j: (0, 0)),
                  pl.BlockSpec((1, D), lambda b, j: (0, 0)),
                  pl.BlockSpec((1, D), lambda b, j: (0, 0))],
        out_specs=pl.BlockSpec((None, tp, D), lambda b, j: (b, j, 0)),
        out_shape=jax.ShapeDtypeStruct((B, S, D), BF16),
        compiler_params=_cparams("parallel", "parallel"),
        name="rwkv_groupnorm_bonus",
    )(y, bonus, gmat, row(gn_g), row(gn_b))
    return _outproj_ln(g.reshape(B * S, D), 0, o.reshape(B * S, D), x, w_out, ln_g, ln_b)


def kernel(x, ln_g, ln_b, fox_w_in, fox_b_f, fox_w_out, dsa_w_in, dsa_kv_norm_g, dsa_w_uk, dsa_w_uv, dsa_w_out, rwkv_mu, rwkv_w_in, rwkv_w0, rwkv_w_lora_a, rwkv_w_lora_b, rwkv_a0, rwkv_a_lora_a, rwkv_a_lora_b, rwkv_k_k, rwkv_k_a, rwkv_r_k, rwkv_gn_g, rwkv_gn_b, rwkv_w_out, ret_w_in, ret_gn_g, ret_w_out):
    B, S, D = x.shape
    h = x.reshape(B * S, D)
    hb = h.astype(BF16)
    h, hb = _fox_layer(h, hb, fox_w_in, fox_b_f, fox_w_out, ln_g[0], ln_b[0], B, S)
    h, hb = _dsa_layer(h, hb, dsa_w_in, dsa_kv_norm_g, dsa_w_uk, dsa_w_uv, dsa_w_out, ln_g[1], ln_b[1], B, S)
    h, hb = _rwkv_layer(h, hb, rwkv_mu, rwkv_w_in, rwkv_w0, rwkv_w_lora_a, rwkv_w_lora_b, rwkv_a0,
                        rwkv_a_lora_a, rwkv_a_lora_b, rwkv_k_k, rwkv_k_a, rwkv_r_k, rwkv_gn_g, rwkv_gn_b,
                        rwkv_w_out, ln_g[2], ln_b[2], B, S)
    h, hb = _ret_layer(h, hb, ret_w_in, ret_gn_g, ret_w_out, ln_g[3], ln_b[3], B, S)
    return h.reshape(B, S, D)
```

```python
import functools
import math

import jax
import jax.numpy as jnp
import numpy as np
from jax import lax
from jax.experimental import pallas as pl
from jax.experimental.pallas import tpu as pltpu

F32 = jnp.float32
BF16 = jnp.bfloat16

D_MODEL = 1024
DEPTH = 4
LN_EPS = 1e-5
RMS_EPS = 1e-6
DN_ALPHA = (2 * DEPTH) ** 0.25
ROPE_THETA = 500000.0

FOX_HEADS = 8
FOX_HEAD_DIM = 128

RET_HEADS = 4
RET_HEAD_DIM = 256
RET_THETA = 10000.0

LANES = 128
VMEM_LIMIT = 48 * 1024 * 1024
NEG_BIG = -1e30
LOG2E = 1.4426950408889634


def _cparams(*sem):
    return pltpu.CompilerParams(dimension_semantics=sem, vmem_limit_bytes=VMEM_LIMIT)


def _dot(a, b):
    return jnp.dot(a, b, preferred_element_type=F32)


def _dot_nt(a, b):
    return lax.dot_general(a, b, (((1,), (1,)), ((), ())), preferred_element_type=F32)


def _dot_tn(a, b):
    return lax.dot_general(a, b, (((0,), (0,)), ((), ())), preferred_element_type=F32)


def _split2(x):
    hi = x.astype(BF16)
    lo = (x - hi.astype(F32)).astype(BF16)
    return hi, lo


def _split3(x):
    p1 = x.astype(BF16)
    r1 = x - p1.astype(F32)
    p2 = r1.astype(BF16)
    p3 = (r1 - p2.astype(F32)).astype(BF16)
    return p1, p2, p3


def _sigmoid(x):
    return 1.0 / (1.0 + jnp.exp(-x))


def _round_robin(gens):
    results = [None] * len(gens)
    live = list(range(len(gens)))
    while live:
        still = []
        for i in live:
            try:
                out = next(gens[i])
            except StopIteration:
                continue
            if out is not None:
                results[i] = out
            still.append(i)
        live = still
    return results


def _pick_tile(n, pref):
    t = min(n, pref)
    while n % t:
        t //= 2
    return t


def _mm_kernel(a_ref, w_ref, o_ref):
    o_ref[...] = _dot(a_ref[...].astype(BF16), w_ref[...]).astype(o_ref.dtype)


def _matmul(a, w, out_dtype=BF16, tm=1024, tn=1024):
    M, K = a.shape
    N = w.shape[1]
    tm = _pick_tile(M, tm)
    if N % tn:
        tn = N
    return pl.pallas_call(
        _mm_kernel,
        grid=(M // tm, N // tn),
        in_specs=[pl.BlockSpec((tm, K), lambda i, j: (i, 0)),
                  pl.BlockSpec((K, tn), lambda i, j: (0, j))],
        out_specs=pl.BlockSpec((tm, tn), lambda i, j: (i, j)),
        out_shape=jax.ShapeDtypeStruct((M, N), out_dtype),
        compiler_params=_cparams("parallel", "arbitrary"),
        name="proj_matmul",
    )(a, w)


def _outproj_ln_kernel(g_ref, o_ref, x_ref, w_ref, lg_ref, lb_ref, xo_ref, xb_ref):
    gate = g_ref[...].astype(F32)
    h = (gate * _sigmoid(gate) * o_ref[...].astype(F32)).astype(BF16)
    z = DN_ALPHA * x_ref[...] + _dot(h, w_ref[...])
    zc = z - jnp.mean(z, axis=-1, keepdims=True)
    var = jnp.mean(zc * zc, axis=-1, keepdims=True)
    out = zc * lax.rsqrt(var + LN_EPS) * lg_ref[...] + lb_ref[...]
    xo_ref[...] = out
    xb_ref[...] = out.astype(BF16)


def _outproj_ln(gate_arr, gate_col, o, x, w_out, ln_g, ln_b, tm=512):
    M, D = x.shape
    tm = _pick_tile(M, tm)
    return pl.pallas_call(
        _outproj_ln_kernel,
        grid=(M // tm,),
        in_specs=[pl.BlockSpec((tm, D), lambda i: (i, gate_col)),
                  pl.BlockSpec((tm, D), lambda i: (i, 0)),
                  pl.BlockSpec((tm, D), lambda i: (i, 0)),
                  pl.BlockSpec((D, D), lambda i: (0, 0)),
                  pl.BlockSpec((1, D), lambda i: (0, 0)),
                  pl.BlockSpec((1, D), lambda i: (0, 0))],
        out_specs=[pl.BlockSpec((tm, D), lambda i: (i, 0)),
                   pl.BlockSpec((tm, D), lambda i: (i, 0))],
        out_shape=[jax.ShapeDtypeStruct((M, D), F32), jax.ShapeDtypeStruct((M, D), BF16)],
        compiler_params=_cparams("parallel"),
        name="outproj_layernorm",
    )(gate_arr, o, x, w_out.astype(BF16), ln_g.reshape(1, D), ln_b.reshape(1, D))


FOX_BIAS_PIECES = 3


def _fox_cum_kernel(x_ref, wh_ref, wl_ref, bf_ref, tril_ref, place_ref, pc_ref, carry_sc):
    @pl.when(pl.program_id(1) == 0)
    def _():
        carry_sc[...] = jnp.zeros_like(carry_sc)

    x_hi, x_lo = _split2(x_ref[...])
    z = _dot(x_hi, wh_ref[...]) + _dot(x_lo, wh_ref[...]) + _dot(x_hi, wl_ref[...]) + bf_ref[...]
    logf = jnp.minimum(z, 0.0) - jnp.log(1.0 + jnp.exp(-jnp.abs(z)))
    p1, p2, p3 = _split3(logf)
    tril = tril_ref[...]
    c = _dot(tril, p1) + _dot(tril, p2) + _dot(tril, p3) + carry_sc[...]
    carry_sc[...] = c[c.shape[0] - 1:, :]
    pieces = _split3(c * (-LOG2E))
    pc_ref[...] = sum(_dot(pieces[p], place_ref[p]) for p in range(FOX_BIAS_PIECES)).astype(pc_ref.dtype)


def _fox_cum(x3, w_f, b_f, ts=512):
    B, S, D = x3.shape
    H = w_f.shape[1]
    ts = _pick_tile(S, ts)
    w_pad = jnp.zeros((D, LANES), F32).at[:, :H].set(w_f)
    w_hi, w_lo = _split2(w_pad)
    b_pad = jnp.zeros((1, LANES), F32).at[0, :H].set(b_f)
    tril = jnp.tril(jnp.ones((ts, ts), BF16))
    place = np.zeros((FOX_BIAS_PIECES, LANES, LANES), np.float32)
    for p in range(FOX_BIAS_PIECES):
        for h in range(H):
            place[p, h, FOX_BIAS_PIECES * h + p] = 1.0
    return pl.pallas_call(
        _fox_cum_kernel,
        grid=(B, S // ts),
        in_specs=[pl.BlockSpec((None, ts, D), lambda b, j: (b, j, 0)),
                  pl.BlockSpec((D, LANES), lambda b, j: (0, 0)),
                  pl.BlockSpec((D, LANES), lambda b, j: (0, 0)),
                  pl.BlockSpec((1, LANES), lambda b, j: (0, 0)),
                  pl.BlockSpec((ts, ts), lambda b, j: (0, 0)),
                  pl.BlockSpec((FOX_BIAS_PIECES, LANES, LANES), lambda b, j: (0, 0, 0))],
        out_specs=pl.BlockSpec((None, ts, LANES), lambda b, j: (b, j, 0)),
        out_shape=jax.ShapeDtypeStruct((B, S, LANES), BF16),
        scratch_shapes=[pltpu.VMEM((1, LANES), F32)],
        compiler_params=_cparams("parallel", "arbitrary"),
        name="fox_decay_cumsum",
    )(x3, w_hi, w_lo, b_pad, tril, jnp.asarray(place, BF16))


ONES_ROWS = 16


def _fox_attn_kernel(q_ref, k_ref, v_ref, pc_ref, o_ref, kaug_sc, vt_sc, m_sc, acc_sc, *, tq, nsub, seq, unroll):
    dh = FOX_HEAD_DIM
    h = pl.program_id(1)
    g = pl.program_id(2)

    @pl.when(g == 0)
    def _():
        kaug_sc[:, :dh] = k_ref[...]
        kaug_sc[:, dh:] = pc_ref[...]
        for c in range(seq // tq):
            rows = slice(c * tq, (c + 1) * tq)
            vt_sc[:dh, rows] = v_ref[rows, :].astype(F32).T.astype(BF16)
        vt_sc[dh:, :] = jnp.ones((ONES_ROWS, seq), BF16)

    lane = lax.broadcasted_iota(jnp.int32, (tq, LANES), 1)
    bias_lanes = (lane >= FOX_BIAS_PIECES * h) & (lane < FOX_BIAS_PIECES * (h + 1))
    ones_h = jnp.where(bias_lanes, 1.0, 0.0).astype(BF16)
    q_aug = [jnp.concatenate([q_ref[a * tq:(a + 1) * tq, :], ones_h], axis=1) for a in range(nsub)]
    m_sc[...] = jnp.full_like(m_sc, NEG_BIG)
    acc_sc[...] = jnp.zeros_like(acc_sc)
    causal = (lax.broadcasted_iota(jnp.int32, (tq, tq), 0) <= lax.broadcasted_iota(jnp.int32, (tq, tq), 1))
    first = g * nsub

    def chain(a, tiles, diag_last):
        offs = [pl.multiple_of(j * tq, tq) for j in tiles]
        scores = []
        for off in offs:
            scores.append(_dot_nt(kaug_sc[pl.ds(off, tq), :], q_aug[a]))
            yield None
        for n, (off, s) in enumerate(zip(offs, scores)):
            if diag_last and n == len(offs) - 1:
                s = jnp.where(causal, s, NEG_BIG)
            m_old = m_sc[a]
            m_new = jnp.maximum(m_old, jnp.max(s, axis=0, keepdims=True))
            alpha = jnp.exp2(m_old - m_new)
            pv = _dot(vt_sc[:, pl.ds(off, tq)], jnp.exp2(s - m_new).astype(BF16))
            yield None
            acc_sc[a] = alpha * acc_sc[a] + pv
            m_sc[a] = m_new
        yield None

    def body(jj, c):
        _round_robin([chain(a, [jj * unroll + u for u in range(unroll)], False) for a in range(nsub)])
        return c

    lax.fori_loop(0, first // unroll, body, 0)
    _round_robin([chain(a, [first + t for t in range(a + 1)], True) for a in range(nsub)])
    for a in range(nsub):
        acc = acc_sc[a]
        o_t = acc[:dh] / acc[dh:dh + 1]
        o_ref[a * tq:(a + 1) * tq, :] = o_t.T.astype(o_ref.dtype)


def _fox_attention(proj3, pieces, tq=256, nsub=8):
    B, S, _ = proj3.shape
    H, dh = FOX_HEADS, FOX_HEAD_DIM
    tq = _pick_tile(S, tq)
    nsub = _pick_tile(S // tq, nsub)
    tg = tq * nsub
    return pl.pallas_call(
        functools.partial(_fox_attn_kernel, tq=tq, nsub=nsub, seq=S, unroll=min(nsub, 4)),
        grid=(B, H, S // tg),
        in_specs=[pl.BlockSpec((None, tg, dh), lambda b, h, g: (b, g, h)),
                  pl.BlockSpec((None, S, dh), lambda b, h, g: (b, 0, H + h)),
                  pl.BlockSpec((None, S, dh), lambda b, h, g: (b, 0, 2 * H + h)),
                  pl.BlockSpec((None, S, LANES), lambda b, h, g: (b, 0, 0))],
        out_specs=pl.BlockSpec((None, tg, dh), lambda b, h, g: (b, g, h)),
        out_shape=jax.ShapeDtypeStruct((B, S, H * dh), BF16),
        scratch_shapes=[pltpu.VMEM((S, dh + LANES), BF16),
                        pltpu.VMEM((dh + ONES_ROWS, S), BF16),
                        pltpu.VMEM((nsub, 1, tq), F32),
                        pltpu.VMEM((nsub, dh + ONES_ROWS, tq), F32)],
        compiler_params=_cparams("parallel", "parallel", "arbitrary"),
        name="fox_attention",
    )(proj3, proj3, proj3, pieces)


def _fox_layer(x, xb, w_in, b_f, w_out, ln_g, ln_b, B, S):
    D = D_MODEL
    H, dh = FOX_HEADS, FOX_HEAD_DIM
    scale = dh ** -0.5 * LOG2E
    w_q, w_k, w_v, w_f, w_g = jnp.split(w_in, [H * dh, 2 * H * dh, 3 * H * dh, 3 * H * dh + H], axis=1)
    w_main = jnp.concatenate([w_q * scale, w_k, w_v, w_g], axis=1).astype(BF16)
    proj = _matmul(xb, w_main)
    pieces = _fox_cum(x.reshape(B, S, D), w_f, b_f)
    o = _fox_attention(proj.reshape(B, S, 4 * D), pieces)
    return _outproj_ln(proj, 3, o.reshape(B * S, D), x, w_out, ln_g, ln_b)


def _ret_kernel(q_ref, k_ref, v_ref, cos_ref, sin_ref, dm_ref, xi_ref, zeta_ref, gc_ref, gn_ref,
                o_ref, r_sc):
    @pl.when(pl.program_id(2) == 0)
    def _():
        r_sc[...] = jnp.zeros_like(r_sc)

    half = RET_HEAD_DIM // 2
    cos = cos_ref[...]
    sin = sin_ref[...]

    def rope(x):
        x1, x2 = x[:, :half], x[:, half:]
        return jnp.concatenate([x1 * cos - x2 * sin, x2 * cos + x1 * sin], axis=-1)

    q = rope(q_ref[...].astype(F32))
    k = rope(k_ref[...].astype(F32)) * (RET_HEAD_DIM ** -0.5)
    v = v_ref[...]
    qb = q.astype(BF16)
    inner = (_dot_nt(qb, k.astype(BF16)) * dm_ref[...]).astype(BF16)
    r_old = r_sc[...]
    o = _dot(inner, v) + _dot(qb, r_old.astype(BF16)) * xi_ref[...]
    kz = (k * zeta_ref[...]).astype(BF16)
    r_sc[...] = r_old * gc_ref[...] + _dot_tn(kz, v)
    o = o * lax.rsqrt(jnp.mean(o * o, axis=-1, keepdims=True) + RMS_EPS) * gn_ref[...]
    o_ref[...] = o.astype(o_ref.dtype)


def _ret_layer(x, xb, w_in, gn_g, w_out, ln_g, ln_b, B, S, chunk=512):
    D = D_MODEL
    H, dk = RET_HEADS, RET_HEAD_DIM
    C = _pick_tile(S, chunk)
    proj = _matmul(xb, w_in.astype(BF16))
    inv = 1.0 / (RET_THETA ** (jnp.arange(0, dk, 2, dtype=F32) / dk))
    ang = jnp.arange(S, dtype=F32)[:, None] * inv[None, :]
    cos, sin = jnp.cos(ang), jnp.sin(ang)
    log_g = jnp.log1p(-(2.0 ** (-5.0 - jnp.arange(H, dtype=F32))))
    pos = jnp.arange(C, dtype=F32)
    diff = pos[:, None] - pos[None, :]
    d_mask = jnp.where(diff[None] >= 0, jnp.exp(jnp.maximum(diff, 0.0)[None] * log_g[:, None, None]), 0.0)
    xi = jnp.broadcast_to(jnp.exp((pos[None, :] + 1.0) * log_g[:, None])[:, :, None], (H, C, dk))
    zeta = jnp.broadcast_to(jnp.exp((C - 1.0 - pos[None, :]) * log_g[:, None])[:, :, None], (H, C, dk))
    g_c = jnp.broadcast_to(jnp.exp(C * log_g)[:, None, None], (H, 1, dk))
    p3 = proj.reshape(B, S, 4 * D)
    o = pl.pallas_call(
        _ret_kernel,
        grid=(B, H, S // C),
        in_specs=[pl.BlockSpec((None, C, dk), lambda b, h, c: (b, c, h)),
                  pl.BlockSpec((None, C, dk), lambda b, h, c: (b, c, H + h)),
                  pl.BlockSpec((None, C, dk), lambda b, h, c: (b, c, 2 * H + h)),
                  pl.BlockSpec((C, dk // 2), lambda b, h, c: (c, 0)),
                  pl.BlockSpec((C, dk // 2), lambda b, h, c: (c, 0)),
                  pl.BlockSpec((None, C, C), lambda b, h, c: (h, 0, 0)),
                  pl.BlockSpec((None, C, dk), lambda b, h, c: (h, 0, 0)),
                  pl.BlockSpec((None, C, dk), lambda b, h, c: (h, 0, 0)),
                  pl.BlockSpec((None, 1, dk), lambda b, h, c: (h, 0, 0)),
                  pl.BlockSpec((1, dk), lambda b, h, c: (0, h))],
        out_specs=pl.BlockSpec((None, C, dk), lambda b, h, c: (b, c, h)),
        out_shape=jax.ShapeDtypeStruct((B, S, D), BF16),
        scratch_shapes=[pltpu.VMEM((dk, dk), F32)],
        compiler_params=_cparams("parallel", "parallel", "arbitrary"),
        name="retnet_retention",
    )(p3, p3, p3, cos, sin, d_mask, xi, zeta, g_c, gn_g.reshape(1, D))
    return _outproj_ln(proj, 3, o.reshape(B * S, D), x, w_out, ln_g, ln_b)


DSA_HEADS = 8
DSA_HEAD_DIM = 128
DSA_ROPE_DIM = 32
DSA_KV_RANK = 128
IDX_HEADS = 8
IDX_DIM = 64
IDX_ROPE_DIM = 16
TOPK_MAX = 256
INT_MIN = -2 ** 31
HALF_MIN = -2 ** 15
HALF_ROWS = 16
DSA_ONES_ROWS = 16


def _rope_perm(width, groups):
    p = np.zeros((width, width), np.float32)
    for start, half in groups:
        for j in range(half):
            p[start + half + j, start + j] = 1.0
            p[start + j, start + half + j] = 1.0
    return p


def _rope_cs(S, width, groups, theta_dims):
    c = jnp.ones((S, width), F32)
    sg = jnp.zeros((S, width), F32)
    pos = jnp.arange(S, dtype=F32)[:, None]
    for (start, half), rot_dim in zip(groups, theta_dims):
        inv = 1.0 / (ROPE_THETA ** (jnp.arange(0, rot_dim, 2, dtype=F32) / rot_dim))
        ang = pos * inv[None, :]
        cos, sin = jnp.cos(ang), jnp.sin(ang)
        c = c.at[:, start:start + half].set(cos).at[:, start + half:start + 2 * half].set(cos)
        sg = sg.at[:, start:start + half].set(-sin).at[:, start + half:start + 2 * half].set(sin)
    return c, sg


def _dsa_prep_kernel(q_ref, qi_ref, ckv_ref, misc_ref, cq_ref, sq_ref, ci_ref, si_ref, cm_ref, sm_ref,
                     pq_ref, pi_ref, pm_ref, selk_ref, selw_ref, wuk_ref, kvg_ref,
                     qf_ref, kvl_ref, kvt_ref, qir_ref, kid_ref, wit_ref):
    H, dh = DSA_HEADS, DSA_HEAD_DIM
    lane = lax.broadcasted_iota(jnp.int32, (1, LANES), 1)
    rope_lanes = lane < DSA_ROPE_DIM
    cq, sq = cq_ref[...], sq_ref[...]
    scale = dh ** -0.5 * LOG2E
    for h in range(H):
        qh = q_ref[:, h * dh:(h + 1) * dh]
        qr = qh.astype(F32) * cq + _dot(qh, pq_ref[...]) * sq
        q_lat = _dot(qr.astype(BF16), wuk_ref[h])
        qf_ref[h, :, :dh] = (q_lat * scale).astype(BF16)
        qf_ref[h, :, dh:] = jnp.where(rope_lanes, qr * scale, 0.0).astype(BF16)
    ci, si = ci_ref[...], si_ref[...]
    for g in range(IDX_HEADS * IDX_DIM // LANES):
        qg = qi_ref[:, g * LANES:(g + 1) * LANES]
        qr = qg.astype(F32) * ci + _dot(qg, pi_ref[...]) * si
        qir_ref[:, g * LANES:(g + 1) * LANES] = (qr * (IDX_DIM ** -0.5)).astype(BF16)
    ckv = ckv_ref[...].astype(F32)
    ckv = ckv * lax.rsqrt(jnp.mean(ckv * ckv, axis=-1, keepdims=True) + RMS_EPS) * kvg_ref[...]
    misc = misc_ref[...]
    mr = (misc.astype(F32) * cm_ref[...] + _dot(misc, pm_ref[...]) * sm_ref[...])
    kvl_ref[:, :DSA_KV_RANK] = ckv.astype(BF16)
    kvl_ref[:, DSA_KV_RANK:] = jnp.where(rope_lanes, mr, 0.0).astype(BF16)
    kvt_ref[:DSA_KV_RANK, :] = ckv.T.astype(BF16)
    kvt_ref[DSA_KV_RANK:, :] = jnp.ones((DSA_ONES_ROWS, ckv.shape[0]), BF16)
    kid_ref[...] = _dot(mr.astype(BF16), selk_ref[...]).astype(BF16)
    wi = _dot(misc, selw_ref[...]) * (IDX_HEADS ** -0.5)
    wit_ref[...] = wi.T[:IDX_HEADS, :]


def _sort_key(x):
    b = pltpu.bitcast(x, jnp.int32)
    return jnp.where(b < 0, b ^ 0x7FFFFFFF, b)


def _dsa_main_kernel(qf_ref, kvl_ref, kvt_ref, qi_ref, kid_ref, wit_ref, wuv_ref, o_ref,
                     key_sc, hi_sc, lo_sc, low_sc, m_sc, acc_sc, *, tq, tk, k_sel, seq):
    H = DSA_HEADS
    i = pl.program_id(1)
    q0 = i * tq
    nj = (q0 + tq + tk - 1) // tk
    qpos = q0 + lax.broadcasted_iota(jnp.int32, (tk, tq), 1)
    kpos0 = lax.broadcasted_iota(jnp.int32, (tk, tq), 0)
    lane = lax.broadcasted_iota(jnp.int32, (1, LANES), 1)
    neg_inf_key = _sort_key(jnp.full((1, 1), -jnp.inf, F32))

    wit = wit_ref[...]
    qi_heads = []
    for h in range(IDX_HEADS):
        g = qi_ref[:, (h // 2) * LANES:(h // 2 + 1) * LANES]
        keep = (lane >= IDX_DIM) if (h % 2) else (lane < IDX_DIM)
        qi_heads.append(jnp.where(keep, g, jnp.zeros_like(g)))

    def score_tile(j, c):
        off = pl.multiple_of(j * tk, tk)
        ki = kid_ref[pl.ds(off, tk), :]
        scores = [_dot_nt(ki, qi_heads[h]) for h in range(IDX_HEADS)]
        isc = jnp.zeros((tk, tq), F32)
        for h in range(IDX_HEADS):
            isc = isc + jnp.maximum(scores[h], 0.0) * wit[h:h + 1, :]
        isc = jnp.where(kpos0 + off <= qpos, isc + 0.0, -jnp.inf)
        key = _sort_key(isc)
        key_sc[pl.ds(off, tk), :] = key
        hi_sc[pl.ds(off, tk), :] = (key >> 16).astype(jnp.int16)
        lo_sc[pl.ds(off, tk), :] = ((key & 0xFFFF) + HALF_MIN).astype(jnp.int16)
        return c

    lax.fori_loop(0, nj, score_tile, 0)

    def count(pred_fn):
        def body(j, acc):
            off = pl.multiple_of(j * tk, tk)
            kt = key_sc[pl.ds(off, tk), :]
            hit = jnp.where(pred_fn(kt, off), 1, 0)
            return acc + jnp.sum(hit.reshape(tk // 8, 8, tq), axis=0)
        acc = lax.fori_loop(0, nj, body, jnp.zeros((8, tq), jnp.int32))
        return jnp.sum(acc, axis=0, keepdims=True)

    rows16 = tk // HALF_ROWS

    def count16(ref, pred_fn):
        def body(j, acc):
            off = pl.multiple_of(j * tk, tk)
            hit = jnp.where(pred_fn(ref[pl.ds(off, tk), :].reshape(rows16, HALF_ROWS, tq)),
                            jnp.int16(1), jnp.int16(0))
            for r in range(rows16):
                acc = acc + hit[r]
            return acc
        acc = lax.fori_loop(0, nj, body, jnp.zeros((HALF_ROWS, tq), jnp.int16))
        return jnp.sum(acc.astype(jnp.int32), axis=0, keepdims=True)

    def as_half(v):
        return jnp.broadcast_to(v.astype(jnp.int16), (HALF_ROWS, tq))[None]

    def search16(ref, base0, cnt0, bits, want):
        def bit_step(t, carry):
            base, cnt_b = carry
            cand = base + lax.shift_left(jnp.int32(1), bits - 1 - t)
            cand16 = as_half(cand)
            c = count16(ref, lambda kt: kt >= cand16)
            ok = c >= want
            return jnp.where(ok, cand, base), jnp.where(ok, c, cnt_b)
        return lax.fori_loop(0, bits, bit_step, (base0, cnt0))

    zero16 = as_half(jnp.zeros((1, tq), jnp.int32))
    cnt_pos = count16(hi_sc, lambda kt: kt >= zero16)
    nonneg = cnt_pos >= k_sel
    t1, ge_hi = search16(hi_sc, jnp.where(nonneg, 0, HALF_MIN), jnp.where(nonneg, cnt_pos, nj * tk), 15, k_sel)
    t1_16 = as_half(t1)
    above = count16(hi_sc, lambda kt: kt > t1_16)

    def build_low(j, c):
        rows = pl.ds(pl.multiple_of(j * tk, tk), tk)
        hi = hi_sc[rows, :].reshape(rows16, HALF_ROWS, tq)
        lo = lo_sc[rows, :].reshape(rows16, HALF_ROWS, tq)
        low_sc[rows, :] = jnp.where(hi == t1_16, lo, jnp.int16(HALF_MIN)).reshape(tk, tq)
        return c

    lax.fori_loop(0, nj, build_low, 0)
    t2, ge_low = search16(low_sc, jnp.full((1, tq), HALF_MIN, jnp.int32), ge_hi - above, 16, k_sel - above)
    t2_16 = as_half(t2)
    thr = lax.shift_left(t1, 16) | (t2 - HALF_MIN)
    n_ge = above + ge_low
    n_gt = above + count16(low_sc, lambda kt: kt > t2_16)
    need = k_sel - n_gt
    excess = ((n_ge - n_gt) > need) & (thr > neg_inf_key)
    any_excess = jnp.max(jnp.where(excess, 1, 0)) > 0

    def tie_cut():
        def step(t, lo):
            cand = lo + lax.shift_left(jnp.int32(1), int(math.log2(seq)) - t)
            c = count(lambda kt, off: (kt == thr) & (kpos0 + off < cand))
            return jnp.where(c < need, cand, lo)
        lo = lax.fori_loop(0, int(math.log2(seq)) + 1, step, jnp.zeros((1, tq), jnp.int32))
        return jnp.where(excess, lo, seq)

    cut = lax.cond(any_excess, tie_cut, lambda: jnp.full((1, tq), seq, jnp.int32))

    m_sc[...] = jnp.full_like(m_sc, NEG_BIG)
    acc_sc[...] = jnp.zeros_like(acc_sc)

    def attn_tile(j, c):
        off = pl.multiple_of(j * tk, tk)
        kt = key_sc[pl.ds(off, tk), :]
        kpos = kpos0 + off
        bias = jnp.where(kt > thr, 0.0, jnp.where(kt == thr, jnp.where(kpos <= cut, 0.0, NEG_BIG), NEG_BIG))
        bias = jnp.where(kpos <= qpos, bias, NEG_BIG)
        kv = kvl_ref[pl.ds(off, tk), :]
        kvt = kvt_ref[:, pl.ds(off, tk)]

        def head_step(h):
            s = _dot_nt(kv, qf_ref[h]) + bias
            yield None
            m_old = m_sc[h]
            m_new = jnp.maximum(m_old, jnp.max(s, axis=0, keepdims=True))
            alpha = jnp.exp2(m_old - m_new)
            pv = _dot(kvt, jnp.exp2(s - m_new).astype(BF16))
            yield None
            acc_sc[h] = alpha * acc_sc[h] + pv
            m_sc[h] = m_new
            yield None

        _round_robin([head_step(h) for h in range(H)])
        return c

    lax.fori_loop(0, nj, attn_tile, 0)
    for h in range(H):
        acc = acc_sc[h]
        o_lat_t = (acc[:DSA_KV_RANK] / acc[DSA_KV_RANK:DSA_KV_RANK + 1]).astype(BF16)
        o_ref[:, h * DSA_HEAD_DIM:(h + 1) * DSA_HEAD_DIM] = _dot_tn(o_lat_t, wuv_ref[h]).astype(o_ref.dtype)


def _dsa_layer(x, xb, w_in, kv_norm_g, w_uk, w_uv, w_out, ln_g, ln_b, B, S, tq=256, tk=512):
    D = D_MODEL
    H, dh, dr, dc = DSA_HEADS, DSA_HEAD_DIM, DSA_ROPE_DIM, DSA_KV_RANK
    HI, di = IDX_HEADS, IDX_DIM
    w_q, w_ckv, w_kr, w_qi, w_ki, w_wi, w_g = jnp.split(
        w_in, np.cumsum([H * dh, dc, dr, HI * di, di, HI]).tolist(), axis=1)
    w_misc = jnp.concatenate([w_kr, w_ki, w_wi, jnp.zeros((D, LANES - dr - di - HI), F32)], axis=1)
    w_main = jnp.concatenate([w_q, w_g, w_qi, w_ckv, w_misc], axis=1).astype(BF16)
    n_main = w_main.shape[1]
    proj = _matmul(xb, w_main, tn=n_main // 2)
    c_q, c_qi, c_ckv, c_misc = 0, 2 * D // LANES, (2 * D + HI * di) // LANES, (2 * D + HI * di + dc) // LANES

    q_groups = [(0, dr // 2)]
    i_groups = [(0, IDX_ROPE_DIM // 2), (di, IDX_ROPE_DIM // 2)]
    m_groups = [(0, dr // 2), (dr, IDX_ROPE_DIM // 2)]
    cq, sq = _rope_cs(S, LANES, q_groups, [dr])
    ci, si = _rope_cs(S, LANES, i_groups, [IDX_ROPE_DIM, IDX_ROPE_DIM])
    cm, sm = _rope_cs(S, LANES, m_groups, [dr, IDX_ROPE_DIM])
    pq = jnp.asarray(_rope_perm(LANES, q_groups), BF16)
    pi = jnp.asarray(_rope_perm(LANES, i_groups), BF16)
    pm = jnp.asarray(_rope_perm(LANES, m_groups), BF16)
    selk = np.zeros((LANES, LANES), np.float32)
    for j in range(di):
        selk[dr + j, j] = 1.0
        selk[dr + j, di + j] = 1.0
    selw = np.zeros((LANES, LANES), np.float32)
    for j in range(HI):
        selw[dr + di + j, j] = 1.0
    wuk = jnp.concatenate([jnp.zeros((H, dr, dc), F32), jnp.transpose(w_uk, (0, 2, 1))], axis=1).astype(BF16)

    ts = _pick_tile(S, 256)
    p3 = proj.reshape(B, S, n_main)
    tab = lambda: pl.BlockSpec((ts, LANES), lambda b, j: (j, 0))
    mat = lambda: pl.BlockSpec((LANES, LANES), lambda b, j: (0, 0))
    qf, kvl, kvt, qir, kid, wit = pl.pallas_call(
        _dsa_prep_kernel,
        grid=(B, S // ts),
        in_specs=[pl.BlockSpec((None, ts, H * dh), lambda b, j: (b, j, 0)),
                  pl.BlockSpec((None, ts, HI * di), lambda b, j: (b, j, c_qi * LANES // (HI * di))),
                  pl.BlockSpec((None, ts, dc), lambda b, j: (b, j, c_ckv)),
                  pl.BlockSpec((None, ts, LANES), lambda b, j: (b, j, c_misc)),
                  tab(), tab(), tab(), tab(), tab(), tab(),
                  mat(), mat(), mat(), mat(), mat(),
                  pl.BlockSpec((H, LANES, dc), lambda b, j: (0, 0, 0)),
                  pl.BlockSpec((1, dc), lambda b, j: (0, 0))],
        out_specs=[pl.BlockSpec((None, H, ts, 2 * dc), lambda b, j: (b, 0, j, 0)),
                   pl.BlockSpec((None, ts, 2 * dc), lambda b, j: (b, j, 0)),
                   pl.BlockSpec((None, dc + DSA_ONES_ROWS, ts), lambda b, j: (b, 0, j)),
                   pl.BlockSpec((None, ts, HI * di), lambda b, j: (b, j, 0)),
                   pl.BlockSpec((None, ts, LANES), lambda b, j: (b, j, 0)),
                   pl.BlockSpec((None, HI, ts), lambda b, j: (b, 0, j))],
        out_shape=[jax.ShapeDtypeStruct((B, H, S, 2 * dc), BF16),
                   jax.ShapeDtypeStruct((B, S, 2 * dc), BF16),
                   jax.ShapeDtypeStruct((B, dc + DSA_ONES_ROWS, S), BF16),
                   jax.ShapeDtypeStruct((B, S, HI * di), BF16),
                   jax.ShapeDtypeStruct((B, S, LANES), BF16),
                   jax.ShapeDtypeStruct((B, HI, S), F32)],
        compiler_params=_cparams("parallel", "parallel"),
        name="dsa_prep",
    )(p3, p3, p3, p3, cq, sq, ci, si, cm, sm, pq, pi, pm,
      jnp.asarray(selk, BF16), jnp.asarray(selw, BF16), wuk, kv_norm_g.reshape(1, dc))

    tq = _pick_tile(S, tq)
    tk = _pick_tile(S, tk)
    k_sel = min(TOPK_MAX, S // 4)
    o = pl.pallas_call(
        functools.partial(_dsa_main_kernel, tq=tq, tk=tk, k_sel=k_sel, seq=S),
        grid=(B, S // tq),
        in_specs=[pl.BlockSpec((None, H, tq, 2 * dc), lambda b, i: (b, 0, i, 0)),
                  pl.BlockSpec((None, S, 2 * dc), lambda b, i: (b, 0, 0)),
                  pl.BlockSpec((None, dc + DSA_ONES_ROWS, S), lambda b, i: (b, 0, 0)),
                  pl.BlockSpec((None, tq, HI * di), lambda b, i: (b, i, 0)),
                  pl.BlockSpec((None, S, LANES), lambda b, i: (b, 0, 0)),
                  pl.BlockSpec((None, HI, tq), lambda b, i: (b, 0, i)),
                  pl.BlockSpec((H, dc, dh), lambda b, i: (0, 0, 0))],
        out_specs=pl.BlockSpec((None, tq, H * dh), lambda b, i: (b, i, 0)),
        out_shape=jax.ShapeDtypeStruct((B, S, H * dh), BF16),
        scratch_shapes=[pltpu.VMEM((S, tq), jnp.int32),
                        pltpu.VMEM((S, tq), jnp.int16), pltpu.VMEM((S, tq), jnp.int16),
                        pltpu.VMEM((S, tq), jnp.int16),
                        pltpu.VMEM((H, 1, tq), F32),
                        pltpu.VMEM((H, dc + DSA_ONES_ROWS, tq), F32)],
        compiler_params=_cparams("parallel", "arbitrary"),
        name="dsa_select_attention",
    )(qf, kvl, kvt, qir, kid, wit, w_uv.astype(BF16))
    return _outproj_ln(proj, 1, o.reshape(B * S, D), x, w_out, ln_g, ln_b)


RWKV_HEADS = 16
RWKV_HEAD_DIM = 64
RWKV_GN_EPS = 64e-5
RWKV_CHUNK = 64
RWKV_SUB = 16


def _group_sum(x, gmat, split=True):
    outs = []
    for c in range(x.shape[1] // LANES):
        xc = x[:, c * LANES:(c + 1) * LANES]
        if split:
            hi, lo = _split2(xc)
            outs.append(_dot(hi, gmat) + _dot(lo, gmat))
        else:
            outs.append(_dot(xc.astype(BF16), gmat))
    return outs[0] if len(outs) == 1 else jnp.concatenate(outs, axis=1)


def _softplus(y):
    return jnp.maximum(y, 0.0) + jnp.log(1.0 + jnp.exp(-jnp.abs(y)))


def _rwkv_proj_kernel(x_ref, xprev_ref, mu_ref, wr_ref, wk_ref, wv_ref, wg_ref, wla_ref, wlb_ref,
                      ala_ref, alb_ref, w0_ref, a0_ref, kk_ref, ka_ref, rk_ref, gmat_ref,
                      r_ref, k_ref, v_ref, g_ref, kap_ref, b_ref, lw_ref, bonus_ref):
    x = x_ref[...]
    ts = x.shape[0]
    prev = jnp.where(pl.program_id(1) == 0, 0.0, xprev_ref[7:8, :])
    rowid = lax.broadcasted_iota(jnp.int32, (ts, 1), 0)
    xx = jnp.where(rowid == 0, prev, pltpu.roll(x, 1, 0)) - x

    def mixed(i):
        return (x + xx * mu_ref[i:i + 1, :]).astype(BF16)

    r = _dot(mixed(0), wr_ref[...])
    k = _dot(mixed(2), wk_ref[...])
    v = _dot(mixed(3), wv_ref[...])
    g_ref[...] = _dot(mixed(5), wg_ref[...]).astype(g_ref.dtype)
    lora_w = _dot(jnp.tanh(_dot(mixed(1), wla_ref[...])).astype(BF16), wlb_ref[...])
    lora_a = _dot(_dot(mixed(4), ala_ref[...]).astype(BF16), alb_ref[...])
    w_log = -_softplus(-(w0_ref[...] + lora_w)) - 0.5
    lw_ref[...] = -jnp.exp(w_log)
    a = _sigmoid(a0_ref[...] + lora_a)
    gmat = gmat_ref[...]
    kk = k * kk_ref[...]
    kap = kk * lax.rsqrt(_group_sum(kk * kk, gmat, split=False) + 1e-12)
    k2 = k * (1.0 + (a - 1.0) * ka_ref[...])
    bonus_ref[...] = _group_sum(r * k2 * rk_ref[...], gmat, split=False) * v
    r_ref[...] = r.astype(r_ref.dtype)
    k_ref[...] = k2.astype(k_ref.dtype)
    v_ref[...] = v.astype(v_ref.dtype)
    kap_ref[...] = kap.astype(kap_ref.dtype)
    b_ref[...] = (kap * a).astype(b_ref.dtype)


def _bd(x, left):
    z = jnp.zeros_like(x)
    return jnp.concatenate([jnp.where(left, x, z), jnp.where(left, z, x)], axis=0)


def _unbd(x_bd):
    c = x_bd.shape[0] // 2
    return x_bd[:c] + x_bd[c:]


def _rwkv_chunk_pair(lw, r, k, v, kap, b, tril, masks):
    C = RWKV_CHUNK
    left, strict, lower, same_sub, eye = masks
    p1, p2, p3 = _split3(lw)
    L = _dot(tril, p1) + _dot(tril, p2) + _dot(tril, p3)
    yield None
    Lc = L[C - 1:C, :]
    e_l, e_lx, e_nl, e_r = jnp.exp(L), jnp.exp(L - lw), jnp.exp(-L), jnp.exp(Lc - L)
    at = _bd(-kap * e_lx, left).astype(BF16)
    rt = _bd(r * e_l, left)
    bt = _bd(b * e_nl, left).astype(BF16)
    kt = _bd(k * e_nl, left).astype(BF16)
    bh = _bd(b * e_r, left).astype(BF16)
    kh = _bd(k * e_r, left).astype(BF16)
    vb = _bd(v, left).astype(BF16)

    a1 = _dot_nt(jnp.concatenate([at, rt.astype(BF16)], axis=0), jnp.concatenate([bt, kt], axis=0))
    yield None
    n = jnp.where(strict, a1[:2 * C, :2 * C], 0.0)
    ak = jnp.where(strict, a1[:2 * C, 2 * C:], 0.0).astype(BF16)
    rb = jnp.where(lower, a1[2 * C:, :2 * C], 0.0).astype(BF16)
    rk = jnp.where(lower, a1[2 * C:, 2 * C:], 0.0).astype(BF16)

    nd = jnp.where(same_sub, n, 0.0)
    no = (n - nd).astype(BF16)
    ndb = nd.astype(BF16)
    n2 = _dot(ndb, ndb)
    akv = _dot(ak, vb)
    yield None
    n2b = n2.astype(BF16)
    n4 = _dot(n2b, n2b)
    t01 = _dot((eye + nd).astype(BF16), (eye + n2).astype(BF16))
    yield None
    n4b = n4.astype(BF16)
    n8 = _dot(n4b, n4b)
    yield None
    t23 = _dot((eye + n4).astype(BF16), (eye + n8).astype(BF16))
    yield None
    tdb = _dot(t01.astype(BF16), t23.astype(BF16)).astype(BF16)
    yield None
    x1 = _dot(tdb, no)
    yield None
    x1b = x1.astype(BF16)
    x2 = _dot(x1b, x1b)
    yield None
    tx = _dot((eye + x1).astype(BF16), (eye + x2).astype(BF16))
    yield None
    t = _dot(tx.astype(BF16), tdb).astype(BF16)
    yield None
    pq = _dot(t, jnp.concatenate([at, akv.astype(BF16)], axis=1)).astype(BF16)
    yield None
    z = jnp.concatenate([pq, jnp.concatenate([jnp.zeros_like(vb), vb], axis=1)], axis=0)
    ry = _dot(jnp.concatenate([rb, rk], axis=1), z)
    mg = _dot_tn(z, jnp.concatenate([bh, kh], axis=0))
    yield None
    rp = rt + ry[:, :2 * C]
    yl = ry[:, 2 * C:]
    yield _unbd(rp), _unbd(yl), _unbd(mg[:2 * C]), _unbd(mg[2 * C:]), jnp.exp(Lc)


def _rwkv_chunk_kernel(lw_ref, r_ref, k_ref, v_ref, kap_ref, b_ref, tril_ref,
                       rp_ref, yl_ref, mm_ref, gg_ref, gam_ref, *, pairs):
    C = RWKV_CHUNK
    lane = lax.broadcasted_iota(jnp.int32, (1, LANES), 1)
    left = lane < RWKV_HEAD_DIM
    ri = lax.broadcasted_iota(jnp.int32, (2 * C, 2 * C), 0)
    ci = lax.broadcasted_iota(jnp.int32, (2 * C, 2 * C), 1)
    same_head = (ri // C) == (ci // C)
    strict = same_head & ((ri % C) > (ci % C))
    lower = same_head & ((ri % C) >= (ci % C))
    same_sub = (ri // RWKV_SUB) == (ci // RWKV_SUB)
    eye = jnp.where(ri == ci, 1.0, 0.0).astype(F32)
    masks = (left, strict, lower, same_sub, eye)
    tril = tril_ref[...]
    slices = [slice(p * LANES, (p + 1) * LANES) for p in range(pairs)]
    results = _round_robin([
        _rwkv_chunk_pair(lw_ref[:, sl], r_ref[:, sl].astype(F32), k_ref[:, sl].astype(F32),
                         v_ref[:, sl].astype(F32), kap_ref[:, sl].astype(F32), b_ref[:, sl].astype(F32),
                         tril, masks)
        for sl in slices])
    for sl, (rp, yl, mm, gg, gam) in zip(slices, results):
        rp_ref[:, sl] = rp.astype(rp_ref.dtype)
        yl_ref[:, sl] = yl
        mm_ref[:, sl] = mm.astype(mm_ref.dtype)
        gg_ref[:, sl] = gg
        gam_ref[:, sl] = gam


def _rwkv_seq_kernel(rp_ref, yl_ref, mm_ref, gg_ref, gam_ref, y_ref, s_sc, *, pairs, cb):
    C = RWKV_CHUNK
    lane = lax.broadcasted_iota(jnp.int32, (1, LANES), 1)
    left = lane < RWKV_HEAD_DIM

    @pl.when(pl.program_id(2) == 0)
    def _():
        s_sc[...] = jnp.zeros_like(s_sc)

    states = [s_sc[p] for p in range(pairs)]
    for c in range(cb):
        rows = slice(c * C, (c + 1) * C)
        for p in range(pairs):
            sl = slice(p * LANES, (p + 1) * LANES)
            s = states[p]
            sb = s.astype(BF16)
            y_ref[rows, sl] = _dot_nt(rp_ref[rows, sl], sb) + yl_ref[rows, sl]
            mm = _bd(mm_ref[c, :, sl], left)
            gg = _bd(gg_ref[c, :, sl], left)
            states[p] = s * gam_ref[c, :, sl] + _dot(sb, mm) + gg
    for p in range(pairs):
        s_sc[p] = states[p]


def _rwkv_post_kernel(y_ref, bonus_ref, gmat_ref, gg_ref, gb_ref, o_ref):
    y = y_ref[...]
    gmat = gmat_ref[...]
    inv_n = 1.0 / RWKV_HEAD_DIM
    yc = y - _group_sum(y, gmat) * inv_n
    var = _group_sum(yc * yc, gmat) * inv_n
    yn = yc * lax.rsqrt(var + RWKV_GN_EPS) * gg_ref[...] + gb_ref[...]
    o_ref[...] = (yn + bonus_ref[...]).astype(o_ref.dtype)


def _rwkv_layer(x, xb, mu, w_in, w0, w_lora_a, w_lora_b, a0, a_lora_a, a_lora_b, k_k, k_a, r_k,
                gn_g, gn_b, w_out, ln_g, ln_b, B, S, ts=512, pairs=8, seq_pairs=4):
    D = D_MODEL
    C = RWKV_CHUNK
    nc = S // C
    ts = _pick_tile(S, ts)
    w_r, w_k, w_v, w_g = [w.astype(BF16) for w in jnp.split(w_in, 4, axis=1)]
    gmat = jnp.asarray(np.kron(np.eye(2, dtype=np.float32), np.ones((RWKV_HEAD_DIM, RWKV_HEAD_DIM), np.float32)), BF16)
    row = lambda a: a.reshape(1, D)
    x3 = x.reshape(B, S, D)
    full = lambda shape: pl.BlockSpec(shape, lambda b, j: (0,) * len(shape))
    tile = lambda: pl.BlockSpec((None, ts, D), lambda b, j: (b, j, 0))
    lr = w_lora_a.shape[1]
    outs = pl.pallas_call(
        _rwkv_proj_kernel,
        grid=(B, S // ts),
        in_specs=[tile(),
                  pl.BlockSpec((None, 8, D), lambda b, j: (b, jnp.maximum(j * (ts // 8) - 1, 0), 0)),
                  full((6, D)), full((D, D)), full((D, D)), full((D, D)), full((D, D)),
                  full((D, lr)), full((lr, D)), full((D, lr)), full((lr, D)),
                  full((1, D)), full((1, D)), full((1, D)), full((1, D)), full((1, D)),
                  full((LANES, LANES))],
        out_specs=[tile() for _ in range(8)],
        out_shape=[jax.ShapeDtypeStruct((B, S, D), dt) for dt in (BF16, BF16, BF16, BF16, BF16, BF16, F32, F32)],
        compiler_params=_cparams("parallel", "arbitrary"),
        name="rwkv_projections",
    )(x3, x3, mu, w_r, w_k, w_v, w_g, w_lora_a.astype(BF16), w_lora_b.astype(BF16),
      a_lora_a.astype(BF16), a_lora_b.astype(BF16), row(w0), row(a0), row(k_k), row(k_a), row(r_k), gmat)
    r, k2, v, g, kap, bvec, lw, bonus = outs

    tril = jnp.tril(jnp.ones((C, C), BF16))
    pw = pairs * LANES
    cblk = lambda: pl.BlockSpec((None, C, pw), lambda b, c, q: (b, c, q))
    sblk = lambda: pl.BlockSpec((None, None, C, pw), lambda b, c, q: (b, c, 0, q))
    rp, yl, mm, gg, gam = pl.pallas_call(
        functools.partial(_rwkv_chunk_kernel, pairs=pairs),
        grid=(B, nc, D // pw),
        in_specs=[cblk() for _ in range(6)] + [pl.BlockSpec((C, C), lambda b, c, q: (0, 0))],
        out_specs=[cblk(), cblk(), sblk(), sblk(),
                   pl.BlockSpec((None, None, 1, pw), lambda b, c, q: (b, c, 0, q))],
        out_shape=[jax.ShapeDtypeStruct((B, S, D), BF16), jax.ShapeDtypeStruct((B, S, D), F32),
                   jax.ShapeDtypeStruct((B, nc, C, D), BF16), jax.ShapeDtypeStruct((B, nc, C, D), F32),
                   jax.ShapeDtypeStruct((B, nc, 1, D), F32)],
        compiler_params=_cparams("parallel", "parallel", "parallel"),
        name="rwkv_chunk_summaries",
    )(lw, r, k2, v, kap, bvec, tril)

    cb = _pick_tile(nc, 8)
    pairs = seq_pairs
    pw = pairs * LANES
    y = pl.pallas_call(
        functools.partial(_rwkv_seq_kernel, pairs=pairs, cb=cb),
        grid=(B, D // pw, nc // cb),
        in_specs=[pl.BlockSpec((None, cb * C, pw), lambda b, q, j: (b, j, q)),
                  pl.BlockSpec((None, cb * C, pw), lambda b, q, j: (b, j, q)),
                  pl.BlockSpec((None, cb, C, pw), lambda b, q, j: (b, j, 0, q)),
                  pl.BlockSpec((None, cb, C, pw), lambda b, q, j: (b, j, 0, q)),
                  pl.BlockSpec((None, cb, 1, pw), lambda b, q, j: (b, j, 0, q))],
        out_specs=pl.BlockSpec((None, cb * C, pw), lambda b, q, j: (b, j, q)),
        out_shape=jax.ShapeDtypeStruct((B, S, D), F32),
        scratch_shapes=[pltpu.VMEM((pairs, 2 * C, LANES), F32)],
        compiler_params=_cparams("parallel", "parallel", "arbitrary"),
        name="rwkv_state_scan",
    )(rp, yl, mm, gg, gam)

    tp = _pick_tile(S, 512)
    o = pl.pallas_call(
        _rwkv_post_kernel,
        grid=(B, S // tp),
        in_specs=[pl.BlockSpec((None, tp, D), lambda b, j: (b, j, 0)),
                  pl.BlockSpec((None, tp, D), lambda b, j: (b, j, 0)),
                  pl.BlockSpec((LANES, LANES), lambda b, j: (0, 0)),
                  pl.BlockSpec((1, D), lambda b, j: (0, 0)),
                  pl.BlockSpec((1, D), lambda b, j: (0, 0))],
        out_specs=pl.BlockSpec((None, tp, D), lambda b, j: (b, j, 0)),
        out_shape=jax.ShapeDtypeStruct((B, S, D), BF16),
        compiler_params=_cparams("parallel", "parallel"),
        name="rwkv_groupnorm_bonus",
    )(y, bonus, gmat, row(gn_g), row(gn_b))
    return _outproj_ln(g.reshape(B * S, D), 0, o.reshape(B * S, D), x, w_out, ln_g, ln_b)


def kernel(x, ln_g, ln_b, fox_w_in, fox_b_f, fox_w_out, dsa_w_in, dsa_kv_norm_g, dsa_w_uk, dsa_w_uv, dsa_w_out, rwkv_mu, rwkv_w_in, rwkv_w0, rwkv_w_lora_a, rwkv_w_lora_b, rwkv_a0, rwkv_a_lora_a, rwkv_a_lora_b, rwkv_k_k, rwkv_k_a, rwkv_r_k, rwkv_gn_g, rwkv_gn_b, rwkv_w_out, ret_w_in, ret_gn_g, ret_w_out):
    B, S, D = x.shape
    h = x.reshape(B * S, D)
    hb = h.astype(BF16)
    h, hb = _fox_layer(h, hb, fox_w_in, fox_b_f, fox_w_out, ln_g[0], ln_b[0], B, S)
    h, hb = _dsa_layer(h, hb, dsa_w_in, dsa_kv_norm_g, dsa_w_uk, dsa_w_uv, dsa_w_out, ln_g[1], ln_b[1], B, S)
    h, hb = _rwkv_layer(h, hb, rwkv_mu, rwkv_w_in, rwkv_w0, rwkv_w_lora_a, rwkv_w_lora_b, rwkv_a0,
                        rwkv_a_lora_a, rwkv_a_lora_b, rwkv_k_k, rwkv_k_a, rwkv_r_k, rwkv_gn_g, rwkv_gn_b,
                        rwkv_w_out, ln_g[2], ln_b[2], B, S)
    h, hb = _ret_layer(h, hb, ret_w_in, ret_gn_g, ret_w_out, ln_g[3], ln_b[3], B, S)
    return h.reshape(B, S, D)
```

```python
import functools
import math

import jax
import jax.numpy as jnp
import numpy as np
from jax import lax
from jax.experimental import pallas as pl
from jax.experimental.pallas import tpu as pltpu

F32 = jnp.float32
BF16 = jnp.bfloat16

D_MODEL = 1024
DEPTH = 4
LN_EPS = 1e-5
RMS_EPS = 1e-6
DN_ALPHA = (2 * DEPTH) ** 0.25
ROPE_THETA = 500000.0

FOX_HEADS = 8
FOX_HEAD_DIM = 128

RET_HEADS = 4
RET_HEAD_DIM = 256
RET_THETA = 10000.0

LANES = 128
VMEM_LIMIT = 48 * 1024 * 1024
NEG_BIG = -1e30
LOG2E = 1.4426950408889634


def _cparams(*sem):
    return pltpu.CompilerParams(dimension_semantics=sem, vmem_limit_bytes=VMEM_LIMIT)


def _dot(a, b):
    return jnp.dot(a, b, preferred_element_type=F32)


def _dot_nt(a, b):
    return lax.dot_general(a, b, (((1,), (1,)), ((), ())), preferred_element_type=F32)


def _dot_tn(a, b):
    return lax.dot_general(a, b, (((0,), (0,)), ((), ())), preferred_element_type=F32)


def _split2(x):
    hi = x.astype(BF16)
    lo = (x - hi.astype(F32)).astype(BF16)
    return hi, lo


def _split3(x):
    p1 = x.astype(BF16)
    r1 = x - p1.astype(F32)
    p2 = r1.astype(BF16)
    p3 = (r1 - p2.astype(F32)).astype(BF16)
    return p1, p2, p3


def _sigmoid(x):
    return 1.0 / (1.0 + jnp.exp(-x))


def _round_robin(gens):
    results = [None] * len(gens)
    live = list(range(len(gens)))
    while live:
        still = []
        for i in live:
            try:
                out = next(gens[i])
            except StopIteration:
                continue
            if out is not None:
                results[i] = out
            still.append(i)
        live = still
    return results


def _pick_tile(n, pref):
    t = min(n, pref)
    while n % t:
        t //= 2
    return t


def _mm_kernel(a_ref, w_ref, o_ref):
    o_ref[...] = _dot(a_ref[...].astype(BF16), w_ref[...]).astype(o_ref.dtype)


def _matmul(a, w, out_dtype=BF16, tm=1024, tn=1024):
    M, K = a.shape
    N = w.shape[1]
    tm = _pick_tile(M, tm)
    if N % tn:
        tn = N
    return pl.pallas_call(
        _mm_kernel,
        grid=(M // tm, N // tn),
        in_specs=[pl.BlockSpec((tm, K), lambda i, j: (i, 0)),
                  pl.BlockSpec((K, tn), lambda i, j: (0, j))],
        out_specs=pl.BlockSpec((tm, tn), lambda i, j: (i, j)),
        out_shape=jax.ShapeDtypeStruct((M, N), out_dtype),
        compiler_params=_cparams("parallel", "arbitrary"),
        name="proj_matmul",
    )(a, w)


def _outproj_ln_kernel(g_ref, o_ref, x_ref, w_ref, lg_ref, lb_ref, xo_ref, xb_ref):
    half_g = g_ref[...] * 0.5
    h = (half_g * o_ref[...]) * (1.0 + jnp.tanh(half_g))
    z = DN_ALPHA * x_ref[...] + _dot(h.astype(BF16), w_ref[...])
    zc = z - jnp.mean(z, axis=-1, keepdims=True)
    var = jnp.mean(zc * zc, axis=-1, keepdims=True)
    out = zc * lax.rsqrt(var + LN_EPS) * lg_ref[...] + lb_ref[...]
    xo_ref[...] = out
    xb_ref[...] = out.astype(BF16)


def _outproj_ln(gate_arr, gate_col, o, x, w_out, ln_g, ln_b, tm=512):
    M, D = x.shape
    tm = _pick_tile(M, tm)
    return pl.pallas_call(
        _outproj_ln_kernel,
        grid=(M // tm,),
        in_specs=[pl.BlockSpec((tm, D), lambda i: (i, gate_col)),
                  pl.BlockSpec((tm, D), lambda i: (i, 0)),
                  pl.BlockSpec((tm, D), lambda i: (i, 0)),
                  pl.BlockSpec((D, D), lambda i: (0, 0)),
                  pl.BlockSpec((1, D), lambda i: (0, 0)),
                  pl.BlockSpec((1, D), lambda i: (0, 0))],
        out_specs=[pl.BlockSpec((tm, D), lambda i: (i, 0)),
                   pl.BlockSpec((tm, D), lambda i: (i, 0))],
        out_shape=[jax.ShapeDtypeStruct((M, D), F32), jax.ShapeDtypeStruct((M, D), BF16)],
        compiler_params=_cparams("parallel"),
        name="outproj_layernorm",
    )(gate_arr, o, x, w_out.astype(BF16), ln_g.reshape(1, D), ln_b.reshape(1, D))


FOX_BIAS_PIECES = 3


def _fox_cum_kernel(x_ref, wh_ref, wl_ref, bf_ref, tril_ref, place_ref, pc_ref, carry_sc):
    @pl.when(pl.program_id(1) == 0)
    def _():
        carry_sc[...] = jnp.zeros_like(carry_sc)

    x_hi, x_lo = _split2(x_ref[...])
    z = _dot(x_hi, wh_ref[...]) + _dot(x_lo, wh_ref[...]) + _dot(x_hi, wl_ref[...]) + bf_ref[...]
    logf = jnp.minimum(z, 0.0) - jnp.log(1.0 + jnp.exp(-jnp.abs(z)))
    p1, p2, p3 = _split3(logf)
    tril = tril_ref[...]
    c = _dot(tril, p1) + _dot(tril, p2) + _dot(tril, p3) + carry_sc[...]
    carry_sc[...] = c[c.shape[0] - 1:, :]
    pieces = _split3(c * (-LOG2E))
    pc_ref[...] = sum(_dot(pieces[p], place_ref[p]) for p in range(FOX_BIAS_PIECES)).astype(pc_ref.dtype)


def _fox_cum(x3, w_f, b_f, ts=512):
    B, S, D = x3.shape
    H = w_f.shape[1]
    ts = _pick_tile(S, ts)
    w_pad = jnp.zeros((D, LANES), F32).at[:, :H].set(w_f)
    w_hi, w_lo = _split2(w_pad)
    b_pad = jnp.zeros((1, LANES), F32).at[0, :H].set(b_f)
    tril = jnp.tril(jnp.ones((ts, ts), BF16))
    place = np.zeros((FOX_BIAS_PIECES, LANES, LANES), np.float32)
    for p in range(FOX_BIAS_PIECES):
        for h in range(H):
            place[p, h, FOX_BIAS_PIECES * h + p] = 1.0
    return pl.pallas_call(
        _fox_cum_kernel,
        grid=(B, S // ts),
        in_specs=[pl.BlockSpec((None, ts, D), lambda b, j: (b, j, 0)),
                  pl.BlockSpec((D, LANES), lambda b, j: (0, 0)),
                  pl.BlockSpec((D, LANES), lambda b, j: (0, 0)),
                  pl.BlockSpec((1, LANES), lambda b, j: (0, 0)),
                  pl.BlockSpec((ts, ts), lambda b, j: (0, 0)),
                  pl.BlockSpec((FOX_BIAS_PIECES, LANES, LANES), lambda b, j: (0, 0, 0))],
        out_specs=pl.BlockSpec((None, ts, LANES), lambda b, j: (b, j, 0)),
        out_shape=jax.ShapeDtypeStruct((B, S, LANES), BF16),
        scratch_shapes=[pltpu.VMEM((1, LANES), F32)],
        compiler_params=_cparams("parallel", "arbitrary"),
        name="fox_decay_cumsum",
    )(x3, w_hi, w_lo, b_pad, tril, jnp.asarray(place, BF16))


ONES_ROWS = 16


def _fox_attn_kernel(q_ref, k_ref, v_ref, pc_ref, o_ref, kaug_sc, vt_sc, m_sc, acc_sc, *, tq, nsub, seq, unroll):
    dh = FOX_HEAD_DIM
    h = pl.program_id(1)
    g = pl.program_id(2)

    @pl.when(g == 0)
    def _():
        kaug_sc[:, :dh] = k_ref[...]
        kaug_sc[:, dh:] = pc_ref[...]
        for c in range(seq // tq):
            rows = slice(c * tq, (c + 1) * tq)
            vt_sc[:dh, rows] = v_ref[rows, :].astype(F32).T.astype(BF16)
        vt_sc[dh:, :] = jnp.ones((ONES_ROWS, seq), BF16)

    lane = lax.broadcasted_iota(jnp.int32, (tq, LANES), 1)
    bias_lanes = (lane >= FOX_BIAS_PIECES * h) & (lane < FOX_BIAS_PIECES * (h + 1))
    ones_h = jnp.where(bias_lanes, 1.0, 0.0).astype(BF16)
    q_aug = [jnp.concatenate([q_ref[a * tq:(a + 1) * tq, :], ones_h], axis=1) for a in range(nsub)]
    m_sc[...] = jnp.full_like(m_sc, NEG_BIG)
    acc_sc[...] = jnp.zeros_like(acc_sc)
    causal = (lax.broadcasted_iota(jnp.int32, (tq, tq), 0) <= lax.broadcasted_iota(jnp.int32, (tq, tq), 1))
    first = g * nsub

    def chain(a, tiles, diag_last):
        offs = [pl.multiple_of(j * tq, tq) for j in tiles]
        scores = []
        for off in offs:
            scores.append(_dot_nt(kaug_sc[pl.ds(off, tq), :], q_aug[a]))
            yield None
        for n, (off, s) in enumerate(zip(offs, scores)):
            if diag_last and n == len(offs) - 1:
                s = jnp.where(causal, s, NEG_BIG)
            m_old = m_sc[a]
            m_new = jnp.maximum(m_old, jnp.max(s, axis=0, keepdims=True))
            alpha = jnp.exp2(m_old - m_new)
            pv = _dot(vt_sc[:, pl.ds(off, tq)], jnp.exp2(s - m_new).astype(BF16))
            yield None
            acc_sc[a] = alpha * acc_sc[a] + pv
            m_sc[a] = m_new
        yield None

    def body(jj, c):
        _round_robin([chain(a, [jj * unroll + u for u in range(unroll)], False) for a in range(nsub)])
        return c

    lax.fori_loop(0, first // unroll, body, 0)
    _round_robin([chain(a, [first + t for t in range(a + 1)], True) for a in range(nsub)])
    for a in range(nsub):
        acc = acc_sc[a]
        o_t = acc[:dh] / acc[dh:dh + 1]
        o_ref[a * tq:(a + 1) * tq, :] = o_t.T.astype(o_ref.dtype)


def _fox_attention(proj3, pieces, tq=256, nsub=8):
    B, S, _ = proj3.shape
    H, dh = FOX_HEADS, FOX_HEAD_DIM
    tq = _pick_tile(S, tq)
    nsub = _pick_tile(S // tq, nsub)
    tg = tq * nsub
    return pl.pallas_call(
        functools.partial(_fox_attn_kernel, tq=tq, nsub=nsub, seq=S, unroll=min(nsub, 4)),
        grid=(B, H, S // tg),
        in_specs=[pl.BlockSpec((None, tg, dh), lambda b, h, g: (b, g, h)),
                  pl.BlockSpec((None, S, dh), lambda b, h, g: (b, 0, H + h)),
                  pl.BlockSpec((None, S, dh), lambda b, h, g: (b, 0, 2 * H + h)),
                  pl.BlockSpec((None, S, LANES), lambda b, h, g: (b, 0, 0))],
        out_specs=pl.BlockSpec((None, tg, dh), lambda b, h, g: (b, g, h)),
        out_shape=jax.ShapeDtypeStruct((B, S, H * dh), BF16),
        scratch_shapes=[pltpu.VMEM((S, dh + LANES), BF16),
                        pltpu.VMEM((dh + ONES_ROWS, S), BF16),
                        pltpu.VMEM((nsub, 1, tq), F32),
                        pltpu.VMEM((nsub, dh + ONES_ROWS, tq), F32)],
        compiler_params=_cparams("parallel", "parallel", "arbitrary"),
        name="fox_attention",
    )(proj3, proj3, proj3, pieces)


def _fox_layer(x, xb, w_in, b_f, w_out, ln_g, ln_b, B, S):
    D = D_MODEL
    H, dh = FOX_HEADS, FOX_HEAD_DIM
    scale = dh ** -0.5 * LOG2E
    w_q, w_k, w_v, w_f, w_g = jnp.split(w_in, [H * dh, 2 * H * dh, 3 * H * dh, 3 * H * dh + H], axis=1)
    w_main = jnp.concatenate([w_q * scale, w_k, w_v, w_g], axis=1).astype(BF16)
    proj = _matmul(xb, w_main)
    pieces = _fox_cum(x.reshape(B, S, D), w_f, b_f)
    o = _fox_attention(proj.reshape(B, S, 4 * D), pieces)
    return _outproj_ln(proj, 3, o.reshape(B * S, D), x, w_out, ln_g, ln_b)


def _ret_kernel(q_ref, k_ref, v_ref, cos_ref, sin_ref, dm_ref, xi_ref, zeta_ref, gc_ref, gn_ref,
                o_ref, r_sc):
    @pl.when(pl.program_id(1) == 0)
    def _():
        r_sc[...] = jnp.zeros_like(r_sc)

    dk = RET_HEAD_DIM
    half = dk // 2
    cos = cos_ref[...]
    sin = sin_ref[...]

    def rope(x):
        x1, x2 = x[:, :half], x[:, half:]
        return jnp.concatenate([x1 * cos - x2 * sin, x2 * cos + x1 * sin], axis=-1)

    def head_chain(h):
        cols = slice(h * dk, (h + 1) * dk)
        q = rope(q_ref[:, cols].astype(F32))
        k = rope(k_ref[:, cols].astype(F32)) * (dk ** -0.5)
        v = v_ref[:, cols]
        qb = q.astype(BF16)
        r_old = r_sc[h]
        scores = _dot_nt(qb, k.astype(BF16))
        cross = _dot(qb, r_old.astype(BF16))
        kz = (k * zeta_ref[h]).astype(BF16)
        r_new = _dot_tn(kz, v)
        yield None
        o = _dot((scores * dm_ref[h]).astype(BF16), v)
        r_sc[h] = r_old * gc_ref[h] + r_new
        yield None
        o = o + cross * xi_ref[h]
        o = o * lax.rsqrt(jnp.mean(o * o, axis=-1, keepdims=True) + RMS_EPS) * gn_ref[:, cols]
        o_ref[:, cols] = o.astype(o_ref.dtype)
        yield None

    _round_robin([head_chain(h) for h in range(RET_HEADS)])


def _ret_layer(x, xb, w_in, gn_g, w_out, ln_g, ln_b, B, S, chunk=512):
    D = D_MODEL
    H, dk = RET_HEADS, RET_HEAD_DIM
    C = _pick_tile(S, chunk)
    proj = _matmul(xb, w_in.astype(BF16))
    inv = 1.0 / (RET_THETA ** (jnp.arange(0, dk, 2, dtype=F32) / dk))
    ang = jnp.arange(S, dtype=F32)[:, None] * inv[None, :]
    cos, sin = jnp.cos(ang), jnp.sin(ang)
    log_g = jnp.log1p(-(2.0 ** (-5.0 - jnp.arange(H, dtype=F32))))
    pos = jnp.arange(C, dtype=F32)
    diff = pos[:, None] - pos[None, :]
    d_mask = jnp.where(diff[None] >= 0, jnp.exp(jnp.maximum(diff, 0.0)[None] * log_g[:, None, None]), 0.0)
    xi = jnp.broadcast_to(jnp.exp((pos[None, :] + 1.0) * log_g[:, None])[:, :, None], (H, C, dk))
    zeta = jnp.broadcast_to(jnp.exp((C - 1.0 - pos[None, :]) * log_g[:, None])[:, :, None], (H, C, dk))
    g_c = jnp.broadcast_to(jnp.exp(C * log_g)[:, None, None], (H, 1, dk))
    p3 = proj.reshape(B, S, 4 * D)
    o = pl.pallas_call(
        _ret_kernel,
        grid=(B, S // C),
        in_specs=[pl.BlockSpec((None, C, D), lambda b, c: (b, c, 0)),
                  pl.BlockSpec((None, C, D), lambda b, c: (b, c, 1)),
                  pl.BlockSpec((None, C, D), lambda b, c: (b, c, 2)),
                  pl.BlockSpec((C, dk // 2), lambda b, c: (c, 0)),
                  pl.BlockSpec((C, dk // 2), lambda b, c: (c, 0)),
                  pl.BlockSpec((H, C, C), lambda b, c: (0, 0, 0)),
                  pl.BlockSpec((H, C, dk), lambda b, c: (0, 0, 0)),
                  pl.BlockSpec((H, C, dk), lambda b, c: (0, 0, 0)),
                  pl.BlockSpec((H, 1, dk), lambda b, c: (0, 0, 0)),
                  pl.BlockSpec((1, D), lambda b, c: (0, 0))],
        out_specs=pl.BlockSpec((None, C, D), lambda b, c: (b, c, 0)),
        out_shape=jax.ShapeDtypeStruct((B, S, D), BF16),
        scratch_shapes=[pltpu.VMEM((H, dk, dk), F32)],
        compiler_params=_cparams("parallel", "arbitrary"),
        name="retnet_retention",
    )(p3, p3, p3, cos, sin, d_mask, xi, zeta, g_c, gn_g.reshape(1, D))
    return _outproj_ln(proj, 3, o.reshape(B * S, D), x, w_out, ln_g, ln_b)


DSA_HEADS = 8
DSA_HEAD_DIM = 128
DSA_ROPE_DIM = 32
DSA_KV_RANK = 128
IDX_HEADS = 8
IDX_DIM = 64
IDX_ROPE_DIM = 16
TOPK_MAX = 256
INT_MIN = -2 ** 31
HALF_MIN = -2 ** 15
HALF_ROWS = 16
DSA_ONES_ROWS = 16


def _rope_perm(width, groups):
    p = np.zeros((width, width), np.float32)
    for start, half in groups:
        for j in range(half):
            p[start + half + j, start + j] = 1.0
            p[start + j, start + half + j] = 1.0
    return p


def _rope_cs(S, width, groups, theta_dims):
    c = jnp.ones((S, width), F32)
    sg = jnp.zeros((S, width), F32)
    pos = jnp.arange(S, dtype=F32)[:, None]
    for (start, half), rot_dim in zip(groups, theta_dims):
        inv = 1.0 / (ROPE_THETA ** (jnp.arange(0, rot_dim, 2, dtype=F32) / rot_dim))
        ang = pos * inv[None, :]
        cos, sin = jnp.cos(ang), jnp.sin(ang)
        c = c.at[:, start:start + half].set(cos).at[:, start + half:start + 2 * half].set(cos)
        sg = sg.at[:, start:start + half].set(-sin).at[:, start + half:start + 2 * half].set(sin)
    return c, sg


def _dsa_prep_kernel(q_ref, qi_ref, ckv_ref, misc_ref, cq_ref, sq_ref, ci_ref, si_ref, cm_ref, sm_ref,
                     pq_ref, pi_ref, pm_ref, selk_ref, selw_ref, wuk_ref, kvg_ref,
                     qf_ref, kvl_ref, kvt_ref, qir_ref, kid_ref, wit_ref):
    H, dh = DSA_HEADS, DSA_HEAD_DIM
    lane = lax.broadcasted_iota(jnp.int32, (1, LANES), 1)
    rope_lanes = lane < DSA_ROPE_DIM
    cq, sq = cq_ref[...], sq_ref[...]
    scale = dh ** -0.5 * LOG2E
    for h in range(H):
        qh = q_ref[:, h * dh:(h + 1) * dh]
        qr = qh.astype(F32) * cq + _dot(qh, pq_ref[...]) * sq
        q_lat = _dot(qr.astype(BF16), wuk_ref[h])
        qf_ref[h, :, :dh] = (q_lat * scale).astype(BF16)
        qf_ref[h, :, dh:] = jnp.where(rope_lanes, qr * scale, 0.0).astype(BF16)
    ci, si = ci_ref[...], si_ref[...]
    for g in range(IDX_HEADS * IDX_DIM // LANES):
        qg = qi_ref[:, g * LANES:(g + 1) * LANES]
        qr = qg.astype(F32) * ci + _dot(qg, pi_ref[...]) * si
        qir_ref[:, g * LANES:(g + 1) * LANES] = (qr * (IDX_DIM ** -0.5)).astype(BF16)
    ckv = ckv_ref[...].astype(F32)
    ckv = ckv * lax.rsqrt(jnp.mean(ckv * ckv, axis=-1, keepdims=True) + RMS_EPS) * kvg_ref[...]
    misc = misc_ref[...]
    mr = (misc.astype(F32) * cm_ref[...] + _dot(misc, pm_ref[...]) * sm_ref[...])
    kvl_ref[:, :DSA_KV_RANK] = ckv.astype(BF16)
    kvl_ref[:, DSA_KV_RANK:] = jnp.where(rope_lanes, mr, 0.0).astype(BF16)
    kvt_ref[:DSA_KV_RANK, :] = ckv.T.astype(BF16)
    kvt_ref[DSA_KV_RANK:, :] = jnp.ones((DSA_ONES_ROWS, ckv.shape[0]), BF16)
    kid_ref[...] = _dot(mr.astype(BF16), selk_ref[...]).astype(BF16)
    wi = _dot(misc, selw_ref[...]) * (IDX_HEADS ** -0.5)
    wit_ref[...] = wi.T[:IDX_HEADS, :]


def _sort_key(x):
    b = pltpu.bitcast(x, jnp.int32)
    return jnp.where(b < 0, b ^ 0x7FFFFFFF, b)


def _dsa_main_kernel(qf_ref, kvl_ref, kvt_ref, qi_ref, kid_ref, wit_ref, wuv_ref, o_ref,
                     key_sc, hi_sc, lo_sc, low_sc, m_sc, acc_sc, *, tq, tk, k_sel, seq):
    H = DSA_HEADS
    i = pl.program_id(1)
    q0 = i * tq
    nj = (q0 + tq + tk - 1) // tk
    qpos = q0 + lax.broadcasted_iota(jnp.int32, (tk, tq), 1)
    kpos0 = lax.broadcasted_iota(jnp.int32, (tk, tq), 0)
    lane = lax.broadcasted_iota(jnp.int32, (1, LANES), 1)
    neg_inf_key = _sort_key(jnp.full((1, 1), -jnp.inf, F32))

    wit = wit_ref[...]
    qi_heads = []
    for h in range(IDX_HEADS):
        g = qi_ref[:, (h // 2) * LANES:(h // 2 + 1) * LANES]
        keep = (lane >= IDX_DIM) if (h % 2) else (lane < IDX_DIM)
        qi_heads.append(jnp.where(keep, g, jnp.zeros_like(g)))

    def score_tile(j, c):
        off = pl.multiple_of(j * tk, tk)
        ki = kid_ref[pl.ds(off, tk), :]
        scores = [_dot_nt(ki, qi_heads[h]) for h in range(IDX_HEADS)]
        isc = jnp.zeros((tk, tq), F32)
        for h in range(IDX_HEADS):
            isc = isc + jnp.maximum(scores[h], 0.0) * wit[h:h + 1, :]
        isc = jnp.where(kpos0 + off <= qpos, isc + 0.0, -jnp.inf)
        key = _sort_key(isc)
        key_sc[pl.ds(off, tk), :] = key
        hi_sc[pl.ds(off, tk), :] = (key >> 16).astype(jnp.int16)
        lo_sc[pl.ds(off, tk), :] = ((key & 0xFFFF) + HALF_MIN).astype(jnp.int16)
        return c

    lax.fori_loop(0, nj, score_tile, 0)

    def count(pred_fn):
        def body(j, acc):
            off = pl.multiple_of(j * tk, tk)
            kt = key_sc[pl.ds(off, tk), :]
            hit = jnp.where(pred_fn(kt, off), 1, 0)
            return acc + jnp.sum(hit.reshape(tk // 8, 8, tq), axis=0)
        acc = lax.fori_loop(0, nj, body, jnp.zeros((8, tq), jnp.int32))
        return jnp.sum(acc, axis=0, keepdims=True)

    rows16 = tk // HALF_ROWS

    def count16(ref, pred_fn):
        def body(j, acc):
            off = pl.multiple_of(j * tk, tk)
            hit = jnp.where(pred_fn(ref[pl.ds(off, tk), :].reshape(rows16, HALF_ROWS, tq)),
                            jnp.int16(1), jnp.int16(0))
            for r in range(rows16):
                acc = acc + hit[r]
            return acc
        acc = lax.fori_loop(0, nj, body, jnp.zeros((HALF_ROWS, tq), jnp.int16))
        return jnp.sum(acc.astype(jnp.int32), axis=0, keepdims=True)

    def as_half(v):
        return jnp.broadcast_to(v.astype(jnp.int16), (HALF_ROWS, tq))[None]

    def search16(ref, base0, cnt0, bits, want):
        def bit_step(t, carry):
            base, cnt_b = carry
            cand = base + lax.shift_left(jnp.int32(1), bits - 1 - t)
            cand16 = as_half(cand)
            c = count16(ref, lambda kt: kt >= cand16)
            ok = c >= want
            return jnp.where(ok, cand, base), jnp.where(ok, c, cnt_b)
        return lax.fori_loop(0, bits, bit_step, (base0, cnt0))

    zero16 = as_half(jnp.zeros((1, tq), jnp.int32))
    cnt_pos = count16(hi_sc, lambda kt: kt >= zero16)
    nonneg = cnt_pos >= k_sel
    t1, ge_hi = search16(hi_sc, jnp.where(nonneg, 0, HALF_MIN), jnp.where(nonneg, cnt_pos, nj * tk), 15, k_sel)
    t1_16 = as_half(t1)
    above = count16(hi_sc, lambda kt: kt > t1_16)

    def build_low(j, c):
        rows = pl.ds(pl.multiple_of(j * tk, tk), tk)
        hi = hi_sc[rows, :].reshape(rows16, HALF_ROWS, tq)
        lo = lo_sc[rows, :].reshape(rows16, HALF_ROWS, tq)
        low_sc[rows, :] = jnp.where(hi == t1_16, lo, jnp.int16(HALF_MIN)).reshape(tk, tq)
        return c

    lax.fori_loop(0, nj, build_low, 0)
    t2, ge_low = search16(low_sc, jnp.full((1, tq), HALF_MIN, jnp.int32), ge_hi - above, 16, k_sel - above)
    t2_16 = as_half(t2)
    thr = lax.shift_left(t1, 16) | (t2 - HALF_MIN)
    n_ge = above + ge_low
    n_gt = above + count16(low_sc, lambda kt: kt > t2_16)
    need = k_sel - n_gt
    excess = ((n_ge - n_gt) > need) & (thr > neg_inf_key)
    any_excess = jnp.max(jnp.where(excess, 1, 0)) > 0

    def tie_cut():
        def step(t, lo):
            cand = lo + lax.shift_left(jnp.int32(1), int(math.log2(seq)) - t)
            c = count(lambda kt, off: (kt == thr) & (kpos0 + off < cand))
            return jnp.where(c < need, cand, lo)
        lo = lax.fori_loop(0, int(math.log2(seq)) + 1, step, jnp.zeros((1, tq), jnp.int32))
        return jnp.where(excess, lo, seq)

    cut = lax.cond(any_excess, tie_cut, lambda: jnp.full((1, tq), seq, jnp.int32))

    m_sc[...] = jnp.full_like(m_sc, NEG_BIG)
    acc_sc[...] = jnp.zeros_like(acc_sc)

    def attn_tile(j, c):
        off = pl.multiple_of(j * tk, tk)
        kt = key_sc[pl.ds(off, tk), :]
        kpos = kpos0 + off
        bias = jnp.where(kt > thr, 0.0, jnp.where(kt == thr, jnp.where(kpos <= cut, 0.0, NEG_BIG), NEG_BIG))
        bias = jnp.where(kpos <= qpos, bias, NEG_BIG)
        kv = kvl_ref[pl.ds(off, tk), :]
        kvt = kvt_ref[:, pl.ds(off, tk)]

        def head_step(h):
            s = _dot_nt(kv, qf_ref[h]) + bias
            yield None
            m_old = m_sc[h]
            m_new = jnp.maximum(m_old, jnp.max(s, axis=0, keepdims=True))
            alpha = jnp.exp2(m_old - m_new)
            pv = _dot(kvt, jnp.exp2(s - m_new).astype(BF16))
            yield None
            acc_sc[h] = alpha * acc_sc[h] + pv
            m_sc[h] = m_new
            yield None

        _round_robin([head_step(h) for h in range(H)])
        return c

    lax.fori_loop(0, nj, attn_tile, 0)
    for h in range(H):
        acc = acc_sc[h]
        o_lat_t = (acc[:DSA_KV_RANK] / acc[DSA_KV_RANK:DSA_KV_RANK + 1]).astype(BF16)
        o_ref[:, h * DSA_HEAD_DIM:(h + 1) * DSA_HEAD_DIM] = _dot_tn(o_lat_t, wuv_ref[h]).astype(o_ref.dtype)


def _dsa_layer(x, xb, w_in, kv_norm_g, w_uk, w_uv, w_out, ln_g, ln_b, B, S, tq=256, tk=512):
    D = D_MODEL
    H, dh, dr, dc = DSA_HEADS, DSA_HEAD_DIM, DSA_ROPE_DIM, DSA_KV_RANK
    HI, di = IDX_HEADS, IDX_DIM
    w_q, w_ckv, w_kr, w_qi, w_ki, w_wi, w_g = jnp.split(
        w_in, np.cumsum([H * dh, dc, dr, HI * di, di, HI]).tolist(), axis=1)
    w_misc = jnp.concatenate([w_kr, w_ki, w_wi, jnp.zeros((D, LANES - dr - di - HI), F32)], axis=1)
    w_main = jnp.concatenate([w_q, w_g, w_qi, w_ckv, w_misc], axis=1).astype(BF16)
    n_main = w_main.shape[1]
    proj = _matmul(xb, w_main, tn=n_main // 2)
    c_q, c_qi, c_ckv, c_misc = 0, 2 * D // LANES, (2 * D + HI * di) // LANES, (2 * D + HI * di + dc) // LANES

    q_groups = [(0, dr // 2)]
    i_groups = [(0, IDX_ROPE_DIM // 2), (di, IDX_ROPE_DIM // 2)]
    m_groups = [(0, dr // 2), (dr, IDX_ROPE_DIM // 2)]
    cq, sq = _rope_cs(S, LANES, q_groups, [dr])
    ci, si = _rope_cs(S, LANES, i_groups, [IDX_ROPE_DIM, IDX_ROPE_DIM])
    cm, sm = _rope_cs(S, LANES, m_groups, [dr, IDX_ROPE_DIM])
    pq = jnp.asarray(_rope_perm(LANES, q_groups), BF16)
    pi = jnp.asarray(_rope_perm(LANES, i_groups), BF16)
    pm = jnp.asarray(_rope_perm(LANES, m_groups), BF16)
    selk = np.zeros((LANES, LANES), np.float32)
    for j in range(di):
        selk[dr + j, j] = 1.0
        selk[dr + j, di + j] = 1.0
    selw = np.zeros((LANES, LANES), np.float32)
    for j in range(HI):
        selw[dr + di + j, j] = 1.0
    wuk = jnp.concatenate([jnp.zeros((H, dr, dc), F32), jnp.transpose(w_uk, (0, 2, 1))], axis=1).astype(BF16)

    ts = _pick_tile(S, 256)
    p3 = proj.reshape(B, S, n_main)
    tab = lambda: pl.BlockSpec((ts, LANES), lambda b, j: (j, 0))
    mat = lambda: pl.BlockSpec((LANES, LANES), lambda b, j: (0, 0))
    qf, kvl, kvt, qir, kid, wit = pl.pallas_call(
        _dsa_prep_kernel,
        grid=(B, S // ts),
        in_specs=[pl.BlockSpec((None, ts, H * dh), lambda b, j: (b, j, 0)),
                  pl.BlockSpec((None, ts, HI * di), lambda b, j: (b, j, c_qi * LANES // (HI * di))),
                  pl.BlockSpec((None, ts, dc), lambda b, j: (b, j, c_ckv)),
                  pl.BlockSpec((None, ts, LANES), lambda b, j: (b, j, c_misc)),
                  tab(), tab(), tab(), tab(), tab(), tab(),
                  mat(), mat(), mat(), mat(), mat(),
                  pl.BlockSpec((H, LANES, dc), lambda b, j: (0, 0, 0)),
                  pl.BlockSpec((1, dc), lambda b, j: (0, 0))],
        out_specs=[pl.BlockSpec((None, H, ts, 2 * dc), lambda b, j: (b, 0, j, 0)),
                   pl.BlockSpec((None, ts, 2 * dc), lambda b, j: (b, j, 0)),
                   pl.BlockSpec((None, dc + DSA_ONES_ROWS, ts), lambda b, j: (b, 0, j)),
                   pl.BlockSpec((None, ts, HI * di), lambda b, j: (b, j, 0)),
                   pl.BlockSpec((None, ts, LANES), lambda b, j: (b, j, 0)),
                   pl.BlockSpec((None, HI, ts), lambda b, j: (b, 0, j))],
        out_shape=[jax.ShapeDtypeStruct((B, H, S, 2 * dc), BF16),
                   jax.ShapeDtypeStruct((B, S, 2 * dc), BF16),
                   jax.ShapeDtypeStruct((B, dc + DSA_ONES_ROWS, S), BF16),
                   jax.ShapeDtypeStruct((B, S, HI * di), BF16),
                   jax.ShapeDtypeStruct((B, S, LANES), BF16),
                   jax.ShapeDtypeStruct((B, HI, S), F32)],
        compiler_params=_cparams("parallel", "parallel"),
        name="dsa_prep",
    )(p3, p3, p3, p3, cq, sq, ci, si, cm, sm, pq, pi, pm,
      jnp.asarray(selk, BF16), jnp.asarray(selw, BF16), wuk, kv_norm_g.reshape(1, dc))

    tq = _pick_tile(S, tq)
    tk = _pick_tile(S, tk)
    k_sel = min(TOPK_MAX, S // 4)
    o = pl.pallas_call(
        functools.partial(_dsa_main_kernel, tq=tq, tk=tk, k_sel=k_sel, seq=S),
        grid=(B, S // tq),
        in_specs=[pl.BlockSpec((None, H, tq, 2 * dc), lambda b, i: (b, 0, i, 0)),
                  pl.BlockSpec((None, S, 2 * dc), lambda b, i: (b, 0, 0)),
                  pl.BlockSpec((None, dc + DSA_ONES_ROWS, S), lambda b, i: (b, 0, 0)),
                  pl.BlockSpec((None, tq, HI * di), lambda b, i: (b, i, 0)),
                  pl.BlockSpec((None, S, LANES), lambda b, i: (b, 0, 0)),
                  pl.BlockSpec((None, HI, tq), lambda b, i: (b, 0, i)),
                  pl.BlockSpec((H, dc, dh), lambda b, i: (0, 0, 0))],
        out_specs=pl.BlockSpec((None, tq, H * dh), lambda b, i: (b, i, 0)),
        out_shape=jax.ShapeDtypeStruct((B, S, H * dh), BF16),
        scratch_shapes=[pltpu.VMEM((S, tq), jnp.int32),
                        pltpu.VMEM((S, tq), jnp.int16), pltpu.VMEM((S, tq), jnp.int16),
                        pltpu.VMEM((S, tq), jnp.int16),
                        pltpu.VMEM((H, 1, tq), F32),
                        pltpu.VMEM((H, dc + DSA_ONES_ROWS, tq), F32)],
        compiler_params=_cparams("parallel", "arbitrary"),
        name="dsa_select_attention",
    )(qf, kvl, kvt, qir, kid, wit, w_uv.astype(BF16))
    return _outproj_ln(proj, 1, o.reshape(B * S, D), x, w_out, ln_g, ln_b)


RWKV_HEADS = 16
RWKV_HEAD_DIM = 64
RWKV_GN_EPS = 64e-5
RWKV_CHUNK = 64
RWKV_SUB = 16


def _group_sum(x, gmat, split=True):
    outs = []
    for c in range(x.shape[1] // LANES):
        xc = x[:, c * LANES:(c + 1) * LANES]
        if split:
            hi, lo = _split2(xc)
            outs.append(_dot(hi, gmat) + _dot(lo, gmat))
        else:
            outs.append(_dot(xc.astype(BF16), gmat))
    return outs[0] if len(outs) == 1 else jnp.concatenate(outs, axis=1)


def _softplus(y):
    return jnp.maximum(y, 0.0) + jnp.log(1.0 + jnp.exp(-jnp.abs(y)))


def _rwkv_proj_kernel(x_ref, xprev_ref, mu_ref, wr_ref, wk_ref, wv_ref, wg_ref, wla_ref, wlb_ref,
                      ala_ref, alb_ref, w0_ref, a0_ref, kk_ref, ka_ref, rk_ref, gmat_ref,
                      r_ref, k_ref, v_ref, g_ref, kap_ref, b_ref, lw_ref, bonus_ref):
    x = x_ref[...]
    ts = x.shape[0]
    prev = jnp.where(pl.program_id(1) == 0, 0.0, xprev_ref[7:8, :])
    rowid = lax.broadcasted_iota(jnp.int32, (ts, 1), 0)
    xx = jnp.where(rowid == 0, prev, pltpu.roll(x, 1, 0)) - x

    def mixed(i):
        return (x + xx * mu_ref[i:i + 1, :]).astype(BF16)

    r = _dot(mixed(0), wr_ref[...])
    k = _dot(mixed(2), wk_ref[...])
    v = _dot(mixed(3), wv_ref[...])
    g_ref[...] = _dot(mixed(5), wg_ref[...]).astype(g_ref.dtype)
    lora_w = _dot(jnp.tanh(_dot(mixed(1), wla_ref[...])).astype(BF16), wlb_ref[...])
    lora_a = _dot(_dot(mixed(4), ala_ref[...]).astype(BF16), alb_ref[...])
    w_log = -_softplus(-(w0_ref[...] + lora_w)) - 0.5
    lw_ref[...] = -jnp.exp(w_log)
    a = _sigmoid(a0_ref[...] + lora_a)
    gmat = gmat_ref[...]
    kk = k * kk_ref[...]
    kap = kk * lax.rsqrt(_group_sum(kk * kk, gmat, split=False) + 1e-12)
    k2 = k * (1.0 + (a - 1.0) * ka_ref[...])
    bonus_ref[...] = _group_sum(r * k2 * rk_ref[...], gmat, split=False) * v
    r_ref[...] = r.astype(r_ref.dtype)
    k_ref[...] = k2.astype(k_ref.dtype)
    v_ref[...] = v.astype(v_ref.dtype)
    kap_ref[...] = kap.astype(kap_ref.dtype)
    b_ref[...] = (kap * a).astype(b_ref.dtype)


def _bd(x, left):
    z = jnp.zeros_like(x)
    return jnp.concatenate([jnp.where(left, x, z), jnp.where(left, z, x)], axis=0)


def _unbd(x_bd):
    c = x_bd.shape[0] // 2
    return x_bd[:c] + x_bd[c:]


def _rwkv_chunk_pair(lw, r, k, v, kap, b, tril, masks):
    C = RWKV_CHUNK
    left, strict, lower, same_sub, eye = masks
    p1, p2, p3 = _split3(lw)
    L = _dot(tril, p1) + _dot(tril, p2) + _dot(tril, p3)
    yield None
    Lc = L[C - 1:C, :]
    e_l, e_lx, e_nl, e_r = jnp.exp(L), jnp.exp(L - lw), jnp.exp(-L), jnp.exp(Lc - L)
    at = _bd(-kap * e_lx, left).astype(BF16)
    rt = _bd(r * e_l, left)
    bt = _bd(b * e_nl, left).astype(BF16)
    kt = _bd(k * e_nl, left).astype(BF16)
    bh = _bd(b * e_r, left).astype(BF16)
    kh = _bd(k * e_r, left).astype(BF16)
    vb = _bd(v, left).astype(BF16)

    a1 = _dot_nt(jnp.concatenate([at, rt.astype(BF16)], axis=0), jnp.concatenate([bt, kt], axis=0))
    yield None
    n = jnp.where(strict, a1[:2 * C, :2 * C], 0.0)
    ak = jnp.where(strict, a1[:2 * C, 2 * C:], 0.0).astype(BF16)
    rb = jnp.where(lower, a1[2 * C:, :2 * C], 0.0).astype(BF16)
    rk = jnp.where(lower, a1[2 * C:, 2 * C:], 0.0).astype(BF16)

    nd = jnp.where(same_sub, n, 0.0)
    no = (n - nd).astype(BF16)
    ndb = nd.astype(BF16)
    n2 = _dot(ndb, ndb)
    akv = _dot(ak, vb)
    yield None
    n2b = n2.astype(BF16)
    n4 = _dot(n2b, n2b)
    t01 = _dot((eye + nd).astype(BF16), (eye + n2).astype(BF16))
    yield None
    n4b = n4.astype(BF16)
    n8 = _dot(n4b, n4b)
    yield None
    t23 = _dot((eye + n4).astype(BF16), (eye + n8).astype(BF16))
    yield None
    tdb = _dot(t01.astype(BF16), t23.astype(BF16)).astype(BF16)
    yield None
    x1 = _dot(tdb, no)
    yield None
    x1b = x1.astype(BF16)
    x2 = _dot(x1b, x1b)
    yield None
    tx = _dot((eye + x1).astype(BF16), (eye + x2).astype(BF16))
    yield None
    t = _dot(tx.astype(BF16), tdb).astype(BF16)
    yield None
    pq = _dot(t, jnp.concatenate([at, akv.astype(BF16)], axis=1)).astype(BF16)
    yield None
    z = jnp.concatenate([pq, jnp.concatenate([jnp.zeros_like(vb), vb], axis=1)], axis=0)
    ry = _dot(jnp.concatenate([rb, rk], axis=1), z)
    mg = _dot_tn(z, jnp.concatenate([bh, kh], axis=0))
    yield None
    rp = rt + ry[:, :2 * C]
    yl = ry[:, 2 * C:]
    yield _unbd(rp), _unbd(yl), _unbd(mg[:2 * C]), _unbd(mg[2 * C:]), jnp.exp(Lc)


def _rwkv_chunk_kernel(lw_ref, r_ref, k_ref, v_ref, kap_ref, b_ref, tril_ref,
                       rp_ref, yl_ref, mm_ref, gg_ref, gam_ref, *, pairs):
    C = RWKV_CHUNK
    lane = lax.broadcasted_iota(jnp.int32, (1, LANES), 1)
    left = lane < RWKV_HEAD_DIM
    ri = lax.broadcasted_iota(jnp.int32, (2 * C, 2 * C), 0)
    ci = lax.broadcasted_iota(jnp.int32, (2 * C, 2 * C), 1)
    same_head = (ri // C) == (ci // C)
    strict = same_head & ((ri % C) > (ci % C))
    lower = same_head & ((ri % C) >= (ci % C))
    same_sub = (ri // RWKV_SUB) == (ci // RWKV_SUB)
    eye = jnp.where(ri == ci, 1.0, 0.0).astype(F32)
    masks = (left, strict, lower, same_sub, eye)
    tril = tril_ref[...]
    slices = [slice(p * LANES, (p + 1) * LANES) for p in range(pairs)]
    results = _round_robin([
        _rwkv_chunk_pair(lw_ref[:, sl], r_ref[:, sl].astype(F32), k_ref[:, sl].astype(F32),
                         v_ref[:, sl].astype(F32), kap_ref[:, sl].astype(F32), b_ref[:, sl].astype(F32),
                         tril, masks)
        for sl in slices])
    for sl, (rp, yl, mm, gg, gam) in zip(slices, results):
        rp_ref[:, sl] = rp.astype(rp_ref.dtype)
        yl_ref[:, sl] = yl
        mm_ref[:, sl] = mm.astype(mm_ref.dtype)
        gg_ref[:, sl] = gg
        gam_ref[:, sl] = gam


def _rwkv_seq_kernel(rp_ref, yl_ref, mm_ref, gg_ref, gam_ref, bonus_ref, gmat_ref, gng_ref, gnb_ref,
                     o_ref, s_sc, y_sc, *, pairs, cb):
    C = RWKV_CHUNK
    lane = lax.broadcasted_iota(jnp.int32, (1, LANES), 1)
    left = lane < RWKV_HEAD_DIM

    @pl.when(pl.program_id(2) == 0)
    def _():
        s_sc[...] = jnp.zeros_like(s_sc)

    states = [s_sc[p] for p in range(pairs)]
    for c in range(cb):
        rows = slice(c * C, (c + 1) * C)
        for p in range(pairs):
            sl = slice(p * LANES, (p + 1) * LANES)
            s = states[p]
            sb = s.astype(BF16)
            y_sc[rows, sl] = _dot_nt(rp_ref[rows, sl], sb) + yl_ref[rows, sl]
            mm = _bd(mm_ref[c, :, sl], left)
            gg = _bd(gg_ref[c, :, sl], left)
            states[p] = s * gam_ref[c, :, sl] + _dot(sb, mm) + gg
    for p in range(pairs):
        s_sc[p] = states[p]

    y = y_sc[...]
    gmat = gmat_ref[...]
    inv_n = 1.0 / RWKV_HEAD_DIM
    yc = y - _group_sum(y, gmat) * inv_n
    var = _group_sum(yc * yc, gmat) * inv_n
    yn = yc * lax.rsqrt(var + RWKV_GN_EPS) * gng_ref[...] + gnb_ref[...]
    o_ref[...] = (yn + bonus_ref[...]).astype(o_ref.dtype)


def _rwkv_layer(x, xb, mu, w_in, w0, w_lora_a, w_lora_b, a0, a_lora_a, a_lora_b, k_k, k_a, r_k,
                gn_g, gn_b, w_out, ln_g, ln_b, B, S, ts=512, pairs=8, seq_pairs=4):
    D = D_MODEL
    C = RWKV_CHUNK
    nc = S // C
    ts = _pick_tile(S, ts)
    w_r, w_k, w_v, w_g = [w.astype(BF16) for w in jnp.split(w_in, 4, axis=1)]
    gmat = jnp.asarray(np.kron(np.eye(2, dtype=np.float32), np.ones((RWKV_HEAD_DIM, RWKV_HEAD_DIM), np.float32)), BF16)
    row = lambda a: a.reshape(1, D)
    x3 = x.reshape(B, S, D)
    full = lambda shape: pl.BlockSpec(shape, lambda b, j: (0,) * len(shape))
    tile = lambda: pl.BlockSpec((None, ts, D), lambda b, j: (b, j, 0))
    lr = w_lora_a.shape[1]
    outs = pl.pallas_call(
        _rwkv_proj_kernel,
        grid=(B, S // ts),
        in_specs=[tile(),
                  pl.BlockSpec((None, 8, D), lambda b, j: (b, jnp.maximum(j * (ts // 8) - 1, 0), 0)),
                  full((6, D)), full((D, D)), full((D, D)), full((D, D)), full((D, D)),
                  full((D, lr)), full((lr, D)), full((D, lr)), full((lr, D)),
                  full((1, D)), full((1, D)), full((1, D)), full((1, D)), full((1, D)),
                  full((LANES, LANES))],
        out_specs=[tile() for _ in range(8)],
        out_shape=[jax.ShapeDtypeStruct((B, S, D), dt) for dt in (BF16, BF16, BF16, BF16, BF16, BF16, F32, F32)],
        compiler_params=_cparams("parallel", "arbitrary"),
        name="rwkv_projections",
    )(x3, x3, mu, w_r, w_k, w_v, w_g, w_lora_a.astype(BF16), w_lora_b.astype(BF16),
      a_lora_a.astype(BF16), a_lora_b.astype(BF16), row(w0), row(a0), row(k_k), row(k_a), row(r_k), gmat)
    r, k2, v, g, kap, bvec, lw, bonus = outs

    tril = jnp.tril(jnp.ones((C, C), BF16))
    pw = pairs * LANES
    cblk = lambda: pl.BlockSpec((None, C, pw), lambda b, c, q: (b, c, q))
    sblk = lambda: pl.BlockSpec((None, None, C, pw), lambda b, c, q: (b, c, 0, q))
    rp, yl, mm, gg, gam = pl.pallas_call(
        functools.partial(_rwkv_chunk_kernel, pairs=pairs),
        grid=(B, nc, D // pw),
        in_specs=[cblk() for _ in range(6)] + [pl.BlockSpec((C, C), lambda b, c, q: (0, 0))],
        out_specs=[cblk(), cblk(), sblk(), sblk(),
                   pl.BlockSpec((None, None, 1, pw), lambda b, c, q: (b, c, 0, q))],
        out_shape=[jax.ShapeDtypeStruct((B, S, D), BF16), jax.ShapeDtypeStruct((B, S, D), F32),
                   jax.ShapeDtypeStruct((B, nc, C, D), BF16), jax.ShapeDtypeStruct((B, nc, C, D), F32),
                   jax.ShapeDtypeStruct((B, nc, 1, D), F32)],
        compiler_params=_cparams("parallel", "parallel", "parallel"),
        name="rwkv_chunk_summaries",
    )(lw, r, k2, v, kap, bvec, tril)

    cb = _pick_tile(nc, 8)
    pairs = seq_pairs
    pw = pairs * LANES
    o = pl.pallas_call(
        functools.partial(_rwkv_seq_kernel, pairs=pairs, cb=cb),
        grid=(B, D // pw, nc // cb),
        in_specs=[pl.BlockSpec((None, cb * C, pw), lambda b, q, j: (b, j, q)),
                  pl.BlockSpec((None, cb * C, pw), lambda b, q, j: (b, j, q)),
                  pl.BlockSpec((None, cb, C, pw), lambda b, q, j: (b, j, 0, q)),
                  pl.BlockSpec((None, cb, C, pw), lambda b, q, j: (b, j, 0, q)),
                  pl.BlockSpec((None, cb, 1, pw), lambda b, q, j: (b, j, 0, q)),
                  pl.BlockSpec((None, cb * C, pw), lambda b, q, j: (b, j, q)),
                  pl.BlockSpec((LANES, LANES), lambda b, q, j: (0, 0)),
                  pl.BlockSpec((1, pw), lambda b, q, j: (0, q)),
                  pl.BlockSpec((1, pw), lambda b, q, j: (0, q))],
        out_specs=pl.BlockSpec((None, cb * C, pw), lambda b, q, j: (b, j, q)),
        out_shape=jax.ShapeDtypeStruct((B, S, D), BF16),
        scratch_shapes=[pltpu.VMEM((pairs, 2 * C, LANES), F32), pltpu.VMEM((cb * C, pw), F32)],
        compiler_params=_cparams("parallel", "parallel", "arbitrary"),
        name="rwkv_state_scan",
    )(rp, yl, mm, gg, gam, bonus, gmat, row(gn_g), row(gn_b))
    return _outproj_ln(g.reshape(B * S, D), 0, o.reshape(B * S, D), x, w_out, ln_g, ln_b)


def kernel(x, ln_g, ln_b, fox_w_in, fox_b_f, fox_w_out, dsa_w_in, dsa_kv_norm_g, dsa_w_uk, dsa_w_uv, dsa_w_out, rwkv_mu, rwkv_w_in, rwkv_w0, rwkv_w_lora_a, rwkv_w_lora_b, rwkv_a0, rwkv_a_lora_a, rwkv_a_lora_b, rwkv_k_k, rwkv_k_a, rwkv_r_k, rwkv_gn_g, rwkv_gn_b, rwkv_w_out, ret_w_in, ret_gn_g, ret_w_out):
    B, S, D = x.shape
    h = x.reshape(B * S, D)
    h, hb = _fox_layer(h, h, fox_w_in, fox_b_f, fox_w_out, ln_g[0], ln_b[0], B, S)
    h, hb = _dsa_layer(h, hb, dsa_w_in, dsa_kv_norm_g, dsa_w_uk, dsa_w_uv, dsa_w_out, ln_g[1], ln_b[1], B, S)
    h, hb = _rwkv_layer(h, hb, rwkv_mu, rwkv_w_in, rwkv_w0, rwkv_w_lora_a, rwkv_w_lora_b, rwkv_a0,
                        rwkv_a_lora_a, rwkv_a_lora_b, rwkv_k_k, rwkv_k_a, rwkv_r_k, rwkv_gn_g, rwkv_gn_b,
                        rwkv_w_out, ln_g[2], ln_b[2], B, S)
    h, hb = _ret_layer(h, hb, ret_w_in, ret_gn_g, ret_w_out, ln_g[3], ln_b[3], B, S)
    return h.reshape(B, S, D)
```

```python
import functools
import math

import jax
import jax.numpy as jnp
import numpy as np
from jax import lax
from jax.experimental import pallas as pl
from jax.experimental.pallas import tpu as pltpu

F32 = jnp.float32
BF16 = jnp.bfloat16

D_MODEL = 1024
DEPTH = 4
LN_EPS = 1e-5
RMS_EPS = 1e-6
DN_ALPHA = (2 * DEPTH) ** 0.25
ROPE_THETA = 500000.0

FOX_HEADS = 8
FOX_HEAD_DIM = 128

RET_HEADS = 4
RET_HEAD_DIM = 256
RET_THETA = 10000.0

LANES = 128
VMEM_LIMIT = 48 * 1024 * 1024
NEG_BIG = -1e30
LOG2E = 1.4426950408889634


def _cparams(*sem):
    return pltpu.CompilerParams(dimension_semantics=sem, vmem_limit_bytes=VMEM_LIMIT)


def _dot(a, b):
    return jnp.dot(a, b, preferred_element_type=F32)


def _dot_nt(a, b):
    return lax.dot_general(a, b, (((1,), (1,)), ((), ())), preferred_element_type=F32)


def _dot_tn(a, b):
    return lax.dot_general(a, b, (((0,), (0,)), ((), ())), preferred_element_type=F32)


def _split2(x):
    hi = x.astype(BF16)
    lo = (x - hi.astype(F32)).astype(BF16)
    return hi, lo


def _split3(x):
    p1 = x.astype(BF16)
    r1 = x - p1.astype(F32)
    p2 = r1.astype(BF16)
    p3 = (r1 - p2.astype(F32)).astype(BF16)
    return p1, p2, p3


def _sigmoid(x):
    return 1.0 / (1.0 + jnp.exp(-x))


def _round_robin(gens):
    results = [None] * len(gens)
    live = list(range(len(gens)))
    while live:
        still = []
        for i in live:
            try:
                out = next(gens[i])
            except StopIteration:
                continue
            if out is not None:
                results[i] = out
            still.append(i)
        live = still
    return results


def _pick_tile(n, pref):
    t = min(n, pref)
    while n % t:
        t //= 2
    return t


def _mm_kernel(a_ref, w_ref, o_ref):
    o_ref[...] = _dot(a_ref[...].astype(BF16), w_ref[...]).astype(o_ref.dtype)


def _matmul(a, w, out_dtype=BF16, tm=2048, tn=1024):
    M, K = a.shape
    N = w.shape[1]
    tm = _pick_tile(M, tm)
    if N % tn:
        tn = N
    return pl.pallas_call(
        _mm_kernel,
        grid=(M // tm, N // tn),
        in_specs=[pl.BlockSpec((tm, K), lambda i, j: (i, 0)),
                  pl.BlockSpec((K, tn), lambda i, j: (0, j))],
        out_specs=pl.BlockSpec((tm, tn), lambda i, j: (i, j)),
        out_shape=jax.ShapeDtypeStruct((M, N), out_dtype),
        compiler_params=_cparams("parallel", "arbitrary"),
        name="proj_matmul",
    )(a, w)


def _outproj_ln_kernel(g_ref, o_ref, x_ref, w_ref, lg_ref, lb_ref, xo_ref, xb_ref):
    half_g = g_ref[...] * 0.5
    h = (half_g * o_ref[...]) * (1.0 + jnp.tanh(half_g))
    z = DN_ALPHA * x_ref[...] + _dot(h.astype(BF16), w_ref[...])
    zc = z - jnp.mean(z, axis=-1, keepdims=True)
    var = jnp.mean(zc * zc, axis=-1, keepdims=True)
    out = zc * lax.rsqrt(var + LN_EPS) * lg_ref[...] + lb_ref[...]
    xo_ref[...] = out
    xb_ref[...] = out.astype(BF16)


def _outproj_ln(gate_arr, gate_col, o, x, w_out, ln_g, ln_b, tm=512):
    M, D = x.shape
    tm = _pick_tile(M, tm)
    return pl.pallas_call(
        _outproj_ln_kernel,
        grid=(M // tm,),
        in_specs=[pl.BlockSpec((tm, D), lambda i: (i, gate_col)),
                  pl.BlockSpec((tm, D), lambda i: (i, 0)),
                  pl.BlockSpec((tm, D), lambda i: (i, 0)),
                  pl.BlockSpec((D, D), lambda i: (0, 0)),
                  pl.BlockSpec((1, D), lambda i: (0, 0)),
                  pl.BlockSpec((1, D), lambda i: (0, 0))],
        out_specs=[pl.BlockSpec((tm, D), lambda i: (i, 0)),
                   pl.BlockSpec((tm, D), lambda i: (i, 0))],
        out_shape=[jax.ShapeDtypeStruct((M, D), F32), jax.ShapeDtypeStruct((M, D), BF16)],
        compiler_params=_cparams("parallel"),
        name="outproj_layernorm",
    )(gate_arr, o, x, w_out.astype(BF16), ln_g.reshape(1, D), ln_b.reshape(1, D))


FOX_BIAS_PIECES = 3


def _fox_cum_kernel(x_ref, wh_ref, wl_ref, bf_ref, tril_ref, place_ref, pc_ref, carry_sc):
    @pl.when(pl.program_id(1) == 0)
    def _():
        carry_sc[...] = jnp.zeros_like(carry_sc)

    x_hi, x_lo = _split2(x_ref[...])
    z = _dot(x_hi, wh_ref[...]) + _dot(x_lo, wh_ref[...]) + _dot(x_hi, wl_ref[...]) + bf_ref[...]
    logf = jnp.minimum(z, 0.0) - jnp.log(1.0 + jnp.exp(-jnp.abs(z)))
    p1, p2, p3 = _split3(logf)
    tril = tril_ref[...]
    c = _dot(tril, p1) + _dot(tril, p2) + _dot(tril, p3) + carry_sc[...]
    carry_sc[...] = c[c.shape[0] - 1:, :]
    pieces = _split3(c * (-LOG2E))
    pc_ref[...] = sum(_dot(pieces[p], place_ref[p]) for p in range(FOX_BIAS_PIECES)).astype(pc_ref.dtype)


def _fox_cum(x3, w_f, b_f, ts=512):
    B, S, D = x3.shape
    H = w_f.shape[1]
    ts = _pick_tile(S, ts)
    w_pad = jnp.zeros((D, LANES), F32).at[:, :H].set(w_f)
    w_hi, w_lo = _split2(w_pad)
    b_pad = jnp.zeros((1, LANES), F32).at[0, :H].set(b_f)
    tril = jnp.tril(jnp.ones((ts, ts), BF16))
    place = np.zeros((FOX_BIAS_PIECES, LANES, LANES), np.float32)
    for p in range(FOX_BIAS_PIECES):
        for h in range(H):
            place[p, h, FOX_BIAS_PIECES * h + p] = 1.0
    return pl.pallas_call(
        _fox_cum_kernel,
        grid=(B, S // ts),
        in_specs=[pl.BlockSpec((None, ts, D), lambda b, j: (b, j, 0)),
                  pl.BlockSpec((D, LANES), lambda b, j: (0, 0)),
                  pl.BlockSpec((D, LANES), lambda b, j: (0, 0)),
                  pl.BlockSpec((1, LANES), lambda b, j: (0, 0)),
                  pl.BlockSpec((ts, ts), lambda b, j: (0, 0)),
                  pl.BlockSpec((FOX_BIAS_PIECES, LANES, LANES), lambda b, j: (0, 0, 0))],
        out_specs=pl.BlockSpec((None, ts, LANES), lambda b, j: (b, j, 0)),
        out_shape=jax.ShapeDtypeStruct((B, S, LANES), BF16),
        scratch_shapes=[pltpu.VMEM((1, LANES), F32)],
        compiler_params=_cparams("parallel", "arbitrary"),
        name="fox_decay_cumsum",
    )(x3, w_hi, w_lo, b_pad, tril, jnp.asarray(place, BF16))


ONES_ROWS = 16


def _fox_attn_kernel(q_ref, k_ref, v_ref, pc_ref, o_ref, kaug_sc, vt_sc, m_sc, acc_sc, *, tq, nsub, seq, unroll):
    dh = FOX_HEAD_DIM
    h = pl.program_id(1)
    g = pl.program_id(2)

    @pl.when(g == 0)
    def _():
        kaug_sc[:, :dh] = k_ref[...]
        kaug_sc[:, dh:] = pc_ref[...]
        for c in range(seq // tq):
            rows = slice(c * tq, (c + 1) * tq)
            vt_sc[:dh, rows] = v_ref[rows, :].astype(F32).T.astype(BF16)
        vt_sc[dh:, :] = jnp.ones((ONES_ROWS, seq), BF16)

    lane = lax.broadcasted_iota(jnp.int32, (tq, LANES), 1)
    bias_lanes = (lane >= FOX_BIAS_PIECES * h) & (lane < FOX_BIAS_PIECES * (h + 1))
    ones_h = jnp.where(bias_lanes, 1.0, 0.0).astype(BF16)
    q_aug = [jnp.concatenate([q_ref[a * tq:(a + 1) * tq, :], ones_h], axis=1) for a in range(nsub)]
    m_sc[...] = jnp.full_like(m_sc, NEG_BIG)
    acc_sc[...] = jnp.zeros_like(acc_sc)
    causal = (lax.broadcasted_iota(jnp.int32, (tq, tq), 0) <= lax.broadcasted_iota(jnp.int32, (tq, tq), 1))
    first = g * nsub

    def chain(a, tiles, diag_last):
        offs = [pl.multiple_of(j * tq, tq) for j in tiles]
        scores = []
        for off in offs:
            scores.append(_dot_nt(kaug_sc[pl.ds(off, tq), :], q_aug[a]))
            yield None
        for n, (off, s) in enumerate(zip(offs, scores)):
            if diag_last and n == len(offs) - 1:
                s = jnp.where(causal, s, NEG_BIG)
            m_old = m_sc[a]
            m_new = jnp.maximum(m_old, jnp.max(s, axis=0, keepdims=True))
            alpha = jnp.exp2(m_old - m_new)
            pv = _dot(vt_sc[:, pl.ds(off, tq)], jnp.exp2(s - m_new).astype(BF16))
            yield None
            acc_sc[a] = alpha * acc_sc[a] + pv
            m_sc[a] = m_new
        yield None

    def body(jj, c):
        _round_robin([chain(a, [jj * unroll + u for u in range(unroll)], False) for a in range(nsub)])
        return c

    lax.fori_loop(0, first // unroll, body, 0)
    _round_robin([chain(a, [first + t for t in range(a + 1)], True) for a in range(nsub)])
    for a in range(nsub):
        acc = acc_sc[a]
        o_t = acc[:dh] / acc[dh:dh + 1]
        o_ref[a * tq:(a + 1) * tq, :] = o_t.T.astype(o_ref.dtype)


def _fox_attention(proj3, pieces, tq=256, nsub=8):
    B, S, _ = proj3.shape
    H, dh = FOX_HEADS, FOX_HEAD_DIM
    tq = _pick_tile(S, tq)
    nsub = _pick_tile(S // tq, nsub)
    tg = tq * nsub
    return pl.pallas_call(
        functools.partial(_fox_attn_kernel, tq=tq, nsub=nsub, seq=S, unroll=min(nsub, 4)),
        grid=(B, H, S // tg),
        in_specs=[pl.BlockSpec((None, tg, dh), lambda b, h, g: (b, g, h)),
                  pl.BlockSpec((None, S, dh), lambda b, h, g: (b, 0, H + h)),
                  pl.BlockSpec((None, S, dh), lambda b, h, g: (b, 0, 2 * H + h)),
                  pl.BlockSpec((None, S, LANES), lambda b, h, g: (b, 0, 0))],
        out_specs=pl.BlockSpec((None, tg, dh), lambda b, h, g: (b, g, h)),
        out_shape=jax.ShapeDtypeStruct((B, S, H * dh), BF16),
        scratch_shapes=[pltpu.VMEM((S, dh + LANES), BF16),
                        pltpu.VMEM((dh + ONES_ROWS, S), BF16),
                        pltpu.VMEM((nsub, 1, tq), F32),
                        pltpu.VMEM((nsub, dh + ONES_ROWS, tq), F32)],
        compiler_params=_cparams("parallel", "parallel", "arbitrary"),
        name="fox_attention",
    )(proj3, proj3, proj3, pieces)


def _fox_layer(x, xb, w_in, b_f, w_out, ln_g, ln_b, B, S):
    D = D_MODEL
    H, dh = FOX_HEADS, FOX_HEAD_DIM
    scale = dh ** -0.5 * LOG2E
    w_q, w_k, w_v, w_f, w_g = jnp.split(w_in, [H * dh, 2 * H * dh, 3 * H * dh, 3 * H * dh + H], axis=1)
    w_main = jnp.concatenate([w_q * scale, w_k, w_v, w_g], axis=1).astype(BF16)
    proj = _matmul(xb, w_main)
    pieces = _fox_cum(x.reshape(B, S, D), w_f, b_f)
    o = _fox_attention(proj.reshape(B, S, 4 * D), pieces)
    return _outproj_ln(proj, 3, o.reshape(B * S, D), x, w_out, ln_g, ln_b)


def _ret_kernel(q_ref, k_ref, v_ref, cos_ref, sin_ref, dm_ref, xi_ref, zeta_ref, gc_ref, gn_ref,
                o_ref, r_sc):
    @pl.when(pl.program_id(1) == 0)
    def _():
        r_sc[...] = jnp.zeros_like(r_sc)

    dk = RET_HEAD_DIM
    half = dk // 2
    cos = cos_ref[...]
    sin = sin_ref[...]

    def rope(x):
        x1, x2 = x[:, :half], x[:, half:]
        return jnp.concatenate([x1 * cos - x2 * sin, x2 * cos + x1 * sin], axis=-1)

    def head_chain(h):
        cols = slice(h * dk, (h + 1) * dk)
        q = rope(q_ref[:, cols].astype(F32))
        k = rope(k_ref[:, cols].astype(F32)) * (dk ** -0.5)
        v = v_ref[:, cols]
        qb = q.astype(BF16)
        r_old = r_sc[h]
        scores = _dot_nt(qb, k.astype(BF16))
        cross = _dot(qb, r_old.astype(BF16))
        kz = (k * zeta_ref[h]).astype(BF16)
        r_new = _dot_tn(kz, v)
        yield None
        o = _dot((scores * dm_ref[h]).astype(BF16), v)
        r_sc[h] = r_old * gc_ref[h] + r_new
        yield None
        o = o + cross * xi_ref[h]
        o = o * lax.rsqrt(jnp.mean(o * o, axis=-1, keepdims=True) + RMS_EPS) * gn_ref[:, cols]
        o_ref[:, cols] = o.astype(o_ref.dtype)
        yield None

    _round_robin([head_chain(h) for h in range(RET_HEADS)])


def _ret_layer(x, xb, w_in, gn_g, w_out, ln_g, ln_b, B, S, chunk=512):
    D = D_MODEL
    H, dk = RET_HEADS, RET_HEAD_DIM
    C = _pick_tile(S, chunk)
    proj = _matmul(xb, w_in.astype(BF16))
    inv = 1.0 / (RET_THETA ** (jnp.arange(0, dk, 2, dtype=F32) / dk))
    ang = jnp.arange(S, dtype=F32)[:, None] * inv[None, :]
    cos, sin = jnp.cos(ang), jnp.sin(ang)
    log_g = jnp.log1p(-(2.0 ** (-5.0 - jnp.arange(H, dtype=F32))))
    pos = jnp.arange(C, dtype=F32)
    diff = pos[:, None] - pos[None, :]
    d_mask = jnp.where(diff[None] >= 0, jnp.exp(jnp.maximum(diff, 0.0)[None] * log_g[:, None, None]), 0.0)
    xi = jnp.broadcast_to(jnp.exp((pos[None, :] + 1.0) * log_g[:, None])[:, :, None], (H, C, dk))
    zeta = jnp.broadcast_to(jnp.exp((C - 1.0 - pos[None, :]) * log_g[:, None])[:, :, None], (H, C, dk))
    g_c = jnp.broadcast_to(jnp.exp(C * log_g)[:, None, None], (H, 1, dk))
    p3 = proj.reshape(B, S, 4 * D)
    o = pl.pallas_call(
        _ret_kernel,
        grid=(B, S // C),
        in_specs=[pl.BlockSpec((None, C, D), lambda b, c: (b, c, 0)),
                  pl.BlockSpec((None, C, D), lambda b, c: (b, c, 1)),
                  pl.BlockSpec((None, C, D), lambda b, c: (b, c, 2)),
                  pl.BlockSpec((C, dk // 2), lambda b, c: (c, 0)),
                  pl.BlockSpec((C, dk // 2), lambda b, c: (c, 0)),
                  pl.BlockSpec((H, C, C), lambda b, c: (0, 0, 0)),
                  pl.BlockSpec((H, C, dk), lambda b, c: (0, 0, 0)),
                  pl.BlockSpec((H, C, dk), lambda b, c: (0, 0, 0)),
                  pl.BlockSpec((H, 1, dk), lambda b, c: (0, 0, 0)),
                  pl.BlockSpec((1, D), lambda b, c: (0, 0))],
        out_specs=pl.BlockSpec((None, C, D), lambda b, c: (b, c, 0)),
        out_shape=jax.ShapeDtypeStruct((B, S, D), BF16),
        scratch_shapes=[pltpu.VMEM((H, dk, dk), F32)],
        compiler_params=_cparams("parallel", "arbitrary"),
        name="retnet_retention",
    )(p3, p3, p3, cos, sin, d_mask, xi, zeta, g_c, gn_g.reshape(1, D))
    return _outproj_ln(proj, 3, o.reshape(B * S, D), x, w_out, ln_g, ln_b)


DSA_HEADS = 8
DSA_HEAD_DIM = 128
DSA_ROPE_DIM = 32
DSA_KV_RANK = 128
IDX_HEADS = 8
IDX_DIM = 64
IDX_ROPE_DIM = 16
TOPK_MAX = 256
INT_MIN = -2 ** 31
HALF_MIN = -2 ** 15
HALF_ROWS = 16
DSA_ONES_ROWS = 16


def _rope_perm(width, groups):
    p = np.zeros((width, width), np.float32)
    for start, half in groups:
        for j in range(half):
            p[start + half + j, start + j] = 1.0
            p[start + j, start + half + j] = 1.0
    return p


def _rope_cs(S, width, groups, theta_dims):
    c = jnp.ones((S, width), F32)
    sg = jnp.zeros((S, width), F32)
    pos = jnp.arange(S, dtype=F32)[:, None]
    for (start, half), rot_dim in zip(groups, theta_dims):
        inv = 1.0 / (ROPE_THETA ** (jnp.arange(0, rot_dim, 2, dtype=F32) / rot_dim))
        ang = pos * inv[None, :]
        cos, sin = jnp.cos(ang), jnp.sin(ang)
        c = c.at[:, start:start + half].set(cos).at[:, start + half:start + 2 * half].set(cos)
        sg = sg.at[:, start:start + half].set(-sin).at[:, start + half:start + 2 * half].set(sin)
    return c, sg


def _dsa_prep_kernel(q_ref, qi_ref, ckv_ref, misc_ref, cq_ref, sq_ref, ci_ref, si_ref, cm_ref, sm_ref,
                     pq_ref, pi_ref, pm_ref, selk_ref, selw_ref, wuk_ref, kvg_ref,
                     qf_ref, kvl_ref, kvt_ref, qir_ref, kid_ref, wit_ref):
    H, dh = DSA_HEADS, DSA_HEAD_DIM
    lane = lax.broadcasted_iota(jnp.int32, (1, LANES), 1)
    rope_lanes = lane < DSA_ROPE_DIM
    cq, sq = cq_ref[...], sq_ref[...]
    scale = dh ** -0.5 * LOG2E
    for h in range(H):
        qh = q_ref[:, h * dh:(h + 1) * dh]
        qr = qh.astype(F32) * cq + _dot(qh, pq_ref[...]) * sq
        q_lat = _dot(qr.astype(BF16), wuk_ref[h])
        qf_ref[h, :, :dh] = (q_lat * scale).astype(BF16)
        qf_ref[h, :, dh:] = jnp.where(rope_lanes, qr * scale, 0.0).astype(BF16)
    ci, si = ci_ref[...], si_ref[...]
    for g in range(IDX_HEADS * IDX_DIM // LANES):
        qg = qi_ref[:, g * LANES:(g + 1) * LANES]
        qr = qg.astype(F32) * ci + _dot(qg, pi_ref[...]) * si
        qir_ref[:, g * LANES:(g + 1) * LANES] = (qr * (IDX_DIM ** -0.5)).astype(BF16)
    ckv = ckv_ref[...].astype(F32)
    ckv = ckv * lax.rsqrt(jnp.mean(ckv * ckv, axis=-1, keepdims=True) + RMS_EPS) * kvg_ref[...]
    misc = misc_ref[...]
    mr = (misc.astype(F32) * cm_ref[...] + _dot(misc, pm_ref[...]) * sm_ref[...])
    kvl_ref[:, :DSA_KV_RANK] = ckv.astype(BF16)
    kvl_ref[:, DSA_KV_RANK:] = jnp.where(rope_lanes, mr, 0.0).astype(BF16)
    kvt_ref[:DSA_KV_RANK, :] = ckv.T.astype(BF16)
    kvt_ref[DSA_KV_RANK:, :] = jnp.ones((DSA_ONES_ROWS, ckv.shape[0]), BF16)
    kid_ref[...] = _dot(mr.astype(BF16), selk_ref[...]).astype(BF16)
    wi = _dot(misc, selw_ref[...]) * (IDX_HEADS ** -0.5)
    wit_ref[...] = wi.T[:IDX_HEADS, :]


def _sort_key(x):
    b = pltpu.bitcast(x, jnp.int32)
    return jnp.where(b < 0, b ^ 0x7FFFFFFF, b)


def _dsa_main_kernel(qf_ref, kvl_ref, kvt_ref, qi_ref, kid_ref, wit_ref, wuv_ref, o_ref,
                     key_sc, hi_sc, lo_sc, low_sc, m_sc, acc_sc, *, tq, tk, k_sel, seq):
    H = DSA_HEADS
    i = pl.program_id(1)
    q0 = i * tq
    nj = (q0 + tq + tk - 1) // tk
    qpos = q0 + lax.broadcasted_iota(jnp.int32, (tk, tq), 1)
    kpos0 = lax.broadcasted_iota(jnp.int32, (tk, tq), 0)
    lane = lax.broadcasted_iota(jnp.int32, (1, LANES), 1)
    neg_inf_key = _sort_key(jnp.full((1, 1), -jnp.inf, F32))

    wit = wit_ref[...]
    qi_heads = []
    for h in range(IDX_HEADS):
        g = qi_ref[:, (h // 2) * LANES:(h // 2 + 1) * LANES]
        keep = (lane >= IDX_DIM) if (h % 2) else (lane < IDX_DIM)
        qi_heads.append(jnp.where(keep, g, jnp.zeros_like(g)))

    def score_tile(j, c):
        off = pl.multiple_of(j * tk, tk)
        ki = kid_ref[pl.ds(off, tk), :]
        scores = [_dot_nt(ki, qi_heads[h]) for h in range(IDX_HEADS)]
        isc = jnp.zeros((tk, tq), F32)
        for h in range(IDX_HEADS):
            isc = isc + jnp.maximum(scores[h], 0.0) * wit[h:h + 1, :]
        isc = jnp.where(kpos0 + off <= qpos, isc + 0.0, -jnp.inf)
        key = _sort_key(isc)
        key_sc[pl.ds(off, tk), :] = key
        hi_sc[pl.ds(off, tk), :] = (key >> 16).astype(jnp.int16)
        lo_sc[pl.ds(off, tk), :] = ((key & 0xFFFF) + HALF_MIN).astype(jnp.int16)
        return c

    lax.fori_loop(0, nj, score_tile, 0)

    def count(pred_fn):
        def body(j, acc):
            off = pl.multiple_of(j * tk, tk)
            kt = key_sc[pl.ds(off, tk), :]
            hit = jnp.where(pred_fn(kt, off), 1, 0)
            return acc + jnp.sum(hit.reshape(tk // 8, 8, tq), axis=0)
        acc = lax.fori_loop(0, nj, body, jnp.zeros((8, tq), jnp.int32))
        return jnp.sum(acc, axis=0, keepdims=True)

    rows16 = tk // HALF_ROWS

    def count16(ref, pred_fn):
        def body(j, acc):
            off = pl.multiple_of(j * tk, tk)
            hit = jnp.where(pred_fn(ref[pl.ds(off, tk), :].reshape(rows16, HALF_ROWS, tq)),
                            jnp.int16(1), jnp.int16(0))
            for r in range(rows16):
                acc = acc + hit[r]
            return acc
        acc = lax.fori_loop(0, nj, body, jnp.zeros((HALF_ROWS, tq), jnp.int16))
        return jnp.sum(acc.astype(jnp.int32), axis=0, keepdims=True)

    def as_half(v):
        return jnp.broadcast_to(v.astype(jnp.int16), (HALF_ROWS, tq))[None]

    def search16(ref, base0, cnt0, bits, want):
        def bit_step(t, carry):
            base, cnt_b = carry
            cand = base + lax.shift_left(jnp.int32(1), bits - 1 - t)
            cand16 = as_half(cand)
            c = count16(ref, lambda kt: kt >= cand16)
            ok = c >= want
            return jnp.where(ok, cand, base), jnp.where(ok, c, cnt_b)
        return lax.fori_loop(0, bits, bit_step, (base0, cnt0))

    zero16 = as_half(jnp.zeros((1, tq), jnp.int32))
    cnt_pos = count16(hi_sc, lambda kt: kt >= zero16)
    nonneg = cnt_pos >= k_sel
    t1, ge_hi = search16(hi_sc, jnp.where(nonneg, 0, HALF_MIN), jnp.where(nonneg, cnt_pos, nj * tk), 15, k_sel)
    t1_16 = as_half(t1)
    above = count16(hi_sc, lambda kt: kt > t1_16)

    def build_low(j, c):
        rows = pl.ds(pl.multiple_of(j * tk, tk), tk)
        hi = hi_sc[rows, :].reshape(rows16, HALF_ROWS, tq)
        lo = lo_sc[rows, :].reshape(rows16, HALF_ROWS, tq)
        low_sc[rows, :] = jnp.where(hi == t1_16, lo, jnp.int16(HALF_MIN)).reshape(tk, tq)
        return c

    lax.fori_loop(0, nj, build_low, 0)
    t2, ge_low = search16(low_sc, jnp.full((1, tq), HALF_MIN, jnp.int32), ge_hi - above, 16, k_sel - above)
    t2_16 = as_half(t2)
    thr = lax.shift_left(t1, 16) | (t2 - HALF_MIN)
    n_ge = above + ge_low
    n_gt = above + count16(low_sc, lambda kt: kt > t2_16)
    need = k_sel - n_gt
    excess = ((n_ge - n_gt) > need) & (thr > neg_inf_key)
    any_excess = jnp.max(jnp.where(excess, 1, 0)) > 0

    def tie_cut():
        def step(t, lo):
            cand = lo + lax.shift_left(jnp.int32(1), int(math.log2(seq)) - t)
            c = count(lambda kt, off: (kt == thr) & (kpos0 + off < cand))
            return jnp.where(c < need, cand, lo)
        lo = lax.fori_loop(0, int(math.log2(seq)) + 1, step, jnp.zeros((1, tq), jnp.int32))
        return jnp.where(excess, lo, seq)

    cut = lax.cond(any_excess, tie_cut, lambda: jnp.full((1, tq), seq, jnp.int32))

    m_sc[...] = jnp.full_like(m_sc, NEG_BIG)
    acc_sc[...] = jnp.zeros_like(acc_sc)

    def attn_tile(j, c):
        off = pl.multiple_of(j * tk, tk)
        kt = key_sc[pl.ds(off, tk), :]
        kpos = kpos0 + off
        bias = jnp.where(kt > thr, 0.0, jnp.where(kt == thr, jnp.where(kpos <= cut, 0.0, NEG_BIG), NEG_BIG))
        bias = jnp.where(kpos <= qpos, bias, NEG_BIG)
        kv = kvl_ref[pl.ds(off, tk), :]
        kvt = kvt_ref[:, pl.ds(off, tk)]

        def head_step(h):
            s = _dot_nt(kv, qf_ref[h]) + bias
            yield None
            m_old = m_sc[h]
            m_new = jnp.maximum(m_old, jnp.max(s, axis=0, keepdims=True))
            alpha = jnp.exp2(m_old - m_new)
            pv = _dot(kvt, jnp.exp2(s - m_new).astype(BF16))
            yield None
            acc_sc[h] = alpha * acc_sc[h] + pv
            m_sc[h] = m_new
            yield None

        _round_robin([head_step(h) for h in range(H)])
        return c

    lax.fori_loop(0, nj, attn_tile, 0)
    for h in range(H):
        acc = acc_sc[h]
        o_lat_t = (acc[:DSA_KV_RANK] / acc[DSA_KV_RANK:DSA_KV_RANK + 1]).astype(BF16)
        o_ref[:, h * DSA_HEAD_DIM:(h + 1) * DSA_HEAD_DIM] = _dot_tn(o_lat_t, wuv_ref[h]).astype(o_ref.dtype)


def _dsa_layer(x, xb, w_in, kv_norm_g, w_uk, w_uv, w_out, ln_g, ln_b, B, S, tq=512, tk=512):
    D = D_MODEL
    H, dh, dr, dc = DSA_HEADS, DSA_HEAD_DIM, DSA_ROPE_DIM, DSA_KV_RANK
    HI, di = IDX_HEADS, IDX_DIM
    w_q, w_ckv, w_kr, w_qi, w_ki, w_wi, w_g = jnp.split(
        w_in, np.cumsum([H * dh, dc, dr, HI * di, di, HI]).tolist(), axis=1)
    w_misc = jnp.concatenate([w_kr, w_ki, w_wi, jnp.zeros((D, LANES - dr - di - HI), F32)], axis=1)
    w_main = jnp.concatenate([w_q, w_g, w_qi, w_ckv, w_misc], axis=1).astype(BF16)
    n_main = w_main.shape[1]
    proj = _matmul(xb, w_main, tn=n_main // 2)
    c_q, c_qi, c_ckv, c_misc = 0, 2 * D // LANES, (2 * D + HI * di) // LANES, (2 * D + HI * di + dc) // LANES

    q_groups = [(0, dr // 2)]
    i_groups = [(0, IDX_ROPE_DIM // 2), (di, IDX_ROPE_DIM // 2)]
    m_groups = [(0, dr // 2), (dr, IDX_ROPE_DIM // 2)]
    cq, sq = _rope_cs(S, LANES, q_groups, [dr])
    ci, si = _rope_cs(S, LANES, i_groups, [IDX_ROPE_DIM, IDX_ROPE_DIM])
    cm, sm = _rope_cs(S, LANES, m_groups, [dr, IDX_ROPE_DIM])
    pq = jnp.asarray(_rope_perm(LANES, q_groups), BF16)
    pi = jnp.asarray(_rope_perm(LANES, i_groups), BF16)
    pm = jnp.asarray(_rope_perm(LANES, m_groups), BF16)
    selk = np.zeros((LANES, LANES), np.float32)
    for j in range(di):
        selk[dr + j, j] = 1.0
        selk[dr + j, di + j] = 1.0
    selw = np.zeros((LANES, LANES), np.float32)
    for j in range(HI):
        selw[dr + di + j, j] = 1.0
    wuk = jnp.concatenate([jnp.zeros((H, dr, dc), F32), jnp.transpose(w_uk, (0, 2, 1))], axis=1).astype(BF16)

    ts = _pick_tile(S, 256)
    p3 = proj.reshape(B, S, n_main)
    tab = lambda: pl.BlockSpec((ts, LANES), lambda b, j: (j, 0))
    mat = lambda: pl.BlockSpec((LANES, LANES), lambda b, j: (0, 0))
    qf, kvl, kvt, qir, kid, wit = pl.pallas_call(
        _dsa_prep_kernel,
        grid=(B, S // ts),
        in_specs=[pl.BlockSpec((None, ts, H * dh), lambda b, j: (b, j, 0)),
                  pl.BlockSpec((None, ts, HI * di), lambda b, j: (b, j, c_qi * LANES // (HI * di))),
                  pl.BlockSpec((None, ts, dc), lambda b, j: (b, j, c_ckv)),
                  pl.BlockSpec((None, ts, LANES), lambda b, j: (b, j, c_misc)),
                  tab(), tab(), tab(), tab(), tab(), tab(),
                  mat(), mat(), mat(), mat(), mat(),
                  pl.BlockSpec((H, LANES, dc), lambda b, j: (0, 0, 0)),
                  pl.BlockSpec((1, dc), lambda b, j: (0, 0))],
        out_specs=[pl.BlockSpec((None, H, ts, 2 * dc), lambda b, j: (b, 0, j, 0)),
                   pl.BlockSpec((None, ts, 2 * dc), lambda b, j: (b, j, 0)),
                   pl.BlockSpec((None, dc + DSA_ONES_ROWS, ts), lambda b, j: (b, 0, j)),
                   pl.BlockSpec((None, ts, HI * di), lambda b, j: (b, j, 0)),
                   pl.BlockSpec((None, ts, LANES), lambda b, j: (b, j, 0)),
                   pl.BlockSpec((None, HI, ts), lambda b, j: (b, 0, j))],
        out_shape=[jax.ShapeDtypeStruct((B, H, S, 2 * dc), BF16),
                   jax.ShapeDtypeStruct((B, S, 2 * dc), BF16),
                   jax.ShapeDtypeStruct((B, dc + DSA_ONES_ROWS, S), BF16),
                   jax.ShapeDtypeStruct((B, S, HI * di), BF16),
                   jax.ShapeDtypeStruct((B, S, LANES), BF16),
                   jax.ShapeDtypeStruct((B, HI, S), F32)],
        compiler_params=_cparams("parallel", "parallel"),
        name="dsa_prep",
    )(p3, p3, p3, p3, cq, sq, ci, si, cm, sm, pq, pi, pm,
      jnp.asarray(selk, BF16), jnp.asarray(selw, BF16), wuk, kv_norm_g.reshape(1, dc))

    tq = _pick_tile(S, tq)
    tk = _pick_tile(S, tk)
    k_sel = min(TOPK_MAX, S // 4)
    o = pl.pallas_call(
        functools.partial(_dsa_main_kernel, tq=tq, tk=tk, k_sel=k_sel, seq=S),
        grid=(B, S // tq),
        in_specs=[pl.BlockSpec((None, H, tq, 2 * dc), lambda b, i: (b, 0, i, 0)),
                  pl.BlockSpec((None, S, 2 * dc), lambda b, i: (b, 0, 0)),
                  pl.BlockSpec((None, dc + DSA_ONES_ROWS, S), lambda b, i: (b, 0, 0)),
                  pl.BlockSpec((None, tq, HI * di), lambda b, i: (b, i, 0)),
                  pl.BlockSpec((None, S, LANES), lambda b, i: (b, 0, 0)),
                  pl.BlockSpec((None, HI, tq), lambda b, i: (b, 0, i)),
                  pl.BlockSpec((H, dc, dh), lambda b, i: (0, 0, 0))],
        out_specs=pl.BlockSpec((None, tq, H * dh), lambda b, i: (b, i, 0)),
        out_shape=jax.ShapeDtypeStruct((B, S, H * dh), BF16),
        scratch_shapes=[pltpu.VMEM((S, tq), jnp.int32),
                        pltpu.VMEM((S, tq), jnp.int16), pltpu.VMEM((S, tq), jnp.int16),
                        pltpu.VMEM((S, tq), jnp.int16),
                        pltpu.VMEM((H, 1, tq), F32),
                        pltpu.VMEM((H, dc + DSA_ONES_ROWS, tq), F32)],
        compiler_params=_cparams("parallel", "arbitrary"),
        name="dsa_select_attention",
    )(qf, kvl, kvt, qir, kid, wit, w_uv.astype(BF16))
    return _outproj_ln(proj, 1, o.reshape(B * S, D), x, w_out, ln_g, ln_b)


RWKV_HEADS = 16
RWKV_HEAD_DIM = 64
RWKV_GN_EPS = 64e-5
RWKV_CHUNK = 64
RWKV_SUB = 16


def _group_sum(x, gmat, split=True):
    outs = []
    for c in range(x.shape[1] // LANES):
        xc = x[:, c * LANES:(c + 1) * LANES]
        if split:
            hi, lo = _split2(xc)
            outs.append(_dot(hi, gmat) + _dot(lo, gmat))
        else:
            outs.append(_dot(xc.astype(BF16), gmat))
    return outs[0] if len(outs) == 1 else jnp.concatenate(outs, axis=1)


def _softplus(y):
    return jnp.maximum(y, 0.0) + jnp.log(1.0 + jnp.exp(-jnp.abs(y)))


def _rwkv_proj_kernel(x_ref, xprev_ref, mu_ref, wr_ref, wk_ref, wv_ref, wg_ref, wla_ref, wlb_ref,
                      ala_ref, alb_ref, w0_ref, a0_ref, kk_ref, ka_ref, rk_ref, gmat_ref,
                      r_ref, k_ref, v_ref, g_ref, kap_ref, b_ref, lw_ref, bonus_ref):
    x = x_ref[...]
    ts = x.shape[0]
    prev = jnp.where(pl.program_id(1) == 0, 0.0, xprev_ref[7:8, :])
    rowid = lax.broadcasted_iota(jnp.int32, (ts, 1), 0)
    xx = jnp.where(rowid == 0, prev, pltpu.roll(x, 1, 0)) - x

    def mixed(i):
        return (x + xx * mu_ref[i:i + 1, :]).astype(BF16)

    r = _dot(mixed(0), wr_ref[...])
    k = _dot(mixed(2), wk_ref[...])
    v = _dot(mixed(3), wv_ref[...])
    g_ref[...] = _dot(mixed(5), wg_ref[...]).astype(g_ref.dtype)
    lora_w = _dot(jnp.tanh(_dot(mixed(1), wla_ref[...])).astype(BF16), wlb_ref[...])
    lora_a = _dot(_dot(mixed(4), ala_ref[...]).astype(BF16), alb_ref[...])
    w_log = -_softplus(-(w0_ref[...] + lora_w)) - 0.5
    lw_ref[...] = -jnp.exp(w_log)
    a = _sigmoid(a0_ref[...] + lora_a)
    gmat = gmat_ref[...]
    kk = k * kk_ref[...]
    kap = kk * lax.rsqrt(_group_sum(kk * kk, gmat, split=False) + 1e-12)
    k2 = k * (1.0 + (a - 1.0) * ka_ref[...])
    bonus_ref[...] = _group_sum(r * k2 * rk_ref[...], gmat, split=False) * v
    r_ref[...] = r.astype(r_ref.dtype)
    k_ref[...] = k2.astype(k_ref.dtype)
    v_ref[...] = v.astype(v_ref.dtype)
    kap_ref[...] = kap.astype(kap_ref.dtype)
    b_ref[...] = (kap * a).astype(b_ref.dtype)


def _bd(x, left):
    z = jnp.zeros_like(x)
    return jnp.concatenate([jnp.where(left, x, z), jnp.where(left, z, x)], axis=0)


def _unbd(x_bd):
    c = x_bd.shape[0] // 2
    return x_bd[:c] + x_bd[c:]


def _rwkv_chunk_pair(lw, r, k, v, kap, b, tril, masks):
    C = RWKV_CHUNK
    left, strict, lower, same_sub, eye = masks
    p1, p2, p3 = _split3(lw)
    L = _dot(tril, p1) + _dot(tril, p2) + _dot(tril, p3)
    yield None
    Lc = L[C - 1:C, :]
    e_l, e_lx, e_nl, e_r = jnp.exp(L), jnp.exp(L - lw), jnp.exp(-L), jnp.exp(Lc - L)
    at = _bd(-kap * e_lx, left).astype(BF16)
    rt = _bd(r * e_l, left)
    bt = _bd(b * e_nl, left).astype(BF16)
    kt = _bd(k * e_nl, left).astype(BF16)
    bh = _bd(b * e_r, left).astype(BF16)
    kh = _bd(k * e_r, left).astype(BF16)
    vb = _bd(v, left).astype(BF16)

    a1 = _dot_nt(jnp.concatenate([at, rt.astype(BF16)], axis=0), jnp.concatenate([bt, kt], axis=0))
    yield None
    n = jnp.where(strict, a1[:2 * C, :2 * C], 0.0)
    ak = jnp.where(strict, a1[:2 * C, 2 * C:], 0.0).astype(BF16)
    rb = jnp.where(lower, a1[2 * C:, :2 * C], 0.0).astype(BF16)
    rk = jnp.where(lower, a1[2 * C:, 2 * C:], 0.0).astype(BF16)

    nd = jnp.where(same_sub, n, 0.0)
    no = (n - nd).astype(BF16)
    ndb = nd.astype(BF16)
    n2 = _dot(ndb, ndb)
    akv = _dot(ak, vb)
    yield None
    n2b = n2.astype(BF16)
    n4 = _dot(n2b, n2b)
    t01 = _dot((eye + nd).astype(BF16), (eye + n2).astype(BF16))
    yield None
    n4b = n4.astype(BF16)
    n8 = _dot(n4b, n4b)
    yield None
    t23 = _dot((eye + n4).astype(BF16), (eye + n8).astype(BF16))
    yield None
    tdb = _dot(t01.astype(BF16), t23.astype(BF16)).astype(BF16)
    yield None
    x1 = _dot(tdb, no)
    yield None
    x1b = x1.astype(BF16)
    x2 = _dot(x1b, x1b)
    yield None
    tx = _dot((eye + x1).astype(BF16), (eye + x2).astype(BF16))
    yield None
    t = _dot(tx.astype(BF16), tdb).astype(BF16)
    yield None
    pq = _dot(t, jnp.concatenate([at, akv.astype(BF16)], axis=1)).astype(BF16)
    yield None
    z = jnp.concatenate([pq, jnp.concatenate([jnp.zeros_like(vb), vb], axis=1)], axis=0)
    ry = _dot(jnp.concatenate([rb, rk], axis=1), z)
    mg = _dot_tn(z, jnp.concatenate([bh, kh], axis=0))
    yield None
    rp = rt + ry[:, :2 * C]
    yl = ry[:, 2 * C:]
    yield _unbd(rp), _unbd(yl), _unbd(mg[:2 * C]), _unbd(mg[2 * C:]), jnp.exp(Lc)


def _rwkv_chunk_kernel(lw_ref, r_ref, k_ref, v_ref, kap_ref, b_ref, tril_ref,
                       rp_ref, yl_ref, mm_ref, gg_ref, gam_ref, *, pairs):
    C = RWKV_CHUNK
    lane = lax.broadcasted_iota(jnp.int32, (1, LANES), 1)
    left = lane < RWKV_HEAD_DIM
    ri = lax.broadcasted_iota(jnp.int32, (2 * C, 2 * C), 0)
    ci = lax.broadcasted_iota(jnp.int32, (2 * C, 2 * C), 1)
    same_head = (ri // C) == (ci // C)
    strict = same_head & ((ri % C) > (ci % C))
    lower = same_head & ((ri % C) >= (ci % C))
    same_sub = (ri // RWKV_SUB) == (ci // RWKV_SUB)
    eye = jnp.where(ri == ci, 1.0, 0.0).astype(F32)
    masks = (left, strict, lower, same_sub, eye)
    tril = tril_ref[...]
    slices = [slice(p * LANES, (p + 1) * LANES) for p in range(pairs)]
    results = _round_robin([
        _rwkv_chunk_pair(lw_ref[:, sl], r_ref[:, sl].astype(F32), k_ref[:, sl].astype(F32),
                         v_ref[:, sl].astype(F32), kap_ref[:, sl].astype(F32), b_ref[:, sl].astype(F32),
                         tril, masks)
        for sl in slices])
    for sl, (rp, yl, mm, gg, gam) in zip(slices, results):
        rp_ref[:, sl] = rp.astype(rp_ref.dtype)
        yl_ref[:, sl] = yl
        mm_ref[:, sl] = mm.astype(mm_ref.dtype)
        gg_ref[:, sl] = gg
        gam_ref[:, sl] = gam


def _rwkv_seq_kernel(rp_ref, yl_ref, mm_ref, gg_ref, gam_ref, bonus_ref, gmat_ref, gng_ref, gnb_ref,
                     o_ref, s_sc, y_sc, *, pairs, cb):
    C = RWKV_CHUNK
    lane = lax.broadcasted_iota(jnp.int32, (1, LANES), 1)
    left = lane < RWKV_HEAD_DIM

    @pl.when(pl.program_id(2) == 0)
    def _():
        s_sc[...] = jnp.zeros_like(s_sc)

    states = [s_sc[p] for p in range(pairs)]
    for c in range(cb):
        rows = slice(c * C, (c + 1) * C)
        for p in range(pairs):
            sl = slice(p * LANES, (p + 1) * LANES)
            s = states[p]
            sb = s.astype(BF16)
            y_sc[rows, sl] = _dot_nt(rp_ref[rows, sl], sb) + yl_ref[rows, sl]
            mm = _bd(mm_ref[c, :, sl], left)
            gg = _bd(gg_ref[c, :, sl], left)
            states[p] = s * gam_ref[c, :, sl] + _dot(sb, mm) + gg
    for p in range(pairs):
        s_sc[p] = states[p]

    y = y_sc[...]
    gmat = gmat_ref[...]
    inv_n = 1.0 / RWKV_HEAD_DIM
    yc = y - _group_sum(y, gmat) * inv_n
    var = _group_sum(yc * yc, gmat) * inv_n
    yn = yc * lax.rsqrt(var + RWKV_GN_EPS) * gng_ref[...] + gnb_ref[...]
    o_ref[...] = (yn + bonus_ref[...]).astype(o_ref.dtype)


def _rwkv_layer(x, xb, mu, w_in, w0, w_lora_a, w_lora_b, a0, a_lora_a, a_lora_b, k_k, k_a, r_k,
                gn_g, gn_b, w_out, ln_g, ln_b, B, S, ts=512, pairs=8, seq_pairs=4):
    D = D_MODEL
    C = RWKV_CHUNK
    nc = S // C
    ts = _pick_tile(S, ts)
    w_r, w_k, w_v, w_g = [w.astype(BF16) for w in jnp.split(w_in, 4, axis=1)]
    gmat = jnp.asarray(np.kron(np.eye(2, dtype=np.float32), np.ones((RWKV_HEAD_DIM, RWKV_HEAD_DIM), np.float32)), BF16)
    row = lambda a: a.reshape(1, D)
    x3 = x.reshape(B, S, D)
    full = lambda shape: pl.BlockSpec(shape, lambda b, j: (0,) * len(shape))
    tile = lambda: pl.BlockSpec((None, ts, D), lambda b, j: (b, j, 0))
    lr = w_lora_a.shape[1]
    outs = pl.pallas_call(
        _rwkv_proj_kernel,
        grid=(B, S // ts),
        in_specs=[tile(),
                  pl.BlockSpec((None, 8, D), lambda b, j: (b, jnp.maximum(j * (ts // 8) - 1, 0), 0)),
                  full((6, D)), full((D, D)), full((D, D)), full((D, D)), full((D, D)),
                  full((D, lr)), full((lr, D)), full((D, lr)), full((lr, D)),
                  full((1, D)), full((1, D)), full((1, D)), full((1, D)), full((1, D)),
                  full((LANES, LANES))],
        out_specs=[tile() for _ in range(8)],
        out_shape=[jax.ShapeDtypeStruct((B, S, D), dt) for dt in (BF16, BF16, BF16, BF16, BF16, BF16, F32, F32)],
        compiler_params=_cparams("parallel", "arbitrary"),
        name="rwkv_projections",
    )(x3, x3, mu, w_r, w_k, w_v, w_g, w_lora_a.astype(BF16), w_lora_b.astype(BF16),
      a_lora_a.astype(BF16), a_lora_b.astype(BF16), row(w0), row(a0), row(k_k), row(k_a), row(r_k), gmat)
    r, k2, v, g, kap, bvec, lw, bonus = outs

    tril = jnp.tril(jnp.ones((C, C), BF16))
    pw = pairs * LANES
    cblk = lambda: pl.BlockSpec((None, C, pw), lambda b, c, q: (b, c, q))
    sblk = lambda: pl.BlockSpec((None, None, C, pw), lambda b, c, q: (b, c, 0, q))
    rp, yl, mm, gg, gam = pl.pallas_call(
        functools.partial(_rwkv_chunk_kernel, pairs=pairs),
        grid=(B, nc, D // pw),
        in_specs=[cblk() for _ in range(6)] + [pl.BlockSpec((C, C), lambda b, c, q: (0, 0))],
        out_specs=[cblk(), cblk(), sblk(), sblk(),
                   pl.BlockSpec((None, None, 1, pw), lambda b, c, q: (b, c, 0, q))],
        out_shape=[jax.ShapeDtypeStruct((B, S, D), BF16), jax.ShapeDtypeStruct((B, S, D), F32),
                   jax.ShapeDtypeStruct((B, nc, C, D), BF16), jax.ShapeDtypeStruct((B, nc, C, D), F32),
                   jax.ShapeDtypeStruct((B, nc, 1, D), F32)],
        compiler_params=_cparams("parallel", "parallel", "parallel"),
        name="rwkv_chunk_summaries",
    )(lw, r, k2, v, kap, bvec, tril)

    cb = _pick_tile(nc, 8)
    pairs = seq_pairs
    pw = pairs * LANES
    o = pl.pallas_call(
        functools.partial(_rwkv_seq_kernel, pairs=pairs, cb=cb),
        grid=(B, D // pw, nc // cb),
        in_specs=[pl.BlockSpec((None, cb * C, pw), lambda b, q, j: (b, j, q)),
                  pl.BlockSpec((None, cb * C, pw), lambda b, q, j: (b, j, q)),
                  pl.BlockSpec((None, cb, C, pw), lambda b, q, j: (b, j, 0, q)),
                  pl.BlockSpec((None, cb, C, pw), lambda b, q, j: (b, j, 0, q)),
                  pl.BlockSpec((None, cb, 1, pw), lambda b, q, j: (b, j, 0, q)),
                  pl.BlockSpec((None, cb * C, pw), lambda b, q, j: (b, j, q)),
                  pl.BlockSpec((LANES, LANES), lambda b, q, j: (0, 0)),
                  pl.BlockSpec((1, pw), lambda b, q, j: (0, q)),
                  pl.BlockSpec((1, pw), lambda b, q, j: (0, q))],
        out_specs=pl.BlockSpec((None, cb * C, pw), lambda b, q, j: (b, j, q)),
        out_shape=jax.ShapeDtypeStruct((B, S, D), BF16),
        scratch_shapes=[pltpu.VMEM((pairs, 2 * C, LANES), F32), pltpu.VMEM((cb * C, pw), F32)],
        compiler_params=_cparams("parallel", "parallel", "arbitrary"),
        name="rwkv_state_scan",
    )(rp, yl, mm, gg, gam, bonus, gmat, row(gn_g), row(gn_b))
    return _outproj_ln(g.reshape(B * S, D), 0, o.reshape(B * S, D), x, w_out, ln_g, ln_b)


def kernel(x, ln_g, ln_b, fox_w_in, fox_b_f, fox_w_out, dsa_w_in, dsa_kv_norm_g, dsa_w_uk, dsa_w_uv, dsa_w_out, rwkv_mu, rwkv_w_in, rwkv_w0, rwkv_w_lora_a, rwkv_w_lora_b, rwkv_a0, rwkv_a_lora_a, rwkv_a_lora_b, rwkv_k_k, rwkv_k_a, rwkv_r_k, rwkv_gn_g, rwkv_gn_b, rwkv_w_out, ret_w_in, ret_gn_g, ret_w_out):
    B, S, D = x.shape
    h = x.reshape(B * S, D)
    h, hb = _fox_layer(h, h, fox_w_in, fox_b_f, fox_w_out, ln_g[0], ln_b[0], B, S)
    h, hb = _dsa_layer(h, hb, dsa_w_in, dsa_kv_norm_g, dsa_w_uk, dsa_w_uv, dsa_w_out, ln_g[1], ln_b[1], B, S)
    h, hb = _rwkv_layer(h, hb, rwkv_mu, rwkv_w_in, rwkv_w0, rwkv_w_lora_a, rwkv_w_lora_b, rwkv_a0,
                        rwkv_a_lora_a, rwkv_a_lora_b, rwkv_k_k, rwkv_k_a, rwkv_r_k, rwkv_gn_g, rwkv_gn_b,
                        rwkv_w_out, ln_g[2], ln_b[2], B, S)
    h, hb = _ret_layer(h, hb, ret_w_in, ret_gn_g, ret_w_out, ln_g[3], ln_b[3], B, S)
    return h.reshape(B, S, D)
```

```python
import functools
import math

import jax
import jax.numpy as jnp
import numpy as np
from jax import lax
from jax.experimental import pallas as pl
from jax.experimental.pallas import tpu as pltpu

F32 = jnp.float32
BF16 = jnp.bfloat16

D_MODEL = 1024
DEPTH = 4
LN_EPS = 1e-5
RMS_EPS = 1e-6
DN_ALPHA = (2 * DEPTH) ** 0.25
ROPE_THETA = 500000.0

FOX_HEADS = 8
FOX_HEAD_DIM = 128

RET_HEADS = 4
RET_HEAD_DIM = 256
RET_THETA = 10000.0

LANES = 128
VMEM_LIMIT = 48 * 1024 * 1024
NEG_BIG = -1e30
LOG2E = 1.4426950408889634


def _cparams(*sem):
    return pltpu.CompilerParams(dimension_semantics=sem, vmem_limit_bytes=VMEM_LIMIT)


def _dot(a, b):
    return jnp.dot(a, b, preferred_element_type=F32)


def _dot_nt(a, b):
    return lax.dot_general(a, b, (((1,), (1,)), ((), ())), preferred_element_type=F32)


def _dot_tn(a, b):
    return lax.dot_general(a, b, (((0,), (0,)), ((), ())), preferred_element_type=F32)


def _split2(x):
    hi = x.astype(BF16)
    lo = (x - hi.astype(F32)).astype(BF16)
    return hi, lo


def _split3(x):
    p1 = x.astype(BF16)
    r1 = x - p1.astype(F32)
    p2 = r1.astype(BF16)
    p3 = (r1 - p2.astype(F32)).astype(BF16)
    return p1, p2, p3


def _sigmoid(x):
    return 1.0 / (1.0 + jnp.exp(-x))


def _round_robin(gens):
    results = [None] * len(gens)
    live = list(range(len(gens)))
    while live:
        still = []
        for i in live:
            try:
                out = next(gens[i])
            except StopIteration:
                continue
            if out is not None:
                results[i] = out
            still.append(i)
        live = still
    return results


def _pick_tile(n, pref):
    t = min(n, pref)
    while n % t:
        t //= 2
    return t


def _mm_kernel(a_ref, w_ref, o_ref):
    o_ref[...] = _dot(a_ref[...].astype(BF16), w_ref[...]).astype(o_ref.dtype)


def _matmul(a, w, out_dtype=BF16, tm=2048, tn=1024):
    M, K = a.shape
    N = w.shape[1]
    tm = _pick_tile(M, tm)
    if N % tn:
        tn = N
    return pl.pallas_call(
        _mm_kernel,
        grid=(M // tm, N // tn),
        in_specs=[pl.BlockSpec((tm, K), lambda i, j: (i, 0)),
                  pl.BlockSpec((K, tn), lambda i, j: (0, j))],
        out_specs=pl.BlockSpec((tm, tn), lambda i, j: (i, j)),
        out_shape=jax.ShapeDtypeStruct((M, N), out_dtype),
        compiler_params=_cparams("parallel", "arbitrary"),
        name="proj_matmul",
    )(a, w)


def _outproj_ln_kernel(g_ref, o_ref, x_ref, w_ref, lg_ref, lb_ref, xo_ref, xb_ref):
    half_g = g_ref[...] * 0.5
    h = (half_g * o_ref[...]) * (1.0 + jnp.tanh(half_g))
    z = x_ref[...] + _dot(h.astype(BF16), w_ref[...])
    zc = z - jnp.mean(z, axis=-1, keepdims=True)
    var = jnp.mean(zc * zc, axis=-1, keepdims=True)
    out = zc * lax.rsqrt(var + LN_EPS / DN_ALPHA ** 2) * lg_ref[...] + lb_ref[...]
    xo_ref[...] = out
    xb_ref[...] = out.astype(BF16)


def _outproj_ln(gate_arr, gate_col, o, x, w_out, ln_g, ln_b, tm=512):
    M, D = x.shape
    tm = _pick_tile(M, tm)
    return pl.pallas_call(
        _outproj_ln_kernel,
        grid=(M // tm,),
        in_specs=[pl.BlockSpec((tm, D), lambda i: (i, gate_col)),
                  pl.BlockSpec((tm, D), lambda i: (i, 0)),
                  pl.BlockSpec((tm, D), lambda i: (i, 0)),
                  pl.BlockSpec((D, D), lambda i: (0, 0)),
                  pl.BlockSpec((1, D), lambda i: (0, 0)),
                  pl.BlockSpec((1, D), lambda i: (0, 0))],
        out_specs=[pl.BlockSpec((tm, D), lambda i: (i, 0)),
                   pl.BlockSpec((tm, D), lambda i: (i, 0))],
        out_shape=[jax.ShapeDtypeStruct((M, D), F32), jax.ShapeDtypeStruct((M, D), BF16)],
        compiler_params=_cparams("parallel"),
        name="outproj_layernorm",
    )(gate_arr, o, x, (w_out * (1.0 / DN_ALPHA)).astype(BF16), ln_g.reshape(1, D), ln_b.reshape(1, D))


FOX_BIAS_PIECES = 3


def _fox_cum_kernel(x_ref, wh_ref, wl_ref, bf_ref, tril_ref, place_ref, pc_ref, carry_sc):
    @pl.when(pl.program_id(1) == 0)
    def _():
        carry_sc[...] = jnp.zeros_like(carry_sc)

    x_hi, x_lo = _split2(x_ref[...])
    z = _dot(x_hi, wh_ref[...]) + _dot(x_lo, wh_ref[...]) + _dot(x_hi, wl_ref[...]) + bf_ref[...]
    logf = jnp.minimum(z, 0.0) - jnp.log(1.0 + jnp.exp(-jnp.abs(z)))
    p1, p2, p3 = _split3(logf)
    tril = tril_ref[...]
    c = _dot(tril, p1) + _dot(tril, p2) + _dot(tril, p3) + carry_sc[...]
    carry_sc[...] = c[c.shape[0] - 1:, :]
    pieces = _split3(c * (-LOG2E))
    pc_ref[...] = sum(_dot(pieces[p], place_ref[p]) for p in range(FOX_BIAS_PIECES)).astype(pc_ref.dtype)


def _fox_cum(x3, w_f, b_f, ts=512):
    B, S, D = x3.shape
    H = w_f.shape[1]
    ts = _pick_tile(S, ts)
    w_pad = jnp.zeros((D, LANES), F32).at[:, :H].set(w_f)
    w_hi, w_lo = _split2(w_pad)
    b_pad = jnp.zeros((1, LANES), F32).at[0, :H].set(b_f)
    tril = jnp.tril(jnp.ones((ts, ts), BF16))
    place = np.zeros((FOX_BIAS_PIECES, LANES, LANES), np.float32)
    for p in range(FOX_BIAS_PIECES):
        for h in range(H):
            place[p, h, FOX_BIAS_PIECES * h + p] = 1.0
    return pl.pallas_call(
        _fox_cum_kernel,
        grid=(B, S // ts),
        in_specs=[pl.BlockSpec((None, ts, D), lambda b, j: (b, j, 0)),
                  pl.BlockSpec((D, LANES), lambda b, j: (0, 0)),
                  pl.BlockSpec((D, LANES), lambda b, j: (0, 0)),
                  pl.BlockSpec((1, LANES), lambda b, j: (0, 0)),
                  pl.BlockSpec((ts, ts), lambda b, j: (0, 0)),
                  pl.BlockSpec((FOX_BIAS_PIECES, LANES, LANES), lambda b, j: (0, 0, 0))],
        out_specs=pl.BlockSpec((None, ts, LANES), lambda b, j: (b, j, 0)),
        out_shape=jax.ShapeDtypeStruct((B, S, LANES), BF16),
        scratch_shapes=[pltpu.VMEM((1, LANES), F32)],
        compiler_params=_cparams("parallel", "arbitrary"),
        name="fox_decay_cumsum",
    )(x3, w_hi, w_lo, b_pad, tril, jnp.asarray(place, BF16))


ONES_ROWS = 16


def _fox_attn_kernel(q_ref, k_ref, v_ref, pc_ref, o_ref, kaug_sc, vt_sc, m_sc, acc_sc, *, tq, nsub, seq, unroll):
    dh = FOX_HEAD_DIM
    h = pl.program_id(1)
    g = pl.program_id(2)

    @pl.when(g == 0)
    def _():
        kaug_sc[:, :dh] = k_ref[...]
        kaug_sc[:, dh:] = pc_ref[...]
        for c in range(seq // tq):
            rows = slice(c * tq, (c + 1) * tq)
            vt_sc[:dh, rows] = v_ref[rows, :].astype(F32).T.astype(BF16)
        vt_sc[dh:, :] = jnp.ones((ONES_ROWS, seq), BF16)

    lane = lax.broadcasted_iota(jnp.int32, (tq, LANES), 1)
    bias_lanes = (lane >= FOX_BIAS_PIECES * h) & (lane < FOX_BIAS_PIECES * (h + 1))
    ones_h = jnp.where(bias_lanes, 1.0, 0.0).astype(BF16)
    q_aug = [jnp.concatenate([q_ref[a * tq:(a + 1) * tq, :], ones_h], axis=1) for a in range(nsub)]
    m_sc[...] = jnp.full_like(m_sc, NEG_BIG)
    acc_sc[...] = jnp.zeros_like(acc_sc)
    causal = (lax.broadcasted_iota(jnp.int32, (tq, tq), 0) <= lax.broadcasted_iota(jnp.int32, (tq, tq), 1))
    first = g * nsub

    def chain(a, tiles, diag_last):
        offs = [pl.multiple_of(j * tq, tq) for j in tiles]
        scores = []
        for off in offs:
            scores.append(_dot_nt(kaug_sc[pl.ds(off, tq), :], q_aug[a]))
            yield None
        for n, (off, s) in enumerate(zip(offs, scores)):
            if diag_last and n == len(offs) - 1:
                s = jnp.where(causal, s, NEG_BIG)
            m_old = m_sc[a]
            m_new = jnp.maximum(m_old, jnp.max(s, axis=0, keepdims=True))
            alpha = jnp.exp2(m_old - m_new)
            pv = _dot(vt_sc[:, pl.ds(off, tq)], jnp.exp2(s - m_new).astype(BF16))
            yield None
            acc_sc[a] = alpha * acc_sc[a] + pv
            m_sc[a] = m_new
        yield None

    def body(jj, c):
        _round_robin([chain(a, [jj * unroll + u for u in range(unroll)], False) for a in range(nsub)])
        return c

    lax.fori_loop(0, first // unroll, body, 0)
    _round_robin([chain(a, [first + t for t in range(a + 1)], True) for a in range(nsub)])
    for a in range(nsub):
        acc = acc_sc[a]
        o_t = acc[:dh] / acc[dh:dh + 1]
        o_ref[a * tq:(a + 1) * tq, :] = o_t.T.astype(o_ref.dtype)


def _fox_attention(proj3, pieces, tq=256, nsub=8):
    B, S, _ = proj3.shape
    H, dh = FOX_HEADS, FOX_HEAD_DIM
    tq = _pick_tile(S, tq)
    nsub = _pick_tile(S // tq, nsub)
    tg = tq * nsub
    return pl.pallas_call(
        functools.partial(_fox_attn_kernel, tq=tq, nsub=nsub, seq=S, unroll=min(nsub, 4)),
        grid=(B, H, S // tg),
        in_specs=[pl.BlockSpec((None, tg, dh), lambda b, h, g: (b, g, h)),
                  pl.BlockSpec((None, S, dh), lambda b, h, g: (b, 0, H + h)),
                  pl.BlockSpec((None, S, dh), lambda b, h, g: (b, 0, 2 * H + h)),
                  pl.BlockSpec((None, S, LANES), lambda b, h, g: (b, 0, 0))],
        out_specs=pl.BlockSpec((None, tg, dh), lambda b, h, g: (b, g, h)),
        out_shape=jax.ShapeDtypeStruct((B, S, H * dh), BF16),
        scratch_shapes=[pltpu.VMEM((S, dh + LANES), BF16),
                        pltpu.VMEM((dh + ONES_ROWS, S), BF16),
                        pltpu.VMEM((nsub, 1, tq), F32),
                        pltpu.VMEM((nsub, dh + ONES_ROWS, tq), F32)],
        compiler_params=_cparams("parallel", "parallel", "arbitrary"),
        name="fox_attention",
    )(proj3, proj3, proj3, pieces)


def _fox_layer(x, xb, w_in, b_f, w_out, ln_g, ln_b, B, S):
    D = D_MODEL
    H, dh = FOX_HEADS, FOX_HEAD_DIM
    scale = dh ** -0.5 * LOG2E
    w_q, w_k, w_v, w_f, w_g = jnp.split(w_in, [H * dh, 2 * H * dh, 3 * H * dh, 3 * H * dh + H], axis=1)
    w_main = jnp.concatenate([w_q * scale, w_k, w_v, w_g], axis=1).astype(BF16)
    proj = _matmul(xb, w_main)
    pieces = _fox_cum(x.reshape(B, S, D), w_f, b_f)
    o = _fox_attention(proj.reshape(B, S, 4 * D), pieces)
    return _outproj_ln(proj, 3, o.reshape(B * S, D), x, w_out, ln_g, ln_b)


def _ret_kernel(q_ref, k_ref, v_ref, cos_ref, sin_ref, dm_ref, xi_ref, zeta_ref, gc_ref, gn_ref,
                o_ref, r_sc):
    @pl.when(pl.program_id(1) == 0)
    def _():
        r_sc[...] = jnp.zeros_like(r_sc)

    dk = RET_HEAD_DIM
    half = dk // 2
    cos = cos_ref[...]
    sin = sin_ref[...]

    def rope(x):
        x1, x2 = x[:, :half], x[:, half:]
        return jnp.concatenate([x1 * cos - x2 * sin, x2 * cos + x1 * sin], axis=-1)

    def head_chain(h):
        cols = slice(h * dk, (h + 1) * dk)
        q = rope(q_ref[:, cols].astype(F32))
        k = rope(k_ref[:, cols].astype(F32)) * (dk ** -0.5)
        v = v_ref[:, cols]
        qb = q.astype(BF16)
        r_old = r_sc[h]
        scores = _dot_nt(qb, k.astype(BF16))
        cross = _dot(qb, r_old.astype(BF16))
        kz = (k * zeta_ref[h]).astype(BF16)
        r_new = _dot_tn(kz, v)
        yield None
        o = _dot((scores * dm_ref[h]).astype(BF16), v)
        r_sc[h] = r_old * gc_ref[h] + r_new
        yield None
        o = o + cross * xi_ref[h]
        o = o * lax.rsqrt(jnp.mean(o * o, axis=-1, keepdims=True) + RMS_EPS) * gn_ref[:, cols]
        o_ref[:, cols] = o.astype(o_ref.dtype)
        yield None

    _round_robin([head_chain(h) for h in range(RET_HEADS)])


def _ret_layer(x, xb, w_in, gn_g, w_out, ln_g, ln_b, B, S, chunk=512):
    D = D_MODEL
    H, dk = RET_HEADS, RET_HEAD_DIM
    C = _pick_tile(S, chunk)
    proj = _matmul(xb, w_in.astype(BF16))
    inv = 1.0 / (RET_THETA ** (jnp.arange(0, dk, 2, dtype=F32) / dk))
    ang = jnp.arange(S, dtype=F32)[:, None] * inv[None, :]
    cos, sin = jnp.cos(ang), jnp.sin(ang)
    log_g = jnp.log1p(-(2.0 ** (-5.0 - jnp.arange(H, dtype=F32))))
    pos = jnp.arange(C, dtype=F32)
    diff = pos[:, None] - pos[None, :]
    d_mask = jnp.where(diff[None] >= 0, jnp.exp(jnp.maximum(diff, 0.0)[None] * log_g[:, None, None]), 0.0)
    xi = jnp.broadcast_to(jnp.exp((pos[None, :] + 1.0) * log_g[:, None])[:, :, None], (H, C, dk))
    zeta = jnp.broadcast_to(jnp.exp((C - 1.0 - pos[None, :]) * log_g[:, None])[:, :, None], (H, C, dk))
    g_c = jnp.broadcast_to(jnp.exp(C * log_g)[:, None, None], (H, 1, dk))
    p3 = proj.reshape(B, S, 4 * D)
    o = pl.pallas_call(
        _ret_kernel,
        grid=(B, S // C),
        in_specs=[pl.BlockSpec((None, C, D), lambda b, c: (b, c, 0)),
                  pl.BlockSpec((None, C, D), lambda b, c: (b, c, 1)),
                  pl.BlockSpec((None, C, D), lambda b, c: (b, c, 2)),
                  pl.BlockSpec((C, dk // 2), lambda b, c: (c, 0)),
                  pl.BlockSpec((C, dk // 2), lambda b, c: (c, 0)),
                  pl.BlockSpec((H, C, C), lambda b, c: (0, 0, 0)),
                  pl.BlockSpec((H, C, dk), lambda b, c: (0, 0, 0)),
                  pl.BlockSpec((H, C, dk), lambda b, c: (0, 0, 0)),
                  pl.BlockSpec((H, 1, dk), lambda b, c: (0, 0, 0)),
                  pl.BlockSpec((1, D), lambda b, c: (0, 0))],
        out_specs=pl.BlockSpec((None, C, D), lambda b, c: (b, c, 0)),
        out_shape=jax.ShapeDtypeStruct((B, S, D), BF16),
        scratch_shapes=[pltpu.VMEM((H, dk, dk), F32)],
        compiler_params=_cparams("parallel", "arbitrary"),
        name="retnet_retention",
    )(p3, p3, p3, cos, sin, d_mask, xi, zeta, g_c, gn_g.reshape(1, D))
    return _outproj_ln(proj, 3, o.reshape(B * S, D), x, w_out, ln_g, ln_b)


DSA_HEADS = 8
DSA_HEAD_DIM = 128
DSA_ROPE_DIM = 32
DSA_KV_RANK = 128
IDX_HEADS = 8
IDX_DIM = 64
IDX_ROPE_DIM = 16
TOPK_MAX = 256
INT_MIN = -2 ** 31
HALF_MIN = -2 ** 15
HALF_ROWS = 16
DSA_ONES_ROWS = 16


def _rope_perm(width, groups):
    p = np.zeros((width, width), np.float32)
    for start, half in groups:
        for j in range(half):
            p[start + half + j, start + j] = 1.0
            p[start + j, start + half + j] = 1.0
    return p


def _rope_cs(S, width, groups, theta_dims):
    c = jnp.ones((S, width), F32)
    sg = jnp.zeros((S, width), F32)
    pos = jnp.arange(S, dtype=F32)[:, None]
    for (start, half), rot_dim in zip(groups, theta_dims):
        inv = 1.0 / (ROPE_THETA ** (jnp.arange(0, rot_dim, 2, dtype=F32) / rot_dim))
        ang = pos * inv[None, :]
        cos, sin = jnp.cos(ang), jnp.sin(ang)
        c = c.at[:, start:start + half].set(cos).at[:, start + half:start + 2 * half].set(cos)
        sg = sg.at[:, start:start + half].set(-sin).at[:, start + half:start + 2 * half].set(sin)
    return c, sg


def _dsa_prep_kernel(q_ref, qi_ref, ckv_ref, misc_ref, cq_ref, sq_ref, ci_ref, si_ref, cm_ref, sm_ref,
                     pq_ref, pi_ref, pm_ref, selk_ref, selw_ref, wuk_ref, kvg_ref,
                     qf_ref, kvl_ref, kvt_ref, qir_ref, kid_ref, wit_ref):
    H, dh = DSA_HEADS, DSA_HEAD_DIM
    lane = lax.broadcasted_iota(jnp.int32, (1, LANES), 1)
    rope_lanes = lane < DSA_ROPE_DIM
    cq, sq = cq_ref[...], sq_ref[...]
    scale = dh ** -0.5 * LOG2E
    for h in range(H):
        qh = q_ref[:, h * dh:(h + 1) * dh]
        qr = qh.astype(F32) * cq + _dot(qh, pq_ref[...]) * sq
        q_lat = _dot(qr.astype(BF16), wuk_ref[h])
        qf_ref[h, :, :dh] = (q_lat * scale).astype(BF16)
        qf_ref[h, :, dh:] = jnp.where(rope_lanes, qr * scale, 0.0).astype(BF16)
    ci, si = ci_ref[...], si_ref[...]
    for g in range(IDX_HEADS * IDX_DIM // LANES):
        qg = qi_ref[:, g * LANES:(g + 1) * LANES]
        qr = qg.astype(F32) * ci + _dot(qg, pi_ref[...]) * si
        qir_ref[:, g * LANES:(g + 1) * LANES] = (qr * (IDX_DIM ** -0.5)).astype(BF16)
    ckv = ckv_ref[...].astype(F32)
    ckv = ckv * lax.rsqrt(jnp.mean(ckv * ckv, axis=-1, keepdims=True) + RMS_EPS) * kvg_ref[...]
    misc = misc_ref[...]
    mr = (misc.astype(F32) * cm_ref[...] + _dot(misc, pm_ref[...]) * sm_ref[...])
    kvl_ref[:, :DSA_KV_RANK] = ckv.astype(BF16)
    kvl_ref[:, DSA_KV_RANK:] = jnp.where(rope_lanes, mr, 0.0).astype(BF16)
    kvt_ref[:DSA_KV_RANK, :] = ckv.T.astype(BF16)
    kvt_ref[DSA_KV_RANK:, :] = jnp.ones((DSA_ONES_ROWS, ckv.shape[0]), BF16)
    kid_ref[...] = _dot(mr.astype(BF16), selk_ref[...]).astype(BF16)
    wi = _dot(misc, selw_ref[...]) * (IDX_HEADS ** -0.5)
    wit_ref[...] = wi.T[:IDX_HEADS, :]


def _sort_key(x):
    b = pltpu.bitcast(x, jnp.int32)
    return jnp.where(b < 0, b ^ 0x7FFFFFFF, b)


def _dsa_main_kernel(qf_ref, kvl_ref, kvt_ref, qi_ref, kid_ref, wit_ref, wuv_ref, o_ref,
                     key_sc, hi_sc, lo_sc, low_sc, m_sc, acc_sc, *, tq, tk, k_sel, seq):
    H = DSA_HEADS
    i = pl.program_id(1)
    q0 = i * tq
    nj = (q0 + tq + tk - 1) // tk
    qpos = q0 + lax.broadcasted_iota(jnp.int32, (tk, tq), 1)
    kpos0 = lax.broadcasted_iota(jnp.int32, (tk, tq), 0)
    lane = lax.broadcasted_iota(jnp.int32, (1, LANES), 1)
    neg_inf_key = _sort_key(jnp.full((1, 1), -jnp.inf, F32))

    wit = wit_ref[...]
    qi_heads = []
    for h in range(IDX_HEADS):
        g = qi_ref[:, (h // 2) * LANES:(h // 2 + 1) * LANES]
        keep = (lane >= IDX_DIM) if (h % 2) else (lane < IDX_DIM)
        qi_heads.append(jnp.where(keep, g, jnp.zeros_like(g)))

    def score_tile(j, c):
        off = pl.multiple_of(j * tk, tk)
        ki = kid_ref[pl.ds(off, tk), :]
        scores = [_dot_nt(ki, qi_heads[h]) for h in range(IDX_HEADS)]
        isc = jnp.zeros((tk, tq), F32)
        for h in range(IDX_HEADS):
            isc = isc + jnp.maximum(scores[h], 0.0) * wit[h:h + 1, :]
        isc = jnp.where(kpos0 + off <= qpos, isc + 0.0, -jnp.inf)
        key = _sort_key(isc)
        key_sc[pl.ds(off, tk), :] = key
        hi_sc[pl.ds(off, tk), :] = (key >> 16).astype(jnp.int16)
        lo_sc[pl.ds(off, tk), :] = ((key & 0xFFFF) + HALF_MIN).astype(jnp.int16)
        return c

    lax.fori_loop(0, nj, score_tile, 0)

    def count(pred_fn):
        def body(j, acc):
            off = pl.multiple_of(j * tk, tk)
            kt = key_sc[pl.ds(off, tk), :]
            hit = jnp.where(pred_fn(kt, off), 1, 0)
            return acc + jnp.sum(hit.reshape(tk // 8, 8, tq), axis=0)
        acc = lax.fori_loop(0, nj, body, jnp.zeros((8, tq), jnp.int32))
        return jnp.sum(acc, axis=0, keepdims=True)

    rows16 = tk // HALF_ROWS

    def count16(ref, pred_fn):
        def body(j, acc):
            off = pl.multiple_of(j * tk, tk)
            hit = jnp.where(pred_fn(ref[pl.ds(off, tk), :].reshape(rows16, HALF_ROWS, tq)),
                            jnp.int16(1), jnp.int16(0))
            for r in range(rows16):
                acc = acc + hit[r]
            return acc
        acc = lax.fori_loop(0, nj, body, jnp.zeros((HALF_ROWS, tq), jnp.int16))
        return jnp.sum(acc.astype(jnp.int32), axis=0, keepdims=True)

    def as_half(v):
        return jnp.broadcast_to(v.astype(jnp.int16), (HALF_ROWS, tq))[None]

    def search16(ref, base0, cnt0, bits, want):
        def bit_step(t, carry):
            base, cnt_b = carry
            cand = base + lax.shift_left(jnp.int32(1), bits - 1 - t)
            cand16 = as_half(cand)
            c = count16(ref, lambda kt: kt >= cand16)
            ok = c >= want
            return jnp.where(ok, cand, base), jnp.where(ok, c, cnt_b)
        return lax.fori_loop(0, bits, bit_step, (base0, cnt0))

    zero16 = as_half(jnp.zeros((1, tq), jnp.int32))
    cnt_pos = count16(hi_sc, lambda kt: kt >= zero16)
    nonneg = cnt_pos >= k_sel
    t1, ge_hi = search16(hi_sc, jnp.where(nonneg, 0, HALF_MIN), jnp.where(nonneg, cnt_pos, nj * tk), 15, k_sel)
    t1_16 = as_half(t1)
    above = count16(hi_sc, lambda kt: kt > t1_16)

    def build_low(j, c):
        rows = pl.ds(pl.multiple_of(j * tk, tk), tk)
        hi = hi_sc[rows, :].reshape(rows16, HALF_ROWS, tq)
        lo = lo_sc[rows, :].reshape(rows16, HALF_ROWS, tq)
        low_sc[rows, :] = jnp.where(hi == t1_16, lo, jnp.int16(HALF_MIN)).reshape(tk, tq)
        return c

    lax.fori_loop(0, nj, build_low, 0)
    t2, ge_low = search16(low_sc, jnp.full((1, tq), HALF_MIN, jnp.int32), ge_hi - above, 16, k_sel - above)
    t2_16 = as_half(t2)
    thr = lax.shift_left(t1, 16) | (t2 - HALF_MIN)
    n_ge = above + ge_low
    n_gt = above + count16(low_sc, lambda kt: kt > t2_16)
    need = k_sel - n_gt
    excess = ((n_ge - n_gt) > need) & (thr > neg_inf_key)
    any_excess = jnp.max(jnp.where(excess, 1, 0)) > 0

    def tie_cut():
        def step(t, lo):
            cand = lo + lax.shift_left(jnp.int32(1), int(math.log2(seq)) - t)
            c = count(lambda kt, off: (kt == thr) & (kpos0 + off < cand))
            return jnp.where(c < need, cand, lo)
        lo = lax.fori_loop(0, int(math.log2(seq)) + 1, step, jnp.zeros((1, tq), jnp.int32))
        return jnp.where(excess, lo, seq)

    cut = lax.cond(any_excess, tie_cut, lambda: jnp.full((1, tq), seq, jnp.int32))

    m_sc[...] = jnp.full_like(m_sc, NEG_BIG)
    acc_sc[...] = jnp.zeros_like(acc_sc)

    def attn_tile(j, c):
        off = pl.multiple_of(j * tk, tk)
        kt = key_sc[pl.ds(off, tk), :]
        kpos = kpos0 + off
        bias = jnp.where(kt > thr, 0.0, jnp.where(kt == thr, jnp.where(kpos <= cut, 0.0, NEG_BIG), NEG_BIG))
        bias = jnp.where(kpos <= qpos, bias, NEG_BIG)
        kv = kvl_ref[pl.ds(off, tk), :]
        kvt = kvt_ref[:, pl.ds(off, tk)]

        def head_step(h):
            s = _dot_nt(kv, qf_ref[h]) + bias
            yield None
            m_old = m_sc[h]
            m_new = jnp.maximum(m_old, jnp.max(s, axis=0, keepdims=True))
            alpha = jnp.exp2(m_old - m_new)
            pv = _dot(kvt, jnp.exp2(s - m_new).astype(BF16))
            yield None
            acc_sc[h] = alpha * acc_sc[h] + pv
            m_sc[h] = m_new
            yield None

        _round_robin([head_step(h) for h in range(H)])
        return c

    lax.fori_loop(0, nj, attn_tile, 0)
    for h in range(H):
        acc = acc_sc[h]
        o_lat_t = (acc[:DSA_KV_RANK] / acc[DSA_KV_RANK:DSA_KV_RANK + 1]).astype(BF16)
        o_ref[:, h * DSA_HEAD_DIM:(h + 1) * DSA_HEAD_DIM] = _dot_tn(o_lat_t, wuv_ref[h]).astype(o_ref.dtype)


def _dsa_layer(x, xb, w_in, kv_norm_g, w_uk, w_uv, w_out, ln_g, ln_b, B, S, tq=512, tk=512):
    D = D_MODEL
    H, dh, dr, dc = DSA_HEADS, DSA_HEAD_DIM, DSA_ROPE_DIM, DSA_KV_RANK
    HI, di = IDX_HEADS, IDX_DIM
    w_q, w_ckv, w_kr, w_qi, w_ki, w_wi, w_g = jnp.split(
        w_in, np.cumsum([H * dh, dc, dr, HI * di, di, HI]).tolist(), axis=1)
    w_misc = jnp.concatenate([w_kr, w_ki, w_wi, jnp.zeros((D, LANES - dr - di - HI), F32)], axis=1)
    w_main = jnp.concatenate([w_q, w_g, w_qi, w_ckv, w_misc], axis=1).astype(BF16)
    n_main = w_main.shape[1]
    proj = _matmul(xb, w_main, tn=n_main // 2)
    c_q, c_qi, c_ckv, c_misc = 0, 2 * D // LANES, (2 * D + HI * di) // LANES, (2 * D + HI * di + dc) // LANES

    q_groups = [(0, dr // 2)]
    i_groups = [(0, IDX_ROPE_DIM // 2), (di, IDX_ROPE_DIM // 2)]
    m_groups = [(0, dr // 2), (dr, IDX_ROPE_DIM // 2)]
    cq, sq = _rope_cs(S, LANES, q_groups, [dr])
    ci, si = _rope_cs(S, LANES, i_groups, [IDX_ROPE_DIM, IDX_ROPE_DIM])
    cm, sm = _rope_cs(S, LANES, m_groups, [dr, IDX_ROPE_DIM])
    pq = jnp.asarray(_rope_perm(LANES, q_groups), BF16)
    pi = jnp.asarray(_rope_perm(LANES, i_groups), BF16)
    pm = jnp.asarray(_rope_perm(LANES, m_groups), BF16)
    selk = np.zeros((LANES, LANES), np.float32)
    for j in range(di):
        selk[dr + j, j] = 1.0
        selk[dr + j, di + j] = 1.0
    selw = np.zeros((LANES, LANES), np.float32)
    for j in range(HI):
        selw[dr + di + j, j] = 1.0
    wuk = jnp.concatenate([jnp.zeros((H, dr, dc), F32), jnp.transpose(w_uk, (0, 2, 1))], axis=1).astype(BF16)

    ts = _pick_tile(S, 256)
    p3 = proj.reshape(B, S, n_main)
    tab = lambda: pl.BlockSpec((ts, LANES), lambda b, j: (j, 0))
    mat = lambda: pl.BlockSpec((LANES, LANES), lambda b, j: (0, 0))
    qf, kvl, kvt, qir, kid, wit = pl.pallas_call(
        _dsa_prep_kernel,
        grid=(B, S // ts),
        in_specs=[pl.BlockSpec((None, ts, H * dh), lambda b, j: (b, j, 0)),
                  pl.BlockSpec((None, ts, HI * di), lambda b, j: (b, j, c_qi * LANES // (HI * di))),
                  pl.BlockSpec((None, ts, dc), lambda b, j: (b, j, c_ckv)),
                  pl.BlockSpec((None, ts, LANES), lambda b, j: (b, j, c_misc)),
                  tab(), tab(), tab(), tab(), tab(), tab(),
                  mat(), mat(), mat(), mat(), mat(),
                  pl.BlockSpec((H, LANES, dc), lambda b, j: (0, 0, 0)),
                  pl.BlockSpec((1, dc), lambda b, j: (0, 0))],
        out_specs=[pl.BlockSpec((None, H, ts, 2 * dc), lambda b, j: (b, 0, j, 0)),
                   pl.BlockSpec((None, ts, 2 * dc), lambda b, j: (b, j, 0)),
                   pl.BlockSpec((None, dc + DSA_ONES_ROWS, ts), lambda b, j: (b, 0, j)),
                   pl.BlockSpec((None, ts, HI * di), lambda b, j: (b, j, 0)),
                   pl.BlockSpec((None, ts, LANES), lambda b, j: (b, j, 0)),
                   pl.BlockSpec((None, HI, ts), lambda b, j: (b, 0, j))],
        out_shape=[jax.ShapeDtypeStruct((B, H, S, 2 * dc), BF16),
                   jax.ShapeDtypeStruct((B, S, 2 * dc), BF16),
                   jax.ShapeDtypeStruct((B, dc + DSA_ONES_ROWS, S), BF16),
                   jax.ShapeDtypeStruct((B, S, HI * di), BF16),
                   jax.ShapeDtypeStruct((B, S, LANES), BF16),
                   jax.ShapeDtypeStruct((B, HI, S), F32)],
        compiler_params=_cparams("parallel", "parallel"),
        name="dsa_prep",
    )(p3, p3, p3, p3, cq, sq, ci, si, cm, sm, pq, pi, pm,
      jnp.asarray(selk, BF16), jnp.asarray(selw, BF16), wuk, kv_norm_g.reshape(1, dc))

    tq = _pick_tile(S, tq)
    tk = _pick_tile(S, tk)
    k_sel = min(TOPK_MAX, S // 4)
    o = pl.pallas_call(
        functools.partial(_dsa_main_kernel, tq=tq, tk=tk, k_sel=k_sel, seq=S),
        grid=(B, S // tq),
        in_specs=[pl.BlockSpec((None, H, tq, 2 * dc), lambda b, i: (b, 0, i, 0)),
                  pl.BlockSpec((None, S, 2 * dc), lambda b, i: (b, 0, 0)),
                  pl.BlockSpec((None, dc + DSA_ONES_ROWS, S), lambda b, i: (b, 0, 0)),
                  pl.BlockSpec((None, tq, HI * di), lambda b, i: (b, i, 0)),
                  pl.BlockSpec((None, S, LANES), lambda b, i: (b, 0, 0)),
                  pl.BlockSpec((None, HI, tq), lambda b, i: (b, 0, i)),
                  pl.BlockSpec((H, dc, dh), lambda b, i: (0, 0, 0))],
        out_specs=pl.BlockSpec((None, tq, H * dh), lambda b, i: (b, i, 0)),
        out_shape=jax.ShapeDtypeStruct((B, S, H * dh), BF16),
        scratch_shapes=[pltpu.VMEM((S, tq), jnp.int32),
                        pltpu.VMEM((S, tq), jnp.int16), pltpu.VMEM((S, tq), jnp.int16),
                        pltpu.VMEM((S, tq), jnp.int16),
                        pltpu.VMEM((H, 1, tq), F32),
                        pltpu.VMEM((H, dc + DSA_ONES_ROWS, tq), F32)],
        compiler_params=_cparams("parallel", "arbitrary"),
        name="dsa_select_attention",
    )(qf, kvl, kvt, qir, kid, wit, w_uv.astype(BF16))
    return _outproj_ln(proj, 1, o.reshape(B * S, D), x, w_out, ln_g, ln_b)


RWKV_HEADS = 16
RWKV_HEAD_DIM = 64
RWKV_GN_EPS = 64e-5
RWKV_CHUNK = 64
RWKV_SUB = 16


def _group_sum(x, gmat, split=True):
    outs = []
    for c in range(x.shape[1] // LANES):
        xc = x[:, c * LANES:(c + 1) * LANES]
        if split:
            hi, lo = _split2(xc)
            outs.append(_dot(hi, gmat) + _dot(lo, gmat))
        else:
            outs.append(_dot(xc.astype(BF16), gmat))
    return outs[0] if len(outs) == 1 else jnp.concatenate(outs, axis=1)


def _softplus(y):
    return jnp.maximum(y, 0.0) + jnp.log(1.0 + jnp.exp(-jnp.abs(y)))


def _rwkv_proj_kernel(x_ref, xprev_ref, mu_ref, wr_ref, wk_ref, wv_ref, wg_ref, wla_ref, wlb_ref,
                      ala_ref, alb_ref, w0_ref, a0_ref, kk_ref, ka_ref, rk_ref, gmat_ref,
                      r_ref, k_ref, v_ref, g_ref, kap_ref, b_ref, lw_ref, bonus_ref):
    x = x_ref[...]
    ts = x.shape[0]
    prev = jnp.where(pl.program_id(1) == 0, 0.0, xprev_ref[7:8, :])
    rowid = lax.broadcasted_iota(jnp.int32, (ts, 1), 0)
    xx = jnp.where(rowid == 0, prev, pltpu.roll(x, 1, 0)) - x

    def mixed(i):
        return (x + xx * mu_ref[i:i + 1, :]).astype(BF16)

    r = _dot(mixed(0), wr_ref[...])
    k = _dot(mixed(2), wk_ref[...])
    v = _dot(mixed(3), wv_ref[...])
    g_ref[...] = _dot(mixed(5), wg_ref[...]).astype(g_ref.dtype)
    lora_w = _dot(jnp.tanh(_dot(mixed(1), wla_ref[...])).astype(BF16), wlb_ref[...])
    lora_a = _dot(_dot(mixed(4), ala_ref[...]).astype(BF16), alb_ref[...])
    w_log = -_softplus(-(w0_ref[...] + lora_w)) - 0.5
    lw_ref[...] = -jnp.exp(w_log)
    a = _sigmoid(a0_ref[...] + lora_a)
    gmat = gmat_ref[...]
    kk = k * kk_ref[...]
    kap = kk * lax.rsqrt(_group_sum(kk * kk, gmat, split=False) + 1e-12)
    k2 = k * (1.0 + (a - 1.0) * ka_ref[...])
    bonus_ref[...] = _group_sum(r * k2 * rk_ref[...], gmat, split=False) * v
    r_ref[...] = r.astype(r_ref.dtype)
    k_ref[...] = k2.astype(k_ref.dtype)
    v_ref[...] = v.astype(v_ref.dtype)
    kap_ref[...] = kap.astype(kap_ref.dtype)
    b_ref[...] = (kap * a).astype(b_ref.dtype)


def _bd(x, left):
    z = jnp.zeros_like(x)
    return jnp.concatenate([jnp.where(left, x, z), jnp.where(left, z, x)], axis=0)


def _unbd(x_bd):
    c = x_bd.shape[0] // 2
    return x_bd[:c] + x_bd[c:]


def _rwkv_chunk_pair(lw, r, k, v, kap, b, tril, masks):
    C = RWKV_CHUNK
    left, strict, lower, same_sub, eye = masks
    p1, p2, p3 = _split3(lw)
    L = _dot(tril, p1) + _dot(tril, p2) + _dot(tril, p3)
    yield None
    Lc = L[C - 1:C, :]
    e_l, e_lx, e_nl, e_r = jnp.exp(L), jnp.exp(L - lw), jnp.exp(-L), jnp.exp(Lc - L)
    at = _bd(-kap * e_lx, left).astype(BF16)
    rt = _bd(r * e_l, left)
    bt = _bd(b * e_nl, left).astype(BF16)
    kt = _bd(k * e_nl, left).astype(BF16)
    bh = _bd(b * e_r, left).astype(BF16)
    kh = _bd(k * e_r, left).astype(BF16)
    vb = _bd(v, left).astype(BF16)

    a1 = _dot_nt(jnp.concatenate([at, rt.astype(BF16)], axis=0), jnp.concatenate([bt, kt], axis=0))
    yield None
    n = jnp.where(strict, a1[:2 * C, :2 * C], 0.0)
    ak = jnp.where(strict, a1[:2 * C, 2 * C:], 0.0).astype(BF16)
    rb = jnp.where(lower, a1[2 * C:, :2 * C], 0.0).astype(BF16)
    rk = jnp.where(lower, a1[2 * C:, 2 * C:], 0.0).astype(BF16)

    nd = jnp.where(same_sub, n, 0.0)
    no = (n - nd).astype(BF16)
    ndb = nd.astype(BF16)
    n2 = _dot(ndb, ndb)
    akv = _dot(ak, vb)
    yield None
    n2b = n2.astype(BF16)
    n4 = _dot(n2b, n2b)
    t01 = _dot((eye + nd).astype(BF16), (eye + n2).astype(BF16))
    yield None
    n4b = n4.astype(BF16)
    n8 = _dot(n4b, n4b)
    yield None
    t23 = _dot((eye + n4).astype(BF16), (eye + n8).astype(BF16))
    yield None
    tdb = _dot(t01.astype(BF16), t23.astype(BF16)).astype(BF16)
    yield None
    x1 = _dot(tdb, no)
    yield None
    x1b = x1.astype(BF16)
    x2 = _dot(x1b, x1b)
    yield None
    tx = _dot((eye + x1).astype(BF16), (eye + x2).astype(BF16))
    yield None
    t = _dot(tx.astype(BF16), tdb).astype(BF16)
    yield None
    pq = _dot(t, jnp.concatenate([at, akv.astype(BF16)], axis=1)).astype(BF16)
    yield None
    z = jnp.concatenate([pq, jnp.concatenate([jnp.zeros_like(vb), vb], axis=1)], axis=0)
    ry = _dot(jnp.concatenate([rb, rk], axis=1), z)
    mg = _dot_tn(z, jnp.concatenate([bh, kh], axis=0))
    yield None
    rp = rt + ry[:, :2 * C]
    yl = ry[:, 2 * C:]
    yield _unbd(rp), _unbd(yl), _unbd(mg[:2 * C]), _unbd(mg[2 * C:]), jnp.exp(Lc)


def _rwkv_chunk_kernel(lw_ref, r_ref, k_ref, v_ref, kap_ref, b_ref, tril_ref,
                       rp_ref, yl_ref, mm_ref, gg_ref, gam_ref, *, pairs, cpb):
    C = RWKV_CHUNK
    lane = lax.broadcasted_iota(jnp.int32, (1, LANES), 1)
    left = lane < RWKV_HEAD_DIM
    ri = lax.broadcasted_iota(jnp.int32, (2 * C, 2 * C), 0)
    ci = lax.broadcasted_iota(jnp.int32, (2 * C, 2 * C), 1)
    same_head = (ri // C) == (ci // C)
    strict = same_head & ((ri % C) > (ci % C))
    lower = same_head & ((ri % C) >= (ci % C))
    same_sub = (ri // RWKV_SUB) == (ci // RWKV_SUB)
    eye = jnp.where(ri == ci, 1.0, 0.0).astype(F32)
    masks = (left, strict, lower, same_sub, eye)
    tril = tril_ref[...]
    jobs = [(c, slice(c * C, (c + 1) * C), slice(p * LANES, (p + 1) * LANES))
            for c in range(cpb) for p in range(pairs)]
    results = _round_robin([
        _rwkv_chunk_pair(lw_ref[rows, sl], r_ref[rows, sl].astype(F32), k_ref[rows, sl].astype(F32),
                         v_ref[rows, sl].astype(F32), kap_ref[rows, sl].astype(F32),
                         b_ref[rows, sl].astype(F32), tril, masks)
        for _, rows, sl in jobs])
    for (c, rows, sl), (rp, yl, mm, gg, gam) in zip(jobs, results):
        rp_ref[rows, sl] = rp.astype(rp_ref.dtype)
        yl_ref[rows, sl] = yl
        mm_ref[c, :, sl] = mm.astype(mm_ref.dtype)
        gg_ref[c, :, sl] = gg
        gam_ref[c, :, sl] = gam


def _rwkv_seq_kernel(rp_ref, yl_ref, mm_ref, gg_ref, gam_ref, bonus_ref, gmat_ref, gng_ref, gnb_ref,
                     o_ref, s_sc, y_sc, *, pairs, cb):
    C = RWKV_CHUNK
    lane = lax.broadcasted_iota(jnp.int32, (1, LANES), 1)
    left = lane < RWKV_HEAD_DIM

    @pl.when(pl.program_id(2) == 0)
    def _():
        s_sc[...] = jnp.zeros_like(s_sc)

    states = [s_sc[p] for p in range(pairs)]
    for c in range(cb):
        rows = slice(c * C, (c + 1) * C)
        for p in range(pairs):
            sl = slice(p * LANES, (p + 1) * LANES)
            s = states[p]
            sb = s.astype(BF16)
            y_sc[rows, sl] = _dot_nt(rp_ref[rows, sl], sb) + yl_ref[rows, sl]
            mm = _bd(mm_ref[c, :, sl], left)
            gg = _bd(gg_ref[c, :, sl], left)
            states[p] = s * gam_ref[c, :, sl] + _dot(sb, mm) + gg
    for p in range(pairs):
        s_sc[p] = states[p]

    y = y_sc[...]
    gmat = gmat_ref[...]
    inv_n = 1.0 / RWKV_HEAD_DIM
    yc = y - _group_sum(y, gmat) * inv_n
    var = _group_sum(yc * yc, gmat) * inv_n
    yn = yc * lax.rsqrt(var + RWKV_GN_EPS) * gng_ref[...] + gnb_ref[...]
    o_ref[...] = (yn + bonus_ref[...]).astype(o_ref.dtype)


def _rwkv_layer(x, xb, mu, w_in, w0, w_lora_a, w_lora_b, a0, a_lora_a, a_lora_b, k_k, k_a, r_k,
                gn_g, gn_b, w_out, ln_g, ln_b, B, S, ts=512, pairs=8, seq_pairs=8, chunks_per_step=2):
    D = D_MODEL
    C = RWKV_CHUNK
    nc = S // C
    ts = _pick_tile(S, ts)
    w_r, w_k, w_v, w_g = [w.astype(BF16) for w in jnp.split(w_in, 4, axis=1)]
    gmat = jnp.asarray(np.kron(np.eye(2, dtype=np.float32), np.ones((RWKV_HEAD_DIM, RWKV_HEAD_DIM), np.float32)), BF16)
    row = lambda a: a.reshape(1, D)
    x3 = x.reshape(B, S, D)
    full = lambda shape: pl.BlockSpec(shape, lambda b, j: (0,) * len(shape))
    tile = lambda: pl.BlockSpec((None, ts, D), lambda b, j: (b, j, 0))
    lr = w_lora_a.shape[1]
    outs = pl.pallas_call(
        _rwkv_proj_kernel,
        grid=(B, S // ts),
        in_specs=[tile(),
                  pl.BlockSpec((None, 8, D), lambda b, j: (b, jnp.maximum(j * (ts // 8) - 1, 0), 0)),
                  full((6, D)), full((D, D)), full((D, D)), full((D, D)), full((D, D)),
                  full((D, lr)), full((lr, D)), full((D, lr)), full((lr, D)),
                  full((1, D)), full((1, D)), full((1, D)), full((1, D)), full((1, D)),
                  full((LANES, LANES))],
        out_specs=[tile() for _ in range(8)],
        out_shape=[jax.ShapeDtypeStruct((B, S, D), dt) for dt in (BF16, BF16, BF16, BF16, BF16, BF16, F32, F32)],
        compiler_params=_cparams("parallel", "arbitrary"),
        name="rwkv_projections",
    )(x3, x3, mu, w_r, w_k, w_v, w_g, w_lora_a.astype(BF16), w_lora_b.astype(BF16),
      a_lora_a.astype(BF16), a_lora_b.astype(BF16), row(w0), row(a0), row(k_k), row(k_a), row(r_k), gmat)
    r, k2, v, g, kap, bvec, lw, bonus = outs

    tril = jnp.tril(jnp.ones((C, C), BF16))
    pw = pairs * LANES
    cpb = _pick_tile(nc, chunks_per_step)
    cblk = lambda: pl.BlockSpec((None, cpb * C, pw), lambda b, c, q: (b, c, q))
    sblk = lambda: pl.BlockSpec((None, cpb, C, pw), lambda b, c, q: (b, c, 0, q))
    rp, yl, mm, gg, gam = pl.pallas_call(
        functools.partial(_rwkv_chunk_kernel, pairs=pairs, cpb=cpb),
        grid=(B, nc // cpb, D // pw),
        in_specs=[cblk() for _ in range(6)] + [pl.BlockSpec((C, C), lambda b, c, q: (0, 0))],
        out_specs=[cblk(), cblk(), sblk(), sblk(),
                   pl.BlockSpec((None, cpb, 1, pw), lambda b, c, q: (b, c, 0, q))],
        out_shape=[jax.ShapeDtypeStruct((B, S, D), BF16), jax.ShapeDtypeStruct((B, S, D), F32),
                   jax.ShapeDtypeStruct((B, nc, C, D), BF16), jax.ShapeDtypeStruct((B, nc, C, D), F32),
                   jax.ShapeDtypeStruct((B, nc, 1, D), F32)],
        compiler_params=_cparams("parallel", "parallel", "parallel"),
        name="rwkv_chunk_summaries",
    )(lw, r, k2, v, kap, bvec, tril)

    cb = _pick_tile(nc, 8)
    pairs = seq_pairs
    pw = pairs * LANES
    o = pl.pallas_call(
        functools.partial(_rwkv_seq_kernel, pairs=pairs, cb=cb),
        grid=(B, D // pw, nc // cb),
        in_specs=[pl.BlockSpec((None, cb * C, pw), lambda b, q, j: (b, j, q)),
                  pl.BlockSpec((None, cb * C, pw), lambda b, q, j: (b, j, q)),
                  pl.BlockSpec((None, cb, C, pw), lambda b, q, j: (b, j, 0, q)),
                  pl.BlockSpec((None, cb, C, pw), lambda b, q, j: (b, j, 0, q)),
                  pl.BlockSpec((None, cb, 1, pw), lambda b, q, j: (b, j, 0, q)),
                  pl.BlockSpec((None, cb * C, pw), lambda b, q, j: (b, j, q)),
                  pl.BlockSpec((LANES, LANES), lambda b, q, j: (0, 0)),
                  pl.BlockSpec((1, pw), lambda b, q, j: (0, q)),
                  pl.BlockSpec((1, pw), lambda b, q, j: (0, q))],
        out_specs=pl.BlockSpec((None, cb * C, pw), lambda b, q, j: (b, j, q)),
        out_shape=jax.ShapeDtypeStruct((B, S, D), BF16),
        scratch_shapes=[pltpu.VMEM((pairs, 2 * C, LANES), F32), pltpu.VMEM((cb * C, pw), F32)],
        compiler_params=_cparams("parallel", "parallel", "arbitrary"),
        name="rwkv_state_scan",
    )(rp, yl, mm, gg, gam, bonus, gmat, row(gn_g), row(gn_b))
    return _outproj_ln(g.reshape(B * S, D), 0, o.reshape(B * S, D), x, w_out, ln_g, ln_b)


def kernel(x, ln_g, ln_b, fox_w_in, fox_b_f, fox_w_out, dsa_w_in, dsa_kv_norm_g, dsa_w_uk, dsa_w_uv, dsa_w_out, rwkv_mu, rwkv_w_in, rwkv_w0, rwkv_w_lora_a, rwkv_w_lora_b, rwkv_a0, rwkv_a_lora_a, rwkv_a_lora_b, rwkv_k_k, rwkv_k_a, rwkv_r_k, rwkv_gn_g, rwkv_gn_b, rwkv_w_out, ret_w_in, ret_gn_g, ret_w_out):
    B, S, D = x.shape
    h = x.reshape(B * S, D)
    h, hb = _fox_layer(h, h, fox_w_in, fox_b_f, fox_w_out, ln_g[0], ln_b[0], B, S)
    h, hb = _dsa_layer(h, hb, dsa_w_in, dsa_kv_norm_g, dsa_w_uk, dsa_w_uv, dsa_w_out, ln_g[1], ln_b[1], B, S)
    h, hb = _rwkv_layer(h, hb, rwkv_mu, rwkv_w_in, rwkv_w0, rwkv_w_lora_a, rwkv_w_lora_b, rwkv_a0,
                        rwkv_a_lora_a, rwkv_a_lora_b, rwkv_k_k, rwkv_k_a, rwkv_r_k, rwkv_gn_g, rwkv_gn_b,
                        rwkv_w_out, ln_g[2], ln_b[2], B, S)
    h, hb = _ret_layer(h, hb, ret_w_in, ret_gn_g, ret_w_out, ln_g[3], ln_b[3], B, S)
    return h.reshape(B, S, D)
```

```python
import functools
import math

import jax
import jax.numpy as jnp
import numpy as np
from jax import lax
from jax.experimental import pallas as pl
from jax.experimental.pallas import tpu as pltpu

F32 = jnp.float32
BF16 = jnp.bfloat16

D_MODEL = 1024
DEPTH = 4
LN_EPS = 1e-5
RMS_EPS = 1e-6
DN_ALPHA = (2 * DEPTH) ** 0.25
ROPE_THETA = 500000.0

FOX_HEADS = 8
FOX_HEAD_DIM = 128

RET_HEADS = 4
RET_HEAD_DIM = 256
RET_THETA = 10000.0

LANES = 128
VMEM_LIMIT = 48 * 1024 * 1024
NEG_BIG = -2.0 ** 100
LOG2E = 1.4426950408889634


def _cparams(*sem):
    return pltpu.CompilerParams(dimension_semantics=sem, vmem_limit_bytes=VMEM_LIMIT)


def _dot(a, b):
    return jnp.dot(a, b, preferred_element_type=F32)


def _dot_nt(a, b):
    return lax.dot_general(a, b, (((1,), (1,)), ((), ())), preferred_element_type=F32)


def _dot_tn(a, b):
    return lax.dot_general(a, b, (((0,), (0,)), ((), ())), preferred_element_type=F32)


def _split2(x):
    hi = x.astype(BF16)
    lo = (x - hi.astype(F32)).astype(BF16)
    return hi, lo


def _split3(x):
    p1 = x.astype(BF16)
    r1 = x - p1.astype(F32)
    p2 = r1.astype(BF16)
    p3 = (r1 - p2.astype(F32)).astype(BF16)
    return p1, p2, p3


def _sigmoid(x):
    return 1.0 / (1.0 + jnp.exp(-x))


def _round_robin(gens):
    results = [None] * len(gens)
    live = list(range(len(gens)))
    while live:
        still = []
        for i in live:
            try:
                out = next(gens[i])
            except StopIteration:
                continue
            if out is not None:
                results[i] = out
            still.append(i)
        live = still
    return results


def _pick_tile(n, pref):
    t = min(n, pref)
    while n % t:
        t //= 2
    return t


def _mm_kernel(a_ref, w_ref, o_ref):
    o_ref[...] = _dot(a_ref[...].astype(BF16), w_ref[...]).astype(o_ref.dtype)


def _matmul(a, w, out_dtype=BF16, tm=2048, tn=1024):
    M, K = a.shape
    N = w.shape[1]
    tm = _pick_tile(M, tm)
    if N % tn:
        tn = N
    return pl.pallas_call(
        _mm_kernel,
        grid=(M // tm, N // tn),
        in_specs=[pl.BlockSpec((tm, K), lambda i, j: (i, 0)),
                  pl.BlockSpec((K, tn), lambda i, j: (0, j))],
        out_specs=pl.BlockSpec((tm, tn), lambda i, j: (i, j)),
        out_shape=jax.ShapeDtypeStruct((M, N), out_dtype),
        compiler_params=_cparams("parallel", "arbitrary"),
        name="proj_matmul",
    )(a, w)


def _outproj_ln_kernel(g_ref, o_ref, x_ref, w_ref, lg_ref, lb_ref, xo_ref, xb_ref):
    half_g = g_ref[...] * 0.5
    h = (half_g * o_ref[...]) * (1.0 + jnp.tanh(half_g))
    z = x_ref[...] + _dot(h.astype(BF16), w_ref[...])
    zc = z - jnp.mean(z, axis=-1, keepdims=True)
    var = jnp.mean(zc * zc, axis=-1, keepdims=True)
    out = zc * lax.rsqrt(var + LN_EPS / DN_ALPHA ** 2) * lg_ref[...] + lb_ref[...]
    xo_ref[...] = out
    xb_ref[...] = out.astype(BF16)


def _outproj_ln(gate_arr, gate_col, o, x, w_out, ln_g, ln_b, tm=1024):
    M, D = x.shape
    tm = _pick_tile(M, tm)
    return pl.pallas_call(
        _outproj_ln_kernel,
        grid=(M // tm,),
        in_specs=[pl.BlockSpec((tm, D), lambda i: (i, gate_col)),
                  pl.BlockSpec((tm, D), lambda i: (i, 0)),
                  pl.BlockSpec((tm, D), lambda i: (i, 0)),
                  pl.BlockSpec((D, D), lambda i: (0, 0)),
                  pl.BlockSpec((1, D), lambda i: (0, 0)),
                  pl.BlockSpec((1, D), lambda i: (0, 0))],
        out_specs=[pl.BlockSpec((tm, D), lambda i: (i, 0)),
                   pl.BlockSpec((tm, D), lambda i: (i, 0))],
        out_shape=[jax.ShapeDtypeStruct((M, D), F32), jax.ShapeDtypeStruct((M, D), BF16)],
        compiler_params=_cparams("parallel"),
        name="outproj_layernorm",
    )(gate_arr, o, x, (w_out * (1.0 / DN_ALPHA)).astype(BF16), ln_g.reshape(1, D), ln_b.reshape(1, D))


FOX_BIAS_PIECES = 3


def _fox_cum_kernel(x_ref, wh_ref, wl_ref, bf_ref, tril_ref, place_ref, pc_ref, carry_sc):
    @pl.when(pl.program_id(1) == 0)
    def _():
        carry_sc[...] = jnp.zeros_like(carry_sc)

    x_hi, x_lo = _split2(x_ref[...])
    z = _dot(x_hi, wh_ref[...]) + _dot(x_lo, wh_ref[...]) + _dot(x_hi, wl_ref[...]) + bf_ref[...]
    logf = jnp.minimum(z, 0.0) - jnp.log(1.0 + jnp.exp(-jnp.abs(z)))
    p1, p2, p3 = _split3(logf)
    tril = tril_ref[...]
    c = _dot(tril, p1) + _dot(tril, p2) + _dot(tril, p3) + carry_sc[...]
    carry_sc[...] = c[c.shape[0] - 1:, :]
    pieces = _split3(c * (-LOG2E))
    pc_ref[...] = sum(_dot(pieces[p], place_ref[p]) for p in range(FOX_BIAS_PIECES)).astype(pc_ref.dtype)


def _fox_cum(x3, w_f, b_f, ts=512):
    B, S, D = x3.shape
    H = w_f.shape[1]
    ts = _pick_tile(S, ts)
    w_pad = jnp.zeros((D, LANES), F32).at[:, :H].set(w_f)
    w_hi, w_lo = _split2(w_pad)
    b_pad = jnp.zeros((1, LANES), F32).at[0, :H].set(b_f)
    tril = jnp.tril(jnp.ones((ts, ts), BF16))
    place = np.zeros((FOX_BIAS_PIECES, LANES, LANES), np.float32)
    for p in range(FOX_BIAS_PIECES):
        for h in range(H):
            place[p, h, FOX_BIAS_PIECES * h + p] = 1.0
    return pl.pallas_call(
        _fox_cum_kernel,
        grid=(B, S // ts),
        in_specs=[pl.BlockSpec((None, ts, D), lambda b, j: (b, j, 0)),
                  pl.BlockSpec((D, LANES), lambda b, j: (0, 0)),
                  pl.BlockSpec((D, LANES), lambda b, j: (0, 0)),
                  pl.BlockSpec((1, LANES), lambda b, j: (0, 0)),
                  pl.BlockSpec((ts, ts), lambda b, j: (0, 0)),
                  pl.BlockSpec((FOX_BIAS_PIECES, LANES, LANES), lambda b, j: (0, 0, 0))],
        out_specs=pl.BlockSpec((None, ts, LANES), lambda b, j: (b, j, 0)),
        out_shape=jax.ShapeDtypeStruct((B, S, LANES), BF16),
        scratch_shapes=[pltpu.VMEM((1, LANES), F32)],
        compiler_params=_cparams("parallel", "arbitrary"),
        name="fox_decay_cumsum",
    )(x3, w_hi, w_lo, b_pad, tril, jnp.asarray(place, BF16))


ONES_ROWS = 16


def _fox_attn_kernel(q_ref, k_ref, v_ref, pc_ref, o_ref, kaug_sc, vt_sc, m_sc, acc_sc, *, tq, nsub, seq, unroll):
    dh = FOX_HEAD_DIM
    h = pl.program_id(1)
    g = pl.program_id(2)

    @pl.when(g == 0)
    def _():
        kaug_sc[:, :dh] = k_ref[...]
        kaug_sc[:, dh:] = pc_ref[...]
        for c in range(seq // tq):
            rows = slice(c * tq, (c + 1) * tq)
            vt_sc[:dh, rows] = v_ref[rows, :].astype(F32).T.astype(BF16)
        vt_sc[dh:, :] = jnp.ones((ONES_ROWS, seq), BF16)

    lane = lax.broadcasted_iota(jnp.int32, (tq, LANES), 1)
    bias_lanes = (lane >= FOX_BIAS_PIECES * h) & (lane < FOX_BIAS_PIECES * (h + 1))
    ones_h = jnp.where(bias_lanes, 1.0, 0.0).astype(BF16)
    q_aug = [jnp.concatenate([q_ref[a * tq:(a + 1) * tq, :], ones_h], axis=1) for a in range(nsub)]
    m_sc[...] = jnp.full_like(m_sc, NEG_BIG)
    acc_sc[...] = jnp.zeros_like(acc_sc)
    causal = (lax.broadcasted_iota(jnp.int32, (tq, tq), 0) <= lax.broadcasted_iota(jnp.int32, (tq, tq), 1))
    first = g * nsub

    def chain(a, tiles, diag_last):
        offs = [pl.multiple_of(j * tq, tq) for j in tiles]
        scores = []
        for off in offs:
            scores.append(_dot_nt(kaug_sc[pl.ds(off, tq), :], q_aug[a]))
            yield None
        for n, (off, s) in enumerate(zip(offs, scores)):
            if diag_last and n == len(offs) - 1:
                s = jnp.where(causal, s, NEG_BIG)
            m_old = m_sc[a]
            m_new = jnp.maximum(m_old, jnp.max(s, axis=0, keepdims=True))
            alpha = jnp.exp2(m_old - m_new)
            pv = _dot(vt_sc[:, pl.ds(off, tq)], jnp.exp2(s - m_new).astype(BF16))
            yield None
            acc_sc[a] = alpha * acc_sc[a] + pv
            m_sc[a] = m_new
        yield None

    def body(jj, c):
        _round_robin([chain(a, [jj * unroll + u for u in range(unroll)], False) for a in range(nsub)])
        return c

    lax.fori_loop(0, first // unroll, body, 0)
    _round_robin([chain(a, [first + t for t in range(a + 1)], True) for a in range(nsub)])
    for a in range(nsub):
        acc = acc_sc[a]
        o_t = acc[:dh] / acc[dh:dh + 1]
        o_ref[a * tq:(a + 1) * tq, :] = o_t.T.astype(o_ref.dtype)


def _fox_attention(proj3, pieces, tq=256, nsub=8):
    B, S, _ = proj3.shape
    H, dh = FOX_HEADS, FOX_HEAD_DIM
    tq = _pick_tile(S, tq)
    nsub = _pick_tile(S // tq, nsub)
    tg = tq * nsub
    return pl.pallas_call(
        functools.partial(_fox_attn_kernel, tq=tq, nsub=nsub, seq=S, unroll=min(nsub, 4)),
        grid=(B, H, S // tg),
        in_specs=[pl.BlockSpec((None, tg, dh), lambda b, h, g: (b, g, h)),
                  pl.BlockSpec((None, S, dh), lambda b, h, g: (b, 0, H + h)),
                  pl.BlockSpec((None, S, dh), lambda b, h, g: (b, 0, 2 * H + h)),
                  pl.BlockSpec((None, S, LANES), lambda b, h, g: (b, 0, 0))],
        out_specs=pl.BlockSpec((None, tg, dh), lambda b, h, g: (b, g, h)),
        out_shape=jax.ShapeDtypeStruct((B, S, H * dh), BF16),
        scratch_shapes=[pltpu.VMEM((S, dh + LANES), BF16),
                        pltpu.VMEM((dh + ONES_ROWS, S), BF16),
                        pltpu.VMEM((nsub, 1, tq), F32),
                        pltpu.VMEM((nsub, dh + ONES_ROWS, tq), F32)],
        compiler_params=_cparams("parallel", "parallel", "arbitrary"),
        name="fox_attention",
    )(proj3, proj3, proj3, pieces)


def _fox_layer(x, xb, w_in, b_f, w_out, ln_g, ln_b, B, S):
    D = D_MODEL
    H, dh = FOX_HEADS, FOX_HEAD_DIM
    scale = dh ** -0.5 * LOG2E
    w_q, w_k, w_v, w_f, w_g = jnp.split(w_in, [H * dh, 2 * H * dh, 3 * H * dh, 3 * H * dh + H], axis=1)
    w_main = jnp.concatenate([w_q * scale, w_k, w_v, w_g], axis=1).astype(BF16)
    proj = _matmul(xb, w_main)
    pieces = _fox_cum(x.reshape(B, S, D), w_f, b_f)
    o = _fox_attention(proj.reshape(B, S, 4 * D), pieces)
    return _outproj_ln(proj, 3, o.reshape(B * S, D), x, w_out, ln_g, ln_b)


def _ret_kernel(q_ref, k_ref, v_ref, cos_ref, sin_ref, dm_ref, xi_ref, zeta_ref, gc_ref, gn_ref,
                o_ref, r_sc):
    @pl.when(pl.program_id(1) == 0)
    def _():
        r_sc[...] = jnp.zeros_like(r_sc)

    dk = RET_HEAD_DIM
    half = dk // 2
    cos = cos_ref[...]
    sin = sin_ref[...]

    def rope(x):
        x1, x2 = x[:, :half], x[:, half:]
        return jnp.concatenate([x1 * cos - x2 * sin, x2 * cos + x1 * sin], axis=-1)

    def head_chain(h):
        cols = slice(h * dk, (h + 1) * dk)
        q = rope(q_ref[:, cols].astype(F32))
        k = rope(k_ref[:, cols].astype(F32)) * (dk ** -0.5)
        v = v_ref[:, cols]
        qb = q.astype(BF16)
        r_old = r_sc[h]
        scores = _dot_nt(qb, k.astype(BF16))
        cross = _dot(qb, r_old.astype(BF16))
        kz = (k * zeta_ref[h]).astype(BF16)
        r_new = _dot_tn(kz, v)
        yield None
        o = _dot((scores * dm_ref[h]).astype(BF16), v)
        r_sc[h] = r_old * gc_ref[h] + r_new
        yield None
        o = o + cross * xi_ref[h]
        o = o * lax.rsqrt(jnp.mean(o * o, axis=-1, keepdims=True) + RMS_EPS) * gn_ref[:, cols]
        o_ref[:, cols] = o.astype(o_ref.dtype)
        yield None

    _round_robin([head_chain(h) for h in range(RET_HEADS)])


def _ret_layer(x, xb, w_in, gn_g, w_out, ln_g, ln_b, B, S, chunk=512):
    D = D_MODEL
    H, dk = RET_HEADS, RET_HEAD_DIM
    C = _pick_tile(S, chunk)
    proj = _matmul(xb, w_in.astype(BF16))
    inv = 1.0 / (RET_THETA ** (jnp.arange(0, dk, 2, dtype=F32) / dk))
    ang = jnp.arange(S, dtype=F32)[:, None] * inv[None, :]
    cos, sin = jnp.cos(ang), jnp.sin(ang)
    log_g = jnp.log1p(-(2.0 ** (-5.0 - jnp.arange(H, dtype=F32))))
    pos = jnp.arange(C, dtype=F32)
    diff = pos[:, None] - pos[None, :]
    d_mask = jnp.where(diff[None] >= 0, jnp.exp(jnp.maximum(diff, 0.0)[None] * log_g[:, None, None]), 0.0)
    xi = jnp.broadcast_to(jnp.exp((pos[None, :] + 1.0) * log_g[:, None])[:, :, None], (H, C, dk))
    zeta = jnp.broadcast_to(jnp.exp((C - 1.0 - pos[None, :]) * log_g[:, None])[:, :, None], (H, C, dk))
    g_c = jnp.broadcast_to(jnp.exp(C * log_g)[:, None, None], (H, 1, dk))
    p3 = proj.reshape(B, S, 4 * D)
    o = pl.pallas_call(
        _ret_kernel,
        grid=(B, S // C),
        in_specs=[pl.BlockSpec((None, C, D), lambda b, c: (b, c, 0)),
                  pl.BlockSpec((None, C, D), lambda b, c: (b, c, 1)),
                  pl.BlockSpec((None, C, D), lambda b, c: (b, c, 2)),
                  pl.BlockSpec((C, dk // 2), lambda b, c: (c, 0)),
                  pl.BlockSpec((C, dk // 2), lambda b, c: (c, 0)),
                  pl.BlockSpec((H, C, C), lambda b, c: (0, 0, 0)),
                  pl.BlockSpec((H, C, dk), lambda b, c: (0, 0, 0)),
                  pl.BlockSpec((H, C, dk), lambda b, c: (0, 0, 0)),
                  pl.BlockSpec((H, 1, dk), lambda b, c: (0, 0, 0)),
                  pl.BlockSpec((1, D), lambda b, c: (0, 0))],
        out_specs=pl.BlockSpec((None, C, D), lambda b, c: (b, c, 0)),
        out_shape=jax.ShapeDtypeStruct((B, S, D), BF16),
        scratch_shapes=[pltpu.VMEM((H, dk, dk), F32)],
        compiler_params=_cparams("parallel", "arbitrary"),
        name="retnet_retention",
    )(p3, p3, p3, cos, sin, d_mask, xi, zeta, g_c, gn_g.reshape(1, D))
    return _outproj_ln(proj, 3, o.reshape(B * S, D), x, w_out, ln_g, ln_b)


DSA_HEADS = 8
DSA_HEAD_DIM = 128
DSA_ROPE_DIM = 32
DSA_KV_RANK = 128
IDX_HEADS = 8
IDX_DIM = 64
IDX_ROPE_DIM = 16
TOPK_MAX = 256
INT_MIN = -2 ** 31
HALF_MIN = -2 ** 15
HALF_ROWS = 16
DSA_ONES_ROWS = 16


def _rope_perm(width, groups):
    p = np.zeros((width, width), np.float32)
    for start, half in groups:
        for j in range(half):
            p[start + half + j, start + j] = 1.0
            p[start + j, start + half + j] = 1.0
    return p


def _rope_cs(S, width, groups, theta_dims):
    c = jnp.ones((S, width), F32)
    sg = jnp.zeros((S, width), F32)
    pos = jnp.arange(S, dtype=F32)[:, None]
    for (start, half), rot_dim in zip(groups, theta_dims):
        inv = 1.0 / (ROPE_THETA ** (jnp.arange(0, rot_dim, 2, dtype=F32) / rot_dim))
        ang = pos * inv[None, :]
        cos, sin = jnp.cos(ang), jnp.sin(ang)
        c = c.at[:, start:start + half].set(cos).at[:, start + half:start + 2 * half].set(cos)
        sg = sg.at[:, start:start + half].set(-sin).at[:, start + half:start + 2 * half].set(sin)
    return c, sg


def _dsa_prep_kernel(q_ref, qi_ref, ckv_ref, misc_ref, cq_ref, sq_ref, ci_ref, si_ref, cm_ref, sm_ref,
                     pq_ref, pi_ref, pm_ref, selk_ref, selw_ref, wuk_ref, kvg_ref,
                     qf_ref, kvl_ref, kvt_ref, qir_ref, kid_ref, wit_ref):
    H, dh = DSA_HEADS, DSA_HEAD_DIM
    lane = lax.broadcasted_iota(jnp.int32, (1, LANES), 1)
    rope_lanes = lane < DSA_ROPE_DIM
    cq, sq = cq_ref[...], sq_ref[...]
    scale = dh ** -0.5 * LOG2E
    for h in range(H):
        qh = q_ref[:, h * dh:(h + 1) * dh]
        qr = qh.astype(F32) * cq + _dot(qh, pq_ref[...]) * sq
        q_lat = _dot(qr.astype(BF16), wuk_ref[h])
        qf_ref[h, :, :dh] = (q_lat * scale).astype(BF16)
        qf_ref[h, :, dh:] = jnp.where(rope_lanes, qr * scale, 0.0).astype(BF16)
    ci, si = ci_ref[...], si_ref[...]
    for g in range(IDX_HEADS * IDX_DIM // LANES):
        qg = qi_ref[:, g * LANES:(g + 1) * LANES]
        qr = qg.astype(F32) * ci + _dot(qg, pi_ref[...]) * si
        qir_ref[:, g * LANES:(g + 1) * LANES] = (qr * (IDX_DIM ** -0.5)).astype(BF16)
    ckv = ckv_ref[...].astype(F32)
    ckv = ckv * lax.rsqrt(jnp.mean(ckv * ckv, axis=-1, keepdims=True) + RMS_EPS) * kvg_ref[...]
    misc = misc_ref[...]
    mr = (misc.astype(F32) * cm_ref[...] + _dot(misc, pm_ref[...]) * sm_ref[...])
    kvl_ref[:, :DSA_KV_RANK] = ckv.astype(BF16)
    kvl_ref[:, DSA_KV_RANK:] = jnp.where(rope_lanes, mr, 0.0).astype(BF16)
    kvt_ref[:DSA_KV_RANK, :] = ckv.T.astype(BF16)
    kvt_ref[DSA_KV_RANK:, :] = jnp.ones((DSA_ONES_ROWS, ckv.shape[0]), BF16)
    kid_ref[...] = _dot(mr.astype(BF16), selk_ref[...]).astype(BF16)
    wi = _dot(misc, selw_ref[...]) * (IDX_HEADS ** -0.5)
    wit_ref[...] = wi.T[:IDX_HEADS, :]


def _sort_key(x):
    b = pltpu.bitcast(x, jnp.int32)
    return jnp.where(b < 0, b ^ 0x7FFFFFFF, b)


def _dsa_main_kernel(qf_ref, kvl_ref, kvt_ref, qi_ref, kid_ref, wit_ref, wuv_ref, o_ref,
                     key_sc, hi_sc, lo_sc, low_sc, m_sc, acc_sc, *, tq, tk, k_sel, seq):
    H = DSA_HEADS
    i = pl.program_id(1)
    q0 = i * tq
    nj = (q0 + tq + tk - 1) // tk
    qpos = q0 + lax.broadcasted_iota(jnp.int32, (tk, tq), 1)
    kpos0 = lax.broadcasted_iota(jnp.int32, (tk, tq), 0)
    lane = lax.broadcasted_iota(jnp.int32, (1, LANES), 1)
    neg_inf_key = _sort_key(jnp.full((1, 1), -jnp.inf, F32))

    wit = wit_ref[...]
    qi_heads = []
    for h in range(IDX_HEADS):
        g = qi_ref[:, (h // 2) * LANES:(h // 2 + 1) * LANES]
        keep = (lane >= IDX_DIM) if (h % 2) else (lane < IDX_DIM)
        qi_heads.append(jnp.where(keep, g, jnp.zeros_like(g)))

    def score_tile(j, c):
        off = pl.multiple_of(j * tk, tk)
        ki = kid_ref[pl.ds(off, tk), :]
        scores = [_dot_nt(ki, qi_heads[h]) for h in range(IDX_HEADS)]
        isc = jnp.zeros((tk, tq), F32)
        for h in range(IDX_HEADS):
            isc = isc + jnp.maximum(scores[h], 0.0) * wit[h:h + 1, :]
        isc = jnp.where(kpos0 + off <= qpos, isc + 0.0, -jnp.inf)
        key = _sort_key(isc)
        key_sc[pl.ds(off, tk), :] = key
        hi_sc[pl.ds(off, tk), :] = (key >> 16).astype(jnp.int16)
        lo_sc[pl.ds(off, tk), :] = ((key & 0xFFFF) + HALF_MIN).astype(jnp.int16)
        return c

    lax.fori_loop(0, nj, score_tile, 0)

    def count(pred_fn):
        def body(j, acc):
            off = pl.multiple_of(j * tk, tk)
            kt = key_sc[pl.ds(off, tk), :]
            hit = jnp.where(pred_fn(kt, off), 1, 0)
            return acc + jnp.sum(hit.reshape(tk // 8, 8, tq), axis=0)
        acc = lax.fori_loop(0, nj, body, jnp.zeros((8, tq), jnp.int32))
        return jnp.sum(acc, axis=0, keepdims=True)

    rows16 = tk // HALF_ROWS

    def count16(ref, pred_fn):
        def body(j, acc):
            off = pl.multiple_of(j * tk, tk)
            hit = jnp.where(pred_fn(ref[pl.ds(off, tk), :].reshape(rows16, HALF_ROWS, tq)),
                            jnp.int16(1), jnp.int16(0))
            for r in range(rows16):
                acc = acc + hit[r]
            return acc
        acc = lax.fori_loop(0, nj, body, jnp.zeros((HALF_ROWS, tq), jnp.int16))
        return jnp.sum(acc.astype(jnp.int32), axis=0, keepdims=True)

    def as_half(v):
        return jnp.broadcast_to(v.astype(jnp.int16), (HALF_ROWS, tq))[None]

    def search16(ref, base0, cnt0, bits, want):
        def bit_step(t, carry):
            base, cnt_b = carry
            cand = base + lax.shift_left(jnp.int32(1), bits - 1 - t)
            cand16 = as_half(cand)
            c = count16(ref, lambda kt: kt >= cand16)
            ok = c >= want
            return jnp.where(ok, cand, base), jnp.where(ok, c, cnt_b)
        return lax.fori_loop(0, bits, bit_step, (base0, cnt0))

    zero16 = as_half(jnp.zeros((1, tq), jnp.int32))
    cnt_pos = count16(hi_sc, lambda kt: kt >= zero16)
    nonneg = cnt_pos >= k_sel
    t1, ge_hi = search16(hi_sc, jnp.where(nonneg, 0, HALF_MIN), jnp.where(nonneg, cnt_pos, nj * tk), 15, k_sel)
    t1_16 = as_half(t1)
    above = count16(hi_sc, lambda kt: kt > t1_16)

    def build_low(j, c):
        rows = pl.ds(pl.multiple_of(j * tk, tk), tk)
        hi = hi_sc[rows, :].reshape(rows16, HALF_ROWS, tq)
        lo = lo_sc[rows, :].reshape(rows16, HALF_ROWS, tq)
        low_sc[rows, :] = jnp.where(hi == t1_16, lo, jnp.int16(HALF_MIN)).reshape(tk, tq)
        return c

    lax.fori_loop(0, nj, build_low, 0)
    t2, ge_low = search16(low_sc, jnp.full((1, tq), HALF_MIN, jnp.int32), ge_hi - above, 16, k_sel - above)
    t2_16 = as_half(t2)
    thr = lax.shift_left(t1, 16) | (t2 - HALF_MIN)
    n_ge = above + ge_low
    n_gt = above + count16(low_sc, lambda kt: kt > t2_16)
    need = k_sel - n_gt
    excess = ((n_ge - n_gt) > need) & (thr > neg_inf_key)
    any_excess = jnp.max(jnp.where(excess, 1, 0)) > 0

    def tie_cut():
        def step(t, lo):
            cand = lo + lax.shift_left(jnp.int32(1), int(math.log2(seq)) - t)
            c = count(lambda kt, off: (kt == thr) & (kpos0 + off < cand))
            return jnp.where(c < need, cand, lo)
        lo = lax.fori_loop(0, int(math.log2(seq)) + 1, step, jnp.zeros((1, tq), jnp.int32))
        return jnp.where(excess, lo, seq)

    cut = lax.cond(any_excess, tie_cut, lambda: jnp.full((1, tq), seq, jnp.int32))

    m_sc[...] = jnp.full_like(m_sc, NEG_BIG)
    acc_sc[...] = jnp.zeros_like(acc_sc)

    def attn_tile(j, c):
        off = pl.multiple_of(j * tk, tk)
        kt = key_sc[pl.ds(off, tk), :]
        kpos = kpos0 + off
        bias = jnp.where(kt > thr, 0.0, jnp.where(kt == thr, jnp.where(kpos <= cut, 0.0, NEG_BIG), NEG_BIG))
        bias = jnp.where(kpos <= qpos, bias, NEG_BIG)
        kv = kvl_ref[pl.ds(off, tk), :]
        kvt = kvt_ref[:, pl.ds(off, tk)]

        def head_step(h):
            s = _dot_nt(kv, qf_ref[h]) + bias
            yield None
            m_old = m_sc[h]
            m_new = jnp.maximum(m_old, jnp.max(s, axis=0, keepdims=True))
            alpha = jnp.exp2(m_old - m_new)
            pv = _dot(kvt, jnp.exp2(s - m_new).astype(BF16))
            yield None
            acc_sc[h] = alpha * acc_sc[h] + pv
            m_sc[h] = m_new
            yield None

        _round_robin([head_step(h) for h in range(H)])
        return c

    lax.fori_loop(0, nj, attn_tile, 0)
    for h in range(H):
        acc = acc_sc[h]
        o_lat_t = (acc[:DSA_KV_RANK] / acc[DSA_KV_RANK:DSA_KV_RANK + 1]).astype(BF16)
        o_ref[:, h * DSA_HEAD_DIM:(h + 1) * DSA_HEAD_DIM] = _dot_tn(o_lat_t, wuv_ref[h]).astype(o_ref.dtype)


def _dsa_layer(x, xb, w_in, kv_norm_g, w_uk, w_uv, w_out, ln_g, ln_b, B, S, tq=512, tk=512):
    D = D_MODEL
    H, dh, dr, dc = DSA_HEADS, DSA_HEAD_DIM, DSA_ROPE_DIM, DSA_KV_RANK
    HI, di = IDX_HEADS, IDX_DIM
    w_q, w_ckv, w_kr, w_qi, w_ki, w_wi, w_g = jnp.split(
        w_in, np.cumsum([H * dh, dc, dr, HI * di, di, HI]).tolist(), axis=1)
    w_misc = jnp.concatenate([w_kr, w_ki, w_wi, jnp.zeros((D, LANES - dr - di - HI), F32)], axis=1)
    w_main = jnp.concatenate([w_q, w_g, w_qi, w_ckv, w_misc], axis=1).astype(BF16)
    n_main = w_main.shape[1]
    proj = _matmul(xb, w_main, tn=n_main // 2)
    c_q, c_qi, c_ckv, c_misc = 0, 2 * D // LANES, (2 * D + HI * di) // LANES, (2 * D + HI * di + dc) // LANES

    q_groups = [(0, dr // 2)]
    i_groups = [(0, IDX_ROPE_DIM // 2), (di, IDX_ROPE_DIM // 2)]
    m_groups = [(0, dr // 2), (dr, IDX_ROPE_DIM // 2)]
    cq, sq = _rope_cs(S, LANES, q_groups, [dr])
    ci, si = _rope_cs(S, LANES, i_groups, [IDX_ROPE_DIM, IDX_ROPE_DIM])
    cm, sm = _rope_cs(S, LANES, m_groups, [dr, IDX_ROPE_DIM])
    pq = jnp.asarray(_rope_perm(LANES, q_groups), BF16)
    pi = jnp.asarray(_rope_perm(LANES, i_groups), BF16)
    pm = jnp.asarray(_rope_perm(LANES, m_groups), BF16)
    selk = np.zeros((LANES, LANES), np.float32)
    for j in range(di):
        selk[dr + j, j] = 1.0
        selk[dr + j, di + j] = 1.0
    selw = np.zeros((LANES, LANES), np.float32)
    for j in range(HI):
        selw[dr + di + j, j] = 1.0
    wuk = jnp.concatenate([jnp.zeros((H, dr, dc), F32), jnp.transpose(w_uk, (0, 2, 1))], axis=1).astype(BF16)

    ts = _pick_tile(S, 512)
    p3 = proj.reshape(B, S, n_main)
    tab = lambda: pl.BlockSpec((ts, LANES), lambda b, j: (j, 0))
    mat = lambda: pl.BlockSpec((LANES, LANES), lambda b, j: (0, 0))
    qf, kvl, kvt, qir, kid, wit = pl.pallas_call(
        _dsa_prep_kernel,
        grid=(B, S // ts),
        in_specs=[pl.BlockSpec((None, ts, H * dh), lambda b, j: (b, j, 0)),
                  pl.BlockSpec((None, ts, HI * di), lambda b, j: (b, j, c_qi * LANES // (HI * di))),
                  pl.BlockSpec((None, ts, dc), lambda b, j: (b, j, c_ckv)),
                  pl.BlockSpec((None, ts, LANES), lambda b, j: (b, j, c_misc)),
                  tab(), tab(), tab(), tab(), tab(), tab(),
                  mat(), mat(), mat(), mat(), mat(),
                  pl.BlockSpec((H, LANES, dc), lambda b, j: (0, 0, 0)),
                  pl.BlockSpec((1, dc), lambda b, j: (0, 0))],
        out_specs=[pl.BlockSpec((None, H, ts, 2 * dc), lambda b, j: (b, 0, j, 0)),
                   pl.BlockSpec((None, ts, 2 * dc), lambda b, j: (b, j, 0)),
                   pl.BlockSpec((None, dc + DSA_ONES_ROWS, ts), lambda b, j: (b, 0, j)),
                   pl.BlockSpec((None, ts, HI * di), lambda b, j: (b, j, 0)),
                   pl.BlockSpec((None, ts, LANES), lambda b, j: (b, j, 0)),
                   pl.BlockSpec((None, HI, ts), lambda b, j: (b, 0, j))],
        out_shape=[jax.ShapeDtypeStruct((B, H, S, 2 * dc), BF16),
                   jax.ShapeDtypeStruct((B, S, 2 * dc), BF16),
                   jax.ShapeDtypeStruct((B, dc + DSA_ONES_ROWS, S), BF16),
                   jax.ShapeDtypeStruct((B, S, HI * di), BF16),
                   jax.ShapeDtypeStruct((B, S, LANES), BF16),
                   jax.ShapeDtypeStruct((B, HI, S), F32)],
        compiler_params=_cparams("parallel", "parallel"),
        name="dsa_prep",
    )(p3, p3, p3, p3, cq, sq, ci, si, cm, sm, pq, pi, pm,
      jnp.asarray(selk, BF16), jnp.asarray(selw, BF16), wuk, kv_norm_g.reshape(1, dc))

    tq = _pick_tile(S, tq)
    tk = _pick_tile(S, tk)
    k_sel = min(TOPK_MAX, S // 4)
    o = pl.pallas_call(
        functools.partial(_dsa_main_kernel, tq=tq, tk=tk, k_sel=k_sel, seq=S),
        grid=(B, S // tq),
        in_specs=[pl.BlockSpec((None, H, tq, 2 * dc), lambda b, i: (b, 0, i, 0)),
                  pl.BlockSpec((None, S, 2 * dc), lambda b, i: (b, 0, 0)),
                  pl.BlockSpec((None, dc + DSA_ONES_ROWS, S), lambda b, i: (b, 0, 0)),
                  pl.BlockSpec((None, tq, HI * di), lambda b, i: (b, i, 0)),
                  pl.BlockSpec((None, S, LANES), lambda b, i: (b, 0, 0)),
                  pl.BlockSpec((None, HI, tq), lambda b, i: (b, 0, i)),
                  pl.BlockSpec((H, dc, dh), lambda b, i: (0, 0, 0))],
        out_specs=pl.BlockSpec((None, tq, H * dh), lambda b, i: (b, i, 0)),
        out_shape=jax.ShapeDtypeStruct((B, S, H * dh), BF16),
        scratch_shapes=[pltpu.VMEM((S, tq), jnp.int32),
                        pltpu.VMEM((S, tq), jnp.int16), pltpu.VMEM((S, tq), jnp.int16),
                        pltpu.VMEM((S, tq), jnp.int16),
                        pltpu.VMEM((H, 1, tq), F32),
                        pltpu.VMEM((H, dc + DSA_ONES_ROWS, tq), F32)],
        compiler_params=_cparams("parallel", "arbitrary"),
        name="dsa_select_attention",
    )(qf, kvl, kvt, qir, kid, wit, w_uv.astype(BF16))
    return _outproj_ln(proj, 1, o.reshape(B * S, D), x, w_out, ln_g, ln_b)


RWKV_HEADS = 16
RWKV_HEAD_DIM = 64
RWKV_GN_EPS = 64e-5
RWKV_CHUNK = 64
RWKV_SUB = 16


def _group_sum(x, gmat, split=True):
    outs = []
    for c in range(x.shape[1] // LANES):
        xc = x[:, c * LANES:(c + 1) * LANES]
        if split:
            hi, lo = _split2(xc)
            outs.append(_dot(hi, gmat) + _dot(lo, gmat))
        else:
            outs.append(_dot(xc.astype(BF16), gmat))
    return outs[0] if len(outs) == 1 else jnp.concatenate(outs, axis=1)


def _softplus(y):
    return jnp.maximum(y, 0.0) + jnp.log(1.0 + jnp.exp(-jnp.abs(y)))


def _rwkv_proj_kernel(x_ref, xprev_ref, mu_ref, wr_ref, wk_ref, wv_ref, wg_ref, wla_ref, wlb_ref,
                      ala_ref, alb_ref, w0_ref, a0_ref, kk_ref, ka_ref, rk_ref, gmat_ref, tril_ref,
                      r_ref, k_ref, v_ref, g_ref, kap_ref, b_ref, cum_ref, bonus_ref):
    x = x_ref[...]
    ts = x.shape[0]
    prev = jnp.where(pl.program_id(1) == 0, 0.0, xprev_ref[7:8, :])
    rowid = lax.broadcasted_iota(jnp.int32, (ts, 1), 0)
    xx = jnp.where(rowid == 0, prev, pltpu.roll(x, 1, 0)) - x

    def mixed(i):
        return (x + xx * mu_ref[i:i + 1, :]).astype(BF16)

    r = _dot(mixed(0), wr_ref[...])
    k = _dot(mixed(2), wk_ref[...])
    v = _dot(mixed(3), wv_ref[...])
    g_ref[...] = _dot(mixed(5), wg_ref[...]).astype(g_ref.dtype)
    lora_w = _dot(jnp.tanh(_dot(mixed(1), wla_ref[...])).astype(BF16), wlb_ref[...])
    lora_a = _dot(_dot(mixed(4), ala_ref[...]).astype(BF16), alb_ref[...])
    w_log = -_softplus(-(w0_ref[...] + lora_w)) - 0.5
    lw = -jnp.exp(w_log)
    tril = tril_ref[...]
    for c in range(ts // RWKV_CHUNK):
        rows = slice(c * RWKV_CHUNK, (c + 1) * RWKV_CHUNK)
        cum_ref[rows, :] = sum(_dot(tril, piece) for piece in _split3(lw[rows, :]))
    a = _sigmoid(a0_ref[...] + lora_a)
    gmat = gmat_ref[...]
    kk = k * kk_ref[...]
    kap = kk * lax.rsqrt(_group_sum(kk * kk, gmat, split=False) + 1e-12)
    k2 = k * (1.0 + (a - 1.0) * ka_ref[...])
    bonus_ref[...] = _group_sum(r * k2 * rk_ref[...], gmat, split=False) * v
    r_ref[...] = r.astype(r_ref.dtype)
    k_ref[...] = k2.astype(k_ref.dtype)
    v_ref[...] = v.astype(v_ref.dtype)
    kap_ref[...] = kap.astype(kap_ref.dtype)
    b_ref[...] = (kap * a).astype(b_ref.dtype)


def _bd(x, left):
    z = jnp.zeros_like(x)
    return jnp.concatenate([jnp.where(left, x, z), jnp.where(left, z, x)], axis=0)


def _unbd(x_bd):
    c = x_bd.shape[0] // 2
    return x_bd[:c] + x_bd[c:]


def _rwkv_chunk_pair(L, r, k, v, kap, b, masks):
    C = RWKV_CHUNK
    left, strict, lower, same_sub, eye, first_row = masks
    Lc = L[C - 1:C, :]
    L_excl = jnp.where(first_row, 0.0, pltpu.roll(L, 1, 0))
    e_l, e_lx, e_nl, e_r = jnp.exp(L), jnp.exp(L_excl), jnp.exp(-L), jnp.exp(Lc - L)
    at = _bd(-kap * e_lx, left).astype(BF16)
    rt = _bd(r * e_l, left)
    bt = _bd(b * e_nl, left).astype(BF16)
    kt = _bd(k * e_nl, left).astype(BF16)
    bh = _bd(b * e_r, left).astype(BF16)
    kh = _bd(k * e_r, left).astype(BF16)
    vb = _bd(v, left).astype(BF16)

    a1 = _dot_nt(jnp.concatenate([at, rt.astype(BF16)], axis=0), jnp.concatenate([bt, kt], axis=0))
    yield None
    n = jnp.where(strict, a1[:2 * C, :2 * C], 0.0)
    ak = jnp.where(strict, a1[:2 * C, 2 * C:], 0.0).astype(BF16)
    rb = jnp.where(lower, a1[2 * C:, :2 * C], 0.0).astype(BF16)
    rk = jnp.where(lower, a1[2 * C:, 2 * C:], 0.0).astype(BF16)

    nd = jnp.where(same_sub, n, 0.0)
    no = (n - nd).astype(BF16)
    ndb = nd.astype(BF16)
    n2 = _dot(ndb, ndb)
    akv = _dot(ak, vb)
    yield None
    n2b = n2.astype(BF16)
    n4 = _dot(n2b, n2b)
    t01 = _dot((eye + nd).astype(BF16), (eye + n2).astype(BF16))
    yield None
    n4b = n4.astype(BF16)
    n8 = _dot(n4b, n4b)
    yield None
    t23 = _dot((eye + n4).astype(BF16), (eye + n8).astype(BF16))
    yield None
    tdb = _dot(t01.astype(BF16), t23.astype(BF16)).astype(BF16)
    yield None
    x1 = _dot(tdb, no)
    yield None
    x1b = x1.astype(BF16)
    x2 = _dot(x1b, x1b)
    yield None
    tx = _dot((eye + x1).astype(BF16), (eye + x2).astype(BF16))
    yield None
    t = _dot(tx.astype(BF16), tdb).astype(BF16)
    yield None
    pq = _dot(t, jnp.concatenate([at, akv.astype(BF16)], axis=1)).astype(BF16)
    yield None
    z = jnp.concatenate([pq, jnp.concatenate([jnp.zeros_like(vb), vb], axis=1)], axis=0)
    ry = _dot(jnp.concatenate([rb, rk], axis=1), z)
    mg = _dot_tn(z, jnp.concatenate([bh, kh], axis=0))
    yield None
    rp = rt + ry[:, :2 * C]
    yl = ry[:, 2 * C:]
    yield _unbd(rp), _unbd(yl), _unbd(mg[:2 * C]), _unbd(mg[2 * C:]), jnp.exp(Lc)


def _rwkv_chunk_kernel(cum_ref, r_ref, k_ref, v_ref, kap_ref, b_ref,
                       rp_ref, yl_ref, mm_ref, gg_ref, gam_ref, *, pairs, cpb):
    C = RWKV_CHUNK
    lane = lax.broadcasted_iota(jnp.int32, (1, LANES), 1)
    left = lane < RWKV_HEAD_DIM
    ri = lax.broadcasted_iota(jnp.int32, (2 * C, 2 * C), 0)
    ci = lax.broadcasted_iota(jnp.int32, (2 * C, 2 * C), 1)
    same_head = (ri // C) == (ci // C)
    strict = same_head & ((ri % C) > (ci % C))
    lower = same_head & ((ri % C) >= (ci % C))
    same_sub = (ri // RWKV_SUB) == (ci // RWKV_SUB)
    eye = jnp.where(ri == ci, 1.0, 0.0).astype(F32)
    first_row = lax.broadcasted_iota(jnp.int32, (C, 1), 0) == 0
    masks = (left, strict, lower, same_sub, eye, first_row)
    jobs = [(c, slice(c * C, (c + 1) * C), slice(p * LANES, (p + 1) * LANES))
            for c in range(cpb) for p in range(pairs)]
    results = _round_robin([
        _rwkv_chunk_pair(cum_ref[rows, sl], r_ref[rows, sl].astype(F32), k_ref[rows, sl].astype(F32),
                         v_ref[rows, sl].astype(F32), kap_ref[rows, sl].astype(F32),
                         b_ref[rows, sl].astype(F32), masks)
        for _, rows, sl in jobs])
    for (c, rows, sl), (rp, yl, mm, gg, gam) in zip(jobs, results):
        rp_ref[rows, sl] = rp.astype(rp_ref.dtype)
        yl_ref[rows, sl] = yl
        mm_ref[c, :, sl] = mm.astype(mm_ref.dtype)
        gg_ref[c, :, sl] = gg
        gam_ref[c, :, sl] = gam


def _rwkv_seq_kernel(rp_ref, yl_ref, mm_ref, gg_ref, gam_ref, bonus_ref, gmat_ref, gng_ref, gnb_ref,
                     o_ref, s_sc, y_sc, *, pairs, cb):
    C = RWKV_CHUNK
    lane = lax.broadcasted_iota(jnp.int32, (1, LANES), 1)
    left = lane < RWKV_HEAD_DIM

    @pl.when(pl.program_id(2) == 0)
    def _():
        s_sc[...] = jnp.zeros_like(s_sc)

    states = [s_sc[p] for p in range(pairs)]
    for c in range(cb):
        rows = slice(c * C, (c + 1) * C)
        for p in range(pairs):
            sl = slice(p * LANES, (p + 1) * LANES)
            s = states[p]
            sb = s.astype(BF16)
            y_sc[rows, sl] = _dot_nt(rp_ref[rows, sl], sb) + yl_ref[rows, sl]
            mm = _bd(mm_ref[c, :, sl], left)
            gg = _bd(gg_ref[c, :, sl], left)
            states[p] = s * gam_ref[c, :, sl] + _dot(sb, mm) + gg
    for p in range(pairs):
        s_sc[p] = states[p]

    y = y_sc[...]
    gmat = gmat_ref[...]
    inv_n = 1.0 / RWKV_HEAD_DIM
    yc = y - _group_sum(y, gmat) * inv_n
    var = _group_sum(yc * yc, gmat) * inv_n
    yn = yc * lax.rsqrt(var + RWKV_GN_EPS) * gng_ref[...] + gnb_ref[...]
    o_ref[...] = (yn + bonus_ref[...]).astype(o_ref.dtype)


def _rwkv_layer(x, xb, mu, w_in, w0, w_lora_a, w_lora_b, a0, a_lora_a, a_lora_b, k_k, k_a, r_k,
                gn_g, gn_b, w_out, ln_g, ln_b, B, S, ts=512, pairs=8, seq_pairs=8, chunks_per_step=2):
    D = D_MODEL
    C = RWKV_CHUNK
    nc = S // C
    ts = _pick_tile(S, ts)
    tril = jnp.tril(jnp.ones((C, C), BF16))
    w_r, w_k, w_v, w_g =[w.astype(BF16) for w in jnp.split(w_in, 4, axis=1)]
    gmat = jnp.asarray(np.kron(np.eye(2, dtype=np.float32), np.ones((RWKV_HEAD_DIM, RWKV_HEAD_DIM), np.float32)), BF16)
    row = lambda a: a.reshape(1, D)
    x3 = x.reshape(B, S, D)
    full = lambda shape: pl.BlockSpec(shape, lambda b, j: (0,) * len(shape))
    tile = lambda: pl.BlockSpec((None, ts, D), lambda b, j: (b, j, 0))
    lr = w_lora_a.shape[1]
    outs = pl.pallas_call(
        _rwkv_proj_kernel,
        grid=(B, S // ts),
        in_specs=[tile(),
                  pl.BlockSpec((None, 8, D), lambda b, j: (b, jnp.maximum(j * (ts // 8) - 1, 0), 0)),
                  full((6, D)), full((D, D)), full((D, D)), full((D, D)), full((D, D)),
                  full((D, lr)), full((lr, D)), full((D, lr)), full((lr, D)),
                  full((1, D)), full((1, D)), full((1, D)), full((1, D)), full((1, D)),
                  full((LANES, LANES)), full((C, C))],
        out_specs=[tile() for _ in range(8)],
        out_shape=[jax.ShapeDtypeStruct((B, S, D), dt) for dt in (BF16, BF16, BF16, BF16, BF16, BF16, F32, F32)],
        compiler_params=_cparams("parallel", "arbitrary"),
        name="rwkv_projections",
    )(x3, x3, mu, w_r, w_k, w_v, w_g, w_lora_a.astype(BF16), w_lora_b.astype(BF16),
      a_lora_a.astype(BF16), a_lora_b.astype(BF16), row(w0), row(a0), row(k_k), row(k_a), row(r_k), gmat, tril)
    r, k2, v, g, kap, bvec, cum, bonus = outs

    pw = pairs * LANES
    cpb = _pick_tile(nc, chunks_per_step)
    cblk = lambda: pl.BlockSpec((None, cpb * C, pw), lambda b, c, q: (b, c, q))
    sblk = lambda: pl.BlockSpec((None, cpb, C, pw), lambda b, c, q: (b, c, 0, q))
    rp, yl, mm, gg, gam = pl.pallas_call(
        functools.partial(_rwkv_chunk_kernel, pairs=pairs, cpb=cpb),
        grid=(B, nc // cpb, D // pw),
        in_specs=[cblk() for _ in range(6)],
        out_specs=[cblk(), cblk(), sblk(), sblk(),
                   pl.BlockSpec((None, cpb, 1, pw), lambda b, c, q: (b, c, 0, q))],
        out_shape=[jax.ShapeDtypeStruct((B, S, D), BF16), jax.ShapeDtypeStruct((B, S, D), F32),
                   jax.ShapeDtypeStruct((B, nc, C, D), BF16), jax.ShapeDtypeStruct((B, nc, C, D), F32),
                   jax.ShapeDtypeStruct((B, nc, 1, D), F32)],
        compiler_params=_cparams("parallel", "parallel", "parallel"),
        name="rwkv_chunk_summaries",
    )(cum, r, k2, v, kap, bvec)

    cb = _pick_tile(nc, 8)
    pairs = seq_pairs
    pw = pairs * LANES
    o = pl.pallas_call(
        functools.partial(_rwkv_seq_kernel, pairs=pairs, cb=cb),
        grid=(B, D // pw, nc // cb),
        in_specs=[pl.BlockSpec((None, cb * C, pw), lambda b, q, j: (b, j, q)),
                  pl.BlockSpec((None, cb * C, pw), lambda b, q, j: (b, j, q)),
                  pl.BlockSpec((None, cb, C, pw), lambda b, q, j: (b, j, 0, q)),
                  pl.BlockSpec((None, cb, C, pw), lambda b, q, j: (b, j, 0, q)),
                  pl.BlockSpec((None, cb, 1, pw), lambda b, q, j: (b, j, 0, q)),
                  pl.BlockSpec((None, cb * C, pw), lambda b, q, j: (b, j, q)),
                  pl.BlockSpec((LANES, LANES), lambda b, q, j: (0, 0)),
                  pl.BlockSpec((1, pw), lambda b, q, j: (0, q)),
                  pl.BlockSpec((1, pw), lambda b, q, j: (0, q))],
        out_specs=pl.BlockSpec((None, cb * C, pw), lambda b, q, j: (b, j, q)),
        out_shape=jax.ShapeDtypeStruct((B, S, D), BF16),
        scratch_shapes=[pltpu.VMEM((pairs, 2 * C, LANES), F32), pltpu.VMEM((cb * C, pw), F32)],
        compiler_params=_cparams("parallel", "parallel", "arbitrary"),
        name="rwkv_state_scan",
    )(rp, yl, mm, gg, gam, bonus, gmat, row(gn_g), row(gn_b))
    return _outproj_ln(g.reshape(B * S, D), 0, o.reshape(B * S, D), x, w_out, ln_g, ln_b)


def kernel(x, ln_g, ln_b, fox_w_in, fox_b_f, fox_w_out, dsa_w_in, dsa_kv_norm_g, dsa_w_uk, dsa_w_uv, dsa_w_out, rwkv_mu, rwkv_w_in, rwkv_w0, rwkv_w_lora_a, rwkv_w_lora_b, rwkv_a0, rwkv_a_lora_a, rwkv_a_lora_b, rwkv_k_k, rwkv_k_a, rwkv_r_k, rwkv_gn_g, rwkv_gn_b, rwkv_w_out, ret_w_in, ret_gn_g, ret_w_out):
    B, S, D = x.shape
    h = x.reshape(B * S, D)
    h, hb = _fox_layer(h, h, fox_w_in, fox_b_f, fox_w_out, ln_g[0], ln_b[0], B, S)
    h, hb = _dsa_layer(h, hb, dsa_w_in, dsa_kv_norm_g, dsa_w_uk, dsa_w_uv, dsa_w_out, ln_g[1], ln_b[1], B, S)
    h, hb = _rwkv_layer(h, hb, rwkv_mu, rwkv_w_in, rwkv_w0, rwkv_w_lora_a, rwkv_w_lora_b, rwkv_a0,
                        rwkv_a_lora_a, rwkv_a_lora_b, rwkv_k_k, rwkv_k_a, rwkv_r_k, rwkv_gn_g, rwkv_gn_b,
                        rwkv_w_out, ln_g[2], ln_b[2], B, S)
    h, hb = _ret_layer(h, hb, ret_w_in, ret_gn_g, ret_w_out, ln_g[3], ln_b[3], B, S)
    return h.reshape(B, S, D)
```

```python
import functools
import math

import jax
import jax.numpy as jnp
import numpy as np
from jax import lax
from jax.experimental import pallas as pl
from jax.experimental.pallas import tpu as pltpu

F32 = jnp.float32
BF16 = jnp.bfloat16

D_MODEL = 1024
DEPTH = 4
LN_EPS = 1e-5
RMS_EPS = 1e-6
DN_ALPHA = (2 * DEPTH) ** 0.25
ROPE_THETA = 500000.0

FOX_HEADS = 8
FOX_HEAD_DIM = 128

RET_HEADS = 4
RET_HEAD_DIM = 256
RET_THETA = 10000.0

LANES = 128
VMEM_LIMIT = 48 * 1024 * 1024
NEG_BIG = -2.0 ** 100
LOG2E = 1.4426950408889634


def _cparams(*sem, vmem_limit=VMEM_LIMIT):
    return pltpu.CompilerParams(dimension_semantics=sem, vmem_limit_bytes=vmem_limit)


def _dot(a, b):
    return jnp.dot(a, b, preferred_element_type=F32)


def _dot_nt(a, b):
    return lax.dot_general(a, b, (((1,), (1,)), ((), ())), preferred_element_type=F32)


def _dot_tn(a, b):
    return lax.dot_general(a, b, (((0,), (0,)), ((), ())), preferred_element_type=F32)


def _split2(x):
    hi = x.astype(BF16)
    lo = (x - hi.astype(F32)).astype(BF16)
    return hi, lo


def _split3(x):
    p1 = x.astype(BF16)
    r1 = x - p1.astype(F32)
    p2 = r1.astype(BF16)
    p3 = (r1 - p2.astype(F32)).astype(BF16)
    return p1, p2, p3


def _sigmoid(x):
    return 1.0 / (1.0 + jnp.exp(-x))


def _round_robin(gens):
    results = [None] * len(gens)
    live = list(range(len(gens)))
    while live:
        still = []
        for i in live:
            try:
                out = next(gens[i])
            except StopIteration:
                continue
            if out is not None:
                results[i] = out
            still.append(i)
        live = still
    return results


def _pick_tile(n, pref):
    t = min(n, pref)
    while n % t:
        t //= 2
    return t


def _mm_kernel(a_ref, w_ref, o_ref):
    o_ref[...] = _dot(a_ref[...].astype(BF16), w_ref[...]).astype(o_ref.dtype)


def _matmul(a, w, out_dtype=BF16, tm=2048, tn=1024):
    M, K = a.shape
    N = w.shape[1]
    tm = _pick_tile(M, tm)
    if N % tn:
        tn = N
    return pl.pallas_call(
        _mm_kernel,
        grid=(M // tm, N // tn),
        in_specs=[pl.BlockSpec((tm, K), lambda i, j: (i, 0)),
                  pl.BlockSpec((K, tn), lambda i, j: (0, j))],
        out_specs=pl.BlockSpec((tm, tn), lambda i, j: (i, j)),
        out_shape=jax.ShapeDtypeStruct((M, N), out_dtype),
        compiler_params=_cparams("parallel", "arbitrary"),
        name="proj_matmul",
    )(a, w)


def _outproj_ln_kernel(g_ref, o_ref, x_ref, w_ref, lg_ref, lb_ref, xo_ref, xb_ref):
    half_g = g_ref[...] * 0.5
    h = (half_g * o_ref[...]) * (1.0 + jnp.tanh(half_g))
    z = x_ref[...] + _dot(h.astype(BF16), w_ref[...])
    zc = z - jnp.mean(z, axis=-1, keepdims=True)
    var = jnp.mean(zc * zc, axis=-1, keepdims=True)
    out = zc * lax.rsqrt(var + LN_EPS / DN_ALPHA ** 2) * lg_ref[...] + lb_ref[...]
    xo_ref[...] = out
    xb_ref[...] = out.astype(BF16)


def _outproj_ln(gate_arr, gate_col, o, x, w_out, ln_g, ln_b, tm=1024):
    M, D = x.shape
    tm = _pick_tile(M, tm)
    return pl.pallas_call(
        _outproj_ln_kernel,
        grid=(M // tm,),
        in_specs=[pl.BlockSpec((tm, D), lambda i: (i, gate_col)),
                  pl.BlockSpec((tm, D), lambda i: (i, 0)),
                  pl.BlockSpec((tm, D), lambda i: (i, 0)),
                  pl.BlockSpec((D, D), lambda i: (0, 0)),
                  pl.BlockSpec((1, D), lambda i: (0, 0)),
                  pl.BlockSpec((1, D), lambda i: (0, 0))],
        out_specs=[pl.BlockSpec((tm, D), lambda i: (i, 0)),
                   pl.BlockSpec((tm, D), lambda i: (i, 0))],
        out_shape=[jax.ShapeDtypeStruct((M, D), F32), jax.ShapeDtypeStruct((M, D), BF16)],
        compiler_params=_cparams("parallel"),
        name="outproj_layernorm",
    )(gate_arr, o, x, (w_out * (1.0 / DN_ALPHA)).astype(BF16), ln_g.reshape(1, D), ln_b.reshape(1, D))


FOX_BIAS_PIECES = 3


def _fox_cum_kernel(x_ref, wh_ref, wl_ref, bf_ref, tril_ref, place_ref, pc_ref, carry_sc):
    @pl.when(pl.program_id(1) == 0)
    def _():
        carry_sc[...] = jnp.zeros_like(carry_sc)

    x_hi, x_lo = _split2(x_ref[...])
    z = _dot(x_hi, wh_ref[...]) + _dot(x_lo, wh_ref[...]) + _dot(x_hi, wl_ref[...]) + bf_ref[...]
    logf = jnp.minimum(z, 0.0) - jnp.log(1.0 + jnp.exp(-jnp.abs(z)))
    p1, p2, p3 = _split3(logf)
    tril = tril_ref[...]
    c = _dot(tril, p1) + _dot(tril, p2) + _dot(tril, p3) + carry_sc[...]
    carry_sc[...] = c[c.shape[0] - 1:, :]
    pieces = _split3(c * (-LOG2E))
    pc_ref[...] = sum(_dot(pieces[p], place_ref[p]) for p in range(FOX_BIAS_PIECES)).astype(pc_ref.dtype)


def _fox_cum(x3, w_f, b_f, ts=512):
    B, S, D = x3.shape
    H = w_f.shape[1]
    ts = _pick_tile(S, ts)
    w_pad = jnp.zeros((D, LANES), F32).at[:, :H].set(w_f)
    w_hi, w_lo = _split2(w_pad)
    b_pad = jnp.zeros((1, LANES), F32).at[0, :H].set(b_f)
    tril = jnp.tril(jnp.ones((ts, ts), BF16))
    place = np.zeros((FOX_BIAS_PIECES, LANES, LANES), np.float32)
    for p in range(FOX_BIAS_PIECES):
        for h in range(H):
            place[p, h, FOX_BIAS_PIECES * h + p] = 1.0
    return pl.pallas_call(
        _fox_cum_kernel,
        grid=(B, S // ts),
        in_specs=[pl.BlockSpec((None, ts, D), lambda b, j: (b, j, 0)),
                  pl.BlockSpec((D, LANES), lambda b, j: (0, 0)),
                  pl.BlockSpec((D, LANES), lambda b, j: (0, 0)),
                  pl.BlockSpec((1, LANES), lambda b, j: (0, 0)),
                  pl.BlockSpec((ts, ts), lambda b, j: (0, 0)),
                  pl.BlockSpec((FOX_BIAS_PIECES, LANES, LANES), lambda b, j: (0, 0, 0))],
        out_specs=pl.BlockSpec((None, ts, LANES), lambda b, j: (b, j, 0)),
        out_shape=jax.ShapeDtypeStruct((B, S, LANES), BF16),
        scratch_shapes=[pltpu.VMEM((1, LANES), F32)],
        compiler_params=_cparams("parallel", "arbitrary"),
        name="fox_decay_cumsum",
    )(x3, w_hi, w_lo, b_pad, tril, jnp.asarray(place, BF16))


ONES_ROWS = 16


def _fox_attn_kernel(q_ref, k_ref, v_ref, pc_ref, o_ref, kaug_sc, vt_sc, m_sc, acc_sc, *, tq, nsub, seq, unroll):
    dh = FOX_HEAD_DIM
    h = pl.program_id(1)
    g = pl.program_id(2)

    @pl.when(g == 0)
    def _():
        kaug_sc[:, :dh] = k_ref[...]
        kaug_sc[:, dh:] = pc_ref[...]
        for c in range(seq // tq):
            rows = slice(c * tq, (c + 1) * tq)
            vt_sc[:dh, rows] = v_ref[rows, :].astype(F32).T.astype(BF16)
        vt_sc[dh:, :] = jnp.ones((ONES_ROWS, seq), BF16)

    lane = lax.broadcasted_iota(jnp.int32, (tq, LANES), 1)
    bias_lanes = (lane >= FOX_BIAS_PIECES * h) & (lane < FOX_BIAS_PIECES * (h + 1))
    ones_h = jnp.where(bias_lanes, 1.0, 0.0).astype(BF16)
    q_aug = [jnp.concatenate([q_ref[a * tq:(a + 1) * tq, :], ones_h], axis=1) for a in range(nsub)]
    m_sc[...] = jnp.full_like(m_sc, NEG_BIG)
    acc_sc[...] = jnp.zeros_like(acc_sc)
    causal = (lax.broadcasted_iota(jnp.int32, (tq, tq), 0) <= lax.broadcasted_iota(jnp.int32, (tq, tq), 1))
    first = g * nsub

    def chain(a, tiles, diag_last):
        offs = [pl.multiple_of(j * tq, tq) for j in tiles]
        scores = []
        for off in offs:
            scores.append(_dot_nt(kaug_sc[pl.ds(off, tq), :], q_aug[a]))
            yield None
        for n, (off, s) in enumerate(zip(offs, scores)):
            if diag_last and n == len(offs) - 1:
                s = jnp.where(causal, s, NEG_BIG)
            m_old = m_sc[a]
            m_new = jnp.maximum(m_old, jnp.max(s, axis=0, keepdims=True))
            alpha = jnp.exp2(m_old - m_new)
            pv = _dot(vt_sc[:, pl.ds(off, tq)], jnp.exp2(s - m_new).astype(BF16))
            yield None
            acc_sc[a] = alpha * acc_sc[a] + pv
            m_sc[a] = m_new
        yield None

    def body(jj, c):
        _round_robin([chain(a, [jj * unroll + u for u in range(unroll)], False) for a in range(nsub)])
        return c

    lax.fori_loop(0, first // unroll, body, 0)
    _round_robin([chain(a, [first + t for t in range(a + 1)], True) for a in range(nsub)])
    for a in range(nsub):
        acc = acc_sc[a]
        o_t = acc[:dh] / acc[dh:dh + 1]
        o_ref[a * tq:(a + 1) * tq, :] = o_t.T.astype(o_ref.dtype)


def _fox_attention(proj3, pieces, tq=256, nsub=8):
    B, S, _ = proj3.shape
    H, dh = FOX_HEADS, FOX_HEAD_DIM
    tq = _pick_tile(S, tq)
    nsub = _pick_tile(S // tq, nsub)
    tg = tq * nsub
    return pl.pallas_call(
        functools.partial(_fox_attn_kernel, tq=tq, nsub=nsub, seq=S, unroll=min(nsub, 4)),
        grid=(B, H, S // tg),
        in_specs=[pl.BlockSpec((None, tg, dh), lambda b, h, g: (b, g, h)),
                  pl.BlockSpec((None, S, dh), lambda b, h, g: (b, 0, H + h)),
                  pl.BlockSpec((None, S, dh), lambda b, h, g: (b, 0, 2 * H + h)),
                  pl.BlockSpec((None, S, LANES), lambda b, h, g: (b, 0, 0))],
        out_specs=pl.BlockSpec((None, tg, dh), lambda b, h, g: (b, g, h)),
        out_shape=jax.ShapeDtypeStruct((B, S, H * dh), BF16),
        scratch_shapes=[pltpu.VMEM((S, dh + LANES), BF16),
                        pltpu.VMEM((dh + ONES_ROWS, S), BF16),
                        pltpu.VMEM((nsub, 1, tq), F32),
                        pltpu.VMEM((nsub, dh + ONES_ROWS, tq), F32)],
        compiler_params=_cparams("parallel", "parallel", "arbitrary"),
        name="fox_attention",
    )(proj3, proj3, proj3, pieces)


def _fox_layer(x, xb, w_in, b_f, w_out, ln_g, ln_b, B, S):
    D = D_MODEL
    H, dh = FOX_HEADS, FOX_HEAD_DIM
    scale = dh ** -0.5 * LOG2E
    w_q, w_k, w_v, w_f, w_g = jnp.split(w_in, [H * dh, 2 * H * dh, 3 * H * dh, 3 * H * dh + H], axis=1)
    w_main = jnp.concatenate([w_q * scale, w_k, w_v, w_g], axis=1).astype(BF16)
    proj = _matmul(xb, w_main)
    pieces = _fox_cum(x.reshape(B, S, D), w_f, b_f)
    o = _fox_attention(proj.reshape(B, S, 4 * D), pieces)
    return _outproj_ln(proj, 3, o.reshape(B * S, D), x, w_out, ln_g, ln_b)


def _ret_kernel(q_ref, k_ref, v_ref, cos_ref, sin_ref, dm_ref, xi_ref, zeta_ref, gc_ref, gn_ref,
                o_ref, r_sc):
    @pl.when(pl.program_id(1) == 0)
    def _():
        r_sc[...] = jnp.zeros_like(r_sc)

    dk = RET_HEAD_DIM
    half = dk // 2
    cos = cos_ref[...]
    sin = sin_ref[...]

    def rope(x):
        x1, x2 = x[:, :half], x[:, half:]
        return jnp.concatenate([x1 * cos - x2 * sin, x2 * cos + x1 * sin], axis=-1)

    def head_chain(h):
        cols = slice(h * dk, (h + 1) * dk)
        q = rope(q_ref[:, cols].astype(F32))
        k = rope(k_ref[:, cols].astype(F32)) * (dk ** -0.5)
        v = v_ref[:, cols]
        qb = q.astype(BF16)
        r_old = r_sc[h]
        scores = _dot_nt(qb, k.astype(BF16))
        cross = _dot(qb, r_old.astype(BF16))
        kz = (k * zeta_ref[h]).astype(BF16)
        r_new = _dot_tn(kz, v)
        yield None
        o = _dot((scores * dm_ref[h]).astype(BF16), v)
        r_sc[h] = r_old * gc_ref[h] + r_new
        yield None
        o = o + cross * xi_ref[h]
        o = o * lax.rsqrt(jnp.mean(o * o, axis=-1, keepdims=True) + RMS_EPS) * gn_ref[:, cols]
        o_ref[:, cols] = o.astype(o_ref.dtype)
        yield None

    _round_robin([head_chain(h) for h in range(RET_HEADS)])


def _ret_layer(x, xb, w_in, gn_g, w_out, ln_g, ln_b, B, S, chunk=512):
    D = D_MODEL
    H, dk = RET_HEADS, RET_HEAD_DIM
    C = _pick_tile(S, chunk)
    proj = _matmul(xb, w_in.astype(BF16))
    inv = 1.0 / (RET_THETA ** (jnp.arange(0, dk, 2, dtype=F32) / dk))
    ang = jnp.arange(S, dtype=F32)[:, None] * inv[None, :]
    cos, sin = jnp.cos(ang), jnp.sin(ang)
    log_g = jnp.log1p(-(2.0 ** (-5.0 - jnp.arange(H, dtype=F32))))
    pos = jnp.arange(C, dtype=F32)
    diff = pos[:, None] - pos[None, :]
    d_mask = jnp.where(diff[None] >= 0, jnp.exp(jnp.maximum(diff, 0.0)[None] * log_g[:, None, None]), 0.0)
    xi = jnp.broadcast_to(jnp.exp((pos[None, :] + 1.0) * log_g[:, None])[:, :, None], (H, C, dk))
    zeta = jnp.broadcast_to(jnp.exp((C - 1.0 - pos[None, :]) * log_g[:, None])[:, :, None], (H, C, dk))
    g_c = jnp.broadcast_to(jnp.exp(C * log_g)[:, None, None], (H, 1, dk))
    p3 = proj.reshape(B, S, 4 * D)
    o = pl.pallas_call(
        _ret_kernel,
        grid=(B, S // C),
        in_specs=[pl.BlockSpec((None, C, D), lambda b, c: (b, c, 0)),
                  pl.BlockSpec((None, C, D), lambda b, c: (b, c, 1)),
                  pl.BlockSpec((None, C, D), lambda b, c: (b, c, 2)),
                  pl.BlockSpec((C, dk // 2), lambda b, c: (c, 0)),
                  pl.BlockSpec((C, dk // 2), lambda b, c: (c, 0)),
                  pl.BlockSpec((H, C, C), lambda b, c: (0, 0, 0)),
                  pl.BlockSpec((H, C, dk), lambda b, c: (0, 0, 0)),
                  pl.BlockSpec((H, C, dk), lambda b, c: (0, 0, 0)),
                  pl.BlockSpec((H, 1, dk), lambda b, c: (0, 0, 0)),
                  pl.BlockSpec((1, D), lambda b, c: (0, 0))],
        out_specs=pl.BlockSpec((None, C, D), lambda b, c: (b, c, 0)),
        out_shape=jax.ShapeDtypeStruct((B, S, D), BF16),
        scratch_shapes=[pltpu.VMEM((H, dk, dk), F32)],
        compiler_params=_cparams("parallel", "arbitrary"),
        name="retnet_retention",
    )(p3, p3, p3, cos, sin, d_mask, xi, zeta, g_c, gn_g.reshape(1, D))
    return _outproj_ln(proj, 3, o.reshape(B * S, D), x, w_out, ln_g, ln_b)


DSA_HEADS = 8
DSA_HEAD_DIM = 128
DSA_ROPE_DIM = 32
DSA_KV_RANK = 128
IDX_HEADS = 8
IDX_DIM = 64
IDX_ROPE_DIM = 16
TOPK_MAX = 256
INT_MIN = -2 ** 31
HALF_MIN = -2 ** 15
HALF_ROWS = 16
DSA_VMEM_LIMIT = 56 * 1024 * 1024
DSA_ONES_ROWS = 16


def _rope_perm(width, groups):
    p = np.zeros((width, width), np.float32)
    for start, half in groups:
        for j in range(half):
            p[start + half + j, start + j] = 1.0
            p[start + j, start + half + j] = 1.0
    return p


def _rope_cs(S, width, groups, theta_dims):
    c = jnp.ones((S, width), F32)
    sg = jnp.zeros((S, width), F32)
    pos = jnp.arange(S, dtype=F32)[:, None]
    for (start, half), rot_dim in zip(groups, theta_dims):
        inv = 1.0 / (ROPE_THETA ** (jnp.arange(0, rot_dim, 2, dtype=F32) / rot_dim))
        ang = pos * inv[None, :]
        cos, sin = jnp.cos(ang), jnp.sin(ang)
        c = c.at[:, start:start + half].set(cos).at[:, start + half:start + 2 * half].set(cos)
        sg = sg.at[:, start:start + half].set(-sin).at[:, start + half:start + 2 * half].set(sin)
    return c, sg


def _dsa_prep_kernel(q_ref, qi_ref, ckv_ref, misc_ref, cq_ref, sq_ref, ci_ref, si_ref, cm_ref, sm_ref,
                     pq_ref, pi_ref, pm_ref, selk_ref, selw_ref, wuk_ref, kvg_ref,
                     qf_ref, kvl_ref, kvt_ref, qir_ref, kid_ref, wit_ref):
    H, dh = DSA_HEADS, DSA_HEAD_DIM
    lane = lax.broadcasted_iota(jnp.int32, (1, LANES), 1)
    rope_lanes = lane < DSA_ROPE_DIM
    cq, sq = cq_ref[...], sq_ref[...]
    scale = dh ** -0.5 * LOG2E
    for h in range(H):
        qh = q_ref[:, h * dh:(h + 1) * dh]
        qr = qh.astype(F32) * cq + _dot(qh, pq_ref[...]) * sq
        q_lat = _dot(qr.astype(BF16), wuk_ref[h])
        qf_ref[h, :, :dh] = (q_lat * scale).astype(BF16)
        qf_ref[h, :, dh:] = jnp.where(rope_lanes, qr * scale, 0.0).astype(BF16)
    ci, si = ci_ref[...], si_ref[...]
    for g in range(IDX_HEADS * IDX_DIM // LANES):
        qg = qi_ref[:, g * LANES:(g + 1) * LANES]
        qr = qg.astype(F32) * ci + _dot(qg, pi_ref[...]) * si
        qir_ref[:, g * LANES:(g + 1) * LANES] = (qr * (IDX_DIM ** -0.5)).astype(BF16)
    ckv = ckv_ref[...].astype(F32)
    ckv = ckv * lax.rsqrt(jnp.mean(ckv * ckv, axis=-1, keepdims=True) + RMS_EPS) * kvg_ref[...]
    misc = misc_ref[...]
    mr = (misc.astype(F32) * cm_ref[...] + _dot(misc, pm_ref[...]) * sm_ref[...])
    kvl_ref[:, :DSA_KV_RANK] = ckv.astype(BF16)
    kvl_ref[:, DSA_KV_RANK:] = jnp.where(rope_lanes, mr, 0.0).astype(BF16)
    kvt_ref[:DSA_KV_RANK, :] = ckv.T.astype(BF16)
    kvt_ref[DSA_KV_RANK:, :] = jnp.ones((DSA_ONES_ROWS, ckv.shape[0]), BF16)
    kid_ref[...] = _dot(mr.astype(BF16), selk_ref[...]).astype(BF16)
    wi = _dot(misc, selw_ref[...]) * (IDX_HEADS ** -0.5)
    wit_ref[...] = wi.T[:IDX_HEADS, :]


def _sort_key(x):
    b = pltpu.bitcast(x, jnp.int32)
    return jnp.where(b < 0, b ^ 0x7FFFFFFF, b)


def _dsa_main_kernel(qf_ref, kvl_ref, kvt_ref, qi_ref, kid_ref, wit_ref, wuv_ref, obuf_ref, o_ref,
                     key_sc, hi_sc, lo_sc, low_sc, m_sc, acc_sc, *, tq, tk, k_sel, seq, qtile):
    del obuf_ref
    H = DSA_HEADS
    q0 = qtile * tq
    nj = (q0 + tq + tk - 1) // tk
    qpos = q0 + lax.broadcasted_iota(jnp.int32, (tk, tq), 1)
    kpos0 = lax.broadcasted_iota(jnp.int32, (tk, tq), 0)
    lane = lax.broadcasted_iota(jnp.int32, (1, LANES), 1)
    neg_inf_key = _sort_key(jnp.full((1, 1), -jnp.inf, F32))

    wit = wit_ref[...]
    qi_heads = []
    for h in range(IDX_HEADS):
        g = qi_ref[:, (h // 2) * LANES:(h // 2 + 1) * LANES]
        keep = (lane >= IDX_DIM) if (h % 2) else (lane < IDX_DIM)
        qi_heads.append(jnp.where(keep, g, jnp.zeros_like(g)))

    def score_tile(j, c):
        off = pl.multiple_of(j * tk, tk)
        ki = kid_ref[pl.ds(off, tk), :]
        scores = [_dot_nt(ki, qi_heads[h]) for h in range(IDX_HEADS)]
        isc = jnp.zeros((tk, tq), F32)
        for h in range(IDX_HEADS):
            isc = isc + jnp.maximum(scores[h], 0.0) * wit[h:h + 1, :]
        isc = jnp.where(kpos0 + off <= qpos, isc + 0.0, -jnp.inf)
        key = _sort_key(isc)
        key_sc[pl.ds(off, tk), :] = key
        hi_sc[pl.ds(off, tk), :] = (key >> 16).astype(jnp.int16)
        lo_sc[pl.ds(off, tk), :] = ((key & 0xFFFF) + HALF_MIN).astype(jnp.int16)
        return c

    lax.fori_loop(0, nj, score_tile, 0)

    def count(pred_fn):
        acc = jnp.zeros((8, tq), jnp.int32)
        for j in range(nj):
            hit = jnp.where(pred_fn(key_sc[j * tk:(j + 1) * tk, :], j * tk), 1, 0)
            acc = acc + jnp.sum(hit.reshape(tk // 8, 8, tq), axis=0)
        return jnp.sum(acc, axis=0, keepdims=True)

    rows16 = tk // HALF_ROWS

    def count16(ref, pred_fn):
        accs = [jnp.zeros((HALF_ROWS, tq), jnp.int16) for _ in range(2)]
        for j in range(nj):
            hit = jnp.where(pred_fn(ref[j * tk:(j + 1) * tk, :].reshape(rows16, HALF_ROWS, tq)),
                            jnp.int16(1), jnp.int16(0))
            for r in range(rows16):
                accs[r % 2] = accs[r % 2] + hit[r]
        return jnp.sum((accs[0] + accs[1]).astype(jnp.int32), axis=0, keepdims=True)

    def as_half(v):
        return jnp.broadcast_to(v.astype(jnp.int16), (HALF_ROWS, tq))[None]

    def search16(ref, base0, cnt0, bits, want):
        def bit_step(t, carry):
            base, cnt_b = carry
            cand = base + lax.shift_left(jnp.int32(1), bits - 1 - t)
            cand16 = as_half(cand)
            c = count16(ref, lambda kt: kt >= cand16)
            ok = c >= want
            return jnp.where(ok, cand, base), jnp.where(ok, c, cnt_b)
        return lax.fori_loop(0, bits, bit_step, (base0, cnt0))

    zero16 = as_half(jnp.zeros((1, tq), jnp.int32))
    cnt_pos = count16(hi_sc, lambda kt: kt >= zero16)
    nonneg = cnt_pos >= k_sel
    t1, ge_hi = search16(hi_sc, jnp.where(nonneg, 0, HALF_MIN), jnp.where(nonneg, cnt_pos, nj * tk), 15, k_sel)
    t1_16 = as_half(t1)
    above = count16(hi_sc, lambda kt: kt > t1_16)

    for j in range(nj):
        rows = slice(j * tk, (j + 1) * tk)
        hi = hi_sc[rows, :].reshape(rows16, HALF_ROWS, tq)
        lo = lo_sc[rows, :].reshape(rows16, HALF_ROWS, tq)
        low_sc[rows, :] = jnp.where(hi == t1_16, lo, jnp.int16(HALF_MIN)).reshape(tk, tq)
    t2, ge_low = search16(low_sc, jnp.full((1, tq), HALF_MIN, jnp.int32), ge_hi - above, 16, k_sel - above)
    t2_16 = as_half(t2)
    thr = lax.shift_left(t1, 16) | (t2 - HALF_MIN)
    n_ge = above + ge_low
    n_gt = above + count16(low_sc, lambda kt: kt > t2_16)
    need = k_sel - n_gt
    excess = ((n_ge - n_gt) > need) & (thr > neg_inf_key)
    any_excess = jnp.max(jnp.where(excess, 1, 0)) > 0

    def tie_cut():
        def step(t, lo):
            cand = lo + lax.shift_left(jnp.int32(1), int(math.log2(seq)) - t)
            c = count(lambda kt, off: (kt == thr) & (kpos0 + off < cand))
            return jnp.where(c < need, cand, lo)
        lo = lax.fori_loop(0, int(math.log2(seq)) + 1, step, jnp.zeros((1, tq), jnp.int32))
        return jnp.where(excess, lo, seq)

    cut = lax.cond(any_excess, tie_cut, lambda: jnp.full((1, tq), seq, jnp.int32))

    m_sc[...] = jnp.full_like(m_sc, NEG_BIG)
    acc_sc[...] = jnp.zeros_like(acc_sc)

    def attn_tile(j, c):
        off = pl.multiple_of(j * tk, tk)
        kt = key_sc[pl.ds(off, tk), :]
        kpos = kpos0 + off
        bias = jnp.where(kt > thr, 0.0, jnp.where(kt == thr, jnp.where(kpos <= cut, 0.0, NEG_BIG), NEG_BIG))
        bias = jnp.where(kpos <= qpos, bias, NEG_BIG)
        kv = kvl_ref[pl.ds(off, tk), :]
        kvt = kvt_ref[:, pl.ds(off, tk)]

        def head_step(h):
            s = _dot_nt(kv, qf_ref[h]) + bias
            yield None
            m_old = m_sc[h]
            m_new = jnp.maximum(m_old, jnp.max(s, axis=0, keepdims=True))
            alpha = jnp.exp2(m_old - m_new)
            pv = _dot(kvt, jnp.exp2(s - m_new).astype(BF16))
            yield None
            acc_sc[h] = alpha * acc_sc[h] + pv
            m_sc[h] = m_new
            yield None

        _round_robin([head_step(h) for h in range(H)])
        return c

    lax.fori_loop(0, nj, attn_tile, 0)
    for h in range(H):
        acc = acc_sc[h]
        o_lat_t = (acc[:DSA_KV_RANK] / acc[DSA_KV_RANK:DSA_KV_RANK + 1]).astype(BF16)
        o_ref[:, h * DSA_HEAD_DIM:(h + 1) * DSA_HEAD_DIM] = _dot_tn(o_lat_t, wuv_ref[h]).astype(o_ref.dtype)


def _dsa_layer(x, xb, w_in, kv_norm_g, w_uk, w_uv, w_out, ln_g, ln_b, B, S, tq=512, tk=512):
    D = D_MODEL
    H, dh, dr, dc = DSA_HEADS, DSA_HEAD_DIM, DSA_ROPE_DIM, DSA_KV_RANK
    HI, di = IDX_HEADS, IDX_DIM
    w_q, w_ckv, w_kr, w_qi, w_ki, w_wi, w_g = jnp.split(
        w_in, np.cumsum([H * dh, dc, dr, HI * di, di, HI]).tolist(), axis=1)
    w_misc = jnp.concatenate([w_kr, w_ki, w_wi, jnp.zeros((D, LANES - dr - di - HI), F32)], axis=1)
    w_main = jnp.concatenate([w_q, w_g, w_qi, w_ckv, w_misc], axis=1).astype(BF16)
    n_main = w_main.shape[1]
    proj = _matmul(xb, w_main, tm=1024, tn=n_main)
    c_q, c_qi, c_ckv, c_misc = 0, 2 * D // LANES, (2 * D + HI * di) // LANES, (2 * D + HI * di + dc) // LANES

    q_groups = [(0, dr // 2)]
    i_groups = [(0, IDX_ROPE_DIM // 2), (di, IDX_ROPE_DIM // 2)]
    m_groups = [(0, dr // 2), (dr, IDX_ROPE_DIM // 2)]
    cq, sq = _rope_cs(S, LANES, q_groups, [dr])
    ci, si = _rope_cs(S, LANES, i_groups, [IDX_ROPE_DIM, IDX_ROPE_DIM])
    cm, sm = _rope_cs(S, LANES, m_groups, [dr, IDX_ROPE_DIM])
    pq = jnp.asarray(_rope_perm(LANES, q_groups), BF16)
    pi = jnp.asarray(_rope_perm(LANES, i_groups), BF16)
    pm = jnp.asarray(_rope_perm(LANES, m_groups), BF16)
    selk = np.zeros((LANES, LANES), np.float32)
    for j in range(di):
        selk[dr + j, j] = 1.0
        selk[dr + j, di + j] = 1.0
    selw = np.zeros((LANES, LANES), np.float32)
    for j in range(HI):
        selw[dr + di + j, j] = 1.0
    wuk = jnp.concatenate([jnp.zeros((H, dr, dc), F32), jnp.transpose(w_uk, (0, 2, 1))], axis=1).astype(BF16)

    ts = _pick_tile(S, 512)
    p3 = proj.reshape(B, S, n_main)
    tab = lambda: pl.BlockSpec((ts, LANES), lambda b, j: (j, 0))
    mat = lambda: pl.BlockSpec((LANES, LANES), lambda b, j: (0, 0))
    qf, kvl, kvt, qir, kid, wit = pl.pallas_call(
        _dsa_prep_kernel,
        grid=(B, S // ts),
        in_specs=[pl.BlockSpec((None, ts, H * dh), lambda b, j: (b, j, 0)),
                  pl.BlockSpec((None, ts, HI * di), lambda b, j: (b, j, c_qi * LANES // (HI * di))),
                  pl.BlockSpec((None, ts, dc), lambda b, j: (b, j, c_ckv)),
                  pl.BlockSpec((None, ts, LANES), lambda b, j: (b, j, c_misc)),
                  tab(), tab(), tab(), tab(), tab(), tab(),
                  mat(), mat(), mat(), mat(), mat(),
                  pl.BlockSpec((H, LANES, dc), lambda b, j: (0, 0, 0)),
                  pl.BlockSpec((1, dc), lambda b, j: (0, 0))],
        out_specs=[pl.BlockSpec((None, H, ts, 2 * dc), lambda b, j: (b, 0, j, 0)),
                   pl.BlockSpec((None, ts, 2 * dc), lambda b, j: (b, j, 0)),
                   pl.BlockSpec((None, dc + DSA_ONES_ROWS, ts), lambda b, j: (b, 0, j)),
                   pl.BlockSpec((None, ts, HI * di), lambda b, j: (b, j, 0)),
                   pl.BlockSpec((None, ts, LANES), lambda b, j: (b, j, 0)),
                   pl.BlockSpec((None, HI, ts), lambda b, j: (b, 0, j))],
        out_shape=[jax.ShapeDtypeStruct((B, H, S, 2 * dc), BF16),
                   jax.ShapeDtypeStruct((B, S, 2 * dc), BF16),
                   jax.ShapeDtypeStruct((B, dc + DSA_ONES_ROWS, S), BF16),
                   jax.ShapeDtypeStruct((B, S, HI * di), BF16),
                   jax.ShapeDtypeStruct((B, S, LANES), BF16),
                   jax.ShapeDtypeStruct((B, HI, S), F32)],
        compiler_params=_cparams("parallel", "parallel"),
        name="dsa_prep",
    )(p3, p3, p3, p3, cq, sq, ci, si, cm, sm, pq, pi, pm,
      jnp.asarray(selk, BF16), jnp.asarray(selw, BF16), wuk, kv_norm_g.reshape(1, dc))

    tq = _pick_tile(S, tq)
    tk = _pick_tile(S, tk)
    k_sel = min(TOPK_MAX, S // 4)
    wuv = w_uv.astype(BF16)
    o = jnp.zeros((B, S, H * dh), BF16)
    for i in range(S // tq):
        nkeys = -(-((i + 1) * tq) // tk) * tk
        o = pl.pallas_call(
            functools.partial(_dsa_main_kernel, tq=tq, tk=tk, k_sel=k_sel, seq=S, qtile=i),
            grid=(B,),
            in_specs=[pl.BlockSpec((None, H, tq, 2 * dc), lambda b, i=i: (b, 0, i, 0)),
                      pl.BlockSpec((None, nkeys, 2 * dc), lambda b: (b, 0, 0)),
                      pl.BlockSpec((None, dc + DSA_ONES_ROWS, nkeys), lambda b: (b, 0, 0)),
                      pl.BlockSpec((None, tq, HI * di), lambda b, i=i: (b, i, 0)),
                      pl.BlockSpec((None, nkeys, LANES), lambda b: (b, 0, 0)),
                      pl.BlockSpec((None, HI, tq), lambda b, i=i: (b, 0, i)),
                      pl.BlockSpec((H, dc, dh), lambda b: (0, 0, 0)),
                      pl.BlockSpec(memory_space=pl.ANY)],
            out_specs=pl.BlockSpec((None, tq, H * dh), lambda b, i=i: (b, i, 0)),
            out_shape=jax.ShapeDtypeStruct((B, S, H * dh), BF16),
            input_output_aliases={7: 0},
            scratch_shapes=[pltpu.VMEM((nkeys, tq), jnp.int32),
                            pltpu.VMEM((nkeys, tq), jnp.int16), pltpu.VMEM((nkeys, tq), jnp.int16),
                            pltpu.VMEM((nkeys, tq), jnp.int16),
                            pltpu.VMEM((H, 1, tq), F32),
                            pltpu.VMEM((H, dc + DSA_ONES_ROWS, tq), F32)],
            compiler_params=_cparams("parallel", vmem_limit=DSA_VMEM_LIMIT),
            name=f"dsa_select_attention_q{i}",
        )(qf, kvl, kvt, qir, kid, wit, wuv, o)
    return _outproj_ln(proj, 1, o.reshape(B * S, D), x, w_out, ln_g, ln_b)


RWKV_HEADS = 16
RWKV_HEAD_DIM = 64
RWKV_GN_EPS = 64e-5
RWKV_CHUNK = 64
RWKV_SUB = 16


def _group_sum(x, gmat, split=True):
    outs = []
    for c in range(x.shape[1] // LANES):
        xc = x[:, c * LANES:(c + 1) * LANES]
        if split:
            hi, lo = _split2(xc)
            outs.append(_dot(hi, gmat) + _dot(lo, gmat))
        else:
            outs.append(_dot(xc.astype(BF16), gmat))
    return outs[0] if len(outs) == 1 else jnp.concatenate(outs, axis=1)


def _softplus(y):
    return jnp.maximum(y, 0.0) + jnp.log(1.0 + jnp.exp(-jnp.abs(y)))


def _rwkv_proj_kernel(x_ref, xprev_ref, mu_ref, wr_ref, wk_ref, wv_ref, wg_ref, wla_ref, wlb_ref,
                      ala_ref, alb_ref, w0_ref, a0_ref, kk_ref, ka_ref, rk_ref, gmat_ref, tril_ref,
                      r_ref, k_ref, v_ref, g_ref, kap_ref, b_ref, cum_ref, bonus_ref):
    x = x_ref[...]
    ts = x.shape[0]
    prev = jnp.where(pl.program_id(1) == 0, 0.0, xprev_ref[7:8, :])
    rowid = lax.broadcasted_iota(jnp.int32, (ts, 1), 0)
    xx = jnp.where(rowid == 0, prev, pltpu.roll(x, 1, 0)) - x

    def mixed(i):
        return (x + xx * mu_ref[i:i + 1, :]).astype(BF16)

    r = _dot(mixed(0), wr_ref[...])
    k = _dot(mixed(2), wk_ref[...])
    v = _dot(mixed(3), wv_ref[...])
    g_ref[...] = _dot(mixed(5), wg_ref[...]).astype(g_ref.dtype)
    lora_w = _dot(jnp.tanh(_dot(mixed(1), wla_ref[...])).astype(BF16), wlb_ref[...])
    lora_a = _dot(_dot(mixed(4), ala_ref[...]).astype(BF16), alb_ref[...])
    w_log = -_softplus(-(w0_ref[...] + lora_w)) - 0.5
    lw = -jnp.exp(w_log)
    tril = tril_ref[...]
    for c in range(ts // RWKV_CHUNK):
        rows = slice(c * RWKV_CHUNK, (c + 1) * RWKV_CHUNK)
        cum_ref[rows, :] = sum(_dot(tril, piece) for piece in _split2(lw[rows, :]))
    a = _sigmoid(a0_ref[...] + lora_a)
    gmat = gmat_ref[...]
    kk = k * kk_ref[...]
    kap = kk * lax.rsqrt(_group_sum(kk * kk, gmat, split=False) + 1e-12)
    k2 = k * (1.0 + (a - 1.0) * ka_ref[...])
    bonus_ref[...] = _group_sum(r * k2 * rk_ref[...], gmat, split=False) * v
    r_ref[...] = r.astype(r_ref.dtype)
    k_ref[...] = k2.astype(k_ref.dtype)
    v_ref[...] = v.astype(v_ref.dtype)
    kap_ref[...] = kap.astype(kap_ref.dtype)
    b_ref[...] = (kap * a).astype(b_ref.dtype)


def _bd(x, left):
    z = jnp.zeros_like(x)
    return jnp.concatenate([jnp.where(left, x, z), jnp.where(left, z, x)], axis=0)


def _unbd(x_bd):
    c = x_bd.shape[0] // 2
    return x_bd[:c] + x_bd[c:]


def _rwkv_chunk_pair(L, r, k, v, kap, b, masks):
    C = RWKV_CHUNK
    left, strict, lower, same_sub, eye, first_row = masks
    Lc = L[C - 1:C, :]
    L_excl = jnp.where(first_row, 0.0, pltpu.roll(L, 1, 0))
    e_l, e_lx, e_nl, e_r = jnp.exp(L), jnp.exp(L_excl), jnp.exp(-L), jnp.exp(Lc - L)
    at = _bd(-kap * e_lx, left).astype(BF16)
    rt = _bd(r * e_l, left)
    bt = _bd(b * e_nl, left).astype(BF16)
    kt = _bd(k * e_nl, left).astype(BF16)
    bh = _bd(b * e_r, left).astype(BF16)
    kh = _bd(k * e_r, left).astype(BF16)
    vb = _bd(v, left).astype(BF16)

    a1 = _dot_nt(jnp.concatenate([at, rt.astype(BF16)], axis=0), jnp.concatenate([bt, kt], axis=0))
    yield None
    n = jnp.where(strict, a1[:2 * C, :2 * C], 0.0)
    ak = jnp.where(strict, a1[:2 * C, 2 * C:], 0.0).astype(BF16)
    rb = jnp.where(lower, a1[2 * C:, :2 * C], 0.0).astype(BF16)
    rk = jnp.where(lower, a1[2 * C:, 2 * C:], 0.0).astype(BF16)

    nd = jnp.where(same_sub, n, 0.0)
    no = (n - nd).astype(BF16)
    ndb = nd.astype(BF16)
    n2 = _dot(ndb, ndb)
    akv = _dot(ak, vb)
    yield None
    n2b = n2.astype(BF16)
    n4 = _dot(n2b, n2b)
    t01 = _dot((eye + nd).astype(BF16), (eye + n2).astype(BF16))
    yield None
    n4b = n4.astype(BF16)
    n8 = _dot(n4b, n4b)
    yield None
    t23 = _dot((eye + n4).astype(BF16), (eye + n8).astype(BF16))
    yield None
    tdb = _dot(t01.astype(BF16), t23.astype(BF16)).astype(BF16)
    yield None
    x1 = _dot(tdb, no)
    yield None
    x1b = x1.astype(BF16)
    x2 = _dot(x1b, x1b)
    yield None
    tx = _dot((eye + x1).astype(BF16), (eye + x2).astype(BF16))
    yield None
    t = _dot(tx.astype(BF16), tdb).astype(BF16)
    yield None
    pq = _dot(t, jnp.concatenate([at, akv.astype(BF16)], axis=1)).astype(BF16)
    yield None
    z = jnp.concatenate([pq, jnp.concatenate([jnp.zeros_like(vb), vb], axis=1)], axis=0)
    ry = _dot(jnp.concatenate([rb, rk], axis=1), z)
    mg = _dot_tn(z, jnp.concatenate([bh, kh], axis=0))
    yield None
    rp = rt + ry[:, :2 * C]
    yl = ry[:, 2 * C:]
    yield _unbd(rp), _unbd(yl), _unbd(mg[:2 * C]), _unbd(mg[2 * C:]), jnp.exp(Lc)


def _rwkv_chunk_kernel(cum_ref, r_ref, k_ref, v_ref, kap_ref, b_ref,
                       rp_ref, yl_ref, mm_ref, gg_ref, gam_ref, *, pairs, cpb):
    C = RWKV_CHUNK
    lane = lax.broadcasted_iota(jnp.int32, (1, LANES), 1)
    left = lane < RWKV_HEAD_DIM
    ri = lax.broadcasted_iota(jnp.int32, (2 * C, 2 * C), 0)
    ci = lax.broadcasted_iota(jnp.int32, (2 * C, 2 * C), 1)
    same_head = (ri // C) == (ci // C)
    strict = same_head & ((ri % C) > (ci % C))
    lower = same_head & ((ri % C) >= (ci % C))
    same_sub = (ri // RWKV_SUB) == (ci // RWKV_SUB)
    eye = jnp.where(ri == ci, 1.0, 0.0).astype(F32)
    first_row = lax.broadcasted_iota(jnp.int32, (C, 1), 0) == 0
    masks = (left, strict, lower, same_sub, eye, first_row)
    jobs = [(c, slice(c * C, (c + 1) * C), slice(p * LANES, (p + 1) * LANES))
            for c in range(cpb) for p in range(pairs)]
    results = _round_robin([
        _rwkv_chunk_pair(cum_ref[rows, sl], r_ref[rows, sl].astype(F32), k_ref[rows, sl].astype(F32),
                         v_ref[rows, sl].astype(F32), kap_ref[rows, sl].astype(F32),
                         b_ref[rows, sl].astype(F32), masks)
        for _, rows, sl in jobs])
    for (c, rows, sl), (rp, yl, mm, gg, gam) in zip(jobs, results):
        rp_ref[rows, sl] = rp.astype(rp_ref.dtype)
        yl_ref[rows, sl] = yl
        mm_ref[c, :, sl] = mm.astype(mm_ref.dtype)
        gg_ref[c, :, sl] = gg
        gam_ref[c, :, sl] = gam


def _rwkv_seq_kernel(rp_ref, yl_ref, mm_ref, gg_ref, gam_ref, bonus_ref, gmat_ref, gng_ref, gnb_ref,
                     o_ref, s_sc, y_sc, *, pairs, cb):
    C = RWKV_CHUNK
    lane = lax.broadcasted_iota(jnp.int32, (1, LANES), 1)
    left = lane < RWKV_HEAD_DIM

    @pl.when(pl.program_id(2) == 0)
    def _():
        s_sc[...] = jnp.zeros_like(s_sc)

    states = [s_sc[p] for p in range(pairs)]
    for c in range(cb):
        rows = slice(c * C, (c + 1) * C)
        for p in range(pairs):
            sl = slice(p * LANES, (p + 1) * LANES)
            s = states[p]
            sb = s.astype(BF16)
            y_sc[rows, sl] = _dot_nt(rp_ref[rows, sl], sb) + yl_ref[rows, sl]
            mm = _bd(mm_ref[c, :, sl], left)
            gg = _bd(gg_ref[c, :, sl], left)
            states[p] = s * gam_ref[c, :, sl] + _dot(sb, mm) + gg
    for p in range(pairs):
        s_sc[p] = states[p]

    y = y_sc[...]
    gmat = gmat_ref[...]
    inv_n = 1.0 / RWKV_HEAD_DIM
    yc = y - _group_sum(y, gmat) * inv_n
    var = _group_sum(yc * yc, gmat) * inv_n
    yn = yc * lax.rsqrt(var + RWKV_GN_EPS) * gng_ref[...] + gnb_ref[...]
    o_ref[...] = (yn + bonus_ref[...]).astype(o_ref.dtype)


def _rwkv_layer(x, xb, mu, w_in, w0, w_lora_a, w_lora_b, a0, a_lora_a, a_lora_b, k_k, k_a, r_k,
                gn_g, gn_b, w_out, ln_g, ln_b, B, S, ts=512, pairs=8, seq_pairs=8, chunks_per_step=2):
    D = D_MODEL
    C = RWKV_CHUNK
    nc = S // C
    ts = _pick_tile(S, ts)
    tril = jnp.tril(jnp.ones((C, C), BF16))
    w_r, w_k, w_v, w_g =[w.astype(BF16) for w in jnp.split(w_in, 4, axis=1)]
    gmat = jnp.asarray(np.kron(np.eye(2, dtype=np.float32), np.ones((RWKV_HEAD_DIM, RWKV_HEAD_DIM), np.float32)), BF16)
    row = lambda a: a.reshape(1, D)
    x3 = x.reshape(B, S, D)
    full = lambda shape: pl.BlockSpec(shape, lambda b, j: (0,) * len(shape))
    tile = lambda: pl.BlockSpec((None, ts, D), lambda b, j: (b, j, 0))
    lr = w_lora_a.shape[1]
    outs = pl.pallas_call(
        _rwkv_proj_kernel,
        grid=(B, S // ts),
        in_specs=[tile(),
                  pl.BlockSpec((None, 8, D), lambda b, j: (b, jnp.maximum(j * (ts // 8) - 1, 0), 0)),
                  full((6, D)), full((D, D)), full((D, D)), full((D, D)), full((D, D)),
                  full((D, lr)), full((lr, D)), full((D, lr)), full((lr, D)),
                  full((1, D)), full((1, D)), full((1, D)), full((1, D)), full((1, D)),
                  full((LANES, LANES)), full((C, C))],
        out_specs=[tile() for _ in range(8)],
        out_shape=[jax.ShapeDtypeStruct((B, S, D), dt) for dt in (BF16, BF16, BF16, BF16, BF16, BF16, F32, F32)],
        compiler_params=_cparams("parallel", "arbitrary"),
        name="rwkv_projections",
    )(x3, x3, mu, w_r, w_k, w_v, w_g, w_lora_a.astype(BF16), w_lora_b.astype(BF16),
      a_lora_a.astype(BF16), a_lora_b.astype(BF16), row(w0), row(a0), row(k_k), row(k_a), row(r_k), gmat, tril)
    r, k2, v, g, kap, bvec, cum, bonus = outs

    pw = pairs * LANES
    cpb = _pick_tile(nc, chunks_per_step)
    cblk = lambda: pl.BlockSpec((None, cpb * C, pw), lambda b, c, q: (b, c, q))
    sblk = lambda: pl.BlockSpec((None, cpb, C, pw), lambda b, c, q: (b, c, 0, q))
    rp, yl, mm, gg, gam = pl.pallas_call(
        functools.partial(_rwkv_chunk_kernel, pairs=pairs, cpb=cpb),
        grid=(B, nc // cpb, D // pw),
        in_specs=[cblk() for _ in range(6)],
        out_specs=[cblk(), cblk(), sblk(), sblk(),
                   pl.BlockSpec((None, cpb, 1, pw), lambda b, c, q: (b, c, 0, q))],
        out_shape=[jax.ShapeDtypeStruct((B, S, D), BF16), jax.ShapeDtypeStruct((B, S, D), F32),
                   jax.ShapeDtypeStruct((B, nc, C, D), BF16), jax.ShapeDtypeStruct((B, nc, C, D), F32),
                   jax.ShapeDtypeStruct((B, nc, 1, D), F32)],
        compiler_params=_cparams("parallel", "parallel", "parallel"),
        name="rwkv_chunk_summaries",
    )(cum, r, k2, v, kap, bvec)

    cb = _pick_tile(nc, 8)
    pairs = seq_pairs
    pw = pairs * LANES
    o = pl.pallas_call(
        functools.partial(_rwkv_seq_kernel, pairs=pairs, cb=cb),
        grid=(B, D // pw, nc // cb),
        in_specs=[pl.BlockSpec((None, cb * C, pw), lambda b, q, j: (b, j, q)),
                  pl.BlockSpec((None, cb * C, pw), lambda b, q, j: (b, j, q)),
                  pl.BlockSpec((None, cb, C, pw), lambda b, q, j: (b, j, 0, q)),
                  pl.BlockSpec((None, cb, C, pw), lambda b, q, j: (b, j, 0, q)),
                  pl.BlockSpec((None, cb, 1, pw), lambda b, q, j: (b, j, 0, q)),
                  pl.BlockSpec((None, cb * C, pw), lambda b, q, j: (b, j, q)),
                  pl.BlockSpec((LANES, LANES), lambda b, q, j: (0, 0)),
                  pl.BlockSpec((1, pw), lambda b, q, j: (0, q)),
                  pl.BlockSpec((1, pw), lambda b, q, j: (0, q))],
        out_specs=pl.BlockSpec((None, cb * C, pw), lambda b, q, j: (b, j, q)),
        out_shape=jax.ShapeDtypeStruct((B, S, D), BF16),
        scratch_shapes=[pltpu.VMEM((pairs, 2 * C, LANES), F32), pltpu.VMEM((cb * C, pw), F32)],
        compiler_params=_cparams("parallel", "parallel", "arbitrary"),
        name="rwkv_state_scan",
    )(rp, yl, mm, gg, gam, bonus, gmat, row(gn_g), row(gn_b))
    return _outproj_ln(g.reshape(B * S, D), 0, o.reshape(B * S, D), x, w_out, ln_g, ln_b)


def kernel(x, ln_g, ln_b, fox_w_in, fox_b_f, fox_w_out, dsa_w_in, dsa_kv_norm_g, dsa_w_uk, dsa_w_uv, dsa_w_out, rwkv_mu, rwkv_w_in, rwkv_w0, rwkv_w_lora_a, rwkv_w_lora_b, rwkv_a0, rwkv_a_lora_a, rwkv_a_lora_b, rwkv_k_k, rwkv_k_a, rwkv_r_k, rwkv_gn_g, rwkv_gn_b, rwkv_w_out, ret_w_in, ret_gn_g, ret_w_out):
    B, S, D = x.shape
    h = x.reshape(B * S, D)
    h, hb = _fox_layer(h, h, fox_w_in, fox_b_f, fox_w_out, ln_g[0], ln_b[0], B, S)
    h, hb = _dsa_layer(h, hb, dsa_w_in, dsa_kv_norm_g, dsa_w_uk, dsa_w_uv, dsa_w_out, ln_g[1], ln_b[1], B, S)
    h, hb = _rwkv_layer(h, hb, rwkv_mu, rwkv_w_in, rwkv_w0, rwkv_w_lora_a, rwkv_w_lora_b, rwkv_a0,
                        rwkv_a_lora_a, rwkv_a_lora_b, rwkv_k_k, rwkv_k_a, rwkv_r_k, rwkv_gn_g, rwkv_gn_b,
                        rwkv_w_out, ln_g[2], ln_b[2], B, S)
    h, hb = _ret_layer(h, hb, ret_w_in, ret_gn_g, ret_w_out, ln_g[3], ln_b[3], B, S)
    return h.reshape(B, S, D)
```

```python
import functools
import math

import jax
import jax.numpy as jnp
import numpy as np
from jax import lax
from jax.experimental import pallas as pl
from jax.experimental.pallas import tpu as pltpu

F32 = jnp.float32
BF16 = jnp.bfloat16

D_MODEL = 1024
DEPTH = 4
LN_EPS = 1e-5
RMS_EPS = 1e-6
DN_ALPHA = (2 * DEPTH) ** 0.25
ROPE_THETA = 500000.0

FOX_HEADS = 8
FOX_HEAD_DIM = 128

RET_HEADS = 4
RET_HEAD_DIM = 256
RET_THETA = 10000.0

LANES = 128
VMEM_LIMIT = 48 * 1024 * 1024
NEG_BIG = -2.0 ** 100
LOG2E = 1.4426950408889634


def _cparams(*sem, vmem_limit=VMEM_LIMIT):
    return pltpu.CompilerParams(dimension_semantics=sem, vmem_limit_bytes=vmem_limit)


def _dot(a, b):
    return jnp.dot(a, b, preferred_element_type=F32)


def _dot_nt(a, b):
    return lax.dot_general(a, b, (((1,), (1,)), ((), ())), preferred_element_type=F32)


def _dot_tn(a, b):
    return lax.dot_general(a, b, (((0,), (0,)), ((), ())), preferred_element_type=F32)


def _split2(x):
    hi = x.astype(BF16)
    lo = (x - hi.astype(F32)).astype(BF16)
    return hi, lo


def _split3(x):
    p1 = x.astype(BF16)
    r1 = x - p1.astype(F32)
    p2 = r1.astype(BF16)
    p3 = (r1 - p2.astype(F32)).astype(BF16)
    return p1, p2, p3


def _sigmoid(x):
    return 1.0 / (1.0 + jnp.exp(-x))


def _round_robin(gens):
    results = [None] * len(gens)
    live = list(range(len(gens)))
    while live:
        still = []
        for i in live:
            try:
                out = next(gens[i])
            except StopIteration:
                continue
            if out is not None:
                results[i] = out
            still.append(i)
        live = still
    return results


def _pick_tile(n, pref):
    t = min(n, pref)
    while n % t:
        t //= 2
    return t


def _mm_kernel(a_ref, w_ref, o_ref):
    o_ref[...] = _dot(a_ref[...].astype(BF16), w_ref[...]).astype(o_ref.dtype)


def _matmul(a, w, out_dtype=BF16, tm=2048, tn=1024):
    M, K = a.shape
    N = w.shape[1]
    tm = _pick_tile(M, tm)
    if N % tn:
        tn = N
    return pl.pallas_call(
        _mm_kernel,
        grid=(M // tm, N // tn),
        in_specs=[pl.BlockSpec((tm, K), lambda i, j: (i, 0)),
                  pl.BlockSpec((K, tn), lambda i, j: (0, j))],
        out_specs=pl.BlockSpec((tm, tn), lambda i, j: (i, j)),
        out_shape=jax.ShapeDtypeStruct((M, N), out_dtype),
        compiler_params=_cparams("parallel", "arbitrary"),
        name="proj_matmul",
    )(a, w)


def _outproj_ln_kernel(g_ref, o_ref, x_ref, w_ref, lg_ref, lb_ref, xo_ref, xb_ref):
    half_g = g_ref[...] * 0.5
    h = (half_g * o_ref[...]) * (1.0 + jnp.tanh(half_g))
    z = x_ref[...] + _dot(h.astype(BF16), w_ref[...])
    zc = z - jnp.mean(z, axis=-1, keepdims=True)
    var = jnp.mean(zc * zc, axis=-1, keepdims=True)
    out = zc * lax.rsqrt(var + LN_EPS / DN_ALPHA ** 2) * lg_ref[...] + lb_ref[...]
    xo_ref[...] = out
    xb_ref[...] = out.astype(BF16)


def _outproj_ln(gate_arr, gate_col, o, x, w_out, ln_g, ln_b, tm=1024):
    M, D = x.shape
    tm = _pick_tile(M, tm)
    return pl.pallas_call(
        _outproj_ln_kernel,
        grid=(M // tm,),
        in_specs=[pl.BlockSpec((tm, D), lambda i: (i, gate_col)),
                  pl.BlockSpec((tm, D), lambda i: (i, 0)),
                  pl.BlockSpec((tm, D), lambda i: (i, 0)),
                  pl.BlockSpec((D, D), lambda i: (0, 0)),
                  pl.BlockSpec((1, D), lambda i: (0, 0)),
                  pl.BlockSpec((1, D), lambda i: (0, 0))],
        out_specs=[pl.BlockSpec((tm, D), lambda i: (i, 0)),
                   pl.BlockSpec((tm, D), lambda i: (i, 0))],
        out_shape=[jax.ShapeDtypeStruct((M, D), F32), jax.ShapeDtypeStruct((M, D), BF16)],
        compiler_params=_cparams("parallel"),
        name="outproj_layernorm",
    )(gate_arr, o, x, (w_out * (1.0 / DN_ALPHA)).astype(BF16), ln_g.reshape(1, D), ln_b.reshape(1, D))


FOX_BIAS_PIECES = 3


def _fox_cum_kernel(x_ref, wh_ref, wl_ref, bf_ref, tril_ref, place_ref, pc_ref, carry_sc):
    @pl.when(pl.program_id(1) == 0)
    def _():
        carry_sc[...] = jnp.zeros_like(carry_sc)

    x_hi, x_lo = _split2(x_ref[...])
    z = _dot(x_hi, wh_ref[...]) + _dot(x_lo, wh_ref[...]) + _dot(x_hi, wl_ref[...]) + bf_ref[...]
    logf = jnp.minimum(z, 0.0) - jnp.log(1.0 + jnp.exp(-jnp.abs(z)))
    p1, p2, p3 = _split3(logf)
    tril = tril_ref[...]
    c = _dot(tril, p1) + _dot(tril, p2) + _dot(tril, p3) + carry_sc[...]
    carry_sc[...] = c[c.shape[0] - 1:, :]
    pieces = _split3(c * (-LOG2E))
    pc_ref[...] = sum(_dot(pieces[p], place_ref[p]) for p in range(FOX_BIAS_PIECES)).astype(pc_ref.dtype)


def _fox_cum(x3, w_f, b_f, ts=512):
    B, S, D = x3.shape
    H = w_f.shape[1]
    ts = _pick_tile(S, ts)
    w_pad = jnp.zeros((D, LANES), F32).at[:, :H].set(w_f)
    w_hi, w_lo = _split2(w_pad)
    b_pad = jnp.zeros((1, LANES), F32).at[0, :H].set(b_f)
    tril = jnp.tril(jnp.ones((ts, ts), BF16))
    place = np.zeros((FOX_BIAS_PIECES, LANES, LANES), np.float32)
    for p in range(FOX_BIAS_PIECES):
        for h in range(H):
            place[p, h, FOX_BIAS_PIECES * h + p] = 1.0
    return pl.pallas_call(
        _fox_cum_kernel,
        grid=(B, S // ts),
        in_specs=[pl.BlockSpec((None, ts, D), lambda b, j: (b, j, 0)),
                  pl.BlockSpec((D, LANES), lambda b, j: (0, 0)),
                  pl.BlockSpec((D, LANES), lambda b, j: (0, 0)),
                  pl.BlockSpec((1, LANES), lambda b, j: (0, 0)),
                  pl.BlockSpec((ts, ts), lambda b, j: (0, 0)),
                  pl.BlockSpec((FOX_BIAS_PIECES, LANES, LANES), lambda b, j: (0, 0, 0))],
        out_specs=pl.BlockSpec((None, ts, LANES), lambda b, j: (b, j, 0)),
        out_shape=jax.ShapeDtypeStruct((B, S, LANES), BF16),
        scratch_shapes=[pltpu.VMEM((1, LANES), F32)],
        compiler_params=_cparams("parallel", "arbitrary"),
        name="fox_decay_cumsum",
    )(x3, w_hi, w_lo, b_pad, tril, jnp.asarray(place, BF16))


ONES_ROWS = 16


def _fox_attn_kernel(q_ref, k_ref, v_ref, pc_ref, o_ref, kaug_sc, vt_sc, m_sc, acc_sc, *, tq, nsub, seq, unroll):
    dh = FOX_HEAD_DIM
    h = pl.program_id(1)
    g = pl.program_id(2)

    @pl.when(g == 0)
    def _():
        kaug_sc[:, :dh] = k_ref[...]
        kaug_sc[:, dh:] = pc_ref[...]
        for c in range(seq // tq):
            rows = slice(c * tq, (c + 1) * tq)
            vt_sc[:dh, rows] = v_ref[rows, :].astype(F32).T.astype(BF16)
        vt_sc[dh:, :] = jnp.ones((ONES_ROWS, seq), BF16)

    feat = lax.broadcasted_iota(jnp.int32, (LANES, tq), 0)
    bias_rows = (feat >= FOX_BIAS_PIECES * h) & (feat < FOX_BIAS_PIECES * (h + 1))
    ones_h = jnp.where(bias_rows, 1.0, 0.0).astype(BF16)
    q_aug = [jnp.concatenate([q_ref[a * tq:(a + 1) * tq, :].astype(F32).T.astype(BF16), ones_h], axis=0)
             for a in range(nsub)]
    m_sc[...] = jnp.full_like(m_sc, NEG_BIG)
    acc_sc[...] = jnp.zeros_like(acc_sc)
    causal = (lax.broadcasted_iota(jnp.int32, (tq, tq), 0) <= lax.broadcasted_iota(jnp.int32, (tq, tq), 1))
    first = g * nsub

    def chain(a, tiles, diag_last):
        offs = [pl.multiple_of(j * tq, tq) for j in tiles]
        scores = []
        for off in offs:
            scores.append(_dot(kaug_sc[pl.ds(off, tq), :], q_aug[a]))
            yield None
        for n, (off, s) in enumerate(zip(offs, scores)):
            if diag_last and n == len(offs) - 1:
                s = jnp.where(causal, s, NEG_BIG)
            m_old = m_sc[a]
            m_new = jnp.maximum(m_old, jnp.max(s, axis=0, keepdims=True))
            alpha = jnp.exp2(m_old - m_new)
            pv = _dot(vt_sc[:, pl.ds(off, tq)], jnp.exp2(s - m_new).astype(BF16))
            yield None
            acc_sc[a] = alpha * acc_sc[a] + pv
            m_sc[a] = m_new
        yield None

    def body(jj, c):
        _round_robin([chain(a, [jj * unroll + u for u in range(unroll)], False) for a in range(nsub)])
        return c

    lax.fori_loop(0, first // unroll, body, 0)
    _round_robin([chain(a, [first + t for t in range(a + 1)], True) for a in range(nsub)])
    for a in range(nsub):
        acc = acc_sc[a]
        o_t = acc[:dh] / acc[dh:dh + 1]
        o_ref[a * tq:(a + 1) * tq, :] = o_t.T.astype(o_ref.dtype)


def _fox_attention(proj3, pieces, tq=256, nsub=8):
    B, S, _ = proj3.shape
    H, dh = FOX_HEADS, FOX_HEAD_DIM
    tq = _pick_tile(S, tq)
    nsub = _pick_tile(S // tq, nsub)
    tg = tq * nsub
    return pl.pallas_call(
        functools.partial(_fox_attn_kernel, tq=tq, nsub=nsub, seq=S, unroll=min(nsub, 4)),
        grid=(B, H, S // tg),
        in_specs=[pl.BlockSpec((None, tg, dh), lambda b, h, g: (b, g, h)),
                  pl.BlockSpec((None, S, dh), lambda b, h, g: (b, 0, H + h)),
                  pl.BlockSpec((None, S, dh), lambda b, h, g: (b, 0, 2 * H + h)),
                  pl.BlockSpec((None, S, LANES), lambda b, h, g: (b, 0, 0))],
        out_specs=pl.BlockSpec((None, tg, dh), lambda b, h, g: (b, g, h)),
        out_shape=jax.ShapeDtypeStruct((B, S, H * dh), BF16),
        scratch_shapes=[pltpu.VMEM((S, dh + LANES), BF16),
                        pltpu.VMEM((dh + ONES_ROWS, S), BF16),
                        pltpu.VMEM((nsub, 1, tq), F32),
                        pltpu.VMEM((nsub, dh + ONES_ROWS, tq), F32)],
        compiler_params=_cparams("parallel", "parallel", "arbitrary"),
        name="fox_attention",
    )(proj3, proj3, proj3, pieces)


def _fox_layer(x, xb, w_in, b_f, w_out, ln_g, ln_b, B, S):
    D = D_MODEL
    H, dh = FOX_HEADS, FOX_HEAD_DIM
    scale = dh ** -0.5 * LOG2E
    w_q, w_k, w_v, w_f, w_g = jnp.split(w_in, [H * dh, 2 * H * dh, 3 * H * dh, 3 * H * dh + H], axis=1)
    w_main = jnp.concatenate([w_q * scale, w_k, w_v, w_g], axis=1).astype(BF16)
    proj = _matmul(xb, w_main)
    pieces = _fox_cum(x.reshape(B, S, D), w_f, b_f)
    o = _fox_attention(proj.reshape(B, S, 4 * D), pieces)
    return _outproj_ln(proj, 3, o.reshape(B * S, D), x, w_out, ln_g, ln_b)


def _ret_kernel(q_ref, k_ref, v_ref, cos_ref, sin_ref, dm_ref, xi_ref, zeta_ref, gc_ref, gn_ref,
                o_ref, r_sc):
    @pl.when(pl.program_id(1) == 0)
    def _():
        r_sc[...] = jnp.zeros_like(r_sc)

    dk = RET_HEAD_DIM
    half = dk // 2
    cos = cos_ref[...]
    sin = sin_ref[...]

    def rope(x):
        x1, x2 = x[:, :half], x[:, half:]
        return jnp.concatenate([x1 * cos - x2 * sin, x2 * cos + x1 * sin], axis=-1)

    def head_chain(h):
        cols = slice(h * dk, (h + 1) * dk)
        q = rope(q_ref[:, cols].astype(F32))
        k = rope(k_ref[:, cols].astype(F32)) * (dk ** -0.5)
        v = v_ref[:, cols]
        qb = q.astype(BF16)
        r_old = r_sc[h]
        scores = _dot_nt(qb, k.astype(BF16))
        cross = _dot(qb, r_old.astype(BF16))
        kz = (k * zeta_ref[h]).astype(BF16)
        r_new = _dot_tn(kz, v)
        yield None
        o = _dot((scores * dm_ref[h]).astype(BF16), v)
        r_sc[h] = r_old * gc_ref[h] + r_new
        yield None
        o = o + cross * xi_ref[h]
        o = o * lax.rsqrt(jnp.mean(o * o, axis=-1, keepdims=True) + RMS_EPS) * gn_ref[:, cols]
        o_ref[:, cols] = o.astype(o_ref.dtype)
        yield None

    _round_robin([head_chain(h) for h in range(RET_HEADS)])


def _ret_layer(x, xb, w_in, gn_g, w_out, ln_g, ln_b, B, S, chunk=512):
    D = D_MODEL
    H, dk = RET_HEADS, RET_HEAD_DIM
    C = _pick_tile(S, chunk)
    proj = _matmul(xb, w_in.astype(BF16))
    inv = 1.0 / (RET_THETA ** (jnp.arange(0, dk, 2, dtype=F32) / dk))
    ang = jnp.arange(S, dtype=F32)[:, None] * inv[None, :]
    cos, sin = jnp.cos(ang), jnp.sin(ang)
    log_g = jnp.log1p(-(2.0 ** (-5.0 - jnp.arange(H, dtype=F32))))
    pos = jnp.arange(C, dtype=F32)
    diff = pos[:, None] - pos[None, :]
    d_mask = jnp.where(diff[None] >= 0, jnp.exp(jnp.maximum(diff, 0.0)[None] * log_g[:, None, None]), 0.0)
    xi = jnp.broadcast_to(jnp.exp((pos[None, :] + 1.0) * log_g[:, None])[:, :, None], (H, C, dk))
    zeta = jnp.broadcast_to(jnp.exp((C - 1.0 - pos[None, :]) * log_g[:, None])[:, :, None], (H, C, dk))
    g_c = jnp.broadcast_to(jnp.exp(C * log_g)[:, None, None], (H, 1, dk))
    p3 = proj.reshape(B, S, 4 * D)
    o = pl.pallas_call(
        _ret_kernel,
        grid=(B, S // C),
        in_specs=[pl.BlockSpec((None, C, D), lambda b, c: (b, c, 0)),
                  pl.BlockSpec((None, C, D), lambda b, c: (b, c, 1)),
                  pl.BlockSpec((None, C, D), lambda b, c: (b, c, 2)),
                  pl.BlockSpec((C, dk // 2), lambda b, c: (c, 0)),
                  pl.BlockSpec((C, dk // 2), lambda b, c: (c, 0)),
                  pl.BlockSpec((H, C, C), lambda b, c: (0, 0, 0)),
                  pl.BlockSpec((H, C, dk), lambda b, c: (0, 0, 0)),
                  pl.BlockSpec((H, C, dk), lambda b, c: (0, 0, 0)),
                  pl.BlockSpec((H, 1, dk), lambda b, c: (0, 0, 0)),
                  pl.BlockSpec((1, D), lambda b, c: (0, 0))],
        out_specs=pl.BlockSpec((None, C, D), lambda b, c: (b, c, 0)),
        out_shape=jax.ShapeDtypeStruct((B, S, D), BF16),
        scratch_shapes=[pltpu.VMEM((H, dk, dk), F32)],
        compiler_params=_cparams("parallel", "arbitrary"),
        name="retnet_retention",
    )(p3, p3, p3, cos, sin, d_mask, xi, zeta, g_c, gn_g.reshape(1, D))
    return _outproj_ln(proj, 3, o.reshape(B * S, D), x, w_out, ln_g, ln_b)


DSA_HEADS = 8
DSA_HEAD_DIM = 128
DSA_ROPE_DIM = 32
DSA_KV_RANK = 128
IDX_HEADS = 8
IDX_DIM = 64
IDX_ROPE_DIM = 16
TOPK_MAX = 256
INT_MIN = -2 ** 31
HALF_MIN = -2 ** 15
HALF_ROWS = 16
DSA_VMEM_LIMIT = 56 * 1024 * 1024
DSA_ONES_ROWS = 16


def _rope_perm(width, groups):
    p = np.zeros((width, width), np.float32)
    for start, half in groups:
        for j in range(half):
            p[start + half + j, start + j] = 1.0
            p[start + j, start + half + j] = 1.0
    return p


def _rope_cs(S, width, groups, theta_dims):
    c = jnp.ones((S, width), F32)
    sg = jnp.zeros((S, width), F32)
    pos = jnp.arange(S, dtype=F32)[:, None]
    for (start, half), rot_dim in zip(groups, theta_dims):
        inv = 1.0 / (ROPE_THETA ** (jnp.arange(0, rot_dim, 2, dtype=F32) / rot_dim))
        ang = pos * inv[None, :]
        cos, sin = jnp.cos(ang), jnp.sin(ang)
        c = c.at[:, start:start + half].set(cos).at[:, start + half:start + 2 * half].set(cos)
        sg = sg.at[:, start:start + half].set(-sin).at[:, start + half:start + 2 * half].set(sin)
    return c, sg


def _dsa_prep_kernel(q_ref, qi_ref, ckv_ref, misc_ref, cq_ref, sq_ref, ci_ref, si_ref, cm_ref, sm_ref,
                     pq_ref, pi_ref, pm_ref, selk_ref, selw_ref, wuk_ref, kvg_ref,
                     qf_ref, kvl_ref, kvt_ref, qir_ref, kid_ref, wit_ref):
    H, dh = DSA_HEADS, DSA_HEAD_DIM
    lane = lax.broadcasted_iota(jnp.int32, (1, LANES), 1)
    rope_lanes = lane < DSA_ROPE_DIM
    cq, sq = cq_ref[...], sq_ref[...]
    scale = dh ** -0.5 * LOG2E
    for h in range(H):
        qh = q_ref[:, h * dh:(h + 1) * dh]
        qr = qh.astype(F32) * cq + _dot(qh, pq_ref[...]) * sq
        q_lat = _dot(qr.astype(BF16), wuk_ref[h])
        qf_ref[h, :dh, :] = (q_lat * scale).T.astype(BF16)
        qf_ref[h, dh:, :] = jnp.where(rope_lanes, qr * scale, 0.0).T.astype(BF16)
    ci, si = ci_ref[...], si_ref[...]
    for g in range(IDX_HEADS * IDX_DIM // LANES):
        qg = qi_ref[:, g * LANES:(g + 1) * LANES]
        qr = qg.astype(F32) * ci + _dot(qg, pi_ref[...]) * si
        qir_ref[g * LANES:(g + 1) * LANES, :] = (qr * (IDX_DIM ** -0.5)).T.astype(BF16)
    ckv = ckv_ref[...].astype(F32)
    ckv = ckv * lax.rsqrt(jnp.mean(ckv * ckv, axis=-1, keepdims=True) + RMS_EPS) * kvg_ref[...]
    misc = misc_ref[...]
    mr = (misc.astype(F32) * cm_ref[...] + _dot(misc, pm_ref[...]) * sm_ref[...])
    kvl_ref[:, :DSA_KV_RANK] = ckv.astype(BF16)
    kvl_ref[:, DSA_KV_RANK:] = jnp.where(rope_lanes, mr, 0.0).astype(BF16)
    kvt_ref[:DSA_KV_RANK, :] = ckv.T.astype(BF16)
    kvt_ref[DSA_KV_RANK:, :] = jnp.ones((DSA_ONES_ROWS, ckv.shape[0]), BF16)
    kid_ref[...] = _dot(mr.astype(BF16), selk_ref[...]).astype(BF16)
    wi = _dot(misc, selw_ref[...]) * (IDX_HEADS ** -0.5)
    wit_ref[...] = wi.T[:IDX_HEADS, :]


def _sort_key(x):
    b = pltpu.bitcast(x, jnp.int32)
    return jnp.where(b < 0, b ^ 0x7FFFFFFF, b)


def _dsa_main_kernel(qf_ref, kvl_ref, kvt_ref, qi_ref, kid_ref, wit_ref, wuv_ref, obuf_ref, o_ref,
                     key_sc, hi_sc, lo_sc, low_sc, m_sc, acc_sc, *, tq, tk, k_sel, seq, qtile):
    del obuf_ref
    H = DSA_HEADS
    q0 = qtile * tq
    nj = (q0 + tq + tk - 1) // tk
    qpos = q0 + lax.broadcasted_iota(jnp.int32, (tk, tq), 1)
    kpos0 = lax.broadcasted_iota(jnp.int32, (tk, tq), 0)
    feat = lax.broadcasted_iota(jnp.int32, (LANES, 1), 0)
    neg_inf_key = _sort_key(jnp.full((1, 1), -jnp.inf, F32))

    wit = wit_ref[...]
    qi_heads = []
    for h in range(IDX_HEADS):
        g = qi_ref[(h // 2) * LANES:(h // 2 + 1) * LANES, :]
        keep = (feat >= IDX_DIM) if (h % 2) else (feat < IDX_DIM)
        qi_heads.append(jnp.where(keep, g, jnp.zeros_like(g)))

    def score_tile(j, c):
        off = pl.multiple_of(j * tk, tk)
        ki = kid_ref[pl.ds(off, tk), :]
        scores = [_dot(ki, qi_heads[h]) for h in range(IDX_HEADS)]
        isc = jnp.zeros((tk, tq), F32)
        for h in range(IDX_HEADS):
            isc = isc + jnp.maximum(scores[h], 0.0) * wit[h:h + 1, :]
        isc = jnp.where(kpos0 + off <= qpos, isc + 0.0, -jnp.inf)
        key = _sort_key(isc)
        key_sc[pl.ds(off, tk), :] = key
        hi_sc[pl.ds(off, tk), :] = (key >> 16).astype(jnp.int16)
        lo_sc[pl.ds(off, tk), :] = ((key & 0xFFFF) + HALF_MIN).astype(jnp.int16)
        return c

    lax.fori_loop(0, nj, score_tile, 0)

    def count(pred_fn):
        acc = jnp.zeros((8, tq), jnp.int32)
        for j in range(nj):
            hit = jnp.where(pred_fn(key_sc[j * tk:(j + 1) * tk, :], j * tk), 1, 0)
            acc = acc + jnp.sum(hit.reshape(tk // 8, 8, tq), axis=0)
        return jnp.sum(acc, axis=0, keepdims=True)

    rows16 = tk // HALF_ROWS

    def count16(ref, pred_fn):
        accs = [jnp.zeros((HALF_ROWS, tq), jnp.int16) for _ in range(2)]
        for j in range(nj):
            hit = jnp.where(pred_fn(ref[j * tk:(j + 1) * tk, :].reshape(rows16, HALF_ROWS, tq)),
                            jnp.int16(1), jnp.int16(0))
            for r in range(rows16):
                accs[r % 2] = accs[r % 2] + hit[r]
        return jnp.sum((accs[0] + accs[1]).astype(jnp.int32), axis=0, keepdims=True)

    def as_half(v):
        return jnp.broadcast_to(v.astype(jnp.int16), (HALF_ROWS, tq))[None]

    def search16(ref, base0, cnt0, bits, want):
        def bit_step(t, carry):
            base, cnt_b = carry
            cand = base + lax.shift_left(jnp.int32(1), bits - 1 - t)
            cand16 = as_half(cand)
            c = count16(ref, lambda kt: kt >= cand16)
            ok = c >= want
            return jnp.where(ok, cand, base), jnp.where(ok, c, cnt_b)
        return lax.fori_loop(0, bits, bit_step, (base0, cnt0))

    zero16 = as_half(jnp.zeros((1, tq), jnp.int32))
    cnt_pos = count16(hi_sc, lambda kt: kt >= zero16)
    nonneg = cnt_pos >= k_sel
    t1, ge_hi = search16(hi_sc, jnp.where(nonneg, 0, HALF_MIN), jnp.where(nonneg, cnt_pos, nj * tk), 15, k_sel)
    t1_16 = as_half(t1)
    above = count16(hi_sc, lambda kt: kt > t1_16)

    for j in range(nj):
        rows = slice(j * tk, (j + 1) * tk)
        hi = hi_sc[rows, :].reshape(rows16, HALF_ROWS, tq)
        lo = lo_sc[rows, :].reshape(rows16, HALF_ROWS, tq)
        low_sc[rows, :] = jnp.where(hi == t1_16, lo, jnp.int16(HALF_MIN)).reshape(tk, tq)
    t2, ge_low = search16(low_sc, jnp.full((1, tq), HALF_MIN, jnp.int32), ge_hi - above, 16, k_sel - above)
    t2_16 = as_half(t2)
    thr = lax.shift_left(t1, 16) | (t2 - HALF_MIN)
    n_ge = above + ge_low
    n_gt = above + count16(low_sc, lambda kt: kt > t2_16)
    need = k_sel - n_gt
    excess = ((n_ge - n_gt) > need) & (thr > neg_inf_key)
    any_excess = jnp.max(jnp.where(excess, 1, 0)) > 0

    def tie_cut():
        def step(t, lo):
            cand = lo + lax.shift_left(jnp.int32(1), int(math.log2(seq)) - t)
            c = count(lambda kt, off: (kt == thr) & (kpos0 + off < cand))
            return jnp.where(c < need, cand, lo)
        lo = lax.fori_loop(0, int(math.log2(seq)) + 1, step, jnp.zeros((1, tq), jnp.int32))
        return jnp.where(excess, lo, seq)

    cut = lax.cond(any_excess, tie_cut, lambda: jnp.full((1, tq), seq, jnp.int32))

    m_sc[...] = jnp.full_like(m_sc, NEG_BIG)
    acc_sc[...] = jnp.zeros_like(acc_sc)

    def attn_tile(j, c):
        off = pl.multiple_of(j * tk, tk)
        kt = key_sc[pl.ds(off, tk), :]
        kpos = kpos0 + off
        bias = jnp.where(kt > thr, 0.0, jnp.where(kt == thr, jnp.where(kpos <= cut, 0.0, NEG_BIG), NEG_BIG))
        bias = jnp.where(kpos <= qpos, bias, NEG_BIG)
        kv = kvl_ref[pl.ds(off, tk), :]
        kvt = kvt_ref[:, pl.ds(off, tk)]

        def head_step(h):
            s = _dot(kv, qf_ref[h]) + bias
            yield None
            m_old = m_sc[h]
            m_new = jnp.maximum(m_old, jnp.max(s, axis=0, keepdims=True))
            alpha = jnp.exp2(m_old - m_new)
            pv = _dot(kvt, jnp.exp2(s - m_new).astype(BF16))
            yield None
            acc_sc[h] = alpha * acc_sc[h] + pv
            m_sc[h] = m_new
            yield None

        _round_robin([head_step(h) for h in range(H)])
        return c

    lax.fori_loop(0, nj, attn_tile, 0)
    for h in range(H):
        acc = acc_sc[h]
        o_lat_t = (acc[:DSA_KV_RANK] / acc[DSA_KV_RANK:DSA_KV_RANK + 1]).astype(BF16)
        o_ref[:, h * DSA_HEAD_DIM:(h + 1) * DSA_HEAD_DIM] = _dot_tn(o_lat_t, wuv_ref[h]).astype(o_ref.dtype)


def _dsa_layer(x, xb, w_in, kv_norm_g, w_uk, w_uv, w_out, ln_g, ln_b, B, S, tq=512, tk=512):
    D = D_MODEL
    H, dh, dr, dc = DSA_HEADS, DSA_HEAD_DIM, DSA_ROPE_DIM, DSA_KV_RANK
    HI, di = IDX_HEADS, IDX_DIM
    w_q, w_ckv, w_kr, w_qi, w_ki, w_wi, w_g = jnp.split(
        w_in, np.cumsum([H * dh, dc, dr, HI * di, di, HI]).tolist(), axis=1)
    w_misc = jnp.concatenate([w_kr, w_ki, w_wi, jnp.zeros((D, LANES - dr - di - HI), F32)], axis=1)
    w_main = jnp.concatenate([w_q, w_g, w_qi, w_ckv, w_misc], axis=1).astype(BF16)
    n_main = w_main.shape[1]
    proj = _matmul(xb, w_main, tm=1024, tn=n_main)
    c_q, c_qi, c_ckv, c_misc = 0, 2 * D // LANES, (2 * D + HI * di) // LANES, (2 * D + HI * di + dc) // LANES

    q_groups = [(0, dr // 2)]
    i_groups = [(0, IDX_ROPE_DIM // 2), (di, IDX_ROPE_DIM // 2)]
    m_groups = [(0, dr // 2), (dr, IDX_ROPE_DIM // 2)]
    cq, sq = _rope_cs(S, LANES, q_groups, [dr])
    ci, si = _rope_cs(S, LANES, i_groups, [IDX_ROPE_DIM, IDX_ROPE_DIM])
    cm, sm = _rope_cs(S, LANES, m_groups, [dr, IDX_ROPE_DIM])
    pq = jnp.asarray(_rope_perm(LANES, q_groups), BF16)
    pi = jnp.asarray(_rope_perm(LANES, i_groups), BF16)
    pm = jnp.asarray(_rope_perm(LANES, m_groups), BF16)
    selk = np.zeros((LANES, LANES), np.float32)
    for j in range(di):
        selk[dr + j, j] = 1.0
        selk[dr + j, di + j] = 1.0
    selw = np.zeros((LANES, LANES), np.float32)
    for j in range(HI):
        selw[dr + di + j, j] = 1.0
    wuk = jnp.concatenate([jnp.zeros((H, dr, dc), F32), jnp.transpose(w_uk, (0, 2, 1))], axis=1).astype(BF16)

    ts = _pick_tile(S, 512)
    p3 = proj.reshape(B, S, n_main)
    tab = lambda: pl.BlockSpec((ts, LANES), lambda b, j: (j, 0))
    mat = lambda: pl.BlockSpec((LANES, LANES), lambda b, j: (0, 0))
    qf, kvl, kvt, qir, kid, wit = pl.pallas_call(
        _dsa_prep_kernel,
        grid=(B, S // ts),
        in_specs=[pl.BlockSpec((None, ts, H * dh), lambda b, j: (b, j, 0)),
                  pl.BlockSpec((None, ts, HI * di), lambda b, j: (b, j, c_qi * LANES // (HI * di))),
                  pl.BlockSpec((None, ts, dc), lambda b, j: (b, j, c_ckv)),
                  pl.BlockSpec((None, ts, LANES), lambda b, j: (b, j, c_misc)),
                  tab(), tab(), tab(), tab(), tab(), tab(),
                  mat(), mat(), mat(), mat(), mat(),
                  pl.BlockSpec((H, LANES, dc), lambda b, j: (0, 0, 0)),
                  pl.BlockSpec((1, dc), lambda b, j: (0, 0))],
        out_specs=[pl.BlockSpec((None, H, 2 * dc, ts), lambda b, j: (b, 0, 0, j)),
                   pl.BlockSpec((None, ts, 2 * dc), lambda b, j: (b, j, 0)),
                   pl.BlockSpec((None, dc + DSA_ONES_ROWS, ts), lambda b, j: (b, 0, j)),
                   pl.BlockSpec((None, HI * di, ts), lambda b, j: (b, 0, j)),
                   pl.BlockSpec((None, ts, LANES), lambda b, j: (b, j, 0)),
                   pl.BlockSpec((None, HI, ts), lambda b, j: (b, 0, j))],
        out_shape=[jax.ShapeDtypeStruct((B, H, 2 * dc, S), BF16),
                   jax.ShapeDtypeStruct((B, S, 2 * dc), BF16),
                   jax.ShapeDtypeStruct((B, dc + DSA_ONES_ROWS, S), BF16),
                   jax.ShapeDtypeStruct((B, HI * di, S), BF16),
                   jax.ShapeDtypeStruct((B, S, LANES), BF16),
                   jax.ShapeDtypeStruct((B, HI, S), F32)],
        compiler_params=_cparams("parallel", "parallel"),
        name="dsa_prep",
    )(p3, p3, p3, p3, cq, sq, ci, si, cm, sm, pq, pi, pm,
      jnp.asarray(selk, BF16), jnp.asarray(selw, BF16), wuk, kv_norm_g.reshape(1, dc))

    tq = _pick_tile(S, tq)
    tk = _pick_tile(S, tk)
    k_sel = min(TOPK_MAX, S // 4)
    wuv = w_uv.astype(BF16)
    o = jnp.zeros((B, S, H * dh), BF16)
    for i in range(S // tq):
        nkeys = -(-((i + 1) * tq) // tk) * tk
        o = pl.pallas_call(
            functools.partial(_dsa_main_kernel, tq=tq, tk=tk, k_sel=k_sel, seq=S, qtile=i),
            grid=(B,),
            in_specs=[pl.BlockSpec((None, H, 2 * dc, tq), lambda b, i=i: (b, 0, 0, i)),
                      pl.BlockSpec((None, nkeys, 2 * dc), lambda b: (b, 0, 0)),
                      pl.BlockSpec((None, dc + DSA_ONES_ROWS, nkeys), lambda b: (b, 0, 0)),
                      pl.BlockSpec((None, HI * di, tq), lambda b, i=i: (b, 0, i)),
                      pl.BlockSpec((None, nkeys, LANES), lambda b: (b, 0, 0)),
                      pl.BlockSpec((None, HI, tq), lambda b, i=i: (b, 0, i)),
                      pl.BlockSpec((H, dc, dh), lambda b: (0, 0, 0)),
                      pl.BlockSpec(memory_space=pl.ANY)],
            out_specs=pl.BlockSpec((None, tq, H * dh), lambda b, i=i: (b, i, 0)),
            out_shape=jax.ShapeDtypeStruct((B, S, H * dh), BF16),
            input_output_aliases={7: 0},
            scratch_shapes=[pltpu.VMEM((nkeys, tq), jnp.int32),
                            pltpu.VMEM((nkeys, tq), jnp.int16), pltpu.VMEM((nkeys, tq), jnp.int16),
                            pltpu.VMEM((nkeys, tq), jnp.int16),
                            pltpu.VMEM((H, 1, tq), F32),
                            pltpu.VMEM((H, dc + DSA_ONES_ROWS, tq), F32)],
            compiler_params=_cparams("parallel", vmem_limit=DSA_VMEM_LIMIT),
            name=f"dsa_select_attention_q{i}",
        )(qf, kvl, kvt, qir, kid, wit, wuv, o)
    return _outproj_ln(proj, 1, o.reshape(B * S, D), x, w_out, ln_g, ln_b)


RWKV_HEADS = 16
RWKV_HEAD_DIM = 64
RWKV_GN_EPS = 64e-5
RWKV_CHUNK = 64
RWKV_SUB = 16


def _group_sum(x, gmat, split=True):
    outs = []
    for c in range(x.shape[1] // LANES):
        xc = x[:, c * LANES:(c + 1) * LANES]
        if split:
            hi, lo = _split2(xc)
            outs.append(_dot(hi, gmat) + _dot(lo, gmat))
        else:
            outs.append(_dot(xc.astype(BF16), gmat))
    return outs[0] if len(outs) == 1 else jnp.concatenate(outs, axis=1)


def _softplus(y):
    return jnp.maximum(y, 0.0) + jnp.log(1.0 + jnp.exp(-jnp.abs(y)))


def _rwkv_proj_kernel(x_ref, xprev_ref, mu_ref, wr_ref, wk_ref, wv_ref, wg_ref, wla_ref, wlb_ref,
                      ala_ref, alb_ref, w0_ref, a0_ref, kk_ref, ka_ref, rk_ref, gmat_ref, tril_ref,
                      r_ref, k_ref, v_ref, g_ref, kap_ref, b_ref, cum_ref, bonus_ref):
    x = x_ref[...]
    ts = x.shape[0]
    prev = jnp.where(pl.program_id(1) == 0, 0.0, xprev_ref[7:8, :])
    rowid = lax.broadcasted_iota(jnp.int32, (ts, 1), 0)
    xx = jnp.where(rowid == 0, prev, pltpu.roll(x, 1, 0)) - x

    def mixed(i):
        return (x + xx * mu_ref[i:i + 1, :]).astype(BF16)

    r = _dot(mixed(0), wr_ref[...])
    k = _dot(mixed(2), wk_ref[...])
    v = _dot(mixed(3), wv_ref[...])
    g_ref[...] = _dot(mixed(5), wg_ref[...]).astype(g_ref.dtype)
    lora_w = _dot(jnp.tanh(_dot(mixed(1), wla_ref[...])).astype(BF16), wlb_ref[...])
    lora_a = _dot(_dot(mixed(4), ala_ref[...]).astype(BF16), alb_ref[...])
    w_log = -_softplus(-(w0_ref[...] + lora_w)) - 0.5
    lw = -jnp.exp(w_log)
    tril = tril_ref[...]
    for c in range(ts // RWKV_CHUNK):
        rows = slice(c * RWKV_CHUNK, (c + 1) * RWKV_CHUNK)
        cum_ref[rows, :] = sum(_dot(tril, piece) for piece in _split2(lw[rows, :]))
    a = _sigmoid(a0_ref[...] + lora_a)
    gmat = gmat_ref[...]
    kk = k * kk_ref[...]
    kap = kk * lax.rsqrt(_group_sum(kk * kk, gmat, split=False) + 1e-12)
    k2 = k * (1.0 + (a - 1.0) * ka_ref[...])
    bonus_ref[...] = _group_sum(r * k2 * rk_ref[...], gmat, split=False) * v
    r_ref[...] = r.astype(r_ref.dtype)
    k_ref[...] = k2.astype(k_ref.dtype)
    v_ref[...] = v.astype(v_ref.dtype)
    kap_ref[...] = kap.astype(kap_ref.dtype)
    b_ref[...] = (kap * a).astype(b_ref.dtype)


def _bd(x, left):
    z = jnp.zeros_like(x)
    return jnp.concatenate([jnp.where(left, x, z), jnp.where(left, z, x)], axis=0)


def _unbd(x_bd):
    c = x_bd.shape[0] // 2
    return x_bd[:c] + x_bd[c:]


def _rwkv_chunk_pair(L, r, k, v, kap, b, masks):
    C = RWKV_CHUNK
    left, strict, lower, same_sub, eye, first_row = masks
    Lc = L[C - 1:C, :]
    L_excl = jnp.where(first_row, 0.0, pltpu.roll(L, 1, 0))
    e_l, e_lx, e_nl, e_r = jnp.exp(L), jnp.exp(L_excl), jnp.exp(-L), jnp.exp(Lc - L)
    at = _bd(-kap * e_lx, left).astype(BF16)
    rt = _bd(r * e_l, left)
    bt = _bd(b * e_nl, left).astype(BF16)
    kt = _bd(k * e_nl, left).astype(BF16)
    bh = _bd(b * e_r, left).astype(BF16)
    kh = _bd(k * e_r, left).astype(BF16)
    vb = _bd(v, left).astype(BF16)

    a1 = _dot_nt(jnp.concatenate([at, rt.astype(BF16)], axis=0), jnp.concatenate([bt, kt], axis=0))
    yield None
    n = jnp.where(strict, a1[:2 * C, :2 * C], 0.0)
    ak = jnp.where(strict, a1[:2 * C, 2 * C:], 0.0).astype(BF16)
    rb = jnp.where(lower, a1[2 * C:, :2 * C], 0.0).astype(BF16)
    rk = jnp.where(lower, a1[2 * C:, 2 * C:], 0.0).astype(BF16)

    nd = jnp.where(same_sub, n, 0.0)
    no = (n - nd).astype(BF16)
    ndb = nd.astype(BF16)
    n2 = _dot(ndb, ndb)
    akv = _dot(ak, vb)
    yield None
    n2b = n2.astype(BF16)
    n4 = _dot(n2b, n2b)
    t01 = _dot((eye + nd).astype(BF16), (eye + n2).astype(BF16))
    yield None
    n4b = n4.astype(BF16)
    n8 = _dot(n4b, n4b)
    yield None
    t23 = _dot((eye + n4).astype(BF16), (eye + n8).astype(BF16))
    yield None
    tdb = _dot(t01.astype(BF16), t23.astype(BF16)).astype(BF16)
    yield None
    x1 = _dot(tdb, no)
    yield None
    x1b = x1.astype(BF16)
    x2 = _dot(x1b, x1b)
    yield None
    tx = _dot((eye + x1).astype(BF16), (eye + x2).astype(BF16))
    yield None
    t = _dot(tx.astype(BF16), tdb).astype(BF16)
    yield None
    pq = _dot(t, jnp.concatenate([at, akv.astype(BF16)], axis=1)).astype(BF16)
    yield None
    z = jnp.concatenate([pq, jnp.concatenate([jnp.zeros_like(vb), vb], axis=1)], axis=0)
    ry = _dot(jnp.concatenate([rb, rk], axis=1), z)
    mg = _dot_tn(z, jnp.concatenate([bh, kh], axis=0))
    yield None
    rp = rt + ry[:, :2 * C]
    yl = ry[:, 2 * C:]
    yield _unbd(rp), _unbd(yl), _unbd(mg[:2 * C]), _unbd(mg[2 * C:]), jnp.exp(Lc)


def _rwkv_chunk_kernel(cum_ref, r_ref, k_ref, v_ref, kap_ref, b_ref,
                       rp_ref, yl_ref, mm_ref, gg_ref, gam_ref, *, pairs, cpb):
    C = RWKV_CHUNK
    lane = lax.broadcasted_iota(jnp.int32, (1, LANES), 1)
    left = lane < RWKV_HEAD_DIM
    ri = lax.broadcasted_iota(jnp.int32, (2 * C, 2 * C), 0)
    ci = lax.broadcasted_iota(jnp.int32, (2 * C, 2 * C), 1)
    same_head = (ri // C) == (ci // C)
    strict = same_head & ((ri % C) > (ci % C))
    lower = same_head & ((ri % C) >= (ci % C))
    same_sub = (ri // RWKV_SUB) == (ci // RWKV_SUB)
    eye = jnp.where(ri == ci, 1.0, 0.0).astype(F32)
    first_row = lax.broadcasted_iota(jnp.int32, (C, 1), 0) == 0
    masks = (left, strict, lower, same_sub, eye, first_row)
    jobs = [(c, slice(c * C, (c + 1) * C), slice(p * LANES, (p + 1) * LANES))
            for c in range(cpb) for p in range(pairs)]
    results = _round_robin([
        _rwkv_chunk_pair(cum_ref[rows, sl], r_ref[rows, sl].astype(F32), k_ref[rows, sl].astype(F32),
                         v_ref[rows, sl].astype(F32), kap_ref[rows, sl].astype(F32),
                         b_ref[rows, sl].astype(F32), masks)
        for _, rows, sl in jobs])
    for (c, rows, sl), (rp, yl, mm, gg, gam) in zip(jobs, results):
        rp_ref[rows, sl] = rp.astype(rp_ref.dtype)
        yl_ref[rows, sl] = yl
        mm_ref[c, :, sl] = mm.astype(mm_ref.dtype)
        gg_ref[c, :, sl] = gg
        gam_ref[c, :, sl] = gam


def _rwkv_seq_kernel(rp_ref, yl_ref, mm_ref, gg_ref, gam_ref, bonus_ref, gmat_ref, gng_ref, gnb_ref,
                     o_ref, s_sc, y_sc, *, pairs, cb):
    C = RWKV_CHUNK
    lane = lax.broadcasted_iota(jnp.int32, (1, LANES), 1)
    left = lane < RWKV_HEAD_DIM

    @pl.when(pl.program_id(2) == 0)
    def _():
        s_sc[...] = jnp.zeros_like(s_sc)

    states = [s_sc[p] for p in range(pairs)]
    for c in range(cb):
        rows = slice(c * C, (c + 1) * C)
        for p in range(pairs):
            sl = slice(p * LANES, (p + 1) * LANES)
            s = states[p]
            sb = s.astype(BF16)
            y_sc[rows, sl] = _dot_nt(rp_ref[rows, sl], sb) + yl_ref[rows, sl]
            mm = _bd(mm_ref[c, :, sl], left)
            gg = _bd(gg_ref[c, :, sl], left)
            states[p] = s * gam_ref[c, :, sl] + _dot(sb, mm) + gg
    for p in range(pairs):
        s_sc[p] = states[p]

    y = y_sc[...]
    gmat = gmat_ref[...]
    inv_n = 1.0 / RWKV_HEAD_DIM
    yc = y - _group_sum(y, gmat) * inv_n
    var = _group_sum(yc * yc, gmat) * inv_n
    yn = yc * lax.rsqrt(var + RWKV_GN_EPS) * gng_ref[...] + gnb_ref[...]
    o_ref[...] = (yn + bonus_ref[...]).astype(o_ref.dtype)


def _rwkv_layer(x, xb, mu, w_in, w0, w_lora_a, w_lora_b, a0, a_lora_a, a_lora_b, k_k, k_a, r_k,
                gn_g, gn_b, w_out, ln_g, ln_b, B, S, ts=512, pairs=8, seq_pairs=8, chunks_per_step=2):
    D = D_MODEL
    C = RWKV_CHUNK
    nc = S // C
    ts = _pick_tile(S, ts)
    tril = jnp.tril(jnp.ones((C, C), BF16))
    w_r, w_k, w_v, w_g =[w.astype(BF16) for w in jnp.split(w_in, 4, axis=1)]
    gmat = jnp.asarray(np.kron(np.eye(2, dtype=np.float32), np.ones((RWKV_HEAD_DIM, RWKV_HEAD_DIM), np.float32)), BF16)
    row = lambda a: a.reshape(1, D)
    x3 = x.reshape(B, S, D)
    full = lambda shape: pl.BlockSpec(shape, lambda b, j: (0,) * len(shape))
    tile = lambda: pl.BlockSpec((None, ts, D), lambda b, j: (b, j, 0))
    lr = w_lora_a.shape[1]
    outs = pl.pallas_call(
        _rwkv_proj_kernel,
        grid=(B, S // ts),
        in_specs=[tile(),
                  pl.BlockSpec((None, 8, D), lambda b, j: (b, jnp.maximum(j * (ts // 8) - 1, 0), 0)),
                  full((6, D)), full((D, D)), full((D, D)), full((D, D)), full((D, D)),
                  full((D, lr)), full((lr, D)), full((D, lr)), full((lr, D)),
                  full((1, D)), full((1, D)), full((1, D)), full((1, D)), full((1, D)),
                  full((LANES, LANES)), full((C, C))],
        out_specs=[tile() for _ in range(8)],
        out_shape=[jax.ShapeDtypeStruct((B, S, D), dt) for dt in (BF16, BF16, BF16, BF16, BF16, BF16, F32, F32)],
        compiler_params=_cparams("parallel", "arbitrary"),
        name="rwkv_projections",
    )(x3, x3, mu, w_r, w_k, w_v, w_g, w_lora_a.astype(BF16), w_lora_b.astype(BF16),
      a_lora_a.astype(BF16), a_lora_b.astype(BF16), row(w0), row(a0), row(k_k), row(k_a), row(r_k), gmat, tril)
    r, k2, v, g, kap, bvec, cum, bonus = outs

    pw = pairs * LANES
    cpb = _pick_tile(nc, chunks_per_step)
    cblk = lambda: pl.BlockSpec((None, cpb * C, pw), lambda b, c, q: (b, c, q))
    sblk = lambda: pl.BlockSpec((None, cpb, C, pw), lambda b, c, q: (b, c, 0, q))
    rp, yl, mm, gg, gam = pl.pallas_call(
        functools.partial(_rwkv_chunk_kernel, pairs=pairs, cpb=cpb),
        grid=(B, nc // cpb, D // pw),
        in_specs=[cblk() for _ in range(6)],
        out_specs=[cblk(), cblk(), sblk(), sblk(),
                   pl.BlockSpec((None, cpb, 1, pw), lambda b, c, q: (b, c, 0, q))],
        out_shape=[jax.ShapeDtypeStruct((B, S, D), BF16), jax.ShapeDtypeStruct((B, S, D), F32),
                   jax.ShapeDtypeStruct((B, nc, C, D), BF16), jax.ShapeDtypeStruct((B, nc, C, D), F32),
                   jax.ShapeDtypeStruct((B, nc, 1, D), F32)],
        compiler_params=_cparams("parallel", "parallel", "parallel"),
        name="rwkv_chunk_summaries",
    )(cum, r, k2, v, kap, bvec)

    cb = _pick_tile(nc, 8)
    pairs = seq_pairs
    pw = pairs * LANES
    o = pl.pallas_call(
        functools.partial(_rwkv_seq_kernel, pairs=pairs, cb=cb),
        grid=(B, D // pw, nc // cb),
        in_specs=[pl.BlockSpec((None, cb * C, pw), lambda b, q, j: (b, j, q)),
                  pl.BlockSpec((None, cb * C, pw), lambda b, q, j: (b, j, q)),
                  pl.BlockSpec((None, cb, C, pw), lambda b, q, j: (b, j, 0, q)),
                  pl.BlockSpec((None, cb, C, pw), lambda b, q, j: (b, j, 0, q)),
                  pl.BlockSpec((None, cb, 1, pw), lambda b, q, j: (b, j, 0, q)),
                  pl.BlockSpec((None, cb * C, pw), lambda b, q, j: (b, j, q)),
                  pl.BlockSpec((LANES, LANES), lambda b, q, j: (0, 0)),
                  pl.BlockSpec((1, pw), lambda b, q, j: (0, q)),
                  pl.BlockSpec((1, pw), lambda b, q, j: (0, q))],
        out_specs=pl.BlockSpec((None, cb * C, pw), lambda b, q, j: (b, j, q)),
        out_shape=jax.ShapeDtypeStruct((B, S, D), BF16),
        scratch_shapes=[pltpu.VMEM((pairs, 2 * C, LANES), F32), pltpu.VMEM((cb * C, pw), F32)],
        compiler_params=_cparams("parallel", "parallel", "arbitrary"),
        name="rwkv_state_scan",
    )(rp, yl, mm, gg, gam, bonus, gmat, row(gn_g), row(gn_b))
    return _outproj_ln(g.reshape(B * S, D), 0, o.reshape(B * S, D), x, w_out, ln_g, ln_b)


def kernel(x, ln_g, ln_b, fox_w_in, fox_b_f, fox_w_out, dsa_w_in, dsa_kv_norm_g, dsa_w_uk, dsa_w_uv, dsa_w_out, rwkv_mu, rwkv_w_in, rwkv_w0, rwkv_w_lora_a, rwkv_w_lora_b, rwkv_a0, rwkv_a_lora_a, rwkv_a_lora_b, rwkv_k_k, rwkv_k_a, rwkv_r_k, rwkv_gn_g, rwkv_gn_b, rwkv_w_out, ret_w_in, ret_gn_g, ret_w_out):
    B, S, D = x.shape
    h = x.reshape(B * S, D)
    h, hb = _fox_layer(h, h, fox_w_in, fox_b_f, fox_w_out, ln_g[0], ln_b[0], B, S)
    h, hb = _dsa_layer(h, hb, dsa_w_in, dsa_kv_norm_g, dsa_w_uk, dsa_w_uv, dsa_w_out, ln_g[1], ln_b[1], B, S)
    h, hb = _rwkv_layer(h, hb, rwkv_mu, rwkv_w_in, rwkv_w0, rwkv_w_lora_a, rwkv_w_lora_b, rwkv_a0,
                        rwkv_a_lora_a, rwkv_a_lora_b, rwkv_k_k, rwkv_k_a, rwkv_r_k, rwkv_gn_g, rwkv_gn_b,
                        rwkv_w_out, ln_g[2], ln_b[2], B, S)
    h, hb = _ret_layer(h, hb, ret_w_in, ret_gn_g, ret_w_out, ln_g[3], ln_b[3], B, S)
    return h.reshape(B, S, D)
```

```python
import functools
import math

import jax
import jax.numpy as jnp
import numpy as np
from jax import lax
from jax.experimental import pallas as pl
from jax.experimental.pallas import tpu as pltpu

F32 = jnp.float32
BF16 = jnp.bfloat16

D_MODEL = 1024
DEPTH = 4
LN_EPS = 1e-5
RMS_EPS = 1e-6
DN_ALPHA = (2 * DEPTH) ** 0.25
ROPE_THETA = 500000.0

FOX_HEADS = 8
FOX_HEAD_DIM = 128

RET_HEADS = 4
RET_HEAD_DIM = 256
RET_THETA = 10000.0

LANES = 128
VMEM_LIMIT = 48 * 1024 * 1024
NEG_BIG = -2.0 ** 100
LOG2E = 1.4426950408889634


def _cparams(*sem, vmem_limit=VMEM_LIMIT):
    return pltpu.CompilerParams(dimension_semantics=sem, vmem_limit_bytes=vmem_limit)


def _dot(a, b):
    return jnp.dot(a, b, preferred_element_type=F32)


def _dot_nt(a, b):
    return lax.dot_general(a, b, (((1,), (1,)), ((), ())), preferred_element_type=F32)


def _dot_tn(a, b):
    return lax.dot_general(a, b, (((0,), (0,)), ((), ())), preferred_element_type=F32)


def _split2(x):
    hi = x.astype(BF16)
    lo = (x - hi.astype(F32)).astype(BF16)
    return hi, lo


def _split3(x):
    p1 = x.astype(BF16)
    r1 = x - p1.astype(F32)
    p2 = r1.astype(BF16)
    p3 = (r1 - p2.astype(F32)).astype(BF16)
    return p1, p2, p3


def _sigmoid(x):
    return 1.0 / (1.0 + jnp.exp(-x))


def _round_robin(gens):
    results = [None] * len(gens)
    live = list(range(len(gens)))
    while live:
        still = []
        for i in live:
            try:
                out = next(gens[i])
            except StopIteration:
                continue
            if out is not None:
                results[i] = out
            still.append(i)
        live = still
    return results


def _pick_tile(n, pref):
    t = min(n, pref)
    while n % t:
        t //= 2
    return t


def _mm_kernel(a_ref, w_ref, o_ref):
    o_ref[...] = _dot(a_ref[...].astype(BF16), w_ref[...]).astype(o_ref.dtype)


def _matmul(a, w, out_dtype=BF16, tm=2048, tn=1024):
    M, K = a.shape
    N = w.shape[1]
    tm = _pick_tile(M, tm)
    if N % tn:
        tn = N
    return pl.pallas_call(
        _mm_kernel,
        grid=(M // tm, N // tn),
        in_specs=[pl.BlockSpec((tm, K), lambda i, j: (i, 0)),
                  pl.BlockSpec((K, tn), lambda i, j: (0, j))],
        out_specs=pl.BlockSpec((tm, tn), lambda i, j: (i, j)),
        out_shape=jax.ShapeDtypeStruct((M, N), out_dtype),
        compiler_params=_cparams("parallel", "arbitrary"),
        name="proj_matmul",
    )(a, w)


def _outproj_ln_kernel(g_ref, o_ref, x_ref, w_ref, lg_ref, lb_ref, xo_ref, xb_ref):
    half_g = g_ref[...] * 0.5
    h = (half_g * o_ref[...]) * (1.0 + jnp.tanh(half_g))
    z = x_ref[...] + _dot(h.astype(BF16), w_ref[...])
    zc = z - jnp.mean(z, axis=-1, keepdims=True)
    var = jnp.mean(zc * zc, axis=-1, keepdims=True)
    out = zc * lax.rsqrt(var + LN_EPS / DN_ALPHA ** 2) * lg_ref[...] + lb_ref[...]
    xo_ref[...] = out
    xb_ref[...] = out.astype(BF16)


def _outproj_ln(gate_arr, gate_col, o, x, w_out, ln_g, ln_b, tm=1024):
    M, D = x.shape
    tm = _pick_tile(M, tm)
    return pl.pallas_call(
        _outproj_ln_kernel,
        grid=(M // tm,),
        in_specs=[pl.BlockSpec((tm, D), lambda i: (i, gate_col)),
                  pl.BlockSpec((tm, D), lambda i: (i, 0)),
                  pl.BlockSpec((tm, D), lambda i: (i, 0)),
                  pl.BlockSpec((D, D), lambda i: (0, 0)),
                  pl.BlockSpec((1, D), lambda i: (0, 0)),
                  pl.BlockSpec((1, D), lambda i: (0, 0))],
        out_specs=[pl.BlockSpec((tm, D), lambda i: (i, 0)),
                   pl.BlockSpec((tm, D), lambda i: (i, 0))],
        out_shape=[jax.ShapeDtypeStruct((M, D), F32), jax.ShapeDtypeStruct((M, D), BF16)],
        compiler_params=_cparams("parallel"),
        name="outproj_layernorm",
    )(gate_arr, o, x, (w_out * (1.0 / DN_ALPHA)).astype(BF16), ln_g.reshape(1, D), ln_b.reshape(1, D))


FOX_BIAS_PIECES = 3


def _fox_cum_kernel(x_ref, wh_ref, wl_ref, bf_ref, tril_ref, place_ref, pc_ref, carry_sc):
    @pl.when(pl.program_id(1) == 0)
    def _():
        carry_sc[...] = jnp.zeros_like(carry_sc)

    x_hi, x_lo = _split2(x_ref[...])
    z = _dot(x_hi, wh_ref[...]) + _dot(x_lo, wh_ref[...]) + _dot(x_hi, wl_ref[...]) + bf_ref[...]
    logf = jnp.minimum(z, 0.0) - jnp.log(1.0 + jnp.exp(-jnp.abs(z)))
    p1, p2, p3 = _split3(logf)
    tril = tril_ref[...]
    c = _dot(tril, p1) + _dot(tril, p2) + _dot(tril, p3) + carry_sc[...]
    carry_sc[...] = c[c.shape[0] - 1:, :]
    pieces = _split3(c * (-LOG2E))
    pc_ref[...] = sum(_dot(pieces[p], place_ref[p]) for p in range(FOX_BIAS_PIECES)).astype(pc_ref.dtype)


def _fox_cum(x3, w_f, b_f, ts=512):
    B, S, D = x3.shape
    H = w_f.shape[1]
    ts = _pick_tile(S, ts)
    w_pad = jnp.zeros((D, LANES), F32).at[:, :H].set(w_f)
    w_hi, w_lo = _split2(w_pad)
    b_pad = jnp.zeros((1, LANES), F32).at[0, :H].set(b_f)
    tril = jnp.asarray(np.tril(np.ones((ts, ts), np.float32)), BF16)
    place = np.zeros((FOX_BIAS_PIECES, LANES, LANES), np.float32)
    for p in range(FOX_BIAS_PIECES):
        for h in range(H):
            place[p, h, FOX_BIAS_PIECES * h + p] = 1.0
    return pl.pallas_call(
        _fox_cum_kernel,
        grid=(B, S // ts),
        in_specs=[pl.BlockSpec((None, ts, D), lambda b, j: (b, j, 0)),
                  pl.BlockSpec((D, LANES), lambda b, j: (0, 0)),
                  pl.BlockSpec((D, LANES), lambda b, j: (0, 0)),
                  pl.BlockSpec((1, LANES), lambda b, j: (0, 0)),
                  pl.BlockSpec((ts, ts), lambda b, j: (0, 0)),
                  pl.BlockSpec((FOX_BIAS_PIECES, LANES, LANES), lambda b, j: (0, 0, 0))],
        out_specs=pl.BlockSpec((None, ts, LANES), lambda b, j: (b, j, 0)),
        out_shape=jax.ShapeDtypeStruct((B, S, LANES), BF16),
        scratch_shapes=[pltpu.VMEM((1, LANES), F32)],
        compiler_params=_cparams("parallel", "arbitrary"),
        name="fox_decay_cumsum",
    )(x3, w_hi, w_lo, b_pad, tril, jnp.asarray(place, BF16))


ONES_ROWS = 16


def _fox_attn_kernel(q_ref, k_ref, v_ref, pc_ref, o_ref, kaug_sc, vt_sc, m_sc, acc_sc, *, tq, nsub, seq, unroll):
    dh = FOX_HEAD_DIM
    h = pl.program_id(1)
    g = pl.program_id(2)

    @pl.when(g == 0)
    def _():
        kaug_sc[:, :dh] = k_ref[...]
        kaug_sc[:, dh:] = pc_ref[...]
        for c in range(seq // tq):
            rows = slice(c * tq, (c + 1) * tq)
            vt_sc[:dh, rows] = v_ref[rows, :].astype(F32).T.astype(BF16)
        vt_sc[dh:, :] = jnp.ones((ONES_ROWS, seq), BF16)

    feat = lax.broadcasted_iota(jnp.int32, (LANES, tq), 0)
    bias_rows = (feat >= FOX_BIAS_PIECES * h) & (feat < FOX_BIAS_PIECES * (h + 1))
    ones_h = jnp.where(bias_rows, 1.0, 0.0).astype(BF16)
    q_aug = [jnp.concatenate([q_ref[a * tq:(a + 1) * tq, :].astype(F32).T.astype(BF16), ones_h], axis=0)
             for a in range(nsub)]
    m_sc[...] = jnp.full_like(m_sc, NEG_BIG)
    acc_sc[...] = jnp.zeros_like(acc_sc)
    causal = (lax.broadcasted_iota(jnp.int32, (tq, tq), 0) <= lax.broadcasted_iota(jnp.int32, (tq, tq), 1))
    first = g * nsub

    def chain(a, tiles, diag_last):
        offs = [pl.multiple_of(j * tq, tq) for j in tiles]
        scores = []
        for off in offs:
            scores.append(_dot(kaug_sc[pl.ds(off, tq), :], q_aug[a]))
            yield None
        for n, (off, s) in enumerate(zip(offs, scores)):
            if diag_last and n == len(offs) - 1:
                s = jnp.where(causal, s, NEG_BIG)
            m_old = m_sc[a]
            m_new = jnp.maximum(m_old, jnp.max(s, axis=0, keepdims=True))
            alpha = jnp.exp2(m_old - m_new)
            pv = _dot(vt_sc[:, pl.ds(off, tq)], jnp.exp2(s - m_new).astype(BF16))
            yield None
            acc_sc[a] = alpha * acc_sc[a] + pv
            m_sc[a] = m_new
        yield None

    def body(jj, c):
        _round_robin([chain(a, [jj * unroll + u for u in range(unroll)], False) for a in range(nsub)])
        return c

    lax.fori_loop(0, first // unroll, body, 0)
    _round_robin([chain(a, [first + t for t in range(a + 1)], True) for a in range(nsub)])
    for a in range(nsub):
        acc = acc_sc[a]
        o_t = acc[:dh] / acc[dh:dh + 1]
        o_ref[a * tq:(a + 1) * tq, :] = o_t.T.astype(o_ref.dtype)


def _fox_attention(proj3, pieces, tq=256, nsub=8):
    B, S, _ = proj3.shape
    H, dh = FOX_HEADS, FOX_HEAD_DIM
    tq = _pick_tile(S, tq)
    nsub = _pick_tile(S // tq, nsub)
    tg = tq * nsub
    return pl.pallas_call(
        functools.partial(_fox_attn_kernel, tq=tq, nsub=nsub, seq=S, unroll=min(nsub, 4)),
        grid=(B, H, S // tg),
        in_specs=[pl.BlockSpec((None, tg, dh), lambda b, h, g: (b, g, h)),
                  pl.BlockSpec((None, S, dh), lambda b, h, g: (b, 0, H + h)),
                  pl.BlockSpec((None, S, dh), lambda b, h, g: (b, 0, 2 * H + h)),
                  pl.BlockSpec((None, S, LANES), lambda b, h, g: (b, 0, 0))],
        out_specs=pl.BlockSpec((None, tg, dh), lambda b, h, g: (b, g, h)),
        out_shape=jax.ShapeDtypeStruct((B, S, H * dh), BF16),
        scratch_shapes=[pltpu.VMEM((S, dh + LANES), BF16),
                        pltpu.VMEM((dh + ONES_ROWS, S), BF16),
                        pltpu.VMEM((nsub, 1, tq), F32),
                        pltpu.VMEM((nsub, dh + ONES_ROWS, tq), F32)],
        compiler_params=_cparams("parallel", "parallel", "arbitrary"),
        name="fox_attention",
    )(proj3, proj3, proj3, pieces)


def _fox_layer(x, xb, w_in, b_f, w_out, ln_g, ln_b, B, S):
    D = D_MODEL
    H, dh = FOX_HEADS, FOX_HEAD_DIM
    scale = dh ** -0.5 * LOG2E
    w_q, w_k, w_v, w_f, w_g = jnp.split(w_in, [H * dh, 2 * H * dh, 3 * H * dh, 3 * H * dh + H], axis=1)
    w_main = jnp.concatenate([w_q * scale, w_k, w_v, w_g], axis=1).astype(BF16)
    proj = _matmul(xb, w_main)
    pieces = _fox_cum(x.reshape(B, S, D), w_f, b_f)
    o = _fox_attention(proj.reshape(B, S, 4 * D), pieces)
    return _outproj_ln(proj, 3, o.reshape(B * S, D), x, w_out, ln_g, ln_b)


def _ret_kernel(q_ref, k_ref, v_ref, cos_ref, sin_ref, dm_ref, xi_ref, zeta_ref, gc_ref, gn_ref,
                o_ref, r_sc):
    @pl.when(pl.program_id(1) == 0)
    def _():
        r_sc[...] = jnp.zeros_like(r_sc)

    dk = RET_HEAD_DIM
    half = dk // 2
    cos = cos_ref[...]
    sin = sin_ref[...]

    def rope(x):
        x1, x2 = x[:, :half], x[:, half:]
        return jnp.concatenate([x1 * cos - x2 * sin, x2 * cos + x1 * sin], axis=-1)

    def head_chain(h):
        cols = slice(h * dk, (h + 1) * dk)
        q = rope(q_ref[:, cols].astype(F32))
        k = rope(k_ref[:, cols].astype(F32)) * (dk ** -0.5)
        v = v_ref[:, cols]
        qb = q.astype(BF16)
        r_old = r_sc[h]
        scores = _dot_nt(qb, k.astype(BF16))
        cross = _dot(qb, r_old.astype(BF16))
        kz = (k * zeta_ref[h]).astype(BF16)
        r_new = _dot_tn(kz, v)
        yield None
        o = _dot((scores * dm_ref[h]).astype(BF16), v)
        r_sc[h] = r_old * gc_ref[h] + r_new
        yield None
        o = o + cross * xi_ref[h]
        o = o * lax.rsqrt(jnp.mean(o * o, axis=-1, keepdims=True) + RMS_EPS) * gn_ref[:, cols]
        o_ref[:, cols] = o.astype(o_ref.dtype)
        yield None

    _round_robin([head_chain(h) for h in range(RET_HEADS)])


def _ret_layer(x, xb, w_in, gn_g, w_out, ln_g, ln_b, B, S, chunk=512):
    D = D_MODEL
    H, dk = RET_HEADS, RET_HEAD_DIM
    C = _pick_tile(S, chunk)
    proj = _matmul(xb, w_in.astype(BF16))
    f32 = np.float32
    inv = (f32(1.0) / (f32(RET_THETA) ** (np.arange(0, dk, 2, dtype=f32) / f32(dk)))).astype(f32)
    ang = (np.arange(S, dtype=f32)[:, None] * inv[None, :]).astype(f32)
    cos, sin = jnp.asarray(np.cos(ang)), jnp.asarray(np.sin(ang))
    log_g = np.log1p(-(f32(2.0) ** (f32(-5.0) - np.arange(H, dtype=f32)))).astype(f32)
    pos = np.arange(C, dtype=f32)
    diff = pos[:, None] - pos[None, :]
    d_mask = jnp.asarray(np.where(diff[None] >= 0, np.exp(np.maximum(diff, 0.0)[None] * log_g[:, None, None]),
                                  0.0).astype(f32))
    xi = jnp.asarray(np.broadcast_to(np.exp((pos[None, :] + 1.0) * log_g[:, None])[:, :, None],
                                     (H, C, dk)).astype(f32))
    zeta = jnp.asarray(np.broadcast_to(np.exp((C - 1.0 - pos[None, :]) * log_g[:, None])[:, :, None],
                                       (H, C, dk)).astype(f32))
    g_c = jnp.asarray(np.broadcast_to(np.exp(f32(C) * log_g)[:, None, None], (H, 1, dk)).astype(f32))
    p3 = proj.reshape(B, S, 4 * D)
    o = pl.pallas_call(
        _ret_kernel,
        grid=(B, S // C),
        in_specs=[pl.BlockSpec((None, C, D), lambda b, c: (b, c, 0)),
                  pl.BlockSpec((None, C, D), lambda b, c: (b, c, 1)),
                  pl.BlockSpec((None, C, D), lambda b, c: (b, c, 2)),
                  pl.BlockSpec((C, dk // 2), lambda b, c: (c, 0)),
                  pl.BlockSpec((C, dk // 2), lambda b, c: (c, 0)),
                  pl.BlockSpec((H, C, C), lambda b, c: (0, 0, 0)),
                  pl.BlockSpec((H, C, dk), lambda b, c: (0, 0, 0)),
                  pl.BlockSpec((H, C, dk), lambda b, c: (0, 0, 0)),
                  pl.BlockSpec((H, 1, dk), lambda b, c: (0, 0, 0)),
                  pl.BlockSpec((1, D), lambda b, c: (0, 0))],
        out_specs=pl.BlockSpec((None, C, D), lambda b, c: (b, c, 0)),
        out_shape=jax.ShapeDtypeStruct((B, S, D), BF16),
        scratch_shapes=[pltpu.VMEM((H, dk, dk), F32)],
        compiler_params=_cparams("parallel", "arbitrary"),
        name="retnet_retention",
    )(p3, p3, p3, cos, sin, d_mask, xi, zeta, g_c, gn_g.reshape(1, D))
    return _outproj_ln(proj, 3, o.reshape(B * S, D), x, w_out, ln_g, ln_b)


DSA_HEADS = 8
DSA_HEAD_DIM = 128
DSA_ROPE_DIM = 32
DSA_KV_RANK = 128
IDX_HEADS = 8
IDX_DIM = 64
IDX_ROPE_DIM = 16
TOPK_MAX = 256
INT_MIN = -2 ** 31
HALF_MIN = -2 ** 15
HALF_ROWS = 16
DSA_VMEM_LIMIT = 56 * 1024 * 1024
DSA_ONES_ROWS = 16


def _rope_perm(width, groups):
    p = np.zeros((width, width), np.float32)
    for start, half in groups:
        for j in range(half):
            p[start + half + j, start + j] = 1.0
            p[start + j, start + half + j] = 1.0
    return p


def _rope_cs(S, width, groups, theta_dims):
    f32 = np.float32
    c = np.ones((S, width), f32)
    sg = np.zeros((S, width), f32)
    pos = np.arange(S, dtype=f32)[:, None]
    for (start, half), rot_dim in zip(groups, theta_dims):
        inv = (f32(1.0) / (f32(ROPE_THETA) ** (np.arange(0, rot_dim, 2, dtype=f32) / f32(rot_dim)))).astype(f32)
        ang = (pos * inv[None, :]).astype(f32)
        cos, sin = np.cos(ang), np.sin(ang)
        c[:, start:start + half] = cos
        c[:, start + half:start + 2 * half] = cos
        sg[:, start:start + half] = -sin
        sg[:, start + half:start + 2 * half] = sin
    return jnp.asarray(c), jnp.asarray(sg)


def _dsa_prep_kernel(q_ref, qi_ref, ckv_ref, misc_ref, cq_ref, sq_ref, ci_ref, si_ref, cm_ref, sm_ref,
                     pq_ref, pi_ref, pm_ref, selk_ref, selw_ref, wuk_ref, kvg_ref,
                     qf_ref, kvl_ref, kvt_ref, qir_ref, kid_ref, wit_ref):
    H, dh = DSA_HEADS, DSA_HEAD_DIM
    lane = lax.broadcasted_iota(jnp.int32, (1, LANES), 1)
    rope_lanes = lane < DSA_ROPE_DIM
    cq, sq = cq_ref[...], sq_ref[...]
    scale = dh ** -0.5 * LOG2E
    for h in range(H):
        qh = q_ref[:, h * dh:(h + 1) * dh]
        qr = qh.astype(F32) * cq + _dot(qh, pq_ref[...]) * sq
        q_lat = _dot(qr.astype(BF16), wuk_ref[h])
        qf_ref[h, :dh, :] = (q_lat * scale).T.astype(BF16)
        qf_ref[h, dh:, :] = jnp.where(rope_lanes, qr * scale, 0.0).T.astype(BF16)
    ci, si = ci_ref[...], si_ref[...]
    for g in range(IDX_HEADS * IDX_DIM // LANES):
        qg = qi_ref[:, g * LANES:(g + 1) * LANES]
        qr = qg.astype(F32) * ci + _dot(qg, pi_ref[...]) * si
        qir_ref[g * LANES:(g + 1) * LANES, :] = (qr * (IDX_DIM ** -0.5)).T.astype(BF16)
    ckv = ckv_ref[...].astype(F32)
    ckv = ckv * lax.rsqrt(jnp.mean(ckv * ckv, axis=-1, keepdims=True) + RMS_EPS) * kvg_ref[...]
    misc = misc_ref[...]
    mr = (misc.astype(F32) * cm_ref[...] + _dot(misc, pm_ref[...]) * sm_ref[...])
    kvl_ref[:, :DSA_KV_RANK] = ckv.astype(BF16)
    kvl_ref[:, DSA_KV_RANK:] = jnp.where(rope_lanes, mr, 0.0).astype(BF16)
    kvt_ref[:DSA_KV_RANK, :] = ckv.T.astype(BF16)
    kvt_ref[DSA_KV_RANK:, :] = jnp.ones((DSA_ONES_ROWS, ckv.shape[0]), BF16)
    kid_ref[...] = _dot(mr.astype(BF16), selk_ref[...]).astype(BF16)
    wi = _dot(misc, selw_ref[...]) * (IDX_HEADS ** -0.5)
    wit_ref[...] = wi.T[:IDX_HEADS, :]


def _sort_key(x):
    b = pltpu.bitcast(x, jnp.int32)
    return jnp.where(b < 0, b ^ 0x7FFFFFFF, b)


def _dsa_main_kernel(qf_ref, kvl_ref, kvt_ref, qi_ref, kid_ref, wit_ref, wuv_ref, obuf_ref, o_ref,
                     key_sc, hi_sc, lo_sc, low_sc, m_sc, acc_sc, *, tq, tk, k_sel, seq, qtile):
    del obuf_ref
    H = DSA_HEADS
    q0 = qtile * tq
    nj = (q0 + tq + tk - 1) // tk
    qpos = q0 + lax.broadcasted_iota(jnp.int32, (tk, tq), 1)
    kpos0 = lax.broadcasted_iota(jnp.int32, (tk, tq), 0)
    feat = lax.broadcasted_iota(jnp.int32, (LANES, 1), 0)
    neg_inf_key = _sort_key(jnp.full((1, 1), -jnp.inf, F32))

    wit = wit_ref[...]
    qi_heads = []
    for h in range(IDX_HEADS):
        g = qi_ref[(h // 2) * LANES:(h // 2 + 1) * LANES, :]
        keep = (feat >= IDX_DIM) if (h % 2) else (feat < IDX_DIM)
        qi_heads.append(jnp.where(keep, g, jnp.zeros_like(g)))

    def score_tile(j, c):
        off = pl.multiple_of(j * tk, tk)
        ki = kid_ref[pl.ds(off, tk), :]
        scores = [_dot(ki, qi_heads[h]) for h in range(IDX_HEADS)]
        isc = jnp.zeros((tk, tq), F32)
        for h in range(IDX_HEADS):
            isc = isc + jnp.maximum(scores[h], 0.0) * wit[h:h + 1, :]
        isc = jnp.where(kpos0 + off <= qpos, isc + 0.0, -jnp.inf)
        key = _sort_key(isc)
        key_sc[pl.ds(off, tk), :] = key
        hi_sc[pl.ds(off, tk), :] = (key >> 16).astype(jnp.int16)
        lo_sc[pl.ds(off, tk), :] = ((key & 0xFFFF) + HALF_MIN).astype(jnp.int16)
        return c

    lax.fori_loop(0, nj, score_tile, 0)

    def count(pred_fn):
        acc = jnp.zeros((8, tq), jnp.int32)
        for j in range(nj):
            hit = jnp.where(pred_fn(key_sc[j * tk:(j + 1) * tk, :], j * tk), 1, 0)
            acc = acc + jnp.sum(hit.reshape(tk // 8, 8, tq), axis=0)
        return jnp.sum(acc, axis=0, keepdims=True)

    rows16 = tk // HALF_ROWS

    def count16(ref, pred_fn):
        accs = [jnp.zeros((HALF_ROWS, tq), jnp.int16) for _ in range(2)]
        for j in range(nj):
            hit = jnp.where(pred_fn(ref[j * tk:(j + 1) * tk, :].reshape(rows16, HALF_ROWS, tq)),
                            jnp.int16(1), jnp.int16(0))
            for r in range(rows16):
                accs[r % 2] = accs[r % 2] + hit[r]
        return jnp.sum((accs[0] + accs[1]).astype(jnp.int32), axis=0, keepdims=True)

    def as_half(v):
        return jnp.broadcast_to(v.astype(jnp.int16), (HALF_ROWS, tq))[None]

    def search16(ref, base0, cnt0, bits, want):
        def bit_step(t, carry):
            base, cnt_b = carry
            cand = base + lax.shift_left(jnp.int32(1), bits - 1 - t)
            cand16 = as_half(cand)
            c = count16(ref, lambda kt: kt >= cand16)
            ok = c >= want
            return jnp.where(ok, cand, base), jnp.where(ok, c, cnt_b)
        return lax.fori_loop(0, bits, bit_step, (base0, cnt0))

    zero16 = as_half(jnp.zeros((1, tq), jnp.int32))
    cnt_pos = count16(hi_sc, lambda kt: kt >= zero16)
    nonneg = cnt_pos >= k_sel
    t1, ge_hi = search16(hi_sc, jnp.where(nonneg, 0, HALF_MIN), jnp.where(nonneg, cnt_pos, nj * tk), 15, k_sel)
    t1_16 = as_half(t1)
    above = count16(hi_sc, lambda kt: kt > t1_16)

    for j in range(nj):
        rows = slice(j * tk, (j + 1) * tk)
        hi = hi_sc[rows, :].reshape(rows16, HALF_ROWS, tq)
        lo = lo_sc[rows, :].reshape(rows16, HALF_ROWS, tq)
        low_sc[rows, :] = jnp.where(hi == t1_16, lo, jnp.int16(HALF_MIN)).reshape(tk, tq)
    t2, ge_low = search16(low_sc, jnp.full((1, tq), HALF_MIN, jnp.int32), ge_hi - above, 16, k_sel - above)
    t2_16 = as_half(t2)
    thr = lax.shift_left(t1, 16) | (t2 - HALF_MIN)
    n_ge = above + ge_low
    n_gt = above + count16(low_sc, lambda kt: kt > t2_16)
    need = k_sel - n_gt
    excess = ((n_ge - n_gt) > need) & (thr > neg_inf_key)
    any_excess = jnp.max(jnp.where(excess, 1, 0)) > 0

    def tie_cut():
        def step(t, lo):
            cand = lo + lax.shift_left(jnp.int32(1), int(math.log2(seq)) - t)
            c = count(lambda kt, off: (kt == thr) & (kpos0 + off < cand))
            return jnp.where(c < need, cand, lo)
        lo = lax.fori_loop(0, int(math.log2(seq)) + 1, step, jnp.zeros((1, tq), jnp.int32))
        return jnp.where(excess, lo, seq)

    cut = lax.cond(any_excess, tie_cut, lambda: jnp.full((1, tq), seq, jnp.int32))

    m_sc[...] = jnp.full_like(m_sc, NEG_BIG)
    acc_sc[...] = jnp.zeros_like(acc_sc)

    def attn_tile(j, c):
        off = pl.multiple_of(j * tk, tk)
        kt = key_sc[pl.ds(off, tk), :]
        kpos = kpos0 + off
        bias = jnp.where(kt > thr, 0.0, jnp.where(kt == thr, jnp.where(kpos <= cut, 0.0, NEG_BIG), NEG_BIG))
        bias = jnp.where(kpos <= qpos, bias, NEG_BIG)
        kv = kvl_ref[pl.ds(off, tk), :]
        kvt = kvt_ref[:, pl.ds(off, tk)]

        def head_step(h):
            s = _dot(kv, qf_ref[h]) + bias
            yield None
            m_old = m_sc[h]
            m_new = jnp.maximum(m_old, jnp.max(s, axis=0, keepdims=True))
            alpha = jnp.exp2(m_old - m_new)
            pv = _dot(kvt, jnp.exp2(s - m_new).astype(BF16))
            yield None
            acc_sc[h] = alpha * acc_sc[h] + pv
            m_sc[h] = m_new
            yield None

        _round_robin([head_step(h) for h in range(H)])
        return c

    lax.fori_loop(0, nj, attn_tile, 0)
    for h in range(H):
        acc = acc_sc[h]
        o_lat_t = (acc[:DSA_KV_RANK] / acc[DSA_KV_RANK:DSA_KV_RANK + 1]).astype(BF16)
        o_ref[:, h * DSA_HEAD_DIM:(h + 1) * DSA_HEAD_DIM] = _dot_tn(o_lat_t, wuv_ref[h]).astype(o_ref.dtype)


def _dsa_layer(x, xb, w_in, kv_norm_g, w_uk, w_uv, w_out, ln_g, ln_b, B, S, tq=512, tk=512):
    D = D_MODEL
    H, dh, dr, dc = DSA_HEADS, DSA_HEAD_DIM, DSA_ROPE_DIM, DSA_KV_RANK
    HI, di = IDX_HEADS, IDX_DIM
    w_q, w_ckv, w_kr, w_qi, w_ki, w_wi, w_g = jnp.split(
        w_in, np.cumsum([H * dh, dc, dr, HI * di, di, HI]).tolist(), axis=1)
    w_misc = jnp.concatenate([w_kr, w_ki, w_wi, jnp.zeros((D, LANES - dr - di - HI), F32)], axis=1)
    w_main = jnp.concatenate([w_q, w_g, w_qi, w_ckv, w_misc], axis=1).astype(BF16)
    n_main = w_main.shape[1]
    proj = _matmul(xb, w_main, tm=1024, tn=n_main)
    c_q, c_qi, c_ckv, c_misc = 0, 2 * D // LANES, (2 * D + HI * di) // LANES, (2 * D + HI * di + dc) // LANES

    q_groups = [(0, dr // 2)]
    i_groups = [(0, IDX_ROPE_DIM // 2), (di, IDX_ROPE_DIM // 2)]
    m_groups = [(0, dr // 2), (dr, IDX_ROPE_DIM // 2)]
    cq, sq = _rope_cs(S, LANES, q_groups, [dr])
    ci, si = _rope_cs(S, LANES, i_groups, [IDX_ROPE_DIM, IDX_ROPE_DIM])
    cm, sm = _rope_cs(S, LANES, m_groups, [dr, IDX_ROPE_DIM])
    pq = jnp.asarray(_rope_perm(LANES, q_groups), BF16)
    pi = jnp.asarray(_rope_perm(LANES, i_groups), BF16)
    pm = jnp.asarray(_rope_perm(LANES, m_groups), BF16)
    selk = np.zeros((LANES, LANES), np.float32)
    for j in range(di):
        selk[dr + j, j] = 1.0
        selk[dr + j, di + j] = 1.0
    selw = np.zeros((LANES, LANES), np.float32)
    for j in range(HI):
        selw[dr + di + j, j] = 1.0
    wuk = jnp.concatenate([jnp.zeros((H, dr, dc), F32), jnp.transpose(w_uk, (0, 2, 1))], axis=1).astype(BF16)

    ts = _pick_tile(S, 512)
    p3 = proj.reshape(B, S, n_main)
    tab = lambda: pl.BlockSpec((ts, LANES), lambda b, j: (j, 0))
    mat = lambda: pl.BlockSpec((LANES, LANES), lambda b, j: (0, 0))
    qf, kvl, kvt, qir, kid, wit = pl.pallas_call(
        _dsa_prep_kernel,
        grid=(B, S // ts),
        in_specs=[pl.BlockSpec((None, ts, H * dh), lambda b, j: (b, j, 0)),
                  pl.BlockSpec((None, ts, HI * di), lambda b, j: (b, j, c_qi * LANES // (HI * di))),
                  pl.BlockSpec((None, ts, dc), lambda b, j: (b, j, c_ckv)),
                  pl.BlockSpec((None, ts, LANES), lambda b, j: (b, j, c_misc)),
                  tab(), tab(), tab(), tab(), tab(), tab(),
                  mat(), mat(), mat(), mat(), mat(),
                  pl.BlockSpec((H, LANES, dc), lambda b, j: (0, 0, 0)),
                  pl.BlockSpec((1, dc), lambda b, j: (0, 0))],
        out_specs=[pl.BlockSpec((None, H, 2 * dc, ts), lambda b, j: (b, 0, 0, j)),
                   pl.BlockSpec((None, ts, 2 * dc), lambda b, j: (b, j, 0)),
                   pl.BlockSpec((None, dc + DSA_ONES_ROWS, ts), lambda b, j: (b, 0, j)),
                   pl.BlockSpec((None, HI * di, ts), lambda b, j: (b, 0, j)),
                   pl.BlockSpec((None, ts, LANES), lambda b, j: (b, j, 0)),
                   pl.BlockSpec((None, HI, ts), lambda b, j: (b, 0, j))],
        out_shape=[jax.ShapeDtypeStruct((B, H, 2 * dc, S), BF16),
                   jax.ShapeDtypeStruct((B, S, 2 * dc), BF16),
                   jax.ShapeDtypeStruct((B, dc + DSA_ONES_ROWS, S), BF16),
                   jax.ShapeDtypeStruct((B, HI * di, S), BF16),
                   jax.ShapeDtypeStruct((B, S, LANES), BF16),
                   jax.ShapeDtypeStruct((B, HI, S), F32)],
        compiler_params=_cparams("parallel", "parallel"),
        name="dsa_prep",
    )(p3, p3, p3, p3, cq, sq, ci, si, cm, sm, pq, pi, pm,
      jnp.asarray(selk, BF16), jnp.asarray(selw, BF16), wuk, kv_norm_g.reshape(1, dc))

    tq = _pick_tile(S, tq)
    tk = _pick_tile(S, tk)
    k_sel = min(TOPK_MAX, S // 4)
    wuv = w_uv.astype(BF16)
    o = jnp.zeros((B, S, H * dh), BF16)
    for i in range(S // tq):
        nkeys = -(-((i + 1) * tq) // tk) * tk
        o = pl.pallas_call(
            functools.partial(_dsa_main_kernel, tq=tq, tk=tk, k_sel=k_sel, seq=S, qtile=i),
            grid=(B,),
            in_specs=[pl.BlockSpec((None, H, 2 * dc, tq), lambda b, i=i: (b, 0, 0, i)),
                      pl.BlockSpec((None, nkeys, 2 * dc), lambda b: (b, 0, 0)),
                      pl.BlockSpec((None, dc + DSA_ONES_ROWS, nkeys), lambda b: (b, 0, 0)),
                      pl.BlockSpec((None, HI * di, tq), lambda b, i=i: (b, 0, i)),
                      pl.BlockSpec((None, nkeys, LANES), lambda b: (b, 0, 0)),
                      pl.BlockSpec((None, HI, tq), lambda b, i=i: (b, 0, i)),
                      pl.BlockSpec((H, dc, dh), lambda b: (0, 0, 0)),
                      pl.BlockSpec(memory_space=pl.ANY)],
            out_specs=pl.BlockSpec((None, tq, H * dh), lambda b, i=i: (b, i, 0)),
            out_shape=jax.ShapeDtypeStruct((B, S, H * dh), BF16),
            input_output_aliases={7: 0},
            scratch_shapes=[pltpu.VMEM((nkeys, tq), jnp.int32),
                            pltpu.VMEM((nkeys, tq), jnp.int16), pltpu.VMEM((nkeys, tq), jnp.int16),
                            pltpu.VMEM((nkeys, tq), jnp.int16),
                            pltpu.VMEM((H, 1, tq), F32),
                            pltpu.VMEM((H, dc + DSA_ONES_ROWS, tq), F32)],
            compiler_params=_cparams("parallel", vmem_limit=DSA_VMEM_LIMIT),
            name=f"dsa_select_attention_q{i}",
        )(qf, kvl, kvt, qir, kid, wit, wuv, o)
    return _outproj_ln(proj, 1, o.reshape(B * S, D), x, w_out, ln_g, ln_b)


RWKV_HEADS = 16
RWKV_HEAD_DIM = 64
RWKV_GN_EPS = 64e-5
RWKV_CHUNK = 64
RWKV_SUB = 16


def _group_sum(x, gmat, split=True):
    outs = []
    for c in range(x.shape[1] // LANES):
        xc = x[:, c * LANES:(c + 1) * LANES]
        if split:
            hi, lo = _split2(xc)
            outs.append(_dot(hi, gmat) + _dot(lo, gmat))
        else:
            outs.append(_dot(xc.astype(BF16), gmat))
    return outs[0] if len(outs) == 1 else jnp.concatenate(outs, axis=1)


def _softplus(y):
    return jnp.maximum(y, 0.0) + jnp.log(1.0 + jnp.exp(-jnp.abs(y)))


def _rwkv_proj_kernel(x_ref, xprev_ref, mu_ref, wr_ref, wk_ref, wv_ref, wg_ref, wla_ref, wlb_ref,
                      ala_ref, alb_ref, w0_ref, a0_ref, kk_ref, ka_ref, rk_ref, gmat_ref, tril_ref,
                      r_ref, k_ref, v_ref, g_ref, kap_ref, b_ref, cum_ref, bonus_ref):
    x = x_ref[...]
    ts = x.shape[0]
    prev = jnp.where(pl.program_id(1) == 0, 0.0, xprev_ref[7:8, :])
    rowid = lax.broadcasted_iota(jnp.int32, (ts, 1), 0)
    xx = jnp.where(rowid == 0, prev, pltpu.roll(x, 1, 0)) - x

    def mixed(i):
        return (x + xx * mu_ref[i:i + 1, :]).astype(BF16)

    r = _dot(mixed(0), wr_ref[...])
    k = _dot(mixed(2), wk_ref[...])
    v = _dot(mixed(3), wv_ref[...])
    g_ref[...] = _dot(mixed(5), wg_ref[...]).astype(g_ref.dtype)
    lora_w = _dot(jnp.tanh(_dot(mixed(1), wla_ref[...])).astype(BF16), wlb_ref[...])
    lora_a = _dot(_dot(mixed(4), ala_ref[...]).astype(BF16), alb_ref[...])
    w_log = -_softplus(-(w0_ref[...] + lora_w)) - 0.5
    lw = -jnp.exp(w_log)
    tril = tril_ref[...]
    for c in range(ts // RWKV_CHUNK):
        rows = slice(c * RWKV_CHUNK, (c + 1) * RWKV_CHUNK)
        cum_ref[rows, :] = sum(_dot(tril, piece) for piece in _split2(lw[rows, :]))
    a = _sigmoid(a0_ref[...] + lora_a)
    gmat = gmat_ref[...]
    kk = k * kk_ref[...]
    kap = kk * lax.rsqrt(_group_sum(kk * kk, gmat, split=False) + 1e-12)
    k2 = k * (1.0 + (a - 1.0) * ka_ref[...])
    bonus_ref[...] = _group_sum(r * k2 * rk_ref[...], gmat, split=False) * v
    r_ref[...] = r.astype(r_ref.dtype)
    k_ref[...] = k2.astype(k_ref.dtype)
    v_ref[...] = v.astype(v_ref.dtype)
    kap_ref[...] = kap.astype(kap_ref.dtype)
    b_ref[...] = (kap * a).astype(b_ref.dtype)


def _bd(x, left):
    z = jnp.zeros_like(x)
    return jnp.concatenate([jnp.where(left, x, z), jnp.where(left, z, x)], axis=0)


def _unbd(x_bd):
    c = x_bd.shape[0] // 2
    return x_bd[:c] + x_bd[c:]


def _rwkv_chunk_pair(L, r, k, v, kap, b, masks):
    C = RWKV_CHUNK
    left, strict, lower, same_sub, eye, first_row = masks
    Lc = L[C - 1:C, :]
    L_excl = jnp.where(first_row, 0.0, pltpu.roll(L, 1, 0))
    e_l, e_lx, e_nl, e_r = jnp.exp(L), jnp.exp(L_excl), jnp.exp(-L), jnp.exp(Lc - L)
    at = _bd(-kap * e_lx, left).astype(BF16)
    rt = _bd(r * e_l, left)
    bt = _bd(b * e_nl, left).astype(BF16)
    kt = _bd(k * e_nl, left).astype(BF16)
    bh = _bd(b * e_r, left).astype(BF16)
    kh = _bd(k * e_r, left).astype(BF16)
    vb = _bd(v, left).astype(BF16)

    a1 = _dot_nt(jnp.concatenate([at, rt.astype(BF16)], axis=0), jnp.concatenate([bt, kt], axis=0))
    yield None
    n = jnp.where(strict, a1[:2 * C, :2 * C], 0.0)
    ak = jnp.where(strict, a1[:2 * C, 2 * C:], 0.0).astype(BF16)
    rb = jnp.where(lower, a1[2 * C:, :2 * C], 0.0).astype(BF16)
    rk = jnp.where(lower, a1[2 * C:, 2 * C:], 0.0).astype(BF16)

    W = 2 * C
    nd = jnp.where(same_sub, n, 0.0)
    no = (n - nd).astype(BF16)
    ndb = nd.astype(BF16)
    n2 = _dot(ndb, ndb)
    akv = _dot(ak, vb)
    yield None
    n2b = n2.astype(BF16)
    t0 = eye + nd
    r = _dot(n2b, jnp.concatenate([n2b, t0.astype(BF16)], axis=1))
    yield None
    n4b = r[:, :W].astype(BF16)
    t01 = t0 + r[:, W:]
    r = _dot(n4b, jnp.concatenate([n4b, t01.astype(BF16)], axis=1))
    yield None
    u = t01 + r[:, W:]
    td = u + _dot(r[:, :W].astype(BF16), u.astype(BF16))
    yield None
    tdb = td.astype(BF16)
    x1 = _dot(tdb, no)
    yield None
    x1b = x1.astype(BF16)
    r = _dot(x1b, jnp.concatenate([x1b, tdb], axis=1))
    yield None
    w = td + r[:, W:]
    t = (w + _dot(r[:, :W].astype(BF16), w.astype(BF16))).astype(BF16)
    yield None
    pq = _dot(t, jnp.concatenate([at, akv.astype(BF16)], axis=1)).astype(BF16)
    yield None
    z = jnp.concatenate([pq, jnp.concatenate([jnp.zeros_like(vb), vb], axis=1)], axis=0)
    ry = _dot(jnp.concatenate([rb, rk], axis=1), z)
    mg = _dot_tn(z, jnp.concatenate([bh, kh], axis=0))
    yield None
    rp = rt + ry[:, :2 * C]
    yl = ry[:, 2 * C:]
    yield _unbd(rp), _unbd(yl), _unbd(mg[:2 * C]), _unbd(mg[2 * C:]), jnp.exp(Lc)


def _rwkv_chunk_kernel(cum_ref, r_ref, k_ref, v_ref, kap_ref, b_ref,
                       rp_ref, yl_ref, mm_ref, gg_ref, gam_ref, *, pairs, cpb):
    C = RWKV_CHUNK
    lane = lax.broadcasted_iota(jnp.int32, (1, LANES), 1)
    left = lane < RWKV_HEAD_DIM
    ri = lax.broadcasted_iota(jnp.int32, (2 * C, 2 * C), 0)
    ci = lax.broadcasted_iota(jnp.int32, (2 * C, 2 * C), 1)
    same_head = (ri // C) == (ci // C)
    strict = same_head & ((ri % C) > (ci % C))
    lower = same_head & ((ri % C) >= (ci % C))
    same_sub = (ri // RWKV_SUB) == (ci // RWKV_SUB)
    eye = jnp.where(ri == ci, 1.0, 0.0).astype(F32)
    first_row = lax.broadcasted_iota(jnp.int32, (C, 1), 0) == 0
    masks = (left, strict, lower, same_sub, eye, first_row)
    jobs = [(c, slice(c * C, (c + 1) * C), slice(p * LANES, (p + 1) * LANES))
            for c in range(cpb) for p in range(pairs)]
    results = _round_robin([
        _rwkv_chunk_pair(cum_ref[rows, sl], r_ref[rows, sl].astype(F32), k_ref[rows, sl].astype(F32),
                         v_ref[rows, sl].astype(F32), kap_ref[rows, sl].astype(F32),
                         b_ref[rows, sl].astype(F32), masks)
        for _, rows, sl in jobs])
    for (c, rows, sl), (rp, yl, mm, gg, gam) in zip(jobs, results):
        rp_ref[rows, sl] = rp.astype(rp_ref.dtype)
        yl_ref[rows, sl] = yl
        mm_ref[c, :, sl] = mm.astype(mm_ref.dtype)
        gg_ref[c, :, sl] = gg
        gam_ref[c, :, sl] = gam


def _rwkv_seq_kernel(rp_ref, yl_ref, mm_ref, gg_ref, gam_ref, bonus_ref, gmat_ref, gng_ref, gnb_ref,
                     o_ref, s_sc, y_sc, *, pairs, cb):
    C = RWKV_CHUNK
    lane = lax.broadcasted_iota(jnp.int32, (1, LANES), 1)
    left = lane < RWKV_HEAD_DIM

    @pl.when(pl.program_id(2) == 0)
    def _():
        s_sc[...] = jnp.zeros_like(s_sc)

    states = [s_sc[p] for p in range(pairs)]
    for c in range(cb):
        rows = slice(c * C, (c + 1) * C)
        for p in range(pairs):
            sl = slice(p * LANES, (p + 1) * LANES)
            s = states[p]
            sb = s.astype(BF16)
            y_sc[rows, sl] = _dot_nt(rp_ref[rows, sl], sb) + yl_ref[rows, sl]
            mm = _bd(mm_ref[c, :, sl], left)
            gg = _bd(gg_ref[c, :, sl], left)
            states[p] = s * gam_ref[c, :, sl] + _dot(sb, mm) + gg
    for p in range(pairs):
        s_sc[p] = states[p]

    y = y_sc[...]
    gmat = gmat_ref[...]
    inv_n = 1.0 / RWKV_HEAD_DIM
    yc = y - _group_sum(y, gmat) * inv_n
    var = _group_sum(yc * yc, gmat) * inv_n
    yn = yc * lax.rsqrt(var + RWKV_GN_EPS) * gng_ref[...] + gnb_ref[...]
    o_ref[...] = (yn + bonus_ref[...]).astype(o_ref.dtype)


def _rwkv_layer(x, xb, mu, w_in, w0, w_lora_a, w_lora_b, a0, a_lora_a, a_lora_b, k_k, k_a, r_k,
                gn_g, gn_b, w_out, ln_g, ln_b, B, S, ts=512, pairs=8, seq_pairs=8, chunks_per_step=2):
    D = D_MODEL
    C = RWKV_CHUNK
    nc = S // C
    ts = _pick_tile(S, ts)
    tril = jnp.asarray(np.tril(np.ones((C, C), np.float32)), BF16)
    w_r, w_k, w_v, w_g =[w.astype(BF16) for w in jnp.split(w_in, 4, axis=1)]
    gmat = jnp.asarray(np.kron(np.eye(2, dtype=np.float32), np.ones((RWKV_HEAD_DIM, RWKV_HEAD_DIM), np.float32)), BF16)
    row = lambda a: a.reshape(1, D)
    x3 = x.reshape(B, S, D)
    full = lambda shape: pl.BlockSpec(shape, lambda b, j: (0,) * len(shape))
    tile = lambda: pl.BlockSpec((None, ts, D), lambda b, j: (b, j, 0))
    lr = w_lora_a.shape[1]
    outs = pl.pallas_call(
        _rwkv_proj_kernel,
        grid=(B, S // ts),
        in_specs=[tile(),
                  pl.BlockSpec((None, 8, D), lambda b, j: (b, jnp.maximum(j * (ts // 8) - 1, 0), 0)),
                  full((6, D)), full((D, D)), full((D, D)), full((D, D)), full((D, D)),
                  full((D, lr)), full((lr, D)), full((D, lr)), full((lr, D)),
                  full((1, D)), full((1, D)), full((1, D)), full((1, D)), full((1, D)),
                  full((LANES, LANES)), full((C, C))],
        out_specs=[tile() for _ in range(8)],
        out_shape=[jax.ShapeDtypeStruct((B, S, D), dt) for dt in (BF16, BF16, BF16, BF16, BF16, BF16, F32, F32)],
        compiler_params=_cparams("parallel", "arbitrary"),
        name="rwkv_projections",
    )(x3, x3, mu, w_r, w_k, w_v, w_g, w_lora_a.astype(BF16), w_lora_b.astype(BF16),
      a_lora_a.astype(BF16), a_lora_b.astype(BF16), row(w0), row(a0), row(k_k), row(k_a), row(r_k), gmat, tril)
    r, k2, v, g, kap, bvec, cum, bonus = outs

    pw = pairs * LANES
    cpb = _pick_tile(nc, chunks_per_step)
    cblk = lambda: pl.BlockSpec((None, cpb * C, pw), lambda b, c, q: (b, c, q))
    sblk = lambda: pl.BlockSpec((None, cpb, C, pw), lambda b, c, q: (b, c, 0, q))
    rp, yl, mm, gg, gam = pl.pallas_call(
        functools.partial(_rwkv_chunk_kernel, pairs=pairs, cpb=cpb),
        grid=(B, nc // cpb, D // pw),
        in_specs=[cblk() for _ in range(6)],
        out_specs=[cblk(), cblk(), sblk(), sblk(),
                   pl.BlockSpec((None, cpb, 1, pw), lambda b, c, q: (b, c, 0, q))],
        out_shape=[jax.ShapeDtypeStruct((B, S, D), BF16), jax.ShapeDtypeStruct((B, S, D), F32),
                   jax.ShapeDtypeStruct((B, nc, C, D), BF16), jax.ShapeDtypeStruct((B, nc, C, D), F32),
                   jax.ShapeDtypeStruct((B, nc, 1, D), F32)],
        compiler_params=_cparams("parallel", "parallel", "parallel"),
        name="rwkv_chunk_summaries",
    )(cum, r, k2, v, kap, bvec)

    cb = _pick_tile(nc, 8)
    pairs = seq_pairs
    pw = pairs * LANES
    o = pl.pallas_call(
        functools.partial(_rwkv_seq_kernel, pairs=pairs, cb=cb),
        grid=(B, D // pw, nc // cb),
        in_specs=[pl.BlockSpec((None, cb * C, pw), lambda b, q, j: (b, j, q)),
                  pl.BlockSpec((None, cb * C, pw), lambda b, q, j: (b, j, q)),
                  pl.BlockSpec((None, cb, C, pw), lambda b, q, j: (b, j, 0, q)),
                  pl.BlockSpec((None, cb, C, pw), lambda b, q, j: (b, j, 0, q)),
                  pl.BlockSpec((None, cb, 1, pw), lambda b, q, j: (b, j, 0, q)),
                  pl.BlockSpec((None, cb * C, pw), lambda b, q, j: (b, j, q)),
                  pl.BlockSpec((LANES, LANES), lambda b, q, j: (0, 0)),
                  pl.BlockSpec((1, pw), lambda b, q, j: (0, q)),
                  pl.BlockSpec((1, pw), lambda b, q, j: (0, q))],
        out_specs=pl.BlockSpec((None, cb * C, pw), lambda b, q, j: (b, j, q)),
        out_shape=jax.ShapeDtypeStruct((B, S, D), BF16),
        scratch_shapes=[pltpu.VMEM((pairs, 2 * C, LANES), F32), pltpu.VMEM((cb * C, pw), F32)],
        compiler_params=_cparams("parallel", "parallel", "arbitrary"),
        name="rwkv_state_scan",
    )(rp, yl, mm, gg, gam, bonus, gmat, row(gn_g), row(gn_b))
    return _outproj_ln(g.reshape(B * S, D), 0, o.reshape(B * S, D), x, w_out, ln_g, ln_b)


def kernel(x, ln_g, ln_b, fox_w_in, fox_b_f, fox_w_out, dsa_w_in, dsa_kv_norm_g, dsa_w_uk, dsa_w_uv, dsa_w_out, rwkv_mu, rwkv_w_in, rwkv_w0, rwkv_w_lora_a, rwkv_w_lora_b, rwkv_a0, rwkv_a_lora_a, rwkv_a_lora_b, rwkv_k_k, rwkv_k_a, rwkv_r_k, rwkv_gn_g, rwkv_gn_b, rwkv_w_out, ret_w_in, ret_gn_g, ret_w_out):
    B, S, D = x.shape
    h = x.reshape(B * S, D)
    h, hb = _fox_layer(h, h, fox_w_in, fox_b_f, fox_w_out, ln_g[0], ln_b[0], B, S)
    h, hb = _dsa_layer(h, hb, dsa_w_in, dsa_kv_norm_g, dsa_w_uk, dsa_w_uv, dsa_w_out, ln_g[1], ln_b[1], B, S)
    h, hb = _rwkv_layer(h, hb, rwkv_mu, rwkv_w_in, rwkv_w0, rwkv_w_lora_a, rwkv_w_lora_b, rwkv_a0,
                        rwkv_a_lora_a, rwkv_a_lora_b, rwkv_k_k, rwkv_k_a, rwkv_r_k, rwkv_gn_g, rwkv_gn_b,
                        rwkv_w_out, ln_g[2], ln_b[2], B, S)
    h, hb = _ret_layer(h, hb, ret_w_in, ret_gn_g, ret_w_out, ln_g[3], ln_b[3], B, S)
    return h.reshape(B, S, D)
```

```python
import functools
import math

import jax
import jax.numpy as jnp
import numpy as np
from jax import lax
from jax.experimental import pallas as pl
from jax.experimental.pallas import tpu as pltpu

F32 = jnp.float32
BF16 = jnp.bfloat16

D_MODEL = 1024
DEPTH = 4
LN_EPS = 1e-5
RMS_EPS = 1e-6
DN_ALPHA = (2 * DEPTH) ** 0.25
ROPE_THETA = 500000.0

FOX_HEADS = 8
FOX_HEAD_DIM = 128

RET_HEADS = 4
RET_HEAD_DIM = 256
RET_THETA = 10000.0

LANES = 128
VMEM_LIMIT = 48 * 1024 * 1024
NEG_BIG = -2.0 ** 100
LOG2E = 1.4426950408889634


def _cparams(*sem, vmem_limit=VMEM_LIMIT):
    return pltpu.CompilerParams(dimension_semantics=sem, vmem_limit_bytes=vmem_limit)


def _dot(a, b):
    return jnp.dot(a, b, preferred_element_type=F32)


def _dot_nt(a, b):
    return lax.dot_general(a, b, (((1,), (1,)), ((), ())), preferred_element_type=F32)


def _dot_tn(a, b):
    return lax.dot_general(a, b, (((0,), (0,)), ((), ())), preferred_element_type=F32)


def _split2(x):
    hi = x.astype(BF16)
    lo = (x - hi.astype(F32)).astype(BF16)
    return hi, lo


def _split3(x):
    p1 = x.astype(BF16)
    r1 = x - p1.astype(F32)
    p2 = r1.astype(BF16)
    p3 = (r1 - p2.astype(F32)).astype(BF16)
    return p1, p2, p3


def _sigmoid(x):
    return 1.0 / (1.0 + jnp.exp(-x))


def _round_robin(gens):
    results = [None] * len(gens)
    live = list(range(len(gens)))
    while live:
        still = []
        for i in live:
            try:
                out = next(gens[i])
            except StopIteration:
                continue
            if out is not None:
                results[i] = out
            still.append(i)
        live = still
    return results


def _pick_tile(n, pref):
    t = min(n, pref)
    while n % t:
        t //= 2
    return t


def _mm_kernel(a_ref, w_ref, o_ref):
    o_ref[...] = _dot(a_ref[...].astype(BF16), w_ref[...]).astype(o_ref.dtype)


def _matmul(a, w, out_dtype=BF16, tm=2048, tn=1024):
    M, K = a.shape
    N = w.shape[1]
    tm = _pick_tile(M, tm)
    if N % tn:
        tn = N
    return pl.pallas_call(
        _mm_kernel,
        grid=(M // tm, N // tn),
        in_specs=[pl.BlockSpec((tm, K), lambda i, j: (i, 0)),
                  pl.BlockSpec((K, tn), lambda i, j: (0, j))],
        out_specs=pl.BlockSpec((tm, tn), lambda i, j: (i, j)),
        out_shape=jax.ShapeDtypeStruct((M, N), out_dtype),
        compiler_params=_cparams("parallel", "arbitrary"),
        name="proj_matmul",
    )(a, w)


def _outproj_ln_kernel(g_ref, o_ref, x_ref, w_ref, lg_ref, lb_ref, xo_ref, xb_ref):
    half_g = g_ref[...] * 0.5
    h = (half_g * o_ref[...]) * (1.0 + jnp.tanh(half_g))
    z = x_ref[...] + _dot(h.astype(BF16), w_ref[...])
    zc = z - jnp.mean(z, axis=-1, keepdims=True)
    var = jnp.mean(zc * zc, axis=-1, keepdims=True)
    out = zc * lax.rsqrt(var + LN_EPS / DN_ALPHA ** 2) * lg_ref[...] + lb_ref[...]
    xo_ref[...] = out
    xb_ref[...] = out.astype(BF16)


def _outproj_ln(gate_arr, gate_col, o, x, w_out, ln_g, ln_b, tm=1024):
    M, D = x.shape
    tm = _pick_tile(M, tm)
    return pl.pallas_call(
        _outproj_ln_kernel,
        grid=(M // tm,),
        in_specs=[pl.BlockSpec((tm, D), lambda i: (i, gate_col)),
                  pl.BlockSpec((tm, D), lambda i: (i, 0)),
                  pl.BlockSpec((tm, D), lambda i: (i, 0)),
                  pl.BlockSpec((D, D), lambda i: (0, 0)),
                  pl.BlockSpec((1, D), lambda i: (0, 0)),
                  pl.BlockSpec((1, D), lambda i: (0, 0))],
        out_specs=[pl.BlockSpec((tm, D), lambda i: (i, 0)),
                   pl.BlockSpec((tm, D), lambda i: (i, 0))],
        out_shape=[jax.ShapeDtypeStruct((M, D), F32), jax.ShapeDtypeStruct((M, D), BF16)],
        compiler_params=_cparams("parallel"),
        name="outproj_layernorm",
    )(gate_arr, o, x, (w_out * (1.0 / DN_ALPHA)).astype(BF16), ln_g.reshape(1, D), ln_b.reshape(1, D))


FOX_BIAS_PIECES = 3


def _fox_cum_kernel(x_ref, wh_ref, wl_ref, bf_ref, tril_ref, place_ref, pc_ref, carry_sc):
    @pl.when(pl.program_id(1) == 0)
    def _():
        carry_sc[...] = jnp.zeros_like(carry_sc)

    x_hi, x_lo = _split2(x_ref[...])
    z = _dot(x_hi, wh_ref[...]) + _dot(x_lo, wh_ref[...]) + _dot(x_hi, wl_ref[...]) + bf_ref[...]
    logf = jnp.minimum(z, 0.0) - jnp.log(1.0 + jnp.exp(-jnp.abs(z)))
    p1, p2, p3 = _split3(logf)
    tril = tril_ref[...]
    c = _dot(tril, p1) + _dot(tril, p2) + _dot(tril, p3) + carry_sc[...]
    carry_sc[...] = c[c.shape[0] - 1:, :]
    pieces = _split3(c * (-LOG2E))
    pc_ref[...] = sum(_dot(pieces[p], place_ref[p]) for p in range(FOX_BIAS_PIECES)).astype(pc_ref.dtype)


def _fox_cum(x3, w_f, b_f, ts=512):
    B, S, D = x3.shape
    H = w_f.shape[1]
    ts = _pick_tile(S, ts)
    w_pad = jnp.zeros((D, LANES), F32).at[:, :H].set(w_f)
    w_hi, w_lo = _split2(w_pad)
    b_pad = jnp.zeros((1, LANES), F32).at[0, :H].set(b_f)
    tril = jnp.asarray(np.tril(np.ones((ts, ts), np.float32)), BF16)
    place = np.zeros((FOX_BIAS_PIECES, LANES, LANES), np.float32)
    for p in range(FOX_BIAS_PIECES):
        for h in range(H):
            place[p, h, FOX_BIAS_PIECES * h + p] = 1.0
    return pl.pallas_call(
        _fox_cum_kernel,
        grid=(B, S // ts),
        in_specs=[pl.BlockSpec((None, ts, D), lambda b, j: (b, j, 0)),
                  pl.BlockSpec((D, LANES), lambda b, j: (0, 0)),
                  pl.BlockSpec((D, LANES), lambda b, j: (0, 0)),
                  pl.BlockSpec((1, LANES), lambda b, j: (0, 0)),
                  pl.BlockSpec((ts, ts), lambda b, j: (0, 0)),
                  pl.BlockSpec((FOX_BIAS_PIECES, LANES, LANES), lambda b, j: (0, 0, 0))],
        out_specs=pl.BlockSpec((None, ts, LANES), lambda b, j: (b, j, 0)),
        out_shape=jax.ShapeDtypeStruct((B, S, LANES), BF16),
        scratch_shapes=[pltpu.VMEM((1, LANES), F32)],
        compiler_params=_cparams("parallel", "arbitrary"),
        name="fox_decay_cumsum",
    )(x3, w_hi, w_lo, b_pad, tril, jnp.asarray(place, BF16))


ONES_ROWS = 16


def _fox_attn_kernel(q_ref, k_ref, v_ref, pc_ref, o_ref, kaug_sc, vt_sc, m_sc, acc_sc, *, tq, nsub, seq, unroll):
    dh = FOX_HEAD_DIM
    h = pl.program_id(1)
    g = pl.program_id(2)

    @pl.when(g == 0)
    def _():
        kaug_sc[:, :dh] = k_ref[...]
        kaug_sc[:, dh:] = pc_ref[...]
        for c in range(seq // tq):
            rows = slice(c * tq, (c + 1) * tq)
            vt_sc[:dh, rows] = v_ref[rows, :].astype(F32).T.astype(BF16)
        vt_sc[dh:, :] = jnp.ones((ONES_ROWS, seq), BF16)

    feat = lax.broadcasted_iota(jnp.int32, (LANES, tq), 0)
    bias_rows = (feat >= FOX_BIAS_PIECES * h) & (feat < FOX_BIAS_PIECES * (h + 1))
    ones_h = jnp.where(bias_rows, 1.0, 0.0).astype(BF16)
    q_aug = [jnp.concatenate([q_ref[a * tq:(a + 1) * tq, :].astype(F32).T.astype(BF16), ones_h], axis=0)
             for a in range(nsub)]
    m_sc[...] = jnp.full_like(m_sc, NEG_BIG)
    acc_sc[...] = jnp.zeros_like(acc_sc)
    causal = (lax.broadcasted_iota(jnp.int32, (tq, tq), 0) <= lax.broadcasted_iota(jnp.int32, (tq, tq), 1))
    first = g * nsub

    def chain(a, tiles, diag_last):
        offs = [pl.multiple_of(j * tq, tq) for j in tiles]
        scores = []
        for off in offs:
            scores.append(_dot(kaug_sc[pl.ds(off, tq), :], q_aug[a]))
            yield None
        for n, (off, s) in enumerate(zip(offs, scores)):
            if diag_last and n == len(offs) - 1:
                s = jnp.where(causal, s, NEG_BIG)
            m_old = m_sc[a]
            m_new = jnp.maximum(m_old, jnp.max(s, axis=0, keepdims=True))
            alpha = jnp.exp2(m_old - m_new)
            pv = _dot(vt_sc[:, pl.ds(off, tq)], jnp.exp2(s - m_new).astype(BF16))
            yield None
            acc_sc[a] = alpha * acc_sc[a] + pv
            m_sc[a] = m_new
        yield None

    def body(jj, c):
        _round_robin([chain(a, [jj * unroll + u for u in range(unroll)], False) for a in range(nsub)])
        return c

    lax.fori_loop(0, first // unroll, body, 0)
    _round_robin([chain(a, [first + t for t in range(a + 1)], True) for a in range(nsub)])
    for a in range(nsub):
        acc = acc_sc[a]
        o_t = acc[:dh] / acc[dh:dh + 1]
        o_ref[a * tq:(a + 1) * tq, :] = o_t.T.astype(o_ref.dtype)


def _fox_attention(proj3, pieces, tq=256, nsub=8):
    B, S, _ = proj3.shape
    H, dh = FOX_HEADS, FOX_HEAD_DIM
    tq = _pick_tile(S, tq)
    nsub = _pick_tile(S // tq, nsub)
    tg = tq * nsub
    return pl.pallas_call(
        functools.partial(_fox_attn_kernel, tq=tq, nsub=nsub, seq=S, unroll=min(nsub, 4)),
        grid=(B, H, S // tg),
        in_specs=[pl.BlockSpec((None, tg, dh), lambda b, h, g: (b, g, h)),
                  pl.BlockSpec((None, S, dh), lambda b, h, g: (b, 0, H + h)),
                  pl.BlockSpec((None, S, dh), lambda b, h, g: (b, 0, 2 * H + h)),
                  pl.BlockSpec((None, S, LANES), lambda b, h, g: (b, 0, 0))],
        out_specs=pl.BlockSpec((None, tg, dh), lambda b, h, g: (b, g, h)),
        out_shape=jax.ShapeDtypeStruct((B, S, H * dh), BF16),
        scratch_shapes=[pltpu.VMEM((S, dh + LANES), BF16),
                        pltpu.VMEM((dh + ONES_ROWS, S), BF16),
                        pltpu.VMEM((nsub, 1, tq), F32),
                        pltpu.VMEM((nsub, dh + ONES_ROWS, tq), F32)],
        compiler_params=_cparams("parallel", "parallel", "arbitrary"),
        name="fox_attention",
    )(proj3, proj3, proj3, pieces)


def _fox_layer(x, xb, w_in, b_f, w_out, ln_g, ln_b, B, S):
    D = D_MODEL
    H, dh = FOX_HEADS, FOX_HEAD_DIM
    scale = dh ** -0.5 * LOG2E
    w_q, w_k, w_v, w_f, w_g = jnp.split(w_in, [H * dh, 2 * H * dh, 3 * H * dh, 3 * H * dh + H], axis=1)
    w_main = jnp.concatenate([w_q * scale, w_k, w_v, w_g], axis=1).astype(BF16)
    proj = _matmul(xb, w_main)
    pieces = _fox_cum(x.reshape(B, S, D), w_f, b_f)
    o = _fox_attention(proj.reshape(B, S, 4 * D), pieces)
    return _outproj_ln(proj, 3, o.reshape(B * S, D), x, w_out, ln_g, ln_b)


def _ret_kernel(q_ref, k_ref, v_ref, cos_ref, sin_ref, dm_ref, xi_ref, zeta_ref, gc_ref, gn_ref,
                o_ref, r_sc):
    @pl.when(pl.program_id(1) == 0)
    def _():
        r_sc[...] = jnp.zeros_like(r_sc)

    dk = RET_HEAD_DIM
    half = dk // 2
    cos = cos_ref[...]
    sin = sin_ref[...]

    def rope(x):
        x1, x2 = x[:, :half], x[:, half:]
        return jnp.concatenate([x1 * cos - x2 * sin, x2 * cos + x1 * sin], axis=-1)

    def head_chain(h):
        cols = slice(h * dk, (h + 1) * dk)
        q = rope(q_ref[:, cols].astype(F32))
        k = rope(k_ref[:, cols].astype(F32)) * (dk ** -0.5)
        v = v_ref[:, cols]
        qb = q.astype(BF16)
        r_old = r_sc[h]
        scores = _dot_nt(qb, k.astype(BF16))
        cross = _dot(qb, r_old.astype(BF16))
        kz = (k * zeta_ref[h]).astype(BF16)
        r_new = _dot_tn(kz, v)
        yield None
        o = _dot((scores * dm_ref[h]).astype(BF16), v)
        r_sc[h] = r_old * gc_ref[h] + r_new
        yield None
        o = o + cross * xi_ref[h]
        o = o * lax.rsqrt(jnp.mean(o * o, axis=-1, keepdims=True) + RMS_EPS) * gn_ref[:, cols]
        o_ref[:, cols] = o.astype(o_ref.dtype)
        yield None

    _round_robin([head_chain(h) for h in range(RET_HEADS)])


def _ret_layer(x, xb, w_in, gn_g, w_out, ln_g, ln_b, B, S, chunk=512):
    D = D_MODEL
    H, dk = RET_HEADS, RET_HEAD_DIM
    C = _pick_tile(S, chunk)
    proj = _matmul(xb, w_in.astype(BF16))
    f32 = np.float32
    inv = (f32(1.0) / (f32(RET_THETA) ** (np.arange(0, dk, 2, dtype=f32) / f32(dk)))).astype(f32)
    ang = (np.arange(S, dtype=f32)[:, None] * inv[None, :]).astype(f32)
    cos, sin = jnp.asarray(np.cos(ang)), jnp.asarray(np.sin(ang))
    log_g = np.log1p(-(f32(2.0) ** (f32(-5.0) - np.arange(H, dtype=f32)))).astype(f32)
    pos = np.arange(C, dtype=f32)
    diff = pos[:, None] - pos[None, :]
    d_mask = jnp.asarray(np.where(diff[None] >= 0, np.exp(np.maximum(diff, 0.0)[None] * log_g[:, None, None]),
                                  0.0).astype(f32))
    xi = jnp.asarray(np.broadcast_to(np.exp((pos[None, :] + 1.0) * log_g[:, None])[:, :, None],
                                     (H, C, dk)).astype(f32))
    zeta = jnp.asarray(np.broadcast_to(np.exp((C - 1.0 - pos[None, :]) * log_g[:, None])[:, :, None],
                                       (H, C, dk)).astype(f32))
    g_c = jnp.asarray(np.broadcast_to(np.exp(f32(C) * log_g)[:, None, None], (H, 1, dk)).astype(f32))
    p3 = proj.reshape(B, S, 4 * D)
    o = pl.pallas_call(
        _ret_kernel,
        grid=(B, S // C),
        in_specs=[pl.BlockSpec((None, C, D), lambda b, c: (b, c, 0)),
                  pl.BlockSpec((None, C, D), lambda b, c: (b, c, 1)),
                  pl.BlockSpec((None, C, D), lambda b, c: (b, c, 2)),
                  pl.BlockSpec((C, dk // 2), lambda b, c: (c, 0)),
                  pl.BlockSpec((C, dk // 2), lambda b, c: (c, 0)),
                  pl.BlockSpec((H, C, C), lambda b, c: (0, 0, 0)),
                  pl.BlockSpec((H, C, dk), lambda b, c: (0, 0, 0)),
                  pl.BlockSpec((H, C, dk), lambda b, c: (0, 0, 0)),
                  pl.BlockSpec((H, 1, dk), lambda b, c: (0, 0, 0)),
                  pl.BlockSpec((1, D), lambda b, c: (0, 0))],
        out_specs=pl.BlockSpec((None, C, D), lambda b, c: (b, c, 0)),
        out_shape=jax.ShapeDtypeStruct((B, S, D), BF16),
        scratch_shapes=[pltpu.VMEM((H, dk, dk), F32)],
        compiler_params=_cparams("parallel", "arbitrary"),
        name="retnet_retention",
    )(p3, p3, p3, cos, sin, d_mask, xi, zeta, g_c, gn_g.reshape(1, D))
    return _outproj_ln(proj, 3, o.reshape(B * S, D), x, w_out, ln_g, ln_b)


DSA_HEADS = 8
DSA_HEAD_DIM = 128
DSA_ROPE_DIM = 32
DSA_KV_RANK = 128
IDX_HEADS = 8
IDX_DIM = 64
IDX_ROPE_DIM = 16
TOPK_MAX = 256
INT_MIN = -2 ** 31
HALF_MIN = -2 ** 15
HALF_ROWS = 16
DSA_VMEM_LIMIT = 56 * 1024 * 1024
DSA_ONES_ROWS = 16


def _rope_perm(width, groups):
    p = np.zeros((width, width), np.float32)
    for start, half in groups:
        for j in range(half):
            p[start + half + j, start + j] = 1.0
            p[start + j, start + half + j] = 1.0
    return p


def _rope_cs(S, width, groups, theta_dims):
    f32 = np.float32
    c = np.ones((S, width), f32)
    sg = np.zeros((S, width), f32)
    pos = np.arange(S, dtype=f32)[:, None]
    for (start, half), rot_dim in zip(groups, theta_dims):
        inv = (f32(1.0) / (f32(ROPE_THETA) ** (np.arange(0, rot_dim, 2, dtype=f32) / f32(rot_dim)))).astype(f32)
        ang = (pos * inv[None, :]).astype(f32)
        cos, sin = np.cos(ang), np.sin(ang)
        c[:, start:start + half] = cos
        c[:, start + half:start + 2 * half] = cos
        sg[:, start:start + half] = -sin
        sg[:, start + half:start + 2 * half] = sin
    return jnp.asarray(c), jnp.asarray(sg)


def _dsa_prep_kernel(q_ref, qi_ref, ckv_ref, misc_ref, cq_ref, sq_ref, ci_ref, si_ref, cm_ref, sm_ref,
                     pq_ref, pi_ref, pm_ref, selk_ref, selw_ref, wuk_ref, kvg_ref,
                     qf_ref, kvl_ref, kvt_ref, qir_ref, kid_ref, wit_ref):
    H, dh = DSA_HEADS, DSA_HEAD_DIM
    lane = lax.broadcasted_iota(jnp.int32, (1, LANES), 1)
    rope_lanes = lane < DSA_ROPE_DIM
    cq, sq = cq_ref[...], sq_ref[...]
    scale = dh ** -0.5 * LOG2E
    for h in range(H):
        qh = q_ref[:, h * dh:(h + 1) * dh]
        qr = qh.astype(F32) * cq + _dot(qh, pq_ref[...]) * sq
        q_lat = _dot(qr.astype(BF16), wuk_ref[h])
        qf_ref[h, :dh, :] = (q_lat * scale).T.astype(BF16)
        qf_ref[h, dh:, :] = jnp.where(rope_lanes, qr * scale, 0.0).T.astype(BF16)
    ci, si = ci_ref[...], si_ref[...]
    for g in range(IDX_HEADS * IDX_DIM // LANES):
        qg = qi_ref[:, g * LANES:(g + 1) * LANES]
        qr = qg.astype(F32) * ci + _dot(qg, pi_ref[...]) * si
        qir_ref[g * LANES:(g + 1) * LANES, :] = (qr * (IDX_DIM ** -0.5)).T.astype(BF16)
    ckv = ckv_ref[...].astype(F32)
    ckv = ckv * lax.rsqrt(jnp.mean(ckv * ckv, axis=-1, keepdims=True) + RMS_EPS) * kvg_ref[...]
    misc = misc_ref[...]
    mr = (misc.astype(F32) * cm_ref[...] + _dot(misc, pm_ref[...]) * sm_ref[...])
    kvl_ref[:, :DSA_KV_RANK] = ckv.astype(BF16)
    kvl_ref[:, DSA_KV_RANK:] = jnp.where(rope_lanes, mr, 0.0).astype(BF16)
    kvt_ref[:DSA_KV_RANK, :] = ckv.T.astype(BF16)
    kvt_ref[DSA_KV_RANK:, :] = jnp.ones((DSA_ONES_ROWS, ckv.shape[0]), BF16)
    kid_ref[...] = _dot(mr.astype(BF16), selk_ref[...]).astype(BF16)
    wi = _dot(misc, selw_ref[...]) * (IDX_HEADS ** -0.5)
    wit_ref[...] = wi.T[:IDX_HEADS, :]


def _sort_key(x):
    b = pltpu.bitcast(x, jnp.int32)
    return jnp.where(b < 0, b ^ 0x7FFFFFFF, b)


def _dsa_main_kernel(qf_ref, kvl_ref, kvt_ref, qi_ref, kid_ref, wit_ref, wuv_ref, obuf_ref, o_ref,
                     key_sc, hi_sc, lo_sc, low_sc, m_sc, acc_sc, *, tq, tk, k_sel, seq, qtile):
    del obuf_ref
    H = DSA_HEADS
    q0 = qtile * tq
    nj = (q0 + tq + tk - 1) // tk
    qpos = q0 + lax.broadcasted_iota(jnp.int32, (tk, tq), 1)
    kpos0 = lax.broadcasted_iota(jnp.int32, (tk, tq), 0)
    feat = lax.broadcasted_iota(jnp.int32, (LANES, 1), 0)
    neg_inf_key = _sort_key(jnp.full((1, 1), -jnp.inf, F32))

    wit = wit_ref[...]
    qi_heads = []
    for h in range(IDX_HEADS):
        g = qi_ref[(h // 2) * LANES:(h // 2 + 1) * LANES, :]
        keep = (feat >= IDX_DIM) if (h % 2) else (feat < IDX_DIM)
        qi_heads.append(jnp.where(keep, g, jnp.zeros_like(g)))

    def score_tile(j, c):
        off = pl.multiple_of(j * tk, tk)
        ki = kid_ref[pl.ds(off, tk), :]
        scores = [_dot(ki, qi_heads[h]) for h in range(IDX_HEADS)]
        isc = jnp.zeros((tk, tq), F32)
        for h in range(IDX_HEADS):
            isc = isc + jnp.maximum(scores[h], 0.0) * wit[h:h + 1, :]
        isc = jnp.where(kpos0 + off <= qpos, isc + 0.0, -jnp.inf)
        key = _sort_key(isc)
        key_sc[pl.ds(off, tk), :] = key
        hi_sc[pl.ds(off, tk), :] = (key >> 16).astype(jnp.int16)
        lo_sc[pl.ds(off, tk), :] = ((key & 0xFFFF) + HALF_MIN).astype(jnp.int16)
        return c

    lax.fori_loop(0, nj, score_tile, 0)

    def count(pred_fn):
        acc = jnp.zeros((8, tq), jnp.int32)
        for j in range(nj):
            hit = jnp.where(pred_fn(key_sc[j * tk:(j + 1) * tk, :], j * tk), 1, 0)
            acc = acc + jnp.sum(hit.reshape(tk // 8, 8, tq), axis=0)
        return jnp.sum(acc, axis=0, keepdims=True)

    rows16 = tk // HALF_ROWS

    def count16(ref, pred_fn):
        accs = [jnp.zeros((HALF_ROWS, tq), jnp.int16) for _ in range(2)]
        for j in range(nj):
            hit = jnp.where(pred_fn(ref[j * tk:(j + 1) * tk, :].reshape(rows16, HALF_ROWS, tq)),
                            jnp.int16(1), jnp.int16(0))
            for r in range(rows16):
                accs[r % 2] = accs[r % 2] + hit[r]
        return jnp.sum((accs[0] + accs[1]).astype(jnp.int32), axis=0, keepdims=True)

    def as_half(v):
        return jnp.broadcast_to(v.astype(jnp.int16), (HALF_ROWS, tq))[None]

    def search16(ref, base0, cnt0, bits, want):
        def bit_step(t, carry):
            base, cnt_b = carry
            cand = base + lax.shift_left(jnp.int32(1), bits - 1 - t)
            cand16 = as_half(cand)
            c = count16(ref, lambda kt: kt >= cand16)
            ok = c >= want
            return jnp.where(ok, cand, base), jnp.where(ok, c, cnt_b)
        return lax.fori_loop(0, bits, bit_step, (base0, cnt0))

    zero16 = as_half(jnp.zeros((1, tq), jnp.int32))
    cnt_pos = count16(hi_sc, lambda kt: kt >= zero16)
    nonneg = cnt_pos >= k_sel
    t1, ge_hi = search16(hi_sc, jnp.where(nonneg, 0, HALF_MIN), jnp.where(nonneg, cnt_pos, nj * tk), 15, k_sel)
    t1_16 = as_half(t1)
    above = count16(hi_sc, lambda kt: kt > t1_16)

    for j in range(nj):
        rows = slice(j * tk, (j + 1) * tk)
        hi = hi_sc[rows, :].reshape(rows16, HALF_ROWS, tq)
        lo = lo_sc[rows, :].reshape(rows16, HALF_ROWS, tq)
        low_sc[rows, :] = jnp.where(hi == t1_16, lo, jnp.int16(HALF_MIN)).reshape(tk, tq)
    t2, ge_low = search16(low_sc, jnp.full((1, tq), HALF_MIN, jnp.int32), ge_hi - above, 16, k_sel - above)
    t2_16 = as_half(t2)
    thr = lax.shift_left(t1, 16) | (t2 - HALF_MIN)
    n_ge = above + ge_low
    n_gt = above + count16(low_sc, lambda kt: kt > t2_16)
    need = k_sel - n_gt
    excess = ((n_ge - n_gt) > need) & (thr > neg_inf_key)
    any_excess = jnp.max(jnp.where(excess, 1, 0)) > 0

    def tie_cut():
        def step(t, lo):
            cand = lo + lax.shift_left(jnp.int32(1), int(math.log2(seq)) - t)
            c = count(lambda kt, off: (kt == thr) & (kpos0 + off < cand))
            return jnp.where(c < need, cand, lo)
        lo = lax.fori_loop(0, int(math.log2(seq)) + 1, step, jnp.zeros((1, tq), jnp.int32))
        return jnp.where(excess, lo, seq)

    cut = lax.cond(any_excess, tie_cut, lambda: jnp.full((1, tq), seq, jnp.int32))

    @pl.when(any_excess)
    def _():
        for j in range(nj):
            rows = slice(j * tk, (j + 1) * tk)
            kt = key_sc[rows, :]
            key_sc[rows, :] = jnp.where((kt == thr) & (kpos0 + j * tk > cut), thr - 1, kt)

    thr_sel = jnp.where(thr > neg_inf_key, thr, neg_inf_key + 1)

    m_sc[...] = jnp.full_like(m_sc, NEG_BIG)
    acc_sc[...] = jnp.zeros_like(acc_sc)

    def attn_tile(j, c):
        off = pl.multiple_of(j * tk, tk)
        bias = jnp.where(key_sc[pl.ds(off, tk), :] >= thr_sel, 0.0, NEG_BIG)
        kv = kvl_ref[pl.ds(off, tk), :]
        kvt = kvt_ref[:, pl.ds(off, tk)]

        def head_step(h):
            s = _dot(kv, qf_ref[h]) + bias
            yield None
            m_old = m_sc[h]
            m_new = jnp.maximum(m_old, jnp.max(s, axis=0, keepdims=True))
            alpha = jnp.exp2(m_old - m_new)
            pv = _dot(kvt, jnp.exp2(s - m_new).astype(BF16))
            yield None
            acc_sc[h] = alpha * acc_sc[h] + pv
            m_sc[h] = m_new
            yield None

        _round_robin([head_step(h) for h in range(H)])
        return c

    lax.fori_loop(0, nj, attn_tile, 0)
    for h in range(H):
        acc = acc_sc[h]
        o_lat_t = (acc[:DSA_KV_RANK] / acc[DSA_KV_RANK:DSA_KV_RANK + 1]).astype(BF16)
        o_ref[:, h * DSA_HEAD_DIM:(h + 1) * DSA_HEAD_DIM] = _dot_tn(o_lat_t, wuv_ref[h]).astype(o_ref.dtype)


def _dsa_layer(x, xb, w_in, kv_norm_g, w_uk, w_uv, w_out, ln_g, ln_b, B, S, tq=512, tk=512):
    D = D_MODEL
    H, dh, dr, dc = DSA_HEADS, DSA_HEAD_DIM, DSA_ROPE_DIM, DSA_KV_RANK
    HI, di = IDX_HEADS, IDX_DIM
    w_q, w_ckv, w_kr, w_qi, w_ki, w_wi, w_g = jnp.split(
        w_in, np.cumsum([H * dh, dc, dr, HI * di, di, HI]).tolist(), axis=1)
    w_misc = jnp.concatenate([w_kr, w_ki, w_wi, jnp.zeros((D, LANES - dr - di - HI), F32)], axis=1)
    w_main = jnp.concatenate([w_q, w_g, w_qi, w_ckv, w_misc], axis=1).astype(BF16)
    n_main = w_main.shape[1]
    proj = _matmul(xb, w_main, tm=1024, tn=n_main)
    c_q, c_qi, c_ckv, c_misc = 0, 2 * D // LANES, (2 * D + HI * di) // LANES, (2 * D + HI * di + dc) // LANES

    q_groups = [(0, dr // 2)]
    i_groups = [(0, IDX_ROPE_DIM // 2), (di, IDX_ROPE_DIM // 2)]
    m_groups = [(0, dr // 2), (dr, IDX_ROPE_DIM // 2)]
    cq, sq = _rope_cs(S, LANES, q_groups, [dr])
    ci, si = _rope_cs(S, LANES, i_groups, [IDX_ROPE_DIM, IDX_ROPE_DIM])
    cm, sm = _rope_cs(S, LANES, m_groups, [dr, IDX_ROPE_DIM])
    pq = jnp.asarray(_rope_perm(LANES, q_groups), BF16)
    pi = jnp.asarray(_rope_perm(LANES, i_groups), BF16)
    pm = jnp.asarray(_rope_perm(LANES, m_groups), BF16)
    selk = np.zeros((LANES, LANES), np.float32)
    for j in range(di):
        selk[dr + j, j] = 1.0
        selk[dr + j, di + j] = 1.0
    selw = np.zeros((LANES, LANES), np.float32)
    for j in range(HI):
        selw[dr + di + j, j] = 1.0
    wuk = jnp.concatenate([jnp.zeros((H, dr, dc), F32), jnp.transpose(w_uk, (0, 2, 1))], axis=1).astype(BF16)

    ts = _pick_tile(S, 512)
    p3 = proj.reshape(B, S, n_main)
    tab = lambda: pl.BlockSpec((ts, LANES), lambda b, j: (j, 0))
    mat = lambda: pl.BlockSpec((LANES, LANES), lambda b, j: (0, 0))
    qf, kvl, kvt, qir, kid, wit = pl.pallas_call(
        _dsa_prep_kernel,
        grid=(B, S // ts),
        in_specs=[pl.BlockSpec((None, ts, H * dh), lambda b, j: (b, j, 0)),
                  pl.BlockSpec((None, ts, HI * di), lambda b, j: (b, j, c_qi * LANES // (HI * di))),
                  pl.BlockSpec((None, ts, dc), lambda b, j: (b, j, c_ckv)),
                  pl.BlockSpec((None, ts, LANES), lambda b, j: (b, j, c_misc)),
                  tab(), tab(), tab(), tab(), tab(), tab(),
                  mat(), mat(), mat(), mat(), mat(),
                  pl.BlockSpec((H, LANES, dc), lambda b, j: (0, 0, 0)),
                  pl.BlockSpec((1, dc), lambda b, j: (0, 0))],
        out_specs=[pl.BlockSpec((None, H, 2 * dc, ts), lambda b, j: (b, 0, 0, j)),
                   pl.BlockSpec((None, ts, 2 * dc), lambda b, j: (b, j, 0)),
                   pl.BlockSpec((None, dc + DSA_ONES_ROWS, ts), lambda b, j: (b, 0, j)),
                   pl.BlockSpec((None, HI * di, ts), lambda b, j: (b, 0, j)),
                   pl.BlockSpec((None, ts, LANES), lambda b, j: (b, j, 0)),
                   pl.BlockSpec((None, HI, ts), lambda b, j: (b, 0, j))],
        out_shape=[jax.ShapeDtypeStruct((B, H, 2 * dc, S), BF16),
                   jax.ShapeDtypeStruct((B, S, 2 * dc), BF16),
                   jax.ShapeDtypeStruct((B, dc + DSA_ONES_ROWS, S), BF16),
                   jax.ShapeDtypeStruct((B, HI * di, S), BF16),
                   jax.ShapeDtypeStruct((B, S, LANES), BF16),
                   jax.ShapeDtypeStruct((B, HI, S), F32)],
        compiler_params=_cparams("parallel", "parallel"),
        name="dsa_prep",
    )(p3, p3, p3, p3, cq, sq, ci, si, cm, sm, pq, pi, pm,
      jnp.asarray(selk, BF16), jnp.asarray(selw, BF16), wuk, kv_norm_g.reshape(1, dc))

    tq = _pick_tile(S, tq)
    tk = _pick_tile(S, tk)
    k_sel = min(TOPK_MAX, S // 4)
    wuv = w_uv.astype(BF16)
    o = jnp.zeros((B, S, H * dh), BF16)
    for i in range(S // tq):
        nkeys = -(-((i + 1) * tq) // tk) * tk
        o = pl.pallas_call(
            functools.partial(_dsa_main_kernel, tq=tq, tk=tk, k_sel=k_sel, seq=S, qtile=i),
            grid=(B,),
            in_specs=[pl.BlockSpec((None, H, 2 * dc, tq), lambda b, i=i: (b, 0, 0, i)),
                      pl.BlockSpec((None, nkeys, 2 * dc), lambda b: (b, 0, 0)),
                      pl.BlockSpec((None, dc + DSA_ONES_ROWS, nkeys), lambda b: (b, 0, 0)),
                      pl.BlockSpec((None, HI * di, tq), lambda b, i=i: (b, 0, i)),
                      pl.BlockSpec((None, nkeys, LANES), lambda b: (b, 0, 0)),
                      pl.BlockSpec((None, HI, tq), lambda b, i=i: (b, 0, i)),
                      pl.BlockSpec((H, dc, dh), lambda b: (0, 0, 0)),
                      pl.BlockSpec(memory_space=pl.ANY)],
            out_specs=pl.BlockSpec((None, tq, H * dh), lambda b, i=i: (b, i, 0)),
            out_shape=jax.ShapeDtypeStruct((B, S, H * dh), BF16),
            input_output_aliases={7: 0},
            scratch_shapes=[pltpu.VMEM((nkeys, tq), jnp.int32),
                            pltpu.VMEM((nkeys, tq), jnp.int16), pltpu.VMEM((nkeys, tq), jnp.int16),
                            pltpu.VMEM((nkeys, tq), jnp.int16),
                            pltpu.VMEM((H, 1, tq), F32),
                            pltpu.VMEM((H, dc + DSA_ONES_ROWS, tq), F32)],
            compiler_params=_cparams("parallel", vmem_limit=DSA_VMEM_LIMIT),
            name=f"dsa_select_attention_q{i}",
        )(qf, kvl, kvt, qir, kid, wit, wuv, o)
    return _outproj_ln(proj, 1, o.reshape(B * S, D), x, w_out, ln_g, ln_b)


RWKV_HEADS = 16
RWKV_HEAD_DIM = 64
RWKV_GN_EPS = 64e-5
RWKV_CHUNK = 64
RWKV_SUB = 16


def _group_sum(x, gmat, split=True):
    outs = []
    for c in range(x.shape[1] // LANES):
        xc = x[:, c * LANES:(c + 1) * LANES]
        if split:
            hi, lo = _split2(xc)
            outs.append(_dot(hi, gmat) + _dot(lo, gmat))
        else:
            outs.append(_dot(xc.astype(BF16), gmat))
    return outs[0] if len(outs) == 1 else jnp.concatenate(outs, axis=1)


def _softplus(y):
    return jnp.maximum(y, 0.0) + jnp.log(1.0 + jnp.exp(-jnp.abs(y)))


def _rwkv_proj_kernel(x_ref, xprev_ref, mu_ref, wr_ref, wk_ref, wv_ref, wg_ref, wla_ref, wlb_ref,
                      ala_ref, alb_ref, w0_ref, a0_ref, kk_ref, ka_ref, rk_ref, gmat_ref, tril_ref,
                      r_ref, k_ref, v_ref, g_ref, kap_ref, b_ref, cum_ref, bonus_ref):
    x = x_ref[...]
    ts = x.shape[0]
    prev = jnp.where(pl.program_id(1) == 0, 0.0, xprev_ref[7:8, :])
    rowid = lax.broadcasted_iota(jnp.int32, (ts, 1), 0)
    xx = jnp.where(rowid == 0, prev, pltpu.roll(x, 1, 0)) - x

    def mixed(i):
        return (x + xx * mu_ref[i:i + 1, :]).astype(BF16)

    r = _dot(mixed(0), wr_ref[...])
    k = _dot(mixed(2), wk_ref[...])
    v = _dot(mixed(3), wv_ref[...])
    g_ref[...] = _dot(mixed(5), wg_ref[...]).astype(g_ref.dtype)
    lora_w = _dot(jnp.tanh(_dot(mixed(1), wla_ref[...])).astype(BF16), wlb_ref[...])
    lora_a = _dot(_dot(mixed(4), ala_ref[...]).astype(BF16), alb_ref[...])
    w_log = -_softplus(-(w0_ref[...] + lora_w)) - 0.5
    lw = -jnp.exp(w_log)
    tril = tril_ref[...]
    for c in range(ts // RWKV_CHUNK):
        rows = slice(c * RWKV_CHUNK, (c + 1) * RWKV_CHUNK)
        cum_ref[rows, :] = sum(_dot(tril, piece) for piece in _split2(lw[rows, :]))
    a = _sigmoid(a0_ref[...] + lora_a)
    gmat = gmat_ref[...]
    kk = k * kk_ref[...]
    kap = kk * lax.rsqrt(_group_sum(kk * kk, gmat, split=False) + 1e-12)
    k2 = k * (1.0 + (a - 1.0) * ka_ref[...])
    bonus_ref[...] = _group_sum(r * k2 * rk_ref[...], gmat, split=False) * v
    r_ref[...] = r.astype(r_ref.dtype)
    k_ref[...] = k2.astype(k_ref.dtype)
    v_ref[...] = v.astype(v_ref.dtype)
    kap_ref[...] = kap.astype(kap_ref.dtype)
    b_ref[...] = (kap * a).astype(b_ref.dtype)


def _bd(x, left):
    z = jnp.zeros_like(x)
    return jnp.concatenate([jnp.where(left, x, z), jnp.where(left, z, x)], axis=0)


def _unbd(x_bd):
    c = x_bd.shape[0] // 2
    return x_bd[:c] + x_bd[c:]


def _rwkv_chunk_pair(L, r, k, v, kap, b, masks):
    C = RWKV_CHUNK
    left, strict, lower, same_sub, eye, first_row = masks
    Lc = L[C - 1:C, :]
    L_excl = jnp.where(first_row, 0.0, pltpu.roll(L, 1, 0))
    e_l, e_lx, e_nl, e_r = jnp.exp(L), jnp.exp(L_excl), jnp.exp(-L), jnp.exp(Lc - L)
    at = _bd(-kap * e_lx, left).astype(BF16)
    rt = _bd(r * e_l, left)
    bt = _bd(b * e_nl, left).astype(BF16)
    kt = _bd(k * e_nl, left).astype(BF16)
    bh = _bd(b * e_r, left).astype(BF16)
    kh = _bd(k * e_r, left).astype(BF16)
    vb = _bd(v, left).astype(BF16)

    a1 = _dot_nt(jnp.concatenate([at, rt.astype(BF16)], axis=0), jnp.concatenate([bt, kt], axis=0))
    yield None
    n = jnp.where(strict, a1[:2 * C, :2 * C], 0.0)
    ak = jnp.where(strict, a1[:2 * C, 2 * C:], 0.0).astype(BF16)
    rb = jnp.where(lower, a1[2 * C:, :2 * C], 0.0).astype(BF16)
    rk = jnp.where(lower, a1[2 * C:, 2 * C:], 0.0).astype(BF16)

    W = 2 * C
    nd = jnp.where(same_sub, n, 0.0)
    no = (n - nd).astype(BF16)
    ndb = nd.astype(BF16)
    n2 = _dot(ndb, ndb)
    akv = _dot(ak, vb)
    yield None
    n2b = n2.astype(BF16)
    t0 = eye + nd
    r = _dot(n2b, jnp.concatenate([n2b, t0.astype(BF16)], axis=1))
    yield None
    n4b = r[:, :W].astype(BF16)
    t01 = t0 + r[:, W:]
    r = _dot(n4b, jnp.concatenate([n4b, t01.astype(BF16)], axis=1))
    yield None
    u = t01 + r[:, W:]
    td = u + _dot(r[:, :W].astype(BF16), u.astype(BF16))
    yield None
    tdb = td.astype(BF16)
    x1 = _dot(tdb, no)
    yield None
    x1b = x1.astype(BF16)
    r = _dot(x1b, jnp.concatenate([x1b, tdb], axis=1))
    yield None
    w = td + r[:, W:]
    t = (w + _dot(r[:, :W].astype(BF16), w.astype(BF16))).astype(BF16)
    yield None
    pq = _dot(t, jnp.concatenate([at, akv.astype(BF16)], axis=1)).astype(BF16)
    yield None
    z = jnp.concatenate([pq, jnp.concatenate([jnp.zeros_like(vb), vb], axis=1)], axis=0)
    ry = _dot(jnp.concatenate([rb, rk], axis=1), z)
    mg = _dot_tn(z, jnp.concatenate([bh, kh], axis=0))
    yield None
    rp = rt + ry[:, :2 * C]
    yl = ry[:, 2 * C:]
    yield _unbd(rp), _unbd(yl), _unbd(mg[:2 * C]), _unbd(mg[2 * C:]), jnp.exp(Lc)


def _rwkv_chunk_kernel(cum_ref, r_ref, k_ref, v_ref, kap_ref, b_ref,
                       rp_ref, yl_ref, mm_ref, gg_ref, gam_ref, *, pairs, cpb):
    C = RWKV_CHUNK
    lane = lax.broadcasted_iota(jnp.int32, (1, LANES), 1)
    left = lane < RWKV_HEAD_DIM
    ri = lax.broadcasted_iota(jnp.int32, (2 * C, 2 * C), 0)
    ci = lax.broadcasted_iota(jnp.int32, (2 * C, 2 * C), 1)
    same_head = (ri // C) == (ci // C)
    strict = same_head & ((ri % C) > (ci % C))
    lower = same_head & ((ri % C) >= (ci % C))
    same_sub = (ri // RWKV_SUB) == (ci // RWKV_SUB)
    eye = jnp.where(ri == ci, 1.0, 0.0).astype(F32)
    first_row = lax.broadcasted_iota(jnp.int32, (C, 1), 0) == 0
    masks = (left, strict, lower, same_sub, eye, first_row)
    jobs = [(c, slice(c * C, (c + 1) * C), slice(p * LANES, (p + 1) * LANES))
            for c in range(cpb) for p in range(pairs)]
    results = _round_robin([
        _rwkv_chunk_pair(cum_ref[rows, sl], r_ref[rows, sl].astype(F32), k_ref[rows, sl].astype(F32),
                         v_ref[rows, sl].astype(F32), kap_ref[rows, sl].astype(F32),
                         b_ref[rows, sl].astype(F32), masks)
        for _, rows, sl in jobs])
    for (c, rows, sl), (rp, yl, mm, gg, gam) in zip(jobs, results):
        rp_ref[rows, sl] = rp.astype(rp_ref.dtype)
        yl_ref[rows, sl] = yl
        mm_ref[c, :, sl] = mm.astype(mm_ref.dtype)
        gg_ref[c, :, sl] = gg
        gam_ref[c, :, sl] = gam


def _rwkv_seq_kernel(rp_ref, yl_ref, mm_ref, gg_ref, gam_ref, bonus_ref, gmat_ref, gng_ref, gnb_ref,
                     o_ref, s_sc, y_sc, *, pairs, cb):
    C = RWKV_CHUNK
    lane = lax.broadcasted_iota(jnp.int32, (1, LANES), 1)
    left = lane < RWKV_HEAD_DIM

    @pl.when(pl.program_id(2) == 0)
    def _():
        s_sc[...] = jnp.zeros_like(s_sc)

    states = [s_sc[p] for p in range(pairs)]
    for c in range(cb):
        rows = slice(c * C, (c + 1) * C)
        for p in range(pairs):
            sl = slice(p * LANES, (p + 1) * LANES)
            s = states[p]
            sb = s.astype(BF16)
            y_sc[rows, sl] = _dot_nt(rp_ref[rows, sl], sb) + yl_ref[rows, sl]
            mm = _bd(mm_ref[c, :, sl], left)
            gg = _bd(gg_ref[c, :, sl], left)
            states[p] = s * gam_ref[c, :, sl] + _dot(sb, mm) + gg
    for p in range(pairs):
        s_sc[p] = states[p]

    y = y_sc[...]
    gmat = gmat_ref[...]
    inv_n = 1.0 / RWKV_HEAD_DIM
    yc = y - _group_sum(y, gmat) * inv_n
    var = _group_sum(yc * yc, gmat) * inv_n
    yn = yc * lax.rsqrt(var + RWKV_GN_EPS) * gng_ref[...] + gnb_ref[...]
    o_ref[...] = (yn + bonus_ref[...]).astype(o_ref.dtype)


def _rwkv_layer(x, xb, mu, w_in, w0, w_lora_a, w_lora_b, a0, a_lora_a, a_lora_b, k_k, k_a, r_k,
                gn_g, gn_b, w_out, ln_g, ln_b, B, S, ts=512, pairs=8, seq_pairs=8, chunks_per_step=2):
    D = D_MODEL
    C = RWKV_CHUNK
    nc = S // C
    ts = _pick_tile(S, ts)
    tril = jnp.asarray(np.tril(np.ones((C, C), np.float32)), BF16)
    w_r, w_k, w_v, w_g =[w.astype(BF16) for w in jnp.split(w_in, 4, axis=1)]
    gmat = jnp.asarray(np.kron(np.eye(2, dtype=np.float32), np.ones((RWKV_HEAD_DIM, RWKV_HEAD_DIM), np.float32)), BF16)
    row = lambda a: a.reshape(1, D)
    x3 = x.reshape(B, S, D)
    full = lambda shape: pl.BlockSpec(shape, lambda b, j: (0,) * len(shape))
    tile = lambda: pl.BlockSpec((None, ts, D), lambda b, j: (b, j, 0))
    lr = w_lora_a.shape[1]
    outs = pl.pallas_call(
        _rwkv_proj_kernel,
        grid=(B, S // ts),
        in_specs=[tile(),
                  pl.BlockSpec((None, 8, D), lambda b, j: (b, jnp.maximum(j * (ts // 8) - 1, 0), 0)),
                  full((6, D)), full((D, D)), full((D, D)), full((D, D)), full((D, D)),
                  full((D, lr)), full((lr, D)), full((D, lr)), full((lr, D)),
                  full((1, D)), full((1, D)), full((1, D)), full((1, D)), full((1, D)),
                  full((LANES, LANES)), full((C, C))],
        out_specs=[tile() for _ in range(8)],
        out_shape=[jax.ShapeDtypeStruct((B, S, D), dt) for dt in (BF16, BF16, BF16, BF16, BF16, BF16, F32, F32)],
        compiler_params=_cparams("parallel", "arbitrary"),
        name="rwkv_projections",
    )(x3, x3, mu, w_r, w_k, w_v, w_g, w_lora_a.astype(BF16), w_lora_b.astype(BF16),
      a_lora_a.astype(BF16), a_lora_b.astype(BF16), row(w0), row(a0), row(k_k), row(k_a), row(r_k), gmat, tril)
    r, k2, v, g, kap, bvec, cum, bonus = outs

    pw = pairs * LANES
    cpb = _pick_tile(nc, chunks_per_step)
    cblk = lambda: pl.BlockSpec((None, cpb * C, pw), lambda b, c, q: (b, c, q))
    sblk = lambda: pl.BlockSpec((None, cpb, C, pw), lambda b, c, q: (b, c, 0, q))
    rp, yl, mm, gg, gam = pl.pallas_call(
        functools.partial(_rwkv_chunk_kernel, pairs=pairs, cpb=cpb),
        grid=(B, nc // cpb, D // pw),
        in_specs=[cblk() for _ in range(6)],
        out_specs=[cblk(), cblk(), sblk(), sblk(),
                   pl.BlockSpec((None, cpb, 1, pw), lambda b, c, q: (b, c, 0, q))],
        out_shape=[jax.ShapeDtypeStruct((B, S, D), BF16), jax.ShapeDtypeStruct((B, S, D), F32),
                   jax.ShapeDtypeStruct((B, nc, C, D), BF16), jax.ShapeDtypeStruct((B, nc, C, D), F32),
                   jax.ShapeDtypeStruct((B, nc, 1, D), F32)],
        compiler_params=_cparams("parallel", "parallel", "parallel"),
        name="rwkv_chunk_summaries",
    )(cum, r, k2, v, kap, bvec)

    cb = _pick_tile(nc, 8)
    pairs = seq_pairs
    pw = pairs * LANES
    o = pl.pallas_call(
        functools.partial(_rwkv_seq_kernel, pairs=pairs, cb=cb),
        grid=(B, D // pw, nc // cb),
        in_specs=[pl.BlockSpec((None, cb * C, pw), lambda b, q, j: (b, j, q)),
                  pl.BlockSpec((None, cb * C, pw), lambda b, q, j: (b, j, q)),
                  pl.BlockSpec((None, cb, C, pw), lambda b, q, j: (b, j, 0, q)),
                  pl.BlockSpec((None, cb, C, pw), lambda b, q, j: (b, j, 0, q)),
                  pl.BlockSpec((None, cb, 1, pw), lambda b, q, j: (b, j, 0, q)),
                  pl.BlockSpec((None, cb * C, pw), lambda b, q, j: (b, j, q)),
                  pl.BlockSpec((LANES, LANES), lambda b, q, j: (0, 0)),
                  pl.BlockSpec((1, pw), lambda b, q, j: (0, q)),
                  pl.BlockSpec((1, pw), lambda b, q, j: (0, q))],
        out_specs=pl.BlockSpec((None, cb * C, pw), lambda b, q, j: (b, j, q)),
        out_shape=jax.ShapeDtypeStruct((B, S, D), BF16),
        scratch_shapes=[pltpu.VMEM((pairs, 2 * C, LANES), F32), pltpu.VMEM((cb * C, pw), F32)],
        compiler_params=_cparams("parallel", "parallel", "arbitrary"),
        name="rwkv_state_scan",
    )(rp, yl, mm, gg, gam, bonus, gmat, row(gn_g), row(gn_b))
    return _outproj_ln(g.reshape(B * S, D), 0, o.reshape(B * S, D), x, w_out, ln_g, ln_b)


def kernel(x, ln_g, ln_b, fox_w_in, fox_b_f, fox_w_out, dsa_w_in, dsa_kv_norm_g, dsa_w_uk, dsa_w_uv, dsa_w_out, rwkv_mu, rwkv_w_in, rwkv_w0, rwkv_w_lora_a, rwkv_w_lora_b, rwkv_a0, rwkv_a_lora_a, rwkv_a_lora_b, rwkv_k_k, rwkv_k_a, rwkv_r_k, rwkv_gn_g, rwkv_gn_b, rwkv_w_out, ret_w_in, ret_gn_g, ret_w_out):
    B, S, D = x.shape
    h = x.reshape(B * S, D)
    h, hb = _fox_layer(h, h, fox_w_in, fox_b_f, fox_w_out, ln_g[0], ln_b[0], B, S)
    h, hb = _dsa_layer(h, hb, dsa_w_in, dsa_kv_norm_g, dsa_w_uk, dsa_w_uv, dsa_w_out, ln_g[1], ln_b[1], B, S)
    h, hb = _rwkv_layer(h, hb, rwkv_mu, rwkv_w_in, rwkv_w0, rwkv_w_lora_a, rwkv_w_lora_b, rwkv_a0,
                        rwkv_a_lora_a, rwkv_a_lora_b, rwkv_k_k, rwkv_k_a, rwkv_r_k, rwkv_gn_g, rwkv_gn_b,
                        rwkv_w_out, ln_g[2], ln_b[2], B, S)
    h, hb = _ret_layer(h, hb, ret_w_in, ret_gn_g, ret_w_out, ln_g[3], ln_b[3], B, S)
    return h.reshape(B, S, D)
```

```python
import functools
import math

import jax
import jax.numpy as jnp
import numpy as np
from jax import lax
from jax.experimental import pallas as pl
from jax.experimental.pallas import tpu as pltpu

F32 = jnp.float32
BF16 = jnp.bfloat16

D_MODEL = 1024
DEPTH = 4
LN_EPS = 1e-5
RMS_EPS = 1e-6
DN_ALPHA = (2 * DEPTH) ** 0.25
ROPE_THETA = 500000.0

FOX_HEADS = 8
FOX_HEAD_DIM = 128

RET_HEADS = 4
RET_HEAD_DIM = 256
RET_THETA = 10000.0

LANES = 128
VMEM_LIMIT = 48 * 1024 * 1024
NEG_BIG = -2.0 ** 100
LOG2E = 1.4426950408889634


def _cparams(*sem, vmem_limit=VMEM_LIMIT):
    return pltpu.CompilerParams(dimension_semantics=sem, vmem_limit_bytes=vmem_limit)


def _dot(a, b):
    return jnp.dot(a, b, preferred_element_type=F32)


def _dot_nt(a, b):
    return lax.dot_general(a, b, (((1,), (1,)), ((), ())), preferred_element_type=F32)


def _dot_tn(a, b):
    return lax.dot_general(a, b, (((0,), (0,)), ((), ())), preferred_element_type=F32)


def _split2(x):
    hi = x.astype(BF16)
    lo = (x - hi.astype(F32)).astype(BF16)
    return hi, lo


def _split3(x):
    p1 = x.astype(BF16)
    r1 = x - p1.astype(F32)
    p2 = r1.astype(BF16)
    p3 = (r1 - p2.astype(F32)).astype(BF16)
    return p1, p2, p3


def _sigmoid(x):
    return 1.0 / (1.0 + jnp.exp(-x))


def _round_robin(gens):
    results = [None] * len(gens)
    live = list(range(len(gens)))
    while live:
        still = []
        for i in live:
            try:
                out = next(gens[i])
            except StopIteration:
                continue
            if out is not None:
                results[i] = out
            still.append(i)
        live = still
    return results


def _pick_tile(n, pref):
    t = min(n, pref)
    while n % t:
        t //= 2
    return t


def _mm_kernel(a_ref, w_ref, o_ref):
    o_ref[...] = _dot(a_ref[...].astype(BF16), w_ref[...]).astype(o_ref.dtype)


def _matmul(a, w, out_dtype=BF16, tm=2048, tn=1024):
    M, K = a.shape
    N = w.shape[1]
    tm = _pick_tile(M, tm)
    if N % tn:
        tn = N
    return pl.pallas_call(
        _mm_kernel,
        grid=(M // tm, N // tn),
        in_specs=[pl.BlockSpec((tm, K), lambda i, j: (i, 0)),
                  pl.BlockSpec((K, tn), lambda i, j: (0, j))],
        out_specs=pl.BlockSpec((tm, tn), lambda i, j: (i, j)),
        out_shape=jax.ShapeDtypeStruct((M, N), out_dtype),
        compiler_params=_cparams("parallel", "arbitrary"),
        name="proj_matmul",
    )(a, w)


def _outproj_ln_kernel(g_ref, o_ref, x_ref, w_ref, lg_ref, lb_ref, xo_ref, xb_ref):
    half_g = g_ref[...] * 0.5
    h = (half_g * o_ref[...]) * (1.0 + jnp.tanh(half_g))
    z = x_ref[...] + _dot(h.astype(BF16), w_ref[...])
    zc = z - jnp.mean(z, axis=-1, keepdims=True)
    var = jnp.mean(zc * zc, axis=-1, keepdims=True)
    out = zc * lax.rsqrt(var + LN_EPS / DN_ALPHA ** 2) * lg_ref[...] + lb_ref[...]
    xo_ref[...] = out
    xb_ref[...] = out.astype(BF16)


def _outproj_ln(gate_arr, gate_col, o, x, w_out, ln_g, ln_b, tm=1024):
    M, D = x.shape
    tm = _pick_tile(M, tm)
    return pl.pallas_call(
        _outproj_ln_kernel,
        grid=(M // tm,),
        in_specs=[pl.BlockSpec((tm, D), lambda i: (i, gate_col)),
                  pl.BlockSpec((tm, D), lambda i: (i, 0)),
                  pl.BlockSpec((tm, D), lambda i: (i, 0)),
                  pl.BlockSpec((D, D), lambda i: (0, 0)),
                  pl.BlockSpec((1, D), lambda i: (0, 0)),
                  pl.BlockSpec((1, D), lambda i: (0, 0))],
        out_specs=[pl.BlockSpec((tm, D), lambda i: (i, 0)),
                   pl.BlockSpec((tm, D), lambda i: (i, 0))],
        out_shape=[jax.ShapeDtypeStruct((M, D), F32), jax.ShapeDtypeStruct((M, D), BF16)],
        compiler_params=_cparams("parallel"),
        name="outproj_layernorm",
    )(gate_arr, o, x, (w_out * (1.0 / DN_ALPHA)).astype(BF16), ln_g.reshape(1, D), ln_b.reshape(1, D))


FOX_BIAS_PIECES = 3


def _fox_cum_kernel(x_ref, wh_ref, wl_ref, bf_ref, tril_ref, place_ref, pc_ref, carry_sc):
    @pl.when(pl.program_id(1) == 0)
    def _():
        carry_sc[...] = jnp.zeros_like(carry_sc)

    x_hi, x_lo = _split2(x_ref[...])
    z = _dot(x_hi, wh_ref[...]) + _dot(x_lo, wh_ref[...]) + _dot(x_hi, wl_ref[...]) + bf_ref[...]
    logf = jnp.minimum(z, 0.0) - jnp.log(1.0 + jnp.exp(-jnp.abs(z)))
    p1, p2, p3 = _split3(logf)
    tril = tril_ref[...]
    c = _dot(tril, p1) + _dot(tril, p2) + _dot(tril, p3) + carry_sc[...]
    carry_sc[...] = c[c.shape[0] - 1:, :]
    pieces = _split3(c * (-LOG2E))
    pc_ref[...] = sum(_dot(pieces[p], place_ref[p]) for p in range(FOX_BIAS_PIECES)).astype(pc_ref.dtype)


def _fox_cum(x3, w_f, b_f, ts=512):
    B, S, D = x3.shape
    H = w_f.shape[1]
    ts = _pick_tile(S, ts)
    w_pad = jnp.zeros((D, LANES), F32).at[:, :H].set(w_f)
    w_hi, w_lo = _split2(w_pad)
    b_pad = jnp.zeros((1, LANES), F32).at[0, :H].set(b_f)
    tril = jnp.asarray(np.tril(np.ones((ts, ts), np.float32)), BF16)
    place = np.zeros((FOX_BIAS_PIECES, LANES, LANES), np.float32)
    for p in range(FOX_BIAS_PIECES):
        for h in range(H):
            place[p, h, FOX_BIAS_PIECES * h + p] = 1.0
    return pl.pallas_call(
        _fox_cum_kernel,
        grid=(B, S // ts),
        in_specs=[pl.BlockSpec((None, ts, D), lambda b, j: (b, j, 0)),
                  pl.BlockSpec((D, LANES), lambda b, j: (0, 0)),
                  pl.BlockSpec((D, LANES), lambda b, j: (0, 0)),
                  pl.BlockSpec((1, LANES), lambda b, j: (0, 0)),
                  pl.BlockSpec((ts, ts), lambda b, j: (0, 0)),
                  pl.BlockSpec((FOX_BIAS_PIECES, LANES, LANES), lambda b, j: (0, 0, 0))],
        out_specs=pl.BlockSpec((None, ts, LANES), lambda b, j: (b, j, 0)),
        out_shape=jax.ShapeDtypeStruct((B, S, LANES), BF16),
        scratch_shapes=[pltpu.VMEM((1, LANES), F32)],
        compiler_params=_cparams("parallel", "arbitrary"),
        name="fox_decay_cumsum",
    )(x3, w_hi, w_lo, b_pad, tril, jnp.asarray(place, BF16))


ONES_ROWS = 16


def _fox_attn_kernel(q_ref, k_ref, v_ref, pc_ref, o_ref, kaug_sc, vt_sc, m_sc, acc_sc, *, tq, nsub, seq, unroll):
    dh = FOX_HEAD_DIM
    h = pl.program_id(1)
    g = pl.program_id(2)

    @pl.when(g == 0)
    def _():
        kaug_sc[:, :dh] = k_ref[...]
        kaug_sc[:, dh:] = pc_ref[...]
        for c in range(seq // tq):
            rows = slice(c * tq, (c + 1) * tq)
            vt_sc[:dh, rows] = v_ref[rows, :].astype(F32).T.astype(BF16)
        vt_sc[dh:, :] = jnp.ones((ONES_ROWS, seq), BF16)

    feat = lax.broadcasted_iota(jnp.int32, (LANES, tq), 0)
    bias_rows = (feat >= FOX_BIAS_PIECES * h) & (feat < FOX_BIAS_PIECES * (h + 1))
    ones_h = jnp.where(bias_rows, 1.0, 0.0).astype(BF16)
    q_aug = [jnp.concatenate([q_ref[a * tq:(a + 1) * tq, :].astype(F32).T.astype(BF16), ones_h], axis=0)
             for a in range(nsub)]
    m_sc[...] = jnp.full_like(m_sc, NEG_BIG)
    acc_sc[...] = jnp.zeros_like(acc_sc)
    causal = (lax.broadcasted_iota(jnp.int32, (tq, tq), 0) <= lax.broadcasted_iota(jnp.int32, (tq, tq), 1))
    first = g * nsub

    def chain(a, tiles, diag_last):
        offs = [pl.multiple_of(j * tq, tq) for j in tiles]
        scores = []
        for off in offs:
            scores.append(_dot(kaug_sc[pl.ds(off, tq), :], q_aug[a]))
            yield None
        for n, (off, s) in enumerate(zip(offs, scores)):
            if diag_last and n == len(offs) - 1:
                s = jnp.where(causal, s, NEG_BIG)
            m_old = m_sc[a]
            m_new = jnp.maximum(m_old, jnp.max(s, axis=0, keepdims=True))
            alpha = jnp.exp2(m_old - m_new)
            pv = _dot(vt_sc[:, pl.ds(off, tq)], jnp.exp2(s - m_new).astype(BF16))
            yield None
            acc_sc[a] = alpha * acc_sc[a] + pv
            m_sc[a] = m_new
        yield None

    def body(jj, c):
        _round_robin([chain(a, [jj * unroll + u for u in range(unroll)], False) for a in range(nsub)])
        return c

    lax.fori_loop(0, first // unroll, body, 0)
    _round_robin([chain(a, [first + t for t in range(a + 1)], True) for a in range(nsub)])
    for a in range(nsub):
        acc = acc_sc[a]
        o_t = acc[:dh] / acc[dh:dh + 1]
        o_ref[a * tq:(a + 1) * tq, :] = o_t.T.astype(o_ref.dtype)


def _fox_attention(proj3, pieces, tq=256, nsub=8):
    B, S, _ = proj3.shape
    H, dh = FOX_HEADS, FOX_HEAD_DIM
    tq = _pick_tile(S, tq)
    nsub = _pick_tile(S // tq, nsub)
    tg = tq * nsub
    return pl.pallas_call(
        functools.partial(_fox_attn_kernel, tq=tq, nsub=nsub, seq=S, unroll=min(nsub, 4)),
        grid=(B, H, S // tg),
        in_specs=[pl.BlockSpec((None, tg, dh), lambda b, h, g: (b, g, h)),
                  pl.BlockSpec((None, S, dh), lambda b, h, g: (b, 0, H + h)),
                  pl.BlockSpec((None, S, dh), lambda b, h, g: (b, 0, 2 * H + h)),
                  pl.BlockSpec((None, S, LANES), lambda b, h, g: (b, 0, 0))],
        out_specs=pl.BlockSpec((None, tg, dh), lambda b, h, g: (b, g, h)),
        out_shape=jax.ShapeDtypeStruct((B, S, H * dh), BF16),
        scratch_shapes=[pltpu.VMEM((S, dh + LANES), BF16),
                        pltpu.VMEM((dh + ONES_ROWS, S), BF16),
                        pltpu.VMEM((nsub, 1, tq), F32),
                        pltpu.VMEM((nsub, dh + ONES_ROWS, tq), F32)],
        compiler_params=_cparams("parallel", "parallel", "arbitrary"),
        name="fox_attention",
    )(proj3, proj3, proj3, pieces)


def _fox_layer(x, xb, w_in, b_f, w_out, ln_g, ln_b, B, S):
    D = D_MODEL
    H, dh = FOX_HEADS, FOX_HEAD_DIM
    scale = dh ** -0.5 * LOG2E
    w_q, w_k, w_v, w_f, w_g = jnp.split(w_in, [H * dh, 2 * H * dh, 3 * H * dh, 3 * H * dh + H], axis=1)
    w_main = jnp.concatenate([w_q * scale, w_k, w_v, w_g], axis=1).astype(BF16)
    proj = _matmul(xb, w_main)
    pieces = _fox_cum(x.reshape(B, S, D), w_f, b_f)
    o = _fox_attention(proj.reshape(B, S, 4 * D), pieces)
    return _outproj_ln(proj, 3, o.reshape(B * S, D), x, w_out, ln_g, ln_b)


def _ret_kernel(q_ref, k_ref, v_ref, cos_ref, sin_ref, dm_ref, xi_ref, zeta_ref, gc_ref, gn_ref,
                o_ref, r_sc):
    @pl.when(pl.program_id(1) == 0)
    def _():
        r_sc[...] = jnp.zeros_like(r_sc)

    dk = RET_HEAD_DIM
    half = dk // 2
    cos = cos_ref[...]
    sin = sin_ref[...]

    def rope(x):
        x1, x2 = x[:, :half], x[:, half:]
        return jnp.concatenate([x1 * cos - x2 * sin, x2 * cos + x1 * sin], axis=-1)

    def head_chain(h):
        cols = slice(h * dk, (h + 1) * dk)
        q = rope(q_ref[:, cols].astype(F32))
        k = rope(k_ref[:, cols].astype(F32)) * (dk ** -0.5)
        v = v_ref[:, cols]
        qb = q.astype(BF16)
        r_old = r_sc[h]
        scores = _dot_nt(qb, k.astype(BF16))
        cross = _dot(qb, r_old.astype(BF16))
        kz = (k * zeta_ref[h]).astype(BF16)
        r_new = _dot_tn(kz, v)
        yield None
        o = _dot((scores * dm_ref[h]).astype(BF16), v)
        r_sc[h] = r_old * gc_ref[h] + r_new
        yield None
        o = o + cross * xi_ref[h]
        o = o * lax.rsqrt(jnp.mean(o * o, axis=-1, keepdims=True) + RMS_EPS) * gn_ref[:, cols]
        o_ref[:, cols] = o.astype(o_ref.dtype)
        yield None

    _round_robin([head_chain(h) for h in range(RET_HEADS)])


def _ret_layer(x, xb, w_in, gn_g, w_out, ln_g, ln_b, B, S, chunk=512):
    D = D_MODEL
    H, dk = RET_HEADS, RET_HEAD_DIM
    C = _pick_tile(S, chunk)
    proj = _matmul(xb, w_in.astype(BF16))
    f32 = np.float32
    inv = (f32(1.0) / (f32(RET_THETA) ** (np.arange(0, dk, 2, dtype=f32) / f32(dk)))).astype(f32)
    ang = (np.arange(S, dtype=f32)[:, None] * inv[None, :]).astype(f32)
    cos, sin = jnp.asarray(np.cos(ang)), jnp.asarray(np.sin(ang))
    log_g = np.log1p(-(f32(2.0) ** (f32(-5.0) - np.arange(H, dtype=f32)))).astype(f32)
    pos = np.arange(C, dtype=f32)
    diff = pos[:, None] - pos[None, :]
    d_mask = jnp.asarray(np.where(diff[None] >= 0, np.exp(np.maximum(diff, 0.0)[None] * log_g[:, None, None]),
                                  0.0).astype(f32))
    xi = jnp.asarray(np.broadcast_to(np.exp((pos[None, :] + 1.0) * log_g[:, None])[:, :, None],
                                     (H, C, dk)).astype(f32))
    zeta = jnp.asarray(np.broadcast_to(np.exp((C - 1.0 - pos[None, :]) * log_g[:, None])[:, :, None],
                                       (H, C, dk)).astype(f32))
    g_c = jnp.asarray(np.broadcast_to(np.exp(f32(C) * log_g)[:, None, None], (H, 1, dk)).astype(f32))
    p3 = proj.reshape(B, S, 4 * D)
    o = pl.pallas_call(
        _ret_kernel,
        grid=(B, S // C),
        in_specs=[pl.BlockSpec((None, C, D), lambda b, c: (b, c, 0)),
                  pl.BlockSpec((None, C, D), lambda b, c: (b, c, 1)),
                  pl.BlockSpec((None, C, D), lambda b, c: (b, c, 2)),
                  pl.BlockSpec((C, dk // 2), lambda b, c: (c, 0)),
                  pl.BlockSpec((C, dk // 2), lambda b, c: (c, 0)),
                  pl.BlockSpec((H, C, C), lambda b, c: (0, 0, 0)),
                  pl.BlockSpec((H, C, dk), lambda b, c: (0, 0, 0)),
                  pl.BlockSpec((H, C, dk), lambda b, c: (0, 0, 0)),
                  pl.BlockSpec((H, 1, dk), lambda b, c: (0, 0, 0)),
                  pl.BlockSpec((1, D), lambda b, c: (0, 0))],
        out_specs=pl.BlockSpec((None, C, D), lambda b, c: (b, c, 0)),
        out_shape=jax.ShapeDtypeStruct((B, S, D), BF16),
        scratch_shapes=[pltpu.VMEM((H, dk, dk), F32)],
        compiler_params=_cparams("parallel", "arbitrary"),
        name="retnet_retention",
    )(p3, p3, p3, cos, sin, d_mask, xi, zeta, g_c, gn_g.reshape(1, D))
    return _outproj_ln(proj, 3, o.reshape(B * S, D), x, w_out, ln_g, ln_b)


DSA_HEADS = 8
DSA_HEAD_DIM = 128
DSA_ROPE_DIM = 32
DSA_KV_RANK = 128
IDX_HEADS = 8
IDX_DIM = 64
IDX_ROPE_DIM = 16
TOPK_MAX = 256
HALF_MIN = -2 ** 15
HALF_ROWS = 16
DSA_VMEM_LIMIT = 56 * 1024 * 1024
DSA_ONES_ROWS = 16


def _rope_perm(width, groups):
    p = np.zeros((width, width), np.float32)
    for start, half in groups:
        for j in range(half):
            p[start + half + j, start + j] = 1.0
            p[start + j, start + half + j] = 1.0
    return p


def _rope_cs(S, width, groups, theta_dims):
    f32 = np.float32
    c = np.ones((S, width), f32)
    sg = np.zeros((S, width), f32)
    pos = np.arange(S, dtype=f32)[:, None]
    for (start, half), rot_dim in zip(groups, theta_dims):
        inv = (f32(1.0) / (f32(ROPE_THETA) ** (np.arange(0, rot_dim, 2, dtype=f32) / f32(rot_dim)))).astype(f32)
        ang = (pos * inv[None, :]).astype(f32)
        cos, sin = np.cos(ang), np.sin(ang)
        c[:, start:start + half] = cos
        c[:, start + half:start + 2 * half] = cos
        sg[:, start:start + half] = -sin
        sg[:, start + half:start + 2 * half] = sin
    return jnp.asarray(c), jnp.asarray(sg)


def _dsa_prep_kernel(q_ref, qi_ref, ckv_ref, misc_ref, cq_ref, sq_ref, ci_ref, si_ref, cm_ref, sm_ref,
                     pq_ref, pi_ref, pm_ref, selk_ref, selw_ref, wuk_ref, kvg_ref,
                     qf_ref, kvl_ref, kvt_ref, qir_ref, kid_ref, wit_ref):
    H, dh = DSA_HEADS, DSA_HEAD_DIM
    lane = lax.broadcasted_iota(jnp.int32, (1, LANES), 1)
    rope_lanes = lane < DSA_ROPE_DIM
    cq, sq = cq_ref[...], sq_ref[...]
    scale = dh ** -0.5 * LOG2E
    for h in range(H):
        qh = q_ref[:, h * dh:(h + 1) * dh]
        qr = qh.astype(F32) * cq + _dot(qh, pq_ref[...]) * sq
        q_lat = _dot(qr.astype(BF16), wuk_ref[h])
        qf_ref[h, :dh, :] = (q_lat * scale).T.astype(BF16)
        qf_ref[h, dh:, :] = jnp.where(rope_lanes, qr * scale, 0.0).T.astype(BF16)
    ci, si = ci_ref[...], si_ref[...]
    for g in range(IDX_HEADS * IDX_DIM // LANES):
        qg = qi_ref[:, g * LANES:(g + 1) * LANES]
        qr = qg.astype(F32) * ci + _dot(qg, pi_ref[...]) * si
        qir_ref[g * LANES:(g + 1) * LANES, :] = (qr * (IDX_DIM ** -0.5)).T.astype(BF16)
    ckv = ckv_ref[...].astype(F32)
    ckv = ckv * lax.rsqrt(jnp.mean(ckv * ckv, axis=-1, keepdims=True) + RMS_EPS) * kvg_ref[...]
    misc = misc_ref[...]
    mr = (misc.astype(F32) * cm_ref[...] + _dot(misc, pm_ref[...]) * sm_ref[...])
    kvl_ref[:, :DSA_KV_RANK] = ckv.astype(BF16)
    kvl_ref[:, DSA_KV_RANK:] = jnp.where(rope_lanes, mr, 0.0).astype(BF16)
    kvt_ref[:DSA_KV_RANK, :] = ckv.T.astype(BF16)
    kvt_ref[DSA_KV_RANK:, :] = jnp.ones((DSA_ONES_ROWS, ckv.shape[0]), BF16)
    kid_ref[...] = _dot(mr.astype(BF16), selk_ref[...]).astype(BF16)
    wi = _dot(misc, selw_ref[...]) * (IDX_HEADS ** -0.5)
    wit_ref[...] = wi.T[:IDX_HEADS, :]


def _sort_key(x):
    b = pltpu.bitcast(x, jnp.int32)
    return jnp.where(b < 0, b ^ 0x7FFFFFFF, b)


def _dsa_main_kernel(qf_ref, kvl_ref, kvt_ref, qi_ref, kid_ref, wit_ref, wuv_ref, obuf_ref, o_ref,
                     key_sc, hi_sc, lo_sc, low_sc, m_sc, acc_sc, *, tq, tk, k_sel, seq, qtile):
    del obuf_ref
    H = DSA_HEADS
    q0 = qtile * tq
    nj = (q0 + tq + tk - 1) // tk
    qpos = q0 + lax.broadcasted_iota(jnp.int32, (tk, tq), 1)
    kpos0 = lax.broadcasted_iota(jnp.int32, (tk, tq), 0)
    feat = lax.broadcasted_iota(jnp.int32, (LANES, 1), 0)
    neg_inf_key = _sort_key(jnp.full((1, 1), -jnp.inf, F32))

    wit = wit_ref[...]
    qi_heads = []
    for h in range(IDX_HEADS):
        g = qi_ref[(h // 2) * LANES:(h // 2 + 1) * LANES, :]
        keep = (feat >= IDX_DIM) if (h % 2) else (feat < IDX_DIM)
        qi_heads.append(jnp.where(keep, g, jnp.zeros_like(g)))

    def score_tile(j, c, causal):
        off = pl.multiple_of(j * tk, tk)
        ki = kid_ref[pl.ds(off, tk), :]
        scores = [_dot(ki, qi_heads[h]) for h in range(IDX_HEADS)]
        isc = jnp.zeros((tk, tq), F32)
        for h in range(IDX_HEADS):
            isc = isc + jnp.maximum(scores[h], 0.0) * wit[h:h + 1, :]
        isc = isc + 0.0
        if causal:
            isc = jnp.where(kpos0 + off <= qpos, isc, -jnp.inf)
        key = _sort_key(isc)
        key_sc[pl.ds(off, tk), :] = key
        hi_sc[pl.ds(off, tk), :] = (key >> 16).astype(jnp.int16)
        lo_sc[pl.ds(off, tk), :] = ((key & 0xFFFF) + HALF_MIN).astype(jnp.int16)
        return c

    n_clear = min(q0 // tk, nj)
    lax.fori_loop(0, n_clear, functools.partial(score_tile, causal=False), 0)
    lax.fori_loop(n_clear, nj, functools.partial(score_tile, causal=True), 0)

    def count(pred_fn):
        acc = jnp.zeros((8, tq), jnp.int32)
        for j in range(nj):
            hit = jnp.where(pred_fn(key_sc[j * tk:(j + 1) * tk, :], j * tk), 1, 0)
            acc = acc + jnp.sum(hit.reshape(tk // 8, 8, tq), axis=0)
        return jnp.sum(acc, axis=0, keepdims=True)

    rows16 = tk // HALF_ROWS

    def count16(ref, pred_fn):
        accs = [jnp.zeros((HALF_ROWS, tq), jnp.int16) for _ in range(2)]
        for j in range(nj):
            hit = jnp.where(pred_fn(ref[j * tk:(j + 1) * tk, :].reshape(rows16, HALF_ROWS, tq)),
                            jnp.int16(1), jnp.int16(0))
            for r in range(rows16):
                accs[r % 2] = accs[r % 2] + hit[r]
        return jnp.sum((accs[0] + accs[1]).astype(jnp.int32), axis=0, keepdims=True)

    def as_half(v):
        return jnp.broadcast_to(v.astype(jnp.int16), (HALF_ROWS, tq))[None]

    def search16(ref, base0, cnt0, rej0, bits, want):
        def bit_step(t, carry):
            base, cnt_b, cnt_r = carry
            cand = base + lax.shift_left(jnp.int32(1), bits - 1 - t)
            cand16 = as_half(cand)
            c = count16(ref, lambda kt: kt >= cand16)
            ok = c >= want
            return jnp.where(ok, cand, base), jnp.where(ok, c, cnt_b), jnp.where(ok, cnt_r, c)
        return lax.fori_loop(0, bits, bit_step, (base0, cnt0, rej0))

    zero16 = as_half(jnp.zeros((1, tq), jnp.int32))
    cnt_pos = count16(hi_sc, lambda kt: kt >= zero16)
    nonneg = cnt_pos >= k_sel
    t1, ge_hi, above = search16(hi_sc, jnp.where(nonneg, 0, HALF_MIN), jnp.where(nonneg, cnt_pos, nj * tk),
                                jnp.where(nonneg, 0, cnt_pos), 15, k_sel)
    t1_16 = as_half(t1)

    for j in range(nj):
        rows = slice(j * tk, (j + 1) * tk)
        hi = hi_sc[rows, :].reshape(rows16, HALF_ROWS, tq)
        lo = lo_sc[rows, :].reshape(rows16, HALF_ROWS, tq)
        low_sc[rows, :] = jnp.where(hi == t1_16, lo, jnp.int16(HALF_MIN)).reshape(tk, tq)
    t2, ge_low, gt_low = search16(low_sc, jnp.full((1, tq), HALF_MIN, jnp.int32), ge_hi - above,
                                  jnp.zeros((1, tq), jnp.int32), 16, k_sel - above)
    thr = lax.shift_left(t1, 16) | (t2 - HALF_MIN)
    n_ge = above + ge_low
    n_gt = above + gt_low
    need = k_sel - n_gt
    excess = ((n_ge - n_gt) > need) & (thr > neg_inf_key)
    any_excess = jnp.max(jnp.where(excess, 1, 0)) > 0

    def tie_cut():
        def step(t, lo):
            cand = lo + lax.shift_left(jnp.int32(1), int(math.log2(seq)) - t)
            c = count(lambda kt, off: (kt == thr) & (kpos0 + off < cand))
            return jnp.where(c < need, cand, lo)
        lo = lax.fori_loop(0, int(math.log2(seq)) + 1, step, jnp.zeros((1, tq), jnp.int32))
        return jnp.where(excess, lo, seq)

    cut = lax.cond(any_excess, tie_cut, lambda: jnp.full((1, tq), seq, jnp.int32))

    m_sc[...] = jnp.full_like(m_sc, NEG_BIG)
    acc_sc[...] = jnp.zeros_like(acc_sc)

    def attn_tile(j, c):
        off = pl.multiple_of(j * tk, tk)
        kt = key_sc[pl.ds(off, tk), :]
        kpos = kpos0 + off
        bias = jnp.where(kt > thr, 0.0, jnp.where(kt == thr, jnp.where(kpos <= cut, 0.0, NEG_BIG), NEG_BIG))
        bias = jnp.where(kpos <= qpos, bias, NEG_BIG)
        kv = kvl_ref[pl.ds(off, tk), :]
        kvt = kvt_ref[:, pl.ds(off, tk)]

        def head_step(h):
            s = _dot(kv, qf_ref[h]) + bias
            yield None
            m_old = m_sc[h]
            m_new = jnp.maximum(m_old, jnp.max(s, axis=0, keepdims=True))
            alpha = jnp.exp2(m_old - m_new)
            pv = _dot(kvt, jnp.exp2(s - m_new).astype(BF16))
            yield None
            acc_sc[h] = alpha * acc_sc[h] + pv
            m_sc[h] = m_new
            yield None

        _round_robin([head_step(h) for h in range(H)])
        return c

    lax.fori_loop(0, nj, attn_tile, 0)
    for h in range(H):
        acc = acc_sc[h]
        o_lat_t = (acc[:DSA_KV_RANK] / acc[DSA_KV_RANK:DSA_KV_RANK + 1]).astype(BF16)
        o_ref[:, h * DSA_HEAD_DIM:(h + 1) * DSA_HEAD_DIM] = _dot_tn(o_lat_t, wuv_ref[h]).astype(o_ref.dtype)


def _dsa_layer(x, xb, w_in, kv_norm_g, w_uk, w_uv, w_out, ln_g, ln_b, B, S, tq=512, tk=512):
    D = D_MODEL
    H, dh, dr, dc = DSA_HEADS, DSA_HEAD_DIM, DSA_ROPE_DIM, DSA_KV_RANK
    HI, di = IDX_HEADS, IDX_DIM
    w_q, w_ckv, w_kr, w_qi, w_ki, w_wi, w_g = jnp.split(
        w_in, np.cumsum([H * dh, dc, dr, HI * di, di, HI]).tolist(), axis=1)
    w_misc = jnp.concatenate([w_kr, w_ki, w_wi, jnp.zeros((D, LANES - dr - di - HI), F32)], axis=1)
    w_main = jnp.concatenate([w_q, w_g, w_qi, w_ckv, w_misc], axis=1).astype(BF16)
    n_main = w_main.shape[1]
    proj = _matmul(xb, w_main, tm=1024, tn=n_main)
    c_q, c_qi, c_ckv, c_misc = 0, 2 * D // LANES, (2 * D + HI * di) // LANES, (2 * D + HI * di + dc) // LANES

    q_groups = [(0, dr // 2)]
    i_groups = [(0, IDX_ROPE_DIM // 2), (di, IDX_ROPE_DIM // 2)]
    m_groups = [(0, dr // 2), (dr, IDX_ROPE_DIM // 2)]
    cq, sq = _rope_cs(S, LANES, q_groups, [dr])
    ci, si = _rope_cs(S, LANES, i_groups, [IDX_ROPE_DIM, IDX_ROPE_DIM])
    cm, sm = _rope_cs(S, LANES, m_groups, [dr, IDX_ROPE_DIM])
    pq = jnp.asarray(_rope_perm(LANES, q_groups), BF16)
    pi = jnp.asarray(_rope_perm(LANES, i_groups), BF16)
    pm = jnp.asarray(_rope_perm(LANES, m_groups), BF16)
    selk = np.zeros((LANES, LANES), np.float32)
    for j in range(di):
        selk[dr + j, j] = 1.0
        selk[dr + j, di + j] = 1.0
    selw = np.zeros((LANES, LANES), np.float32)
    for j in range(HI):
        selw[dr + di + j, j] = 1.0
    wuk = jnp.concatenate([jnp.zeros((H, dr, dc), F32), jnp.transpose(w_uk, (0, 2, 1))], axis=1).astype(BF16)

    ts = _pick_tile(S, 512)
    p3 = proj.reshape(B, S, n_main)
    tab = lambda: pl.BlockSpec((ts, LANES), lambda b, j: (j, 0))
    mat = lambda: pl.BlockSpec((LANES, LANES), lambda b, j: (0, 0))
    qf, kvl, kvt, qir, kid, wit = pl.pallas_call(
        _dsa_prep_kernel,
        grid=(B, S // ts),
        in_specs=[pl.BlockSpec((None, ts, H * dh), lambda b, j: (b, j, 0)),
                  pl.BlockSpec((None, ts, HI * di), lambda b, j: (b, j, c_qi * LANES // (HI * di))),
                  pl.BlockSpec((None, ts, dc), lambda b, j: (b, j, c_ckv)),
                  pl.BlockSpec((None, ts, LANES), lambda b, j: (b, j, c_misc)),
                  tab(), tab(), tab(), tab(), tab(), tab(),
                  mat(), mat(), mat(), mat(), mat(),
                  pl.BlockSpec((H, LANES, dc), lambda b, j: (0, 0, 0)),
                  pl.BlockSpec((1, dc), lambda b, j: (0, 0))],
        out_specs=[pl.BlockSpec((None, H, 2 * dc, ts), lambda b, j: (b, 0, 0, j)),
                   pl.BlockSpec((None, ts, 2 * dc), lambda b, j: (b, j, 0)),
                   pl.BlockSpec((None, dc + DSA_ONES_ROWS, ts), lambda b, j: (b, 0, j)),
                   pl.BlockSpec((None, HI * di, ts), lambda b, j: (b, 0, j)),
                   pl.BlockSpec((None, ts, LANES), lambda b, j: (b, j, 0)),
                   pl.BlockSpec((None, HI, ts), lambda b, j: (b, 0, j))],
        out_shape=[jax.ShapeDtypeStruct((B, H, 2 * dc, S), BF16),
                   jax.ShapeDtypeStruct((B, S, 2 * dc), BF16),
                   jax.ShapeDtypeStruct((B, dc + DSA_ONES_ROWS, S), BF16),
                   jax.ShapeDtypeStruct((B, HI * di, S), BF16),
                   jax.ShapeDtypeStruct((B, S, LANES), BF16),
                   jax.ShapeDtypeStruct((B, HI, S), F32)],
        compiler_params=_cparams("parallel", "parallel"),
        name="dsa_prep",
    )(p3, p3, p3, p3, cq, sq, ci, si, cm, sm, pq, pi, pm,
      jnp.asarray(selk, BF16), jnp.asarray(selw, BF16), wuk, kv_norm_g.reshape(1, dc))

    tq = _pick_tile(S, tq)
    tk = _pick_tile(S, tk)
    k_sel = min(TOPK_MAX, S // 4)
    wuv = w_uv.astype(BF16)
    o = jnp.zeros((B, S, H * dh), BF16)
    for i in range(S // tq):
        nkeys = -(-((i + 1) * tq) // tk) * tk
        o = pl.pallas_call(
            functools.partial(_dsa_main_kernel, tq=tq, tk=tk, k_sel=k_sel, seq=S, qtile=i),
            grid=(B,),
            in_specs=[pl.BlockSpec((None, H, 2 * dc, tq), lambda b, i=i: (b, 0, 0, i)),
                      pl.BlockSpec((None, nkeys, 2 * dc), lambda b: (b, 0, 0)),
                      pl.BlockSpec((None, dc + DSA_ONES_ROWS, nkeys), lambda b: (b, 0, 0)),
                      pl.BlockSpec((None, HI * di, tq), lambda b, i=i: (b, 0, i)),
                      pl.BlockSpec((None, nkeys, LANES), lambda b: (b, 0, 0)),
                      pl.BlockSpec((None, HI, tq), lambda b, i=i: (b, 0, i)),
                      pl.BlockSpec((H, dc, dh), lambda b: (0, 0, 0)),
                      pl.BlockSpec(memory_space=pl.ANY)],
            out_specs=pl.BlockSpec((None, tq, H * dh), lambda b, i=i: (b, i, 0)),
            out_shape=jax.ShapeDtypeStruct((B, S, H * dh), BF16),
            input_output_aliases={7: 0},
            scratch_shapes=[pltpu.VMEM((nkeys, tq), jnp.int32),
                            pltpu.VMEM((nkeys, tq), jnp.int16), pltpu.VMEM((nkeys, tq), jnp.int16),
                            pltpu.VMEM((nkeys, tq), jnp.int16),
                            pltpu.VMEM((H, 1, tq), F32),
                            pltpu.VMEM((H, dc + DSA_ONES_ROWS, tq), F32)],
            compiler_params=_cparams("parallel", vmem_limit=DSA_VMEM_LIMIT),
            name=f"dsa_select_attention_q{i}",
        )(qf, kvl, kvt, qir, kid, wit, wuv, o)
    return _outproj_ln(proj, 1, o.reshape(B * S, D), x, w_out, ln_g, ln_b)


RWKV_HEADS = 16
RWKV_HEAD_DIM = 64
RWKV_GN_EPS = 64e-5
RWKV_CHUNK = 64
RWKV_SUB = 16


def _group_sum(x, gmat, split=True):
    outs = []
    for c in range(x.shape[1] // LANES):
        xc = x[:, c * LANES:(c + 1) * LANES]
        if split:
            hi, lo = _split2(xc)
            outs.append(_dot(hi, gmat) + _dot(lo, gmat))
        else:
            outs.append(_dot(xc.astype(BF16), gmat))
    return outs[0] if len(outs) == 1 else jnp.concatenate(outs, axis=1)


def _softplus(y):
    return jnp.maximum(y, 0.0) + jnp.log(1.0 + jnp.exp(-jnp.abs(y)))


def _rwkv_proj_kernel(x_ref, xprev_ref, mu_ref, wr_ref, wk_ref, wv_ref, wg_ref, wla_ref, wlb_ref,
                      ala_ref, alb_ref, w0_ref, a0_ref, kk_ref, ka_ref, rk_ref, gmat_ref, tril_ref,
                      r_ref, k_ref, v_ref, g_ref, kap_ref, b_ref, cum_ref, bonus_ref):
    x = x_ref[...]
    ts = x.shape[0]
    prev = jnp.where(pl.program_id(1) == 0, 0.0, xprev_ref[7:8, :])
    rowid = lax.broadcasted_iota(jnp.int32, (ts, 1), 0)
    xx = jnp.where(rowid == 0, prev, pltpu.roll(x, 1, 0)) - x

    def mixed(i):
        return (x + xx * mu_ref[i:i + 1, :]).astype(BF16)

    r = _dot(mixed(0), wr_ref[...])
    k = _dot(mixed(2), wk_ref[...])
    v = _dot(mixed(3), wv_ref[...])
    g_ref[...] = _dot(mixed(5), wg_ref[...]).astype(g_ref.dtype)
    lora_w = _dot(jnp.tanh(_dot(mixed(1), wla_ref[...])).astype(BF16), wlb_ref[...])
    lora_a = _dot(_dot(mixed(4), ala_ref[...]).astype(BF16), alb_ref[...])
    w_log = -_softplus(-(w0_ref[...] + lora_w)) - 0.5
    lw = -jnp.exp(w_log)
    tril = tril_ref[...]
    for c in range(ts // RWKV_CHUNK):
        rows = slice(c * RWKV_CHUNK, (c + 1) * RWKV_CHUNK)
        cum_ref[rows, :] = sum(_dot(tril, piece) for piece in _split2(lw[rows, :]))
    a = _sigmoid(a0_ref[...] + lora_a)
    gmat = gmat_ref[...]
    kk = k * kk_ref[...]
    kap = kk * lax.rsqrt(_group_sum(kk * kk, gmat, split=False) + 1e-12)
    k2 = k * (1.0 + (a - 1.0) * ka_ref[...])
    bonus_ref[...] = _group_sum(r * k2 * rk_ref[...], gmat, split=False) * v
    r_ref[...] = r.astype(r_ref.dtype)
    k_ref[...] = k2.astype(k_ref.dtype)
    v_ref[...] = v.astype(v_ref.dtype)
    kap_ref[...] = kap.astype(kap_ref.dtype)
    b_ref[...] = (kap * a).astype(b_ref.dtype)


def _bd(x, left):
    z = jnp.zeros_like(x)
    return jnp.concatenate([jnp.where(left, x, z), jnp.where(left, z, x)], axis=0)


def _unbd(x_bd):
    c = x_bd.shape[0] // 2
    return x_bd[:c] + x_bd[c:]


def _rwkv_chunk_pair(L, r, k, v, kap, b, masks):
    C = RWKV_CHUNK
    left, strict, lower, same_sub, eye, first_row = masks
    Lc = L[C - 1:C, :]
    L_excl = jnp.where(first_row, 0.0, pltpu.roll(L, 1, 0))
    e_l, e_lx, e_nl, e_r = jnp.exp(L), jnp.exp(L_excl), jnp.exp(-L), jnp.exp(Lc - L)
    at = _bd(-kap * e_lx, left).astype(BF16)
    rt = _bd(r * e_l, left)
    bt = _bd(b * e_nl, left).astype(BF16)
    kt = _bd(k * e_nl, left).astype(BF16)
    bh = _bd(b * e_r, left).astype(BF16)
    kh = _bd(k * e_r, left).astype(BF16)
    vb = _bd(v, left).astype(BF16)

    a1 = _dot_nt(jnp.concatenate([at, rt.astype(BF16)], axis=0), jnp.concatenate([bt, kt], axis=0))
    yield None
    n = jnp.where(strict, a1[:2 * C, :2 * C], 0.0)
    ak = jnp.where(strict, a1[:2 * C, 2 * C:], 0.0).astype(BF16)
    rb = jnp.where(lower, a1[2 * C:, :2 * C], 0.0).astype(BF16)
    rk = jnp.where(lower, a1[2 * C:, 2 * C:], 0.0).astype(BF16)

    W = 2 * C
    nd = jnp.where(same_sub, n, 0.0)
    no = (n - nd).astype(BF16)
    ndb = nd.astype(BF16)
    n2 = _dot(ndb, ndb)
    akv = _dot(ak, vb)
    yield None
    n2b = n2.astype(BF16)
    t0 = eye + nd
    r = _dot(n2b, jnp.concatenate([n2b, t0.astype(BF16)], axis=1))
    yield None
    n4b = r[:, :W].astype(BF16)
    t01 = t0 + r[:, W:]
    r = _dot(n4b, jnp.concatenate([n4b, t01.astype(BF16)], axis=1))
    yield None
    u = t01 + r[:, W:]
    td = u + _dot(r[:, :W].astype(BF16), u.astype(BF16))
    yield None
    tdb = td.astype(BF16)
    x1 = _dot(tdb, no)
    yield None
    x1b = x1.astype(BF16)
    r = _dot(x1b, jnp.concatenate([x1b, tdb], axis=1))
    yield None
    w = td + r[:, W:]
    t = (w + _dot(r[:, :W].astype(BF16), w.astype(BF16))).astype(BF16)
    yield None
    pq = _dot(t, jnp.concatenate([at, akv.astype(BF16)], axis=1)).astype(BF16)
    yield None
    z = jnp.concatenate([pq, jnp.concatenate([jnp.zeros_like(vb), vb], axis=1)], axis=0)
    ry = _dot(jnp.concatenate([rb, rk], axis=1), z)
    mg = _dot_tn(z, jnp.concatenate([bh, kh], axis=0))
    yield None
    rp = rt + ry[:, :2 * C]
    yl = ry[:, 2 * C:]
    yield _unbd(rp), _unbd(yl), _unbd(mg[:2 * C]), _unbd(mg[2 * C:]), jnp.exp(Lc)


def _rwkv_chunk_kernel(cum_ref, r_ref, k_ref, v_ref, kap_ref, b_ref,
                       rp_ref, yl_ref, mm_ref, gg_ref, gam_ref, *, pairs, cpb):
    C = RWKV_CHUNK
    lane = lax.broadcasted_iota(jnp.int32, (1, LANES), 1)
    left = lane < RWKV_HEAD_DIM
    ri = lax.broadcasted_iota(jnp.int32, (2 * C, 2 * C), 0)
    ci = lax.broadcasted_iota(jnp.int32, (2 * C, 2 * C), 1)
    same_head = (ri // C) == (ci // C)
    strict = same_head & ((ri % C) > (ci % C))
    lower = same_head & ((ri % C) >= (ci % C))
    same_sub = (ri // RWKV_SUB) == (ci // RWKV_SUB)
    eye = jnp.where(ri == ci, 1.0, 0.0).astype(F32)
    first_row = lax.broadcasted_iota(jnp.int32, (C, 1), 0) == 0
    masks = (left, strict, lower, same_sub, eye, first_row)
    jobs = [(c, slice(c * C, (c + 1) * C), slice(p * LANES, (p + 1) * LANES))
            for c in range(cpb) for p in range(pairs)]
    results = _round_robin([
        _rwkv_chunk_pair(cum_ref[rows, sl], r_ref[rows, sl].astype(F32), k_ref[rows, sl].astype(F32),
                         v_ref[rows, sl].astype(F32), kap_ref[rows, sl].astype(F32),
                         b_ref[rows, sl].astype(F32), masks)
        for _, rows, sl in jobs])
    for (c, rows, sl), (rp, yl, mm, gg, gam) in zip(jobs, results):
        rp_ref[rows, sl] = rp.astype(rp_ref.dtype)
        yl_ref[rows, sl] = yl
        mm_ref[c, :, sl] = mm.astype(mm_ref.dtype)
        gg_ref[c, :, sl] = gg
        gam_ref[c, :, sl] = gam


def _rwkv_seq_kernel(rp_ref, yl_ref, mm_ref, gg_ref, gam_ref, bonus_ref, gmat_ref, gng_ref, gnb_ref,
                     o_ref, s_sc, y_sc, *, pairs, cb):
    C = RWKV_CHUNK
    lane = lax.broadcasted_iota(jnp.int32, (1, LANES), 1)
    left = lane < RWKV_HEAD_DIM

    @pl.when(pl.program_id(2) == 0)
    def _():
        s_sc[...] = jnp.zeros_like(s_sc)

    states = [s_sc[p] for p in range(pairs)]
    for c in range(cb):
        rows = slice(c * C, (c + 1) * C)
        for p in range(pairs):
            sl = slice(p * LANES, (p + 1) * LANES)
            s = states[p]
            sb = s.astype(BF16)
            y_sc[rows, sl] = _dot_nt(rp_ref[rows, sl], sb) + yl_ref[rows, sl]
            mm = _bd(mm_ref[c, :, sl], left)
            gg = _bd(gg_ref[c, :, sl], left)
            states[p] = s * gam_ref[c, :, sl] + _dot(sb, mm) + gg
    for p in range(pairs):
        s_sc[p] = states[p]

    y = y_sc[...]
    gmat = gmat_ref[...]
    inv_n = 1.0 / RWKV_HEAD_DIM
    yc = y - _group_sum(y, gmat) * inv_n
    var = _group_sum(yc * yc, gmat) * inv_n
    yn = yc * lax.rsqrt(var + RWKV_GN_EPS) * gng_ref[...] + gnb_ref[...]
    o_ref[...] = (yn + bonus_ref[...]).astype(o_ref.dtype)


def _rwkv_layer(x, xb, mu, w_in, w0, w_lora_a, w_lora_b, a0, a_lora_a, a_lora_b, k_k, k_a, r_k,
                gn_g, gn_b, w_out, ln_g, ln_b, B, S, ts=512, pairs=8, seq_pairs=8, chunks_per_step=2):
    D = D_MODEL
    C = RWKV_CHUNK
    nc = S // C
    ts = _pick_tile(S, ts)
    tril = jnp.asarray(np.tril(np.ones((C, C), np.float32)), BF16)
    w_r, w_k, w_v, w_g =[w.astype(BF16) for w in jnp.split(w_in, 4, axis=1)]
    gmat = jnp.asarray(np.kron(np.eye(2, dtype=np.float32), np.ones((RWKV_HEAD_DIM, RWKV_HEAD_DIM), np.float32)), BF16)
    row = lambda a: a.reshape(1, D)
    x3 = x.reshape(B, S, D)
    full = lambda shape: pl.BlockSpec(shape, lambda b, j: (0,) * len(shape))
    tile = lambda: pl.BlockSpec((None, ts, D), lambda b, j: (b, j, 0))
    lr = w_lora_a.shape[1]
    outs = pl.pallas_call(
        _rwkv_proj_kernel,
        grid=(B, S // ts),
        in_specs=[tile(),
                  pl.BlockSpec((None, 8, D), lambda b, j: (b, jnp.maximum(j * (ts // 8) - 1, 0), 0)),
                  full((6, D)), full((D, D)), full((D, D)), full((D, D)), full((D, D)),
                  full((D, lr)), full((lr, D)), full((D, lr)), full((lr, D)),
                  full((1, D)), full((1, D)), full((1, D)), full((1, D)), full((1, D)),
                  full((LANES, LANES)), full((C, C))],
        out_specs=[tile() for _ in range(8)],
        out_shape=[jax.ShapeDtypeStruct((B, S, D), dt) for dt in (BF16, BF16, BF16, BF16, BF16, BF16, F32, F32)],
        compiler_params=_cparams("parallel", "arbitrary"),
        name="rwkv_projections",
    )(x3, x3, mu, w_r, w_k, w_v, w_g, w_lora_a.astype(BF16), w_lora_b.astype(BF16),
      a_lora_a.astype(BF16), a_lora_b.astype(BF16), row(w0), row(a0), row(k_k), row(k_a), row(r_k), gmat, tril)
    r, k2, v, g, kap, bvec, cum, bonus = outs

    pw = pairs * LANES
    cpb = _pick_tile(nc, chunks_per_step)
    cblk = lambda: pl.BlockSpec((None, cpb * C, pw), lambda b, c, q: (b, c, q))
    sblk = lambda: pl.BlockSpec((None, cpb, C, pw), lambda b, c, q: (b, c, 0, q))
    rp, yl, mm, gg, gam = pl.pallas_call(
        functools.partial(_rwkv_chunk_kernel, pairs=pairs, cpb=cpb),
        grid=(B, nc // cpb, D // pw),
        in_specs=[cblk() for _ in range(6)],
        out_specs=[cblk(), cblk(), sblk(), sblk(),
                   pl.BlockSpec((None, cpb, 1, pw), lambda b, c, q: (b, c, 0, q))],
        out_shape=[jax.ShapeDtypeStruct((B, S, D), BF16), jax.ShapeDtypeStruct((B, S, D), F32),
                   jax.ShapeDtypeStruct((B, nc, C, D), BF16), jax.ShapeDtypeStruct((B, nc, C, D), F32),
                   jax.ShapeDtypeStruct((B, nc, 1, D), F32)],
        compiler_params=_cparams("parallel", "parallel", "parallel"),
        name="rwkv_chunk_summaries",
    )(cum, r, k2, v, kap, bvec)

    cb = _pick_tile(nc, 8)
    pairs = seq_pairs
    pw = pairs * LANES
    o = pl.pallas_call(
        functools.partial(_rwkv_seq_kernel, pairs=pairs, cb=cb),
        grid=(B, D // pw, nc // cb),
        in_specs=[pl.BlockSpec((None, cb * C, pw), lambda b, q, j: (b, j, q)),
                  pl.BlockSpec((None, cb * C, pw), lambda b, q, j: (b, j, q)),
                  pl.BlockSpec((None, cb, C, pw), lambda b, q, j: (b, j, 0, q)),
                  pl.BlockSpec((None, cb, C, pw), lambda b, q, j: (b, j, 0, q)),
                  pl.BlockSpec((None, cb, 1, pw), lambda b, q, j: (b, j, 0, q)),
                  pl.BlockSpec((None, cb * C, pw), lambda b, q, j: (b, j, q)),
                  pl.BlockSpec((LANES, LANES), lambda b, q, j: (0, 0)),
                  pl.BlockSpec((1, pw), lambda b, q, j: (0, q)),
                  pl.BlockSpec((1, pw), lambda b, q, j: (0, q))],
        out_specs=pl.BlockSpec((None, cb * C, pw), lambda b, q, j: (b, j, q)),
        out_shape=jax.ShapeDtypeStruct((B, S, D), BF16),
        scratch_shapes=[pltpu.VMEM((pairs, 2 * C, LANES), F32), pltpu.VMEM((cb * C, pw), F32)],
        compiler_params=_cparams("parallel", "parallel", "arbitrary"),
        name="rwkv_state_scan",
    )(rp, yl, mm, gg, gam, bonus, gmat, row(gn_g), row(gn_b))
    return _outproj_ln(g.reshape(B * S, D), 0, o.reshape(B * S, D), x, w_out, ln_g, ln_b)


def kernel(x, ln_g, ln_b, fox_w_in, fox_b_f, fox_w_out, dsa_w_in, dsa_kv_norm_g, dsa_w_uk, dsa_w_uv, dsa_w_out, rwkv_mu, rwkv_w_in, rwkv_w0, rwkv_w_lora_a, rwkv_w_lora_b, rwkv_a0, rwkv_a_lora_a, rwkv_a_lora_b, rwkv_k_k, rwkv_k_a, rwkv_r_k, rwkv_gn_g, rwkv_gn_b, rwkv_w_out, ret_w_in, ret_gn_g, ret_w_out):
    B, S, D = x.shape
    h = x.reshape(B * S, D)
    h, hb = _fox_layer(h, h, fox_w_in, fox_b_f, fox_w_out, ln_g[0], ln_b[0], B, S)
    h, hb = _dsa_layer(h, hb, dsa_w_in, dsa_kv_norm_g, dsa_w_uk, dsa_w_uv, dsa_w_out, ln_g[1], ln_b[1], B, S)
    h, hb = _rwkv_layer(h, hb, rwkv_mu, rwkv_w_in, rwkv_w0, rwkv_w_lora_a, rwkv_w_lora_b, rwkv_a0,
                        rwkv_a_lora_a, rwkv_a_lora_b, rwkv_k_k, rwkv_k_a, rwkv_r_k, rwkv_gn_g, rwkv_gn_b,
                        rwkv_w_out, ln_g[2], ln_b[2], B, S)
    h, hb = _ret_layer(h, hb, ret_w_in, ret_gn_g, ret_w_out, ln_g[3], ln_b[3], B, S)
    return h.reshape(B, S, D)
```

```python
import functools
import math

import jax
import jax.numpy as jnp
import numpy as np
from jax import lax
from jax.experimental import pallas as pl
from jax.experimental.pallas import tpu as pltpu

F32 = jnp.float32
BF16 = jnp.bfloat16

D_MODEL = 1024
DEPTH = 4
LN_EPS = 1e-5
RMS_EPS = 1e-6
DN_ALPHA = (2 * DEPTH) ** 0.25
ROPE_THETA = 500000.0

FOX_HEADS = 8
FOX_HEAD_DIM = 128

RET_HEADS = 4
RET_HEAD_DIM = 256
RET_THETA = 10000.0

LANES = 128
VMEM_LIMIT = 48 * 1024 * 1024
NEG_BIG = -2.0 ** 100
LOG2E = 1.4426950408889634


def _cparams(*sem, vmem_limit=VMEM_LIMIT):
    return pltpu.CompilerParams(dimension_semantics=sem, vmem_limit_bytes=vmem_limit)


def _dot(a, b):
    return jnp.dot(a, b, preferred_element_type=F32)


def _dot_nt(a, b):
    return lax.dot_general(a, b, (((1,), (1,)), ((), ())), preferred_element_type=F32)


def _dot_tn(a, b):
    return lax.dot_general(a, b, (((0,), (0,)), ((), ())), preferred_element_type=F32)


def _split2(x):
    hi = x.astype(BF16)
    lo = (x - hi.astype(F32)).astype(BF16)
    return hi, lo


def _split3(x):
    p1 = x.astype(BF16)
    r1 = x - p1.astype(F32)
    p2 = r1.astype(BF16)
    p3 = (r1 - p2.astype(F32)).astype(BF16)
    return p1, p2, p3


def _sigmoid(x):
    return 1.0 / (1.0 + jnp.exp(-x))


def _round_robin(gens):
    results = [None] * len(gens)
    live = list(range(len(gens)))
    while live:
        still = []
        for i in live:
            try:
                out = next(gens[i])
            except StopIteration:
                continue
            if out is not None:
                results[i] = out
            still.append(i)
        live = still
    return results


def _pick_tile(n, pref):
    t = min(n, pref)
    while n % t:
        t //= 2
    return t


def _mm_kernel(a_ref, w_ref, o_ref):
    o_ref[...] = _dot(a_ref[...].astype(BF16), w_ref[...]).astype(o_ref.dtype)


def _matmul(a, w, out_dtype=BF16, tm=2048, tn=1024):
    M, K = a.shape
    N = w.shape[1]
    tm = _pick_tile(M, tm)
    if N % tn:
        tn = N
    return pl.pallas_call(
        _mm_kernel,
        grid=(M // tm, N // tn),
        in_specs=[pl.BlockSpec((tm, K), lambda i, j: (i, 0)),
                  pl.BlockSpec((K, tn), lambda i, j: (0, j))],
        out_specs=pl.BlockSpec((tm, tn), lambda i, j: (i, j)),
        out_shape=jax.ShapeDtypeStruct((M, N), out_dtype),
        compiler_params=_cparams("parallel", "arbitrary"),
        name="proj_matmul",
    )(a, w)


def _outproj_ln_kernel(g_ref, o_ref, x_ref, w_ref, lg_ref, lb_ref, xo_ref, *maybe_xb_ref):
    half_g = g_ref[...] * 0.5
    h = (half_g * o_ref[...]) * (1.0 + jnp.tanh(half_g))
    z = x_ref[...] + _dot(h.astype(BF16), w_ref[...])
    zc = z - jnp.mean(z, axis=-1, keepdims=True)
    var = jnp.mean(zc * zc, axis=-1, keepdims=True)
    out = zc * lax.rsqrt(var + LN_EPS / DN_ALPHA ** 2) * lg_ref[...] + lb_ref[...]
    xo_ref[...] = out
    for xb_ref in maybe_xb_ref:
        xb_ref[...] = out.astype(BF16)


def _outproj_ln(gate_arr, gate_col, o, x, w_out, ln_g, ln_b, tm=1024, bf16_copy=True):
    M, D = x.shape
    tm = _pick_tile(M, tm)
    n_out = 2 if bf16_copy else 1
    outs = pl.pallas_call(
        _outproj_ln_kernel,
        grid=(M // tm,),
        in_specs=[pl.BlockSpec((tm, D), lambda i: (i, gate_col)),
                  pl.BlockSpec((tm, D), lambda i: (i, 0)),
                  pl.BlockSpec((tm, D), lambda i: (i, 0)),
                  pl.BlockSpec((D, D), lambda i: (0, 0)),
                  pl.BlockSpec((1, D), lambda i: (0, 0)),
                  pl.BlockSpec((1, D), lambda i: (0, 0))],
        out_specs=[pl.BlockSpec((tm, D), lambda i: (i, 0)) for _ in range(n_out)],
        out_shape=[jax.ShapeDtypeStruct((M, D), F32), jax.ShapeDtypeStruct((M, D), BF16)][:n_out],
        compiler_params=_cparams("parallel"),
        name="outproj_layernorm",
    )(gate_arr, o, x, (w_out * (1.0 / DN_ALPHA)).astype(BF16), ln_g.reshape(1, D), ln_b.reshape(1, D))
    return (outs[0], outs[1]) if bf16_copy else (outs[0], None)


FOX_BIAS_PIECES = 3


def _fox_cum_kernel(x_ref, wh_ref, wl_ref, bf_ref, tril_ref, place_ref, pc_ref, carry_sc):
    @pl.when(pl.program_id(1) == 0)
    def _():
        carry_sc[...] = jnp.zeros_like(carry_sc)

    x_hi, x_lo = _split2(x_ref[...])
    z = _dot(x_hi, wh_ref[...]) + _dot(x_lo, wh_ref[...]) + _dot(x_hi, wl_ref[...]) + bf_ref[...]
    logf = jnp.minimum(z, 0.0) - jnp.log(1.0 + jnp.exp(-jnp.abs(z)))
    p1, p2, p3 = _split3(logf)
    tril = tril_ref[...]
    c = _dot(tril, p1) + _dot(tril, p2) + _dot(tril, p3) + carry_sc[...]
    carry_sc[...] = c[c.shape[0] - 1:, :]
    pieces = _split3(c * (-LOG2E))
    pc_ref[...] = sum(_dot(pieces[p], place_ref[p]) for p in range(FOX_BIAS_PIECES)).astype(pc_ref.dtype)


def _fox_cum(x3, w_f, b_f, ts=512):
    B, S, D = x3.shape
    H = w_f.shape[1]
    ts = _pick_tile(S, ts)
    w_pad = jnp.zeros((D, LANES), F32).at[:, :H].set(w_f)
    w_hi, w_lo = _split2(w_pad)
    b_pad = jnp.zeros((1, LANES), F32).at[0, :H].set(b_f)
    tril = jnp.asarray(np.tril(np.ones((ts, ts), np.float32)), BF16)
    place = np.zeros((FOX_BIAS_PIECES, LANES, LANES), np.float32)
    for p in range(FOX_BIAS_PIECES):
        for h in range(H):
            place[p, h, FOX_BIAS_PIECES * h + p] = 1.0
    return pl.pallas_call(
        _fox_cum_kernel,
        grid=(B, S // ts),
        in_specs=[pl.BlockSpec((None, ts, D), lambda b, j: (b, j, 0)),
                  pl.BlockSpec((D, LANES), lambda b, j: (0, 0)),
                  pl.BlockSpec((D, LANES), lambda b, j: (0, 0)),
                  pl.BlockSpec((1, LANES), lambda b, j: (0, 0)),
                  pl.BlockSpec((ts, ts), lambda b, j: (0, 0)),
                  pl.BlockSpec((FOX_BIAS_PIECES, LANES, LANES), lambda b, j: (0, 0, 0))],
        out_specs=pl.BlockSpec((None, ts, LANES), lambda b, j: (b, j, 0)),
        out_shape=jax.ShapeDtypeStruct((B, S, LANES), BF16),
        scratch_shapes=[pltpu.VMEM((1, LANES), F32)],
        compiler_params=_cparams("parallel", "arbitrary"),
        name="fox_decay_cumsum",
    )(x3, w_hi, w_lo, b_pad, tril, jnp.asarray(place, BF16))


ONES_ROWS = 16


def _fox_attn_kernel(q_ref, k_ref, v_ref, pc_ref, o_ref, kaug_sc, vt_sc, m_sc, acc_sc, *, tq, nsub, seq, unroll):
    dh = FOX_HEAD_DIM
    h = pl.program_id(1)
    g = pl.program_id(2)

    @pl.when(g == 0)
    def _():
        kaug_sc[:, :dh] = k_ref[...]
        kaug_sc[:, dh:] = pc_ref[...]
        for c in range(seq // tq):
            rows = slice(c * tq, (c + 1) * tq)
            vt_sc[:dh, rows] = v_ref[rows, :].astype(F32).T.astype(BF16)
        vt_sc[dh:, :] = jnp.ones((ONES_ROWS, seq), BF16)

    feat = lax.broadcasted_iota(jnp.int32, (LANES, tq), 0)
    bias_rows = (feat >= FOX_BIAS_PIECES * h) & (feat < FOX_BIAS_PIECES * (h + 1))
    ones_h = jnp.where(bias_rows, 1.0, 0.0).astype(BF16)
    q_aug = [jnp.concatenate([q_ref[a * tq:(a + 1) * tq, :].astype(F32).T.astype(BF16), ones_h], axis=0)
             for a in range(nsub)]
    m_sc[...] = jnp.full_like(m_sc, NEG_BIG)
    acc_sc[...] = jnp.zeros_like(acc_sc)
    causal = (lax.broadcasted_iota(jnp.int32, (tq, tq), 0) <= lax.broadcasted_iota(jnp.int32, (tq, tq), 1))
    first = g * nsub

    def chain(a, tiles, diag_last):
        offs = [pl.multiple_of(j * tq, tq) for j in tiles]
        scores = []
        for off in offs:
            scores.append(_dot(kaug_sc[pl.ds(off, tq), :], q_aug[a]))
            yield None
        for n, (off, s) in enumerate(zip(offs, scores)):
            if diag_last and n == len(offs) - 1:
                s = jnp.where(causal, s, NEG_BIG)
            m_old = m_sc[a]
            m_new = jnp.maximum(m_old, jnp.max(s, axis=0, keepdims=True))
            alpha = jnp.exp2(m_old - m_new)
            pv = _dot(vt_sc[:, pl.ds(off, tq)], jnp.exp2(s - m_new).astype(BF16))
            yield None
            acc_sc[a] = alpha * acc_sc[a] + pv
            m_sc[a] = m_new
        yield None

    def body(jj, c):
        _round_robin([chain(a, [jj * unroll + u for u in range(unroll)], False) for a in range(nsub)])
        return c

    lax.fori_loop(0, first // unroll, body, 0)
    _round_robin([chain(a, [first + t for t in range(a + 1)], True) for a in range(nsub)])
    for a in range(nsub):
        acc = acc_sc[a]
        o_t = acc[:dh] / acc[dh:dh + 1]
        o_ref[a * tq:(a + 1) * tq, :] = o_t.T.astype(o_ref.dtype)


def _fox_attention(proj3, pieces, tq=256, nsub=8):
    B, S, _ = proj3.shape
    H, dh = FOX_HEADS, FOX_HEAD_DIM
    tq = _pick_tile(S, tq)
    nsub = _pick_tile(S // tq, nsub)
    tg = tq * nsub
    return pl.pallas_call(
        functools.partial(_fox_attn_kernel, tq=tq, nsub=nsub, seq=S, unroll=min(nsub, 4)),
        grid=(B, H, S // tg),
        in_specs=[pl.BlockSpec((None, tg, dh), lambda b, h, g: (b, g, h)),
                  pl.BlockSpec((None, S, dh), lambda b, h, g: (b, 0, H + h)),
                  pl.BlockSpec((None, S, dh), lambda b, h, g: (b, 0, 2 * H + h)),
                  pl.BlockSpec((None, S, LANES), lambda b, h, g: (b, 0, 0))],
        out_specs=pl.BlockSpec((None, tg, dh), lambda b, h, g: (b, g, h)),
        out_shape=jax.ShapeDtypeStruct((B, S, H * dh), BF16),
        scratch_shapes=[pltpu.VMEM((S, dh + LANES), BF16),
                        pltpu.VMEM((dh + ONES_ROWS, S), BF16),
                        pltpu.VMEM((nsub, 1, tq), F32),
                        pltpu.VMEM((nsub, dh + ONES_ROWS, tq), F32)],
        compiler_params=_cparams("parallel", "parallel", "arbitrary"),
        name="fox_attention",
    )(proj3, proj3, proj3, pieces)


def _fox_layer(x, xb, w_in, b_f, w_out, ln_g, ln_b, B, S):
    D = D_MODEL
    H, dh = FOX_HEADS, FOX_HEAD_DIM
    scale = dh ** -0.5 * LOG2E
    w_q, w_k, w_v, w_f, w_g = jnp.split(w_in, [H * dh, 2 * H * dh, 3 * H * dh, 3 * H * dh + H], axis=1)
    w_main = jnp.concatenate([w_q * scale, w_k, w_v, w_g], axis=1).astype(BF16)
    proj = _matmul(xb, w_main)
    pieces = _fox_cum(x.reshape(B, S, D), w_f, b_f)
    o = _fox_attention(proj.reshape(B, S, 4 * D), pieces)
    return _outproj_ln(proj, 3, o.reshape(B * S, D), x, w_out, ln_g, ln_b)


def _ret_kernel(q_ref, k_ref, v_ref, cos_ref, sin_ref, dm_ref, xi_ref, zeta_ref, gc_ref, gn_ref,
                o_ref, r_sc):
    @pl.when(pl.program_id(1) == 0)
    def _():
        r_sc[...] = jnp.zeros_like(r_sc)

    dk = RET_HEAD_DIM
    half = dk // 2
    cos = cos_ref[...]
    sin = sin_ref[...]

    def rope(x):
        x1, x2 = x[:, :half], x[:, half:]
        return jnp.concatenate([x1 * cos - x2 * sin, x2 * cos + x1 * sin], axis=-1)

    def head_chain(h):
        cols = slice(h * dk, (h + 1) * dk)
        q = rope(q_ref[:, cols].astype(F32))
        k = rope(k_ref[:, cols].astype(F32)) * (dk ** -0.5)
        v = v_ref[:, cols]
        qb = q.astype(BF16)
        r_old = r_sc[h]
        scores = _dot_nt(qb, k.astype(BF16))
        cross = _dot(qb, r_old.astype(BF16))
        kz = (k * zeta_ref[h]).astype(BF16)
        r_new = _dot_tn(kz, v)
        yield None
        o = _dot((scores * dm_ref[h]).astype(BF16), v)
        r_sc[h] = r_old * gc_ref[h] + r_new
        yield None
        o = o + cross * xi_ref[h]
        o = o * lax.rsqrt(jnp.mean(o * o, axis=-1, keepdims=True) + RMS_EPS) * gn_ref[:, cols]
        o_ref[:, cols] = o.astype(o_ref.dtype)
        yield None

    _round_robin([head_chain(h) for h in range(RET_HEADS)])


def _ret_layer(x, xb, w_in, gn_g, w_out, ln_g, ln_b, B, S, chunk=512):
    D = D_MODEL
    H, dk = RET_HEADS, RET_HEAD_DIM
    C = _pick_tile(S, chunk)
    proj = _matmul(xb, w_in.astype(BF16))
    f32 = np.float32
    inv = (f32(1.0) / (f32(RET_THETA) ** (np.arange(0, dk, 2, dtype=f32) / f32(dk)))).astype(f32)
    ang = (np.arange(S, dtype=f32)[:, None] * inv[None, :]).astype(f32)
    cos, sin = jnp.asarray(np.cos(ang)), jnp.asarray(np.sin(ang))
    log_g = np.log1p(-(f32(2.0) ** (f32(-5.0) - np.arange(H, dtype=f32)))).astype(f32)
    pos = np.arange(C, dtype=f32)
    diff = pos[:, None] - pos[None, :]
    d_mask = jnp.asarray(np.where(diff[None] >= 0, np.exp(np.maximum(diff, 0.0)[None] * log_g[:, None, None]),
                                  0.0).astype(f32))
    xi = jnp.asarray(np.broadcast_to(np.exp((pos[None, :] + 1.0) * log_g[:, None])[:, :, None],
                                     (H, C, dk)).astype(f32))
    zeta = jnp.asarray(np.broadcast_to(np.exp((C - 1.0 - pos[None, :]) * log_g[:, None])[:, :, None],
                                       (H, C, dk)).astype(f32))
    g_c = jnp.asarray(np.broadcast_to(np.exp(f32(C) * log_g)[:, None, None], (H, 1, dk)).astype(f32))
    p3 = proj.reshape(B, S, 4 * D)
    o = pl.pallas_call(
        _ret_kernel,
        grid=(B, S // C),
        in_specs=[pl.BlockSpec((None, C, D), lambda b, c: (b, c, 0)),
                  pl.BlockSpec((None, C, D), lambda b, c: (b, c, 1)),
                  pl.BlockSpec((None, C, D), lambda b, c: (b, c, 2)),
                  pl.BlockSpec((C, dk // 2), lambda b, c: (c, 0)),
                  pl.BlockSpec((C, dk // 2), lambda b, c: (c, 0)),
                  pl.BlockSpec((H, C, C), lambda b, c: (0, 0, 0)),
                  pl.BlockSpec((H, C, dk), lambda b, c: (0, 0, 0)),
                  pl.BlockSpec((H, C, dk), lambda b, c: (0, 0, 0)),
                  pl.BlockSpec((H, 1, dk), lambda b, c: (0, 0, 0)),
                  pl.BlockSpec((1, D), lambda b, c: (0, 0))],
        out_specs=pl.BlockSpec((None, C, D), lambda b, c: (b, c, 0)),
        out_shape=jax.ShapeDtypeStruct((B, S, D), BF16),
        scratch_shapes=[pltpu.VMEM((H, dk, dk), F32)],
        compiler_params=_cparams("parallel", "arbitrary"),
        name="retnet_retention",
    )(p3, p3, p3, cos, sin, d_mask, xi, zeta, g_c, gn_g.reshape(1, D))
    return _outproj_ln(proj, 3, o.reshape(B * S, D), x, w_out, ln_g, ln_b, bf16_copy=False)


DSA_HEADS = 8
DSA_HEAD_DIM = 128
DSA_ROPE_DIM = 32
DSA_KV_RANK = 128
IDX_HEADS = 8
IDX_DIM = 64
IDX_ROPE_DIM = 16
TOPK_MAX = 256
HALF_MIN = -2 ** 15
HALF_ROWS = 16
DSA_VMEM_LIMIT = 56 * 1024 * 1024
DSA_ONES_ROWS = 16


def _rope_perm(width, groups):
    p = np.zeros((width, width), np.float32)
    for start, half in groups:
        for j in range(half):
            p[start + half + j, start + j] = 1.0
            p[start + j, start + half + j] = 1.0
    return p


def _rope_cs(S, width, groups, theta_dims):
    f32 = np.float32
    c = np.ones((S, width), f32)
    sg = np.zeros((S, width), f32)
    pos = np.arange(S, dtype=f32)[:, None]
    for (start, half), rot_dim in zip(groups, theta_dims):
        inv = (f32(1.0) / (f32(ROPE_THETA) ** (np.arange(0, rot_dim, 2, dtype=f32) / f32(rot_dim)))).astype(f32)
        ang = (pos * inv[None, :]).astype(f32)
        cos, sin = np.cos(ang), np.sin(ang)
        c[:, start:start + half] = cos
        c[:, start + half:start + 2 * half] = cos
        sg[:, start:start + half] = -sin
        sg[:, start + half:start + 2 * half] = sin
    return jnp.asarray(c), jnp.asarray(sg)


def _dsa_prep_kernel(q_ref, qi_ref, ckv_ref, misc_ref, cq_ref, sq_ref, ci_ref, si_ref, cm_ref, sm_ref,
                     pq_ref, pi_ref, pm_ref, selk_ref, selw_ref, wuk_ref, kvg_ref,
                     qf_ref, kvl_ref, kvt_ref, qir_ref, kid_ref, wit_ref):
    H, dh = DSA_HEADS, DSA_HEAD_DIM
    lane = lax.broadcasted_iota(jnp.int32, (1, LANES), 1)
    rope_lanes = lane < DSA_ROPE_DIM
    cq, sq = cq_ref[...], sq_ref[...]
    scale = dh ** -0.5 * LOG2E
    for h in range(H):
        qh = q_ref[:, h * dh:(h + 1) * dh]
        qr = qh.astype(F32) * cq + _dot(qh, pq_ref[...]) * sq
        q_lat = _dot(qr.astype(BF16), wuk_ref[h])
        qf_ref[h, :dh, :] = (q_lat * scale).T.astype(BF16)
        qf_ref[h, dh:, :] = jnp.where(rope_lanes, qr * scale, 0.0).T.astype(BF16)
    ci, si = ci_ref[...], si_ref[...]
    for g in range(IDX_HEADS * IDX_DIM // LANES):
        qg = qi_ref[:, g * LANES:(g + 1) * LANES]
        qr = qg.astype(F32) * ci + _dot(qg, pi_ref[...]) * si
        qir_ref[g * LANES:(g + 1) * LANES, :] = (qr * (IDX_DIM ** -0.5)).T.astype(BF16)
    ckv = ckv_ref[...].astype(F32)
    ckv = ckv * lax.rsqrt(jnp.mean(ckv * ckv, axis=-1, keepdims=True) + RMS_EPS) * kvg_ref[...]
    misc = misc_ref[...]
    mr = (misc.astype(F32) * cm_ref[...] + _dot(misc, pm_ref[...]) * sm_ref[...])
    kvl_ref[:, :DSA_KV_RANK] = ckv.astype(BF16)
    kvl_ref[:, DSA_KV_RANK:] = jnp.where(rope_lanes, mr, 0.0).astype(BF16)
    kvt_ref[:DSA_KV_RANK, :] = ckv.T.astype(BF16)
    kvt_ref[DSA_KV_RANK:, :] = jnp.ones((DSA_ONES_ROWS, ckv.shape[0]), BF16)
    kid_ref[...] = _dot(mr.astype(BF16), selk_ref[...]).astype(BF16)
    wi = _dot(misc, selw_ref[...]) * (IDX_HEADS ** -0.5)
    wit_ref[...] = wi.T[:IDX_HEADS, :]


def _sort_key(x):
    b = pltpu.bitcast(x, jnp.int32)
    return jnp.where(b < 0, b ^ 0x7FFFFFFF, b)


def _dsa_main_kernel(qf_ref, kvl_ref, kvt_ref, qi_ref, kid_ref, wit_ref, wuv_ref, obuf_ref, o_ref,
                     key_sc, hi_sc, lo_sc, low_sc, m_sc, acc_sc, *, tq, tk, k_sel, seq, qtile):
    del obuf_ref
    H = DSA_HEADS
    q0 = qtile * tq
    nj = (q0 + tq + tk - 1) // tk
    qpos = q0 + lax.broadcasted_iota(jnp.int32, (tk, tq), 1)
    kpos0 = lax.broadcasted_iota(jnp.int32, (tk, tq), 0)
    feat = lax.broadcasted_iota(jnp.int32, (LANES, 1), 0)
    neg_inf_key = _sort_key(jnp.full((1, 1), -jnp.inf, F32))

    wit = wit_ref[...]
    qi_heads = []
    for h in range(IDX_HEADS):
        g = qi_ref[(h // 2) * LANES:(h // 2 + 1) * LANES, :]
        keep = (feat >= IDX_DIM) if (h % 2) else (feat < IDX_DIM)
        qi_heads.append(jnp.where(keep, g, jnp.zeros_like(g)))

    def score_tile(j, c, causal):
        off = pl.multiple_of(j * tk, tk)
        ki = kid_ref[pl.ds(off, tk), :]
        scores = [_dot(ki, qi_heads[h]) for h in range(IDX_HEADS)]
        isc = jnp.zeros((tk, tq), F32)
        for h in range(IDX_HEADS):
            isc = isc + jnp.maximum(scores[h], 0.0) * wit[h:h + 1, :]
        isc = isc + 0.0
        if causal:
            isc = jnp.where(kpos0 + off <= qpos, isc, -jnp.inf)
        key = _sort_key(isc)
        key_sc[pl.ds(off, tk), :] = key
        hi_sc[pl.ds(off, tk), :] = (key >> 16).astype(jnp.int16)
        lo_sc[pl.ds(off, tk), :] = ((key & 0xFFFF) + HALF_MIN).astype(jnp.int16)
        return c

    n_clear = min(q0 // tk, nj)
    lax.fori_loop(0, n_clear, functools.partial(score_tile, causal=False), 0)
    lax.fori_loop(n_clear, nj, functools.partial(score_tile, causal=True), 0)

    def count(pred_fn):
        acc = jnp.zeros((8, tq), jnp.int32)
        for j in range(nj):
            hit = jnp.where(pred_fn(key_sc[j * tk:(j + 1) * tk, :], j * tk), 1, 0)
            acc = acc + jnp.sum(hit.reshape(tk // 8, 8, tq), axis=0)
        return jnp.sum(acc, axis=0, keepdims=True)

    rows16 = tk // HALF_ROWS

    def count16(ref, pred_fn):
        accs = [jnp.zeros((HALF_ROWS, tq), jnp.int16) for _ in range(2)]
        for j in range(nj):
            hit = jnp.where(pred_fn(ref[j * tk:(j + 1) * tk, :].reshape(rows16, HALF_ROWS, tq)),
                            jnp.int16(1), jnp.int16(0))
            for r in range(rows16):
                accs[r % 2] = accs[r % 2] + hit[r]
        return jnp.sum((accs[0] + accs[1]).astype(jnp.int32), axis=0, keepdims=True)

    def as_half(v):
        return jnp.broadcast_to(v.astype(jnp.int16), (HALF_ROWS, tq))[None]

    def search16(ref, base0, cnt0, rej0, bits, want):
        def bit_step(t, carry):
            base, cnt_b, cnt_r = carry
            cand = base + lax.shift_left(jnp.int32(1), bits - 1 - t)
            cand16 = as_half(cand)
            c = count16(ref, lambda kt: kt >= cand16)
            ok = c >= want
            return jnp.where(ok, cand, base), jnp.where(ok, c, cnt_b), jnp.where(ok, cnt_r, c)
        return lax.fori_loop(0, bits, bit_step, (base0, cnt0, rej0))

    zero16 = as_half(jnp.zeros((1, tq), jnp.int32))
    cnt_pos = count16(hi_sc, lambda kt: kt >= zero16)
    nonneg = cnt_pos >= k_sel
    t1, ge_hi, above = search16(hi_sc, jnp.where(nonneg, 0, HALF_MIN), jnp.where(nonneg, cnt_pos, nj * tk),
                                jnp.where(nonneg, 0, cnt_pos), 15, k_sel)
    t1_16 = as_half(t1)

    for j in range(nj):
        rows = slice(j * tk, (j + 1) * tk)
        hi = hi_sc[rows, :].reshape(rows16, HALF_ROWS, tq)
        lo = lo_sc[rows, :].reshape(rows16, HALF_ROWS, tq)
        low_sc[rows, :] = jnp.where(hi == t1_16, lo, jnp.int16(HALF_MIN)).reshape(tk, tq)
    t2, ge_low, gt_low = search16(low_sc, jnp.full((1, tq), HALF_MIN, jnp.int32), ge_hi - above,
                                  jnp.zeros((1, tq), jnp.int32), 16, k_sel - above)
    thr = lax.shift_left(t1, 16) | (t2 - HALF_MIN)
    n_ge = above + ge_low
    n_gt = above + gt_low
    need = k_sel - n_gt
    excess = ((n_ge - n_gt) > need) & (thr > neg_inf_key)
    any_excess = jnp.max(jnp.where(excess, 1, 0)) > 0

    def tie_cut():
        def step(t, lo):
            cand = lo + lax.shift_left(jnp.int32(1), int(math.log2(seq)) - t)
            c = count(lambda kt, off: (kt == thr) & (kpos0 + off < cand))
            return jnp.where(c < need, cand, lo)
        lo = lax.fori_loop(0, int(math.log2(seq)) + 1, step, jnp.zeros((1, tq), jnp.int32))
        return jnp.where(excess, lo, seq)

    cut = lax.cond(any_excess, tie_cut, lambda: jnp.full((1, tq), seq, jnp.int32))

    m_sc[...] = jnp.full_like(m_sc, NEG_BIG)
    acc_sc[...] = jnp.zeros_like(acc_sc)

    def attn_tile(j, c):
        off = pl.multiple_of(j * tk, tk)
        kt = key_sc[pl.ds(off, tk), :]
        kpos = kpos0 + off
        bias = jnp.where(kt > thr, 0.0, jnp.where(kt == thr, jnp.where(kpos <= cut, 0.0, NEG_BIG), NEG_BIG))
        bias = jnp.where(kpos <= qpos, bias, NEG_BIG)
        kv = kvl_ref[pl.ds(off, tk), :]
        kvt = kvt_ref[:, pl.ds(off, tk)]

        def head_step(h):
            s = _dot(kv, qf_ref[h]) + bias
            yield None
            m_old = m_sc[h]
            m_new = jnp.maximum(m_old, jnp.max(s, axis=0, keepdims=True))
            alpha = jnp.exp2(m_old - m_new)
            pv = _dot(kvt, jnp.exp2(s - m_new).astype(BF16))
            yield None
            acc_sc[h] = alpha * acc_sc[h] + pv
            m_sc[h] = m_new
            yield None

        _round_robin([head_step(h) for h in range(H)])
        return c

    lax.fori_loop(0, nj, attn_tile, 0)
    for h in range(H):
        acc = acc_sc[h]
        o_lat_t = (acc[:DSA_KV_RANK] / acc[DSA_KV_RANK:DSA_KV_RANK + 1]).astype(BF16)
        o_ref[:, h * DSA_HEAD_DIM:(h + 1) * DSA_HEAD_DIM] = _dot_tn(o_lat_t, wuv_ref[h]).astype(o_ref.dtype)


def _dsa_layer(x, xb, w_in, kv_norm_g, w_uk, w_uv, w_out, ln_g, ln_b, B, S, tq=512, tk=512):
    D = D_MODEL
    H, dh, dr, dc = DSA_HEADS, DSA_HEAD_DIM, DSA_ROPE_DIM, DSA_KV_RANK
    HI, di = IDX_HEADS, IDX_DIM
    w_q, w_ckv, w_kr, w_qi, w_ki, w_wi, w_g = jnp.split(
        w_in, np.cumsum([H * dh, dc, dr, HI * di, di, HI]).tolist(), axis=1)
    w_misc = jnp.concatenate([w_kr, w_ki, w_wi, jnp.zeros((D, LANES - dr - di - HI), F32)], axis=1)
    w_main = jnp.concatenate([w_q, w_g, w_qi, w_ckv, w_misc], axis=1).astype(BF16)
    n_main = w_main.shape[1]
    proj = _matmul(xb, w_main, tm=1024, tn=n_main)
    c_q, c_qi, c_ckv, c_misc = 0, 2 * D // LANES, (2 * D + HI * di) // LANES, (2 * D + HI * di + dc) // LANES

    q_groups = [(0, dr // 2)]
    i_groups = [(0, IDX_ROPE_DIM // 2), (di, IDX_ROPE_DIM // 2)]
    m_groups = [(0, dr // 2), (dr, IDX_ROPE_DIM // 2)]
    cq, sq = _rope_cs(S, LANES, q_groups, [dr])
    ci, si = _rope_cs(S, LANES, i_groups, [IDX_ROPE_DIM, IDX_ROPE_DIM])
    cm, sm = _rope_cs(S, LANES, m_groups, [dr, IDX_ROPE_DIM])
    pq = jnp.asarray(_rope_perm(LANES, q_groups), BF16)
    pi = jnp.asarray(_rope_perm(LANES, i_groups), BF16)
    pm = jnp.asarray(_rope_perm(LANES, m_groups), BF16)
    selk = np.zeros((LANES, LANES), np.float32)
    for j in range(di):
        selk[dr + j, j] = 1.0
        selk[dr + j, di + j] = 1.0
    selw = np.zeros((LANES, LANES), np.float32)
    for j in range(HI):
        selw[dr + di + j, j] = 1.0
    wuk = jnp.concatenate([jnp.zeros((H, dr, dc), F32), jnp.transpose(w_uk, (0, 2, 1))], axis=1).astype(BF16)

    ts = _pick_tile(S, 512)
    p3 = proj.reshape(B, S, n_main)
    tab = lambda: pl.BlockSpec((ts, LANES), lambda b, j: (j, 0))
    mat = lambda: pl.BlockSpec((LANES, LANES), lambda b, j: (0, 0))
    qf, kvl, kvt, qir, kid, wit = pl.pallas_call(
        _dsa_prep_kernel,
        grid=(B, S // ts),
        in_specs=[pl.BlockSpec((None, ts, H * dh), lambda b, j: (b, j, 0)),
                  pl.BlockSpec((None, ts, HI * di), lambda b, j: (b, j, c_qi * LANES // (HI * di))),
                  pl.BlockSpec((None, ts, dc), lambda b, j: (b, j, c_ckv)),
                  pl.BlockSpec((None, ts, LANES), lambda b, j: (b, j, c_misc)),
                  tab(), tab(), tab(), tab(), tab(), tab(),
                  mat(), mat(), mat(), mat(), mat(),
                  pl.BlockSpec((H, LANES, dc), lambda b, j: (0, 0, 0)),
                  pl.BlockSpec((1, dc), lambda b, j: (0, 0))],
        out_specs=[pl.BlockSpec((None, H, 2 * dc, ts), lambda b, j: (b, 0, 0, j)),
                   pl.BlockSpec((None, ts, 2 * dc), lambda b, j: (b, j, 0)),
                   pl.BlockSpec((None, dc + DSA_ONES_ROWS, ts), lambda b, j: (b, 0, j)),
                   pl.BlockSpec((None, HI * di, ts), lambda b, j: (b, 0, j)),
                   pl.BlockSpec((None, ts, LANES), lambda b, j: (b, j, 0)),
                   pl.BlockSpec((None, HI, ts), lambda b, j: (b, 0, j))],
        out_shape=[jax.ShapeDtypeStruct((B, H, 2 * dc, S), BF16),
                   jax.ShapeDtypeStruct((B, S, 2 * dc), BF16),
                   jax.ShapeDtypeStruct((B, dc + DSA_ONES_ROWS, S), BF16),
                   jax.ShapeDtypeStruct((B, HI * di, S), BF16),
                   jax.ShapeDtypeStruct((B, S, LANES), BF16),
                   jax.ShapeDtypeStruct((B, HI, S), F32)],
        compiler_params=_cparams("parallel", "parallel"),
        name="dsa_prep",
    )(p3, p3, p3, p3, cq, sq, ci, si, cm, sm, pq, pi, pm,
      jnp.asarray(selk, BF16), jnp.asarray(selw, BF16), wuk, kv_norm_g.reshape(1, dc))

    tq = _pick_tile(S, tq)
    tk = _pick_tile(S, tk)
    k_sel = min(TOPK_MAX, S // 4)
    wuv = w_uv.astype(BF16)
    o = jnp.zeros((B, S, H * dh), BF16)
    for i in range(S // tq):
        nkeys = -(-((i + 1) * tq) // tk) * tk
        o = pl.pallas_call(
            functools.partial(_dsa_main_kernel, tq=tq, tk=tk, k_sel=k_sel, seq=S, qtile=i),
            grid=(B,),
            in_specs=[pl.BlockSpec((None, H, 2 * dc, tq), lambda b, i=i: (b, 0, 0, i)),
                      pl.BlockSpec((None, nkeys, 2 * dc), lambda b: (b, 0, 0)),
                      pl.BlockSpec((None, dc + DSA_ONES_ROWS, nkeys), lambda b: (b, 0, 0)),
                      pl.BlockSpec((None, HI * di, tq), lambda b, i=i: (b, 0, i)),
                      pl.BlockSpec((None, nkeys, LANES), lambda b: (b, 0, 0)),
                      pl.BlockSpec((None, HI, tq), lambda b, i=i: (b, 0, i)),
                      pl.BlockSpec((H, dc, dh), lambda b: (0, 0, 0)),
                      pl.BlockSpec(memory_space=pl.ANY)],
            out_specs=pl.BlockSpec((None, tq, H * dh), lambda b, i=i: (b, i, 0)),
            out_shape=jax.ShapeDtypeStruct((B, S, H * dh), BF16),
            input_output_aliases={7: 0},
            scratch_shapes=[pltpu.VMEM((nkeys, tq), jnp.int32),
                            pltpu.VMEM((nkeys, tq), jnp.int16), pltpu.VMEM((nkeys, tq), jnp.int16),
                            pltpu.VMEM((nkeys, tq), jnp.int16),
                            pltpu.VMEM((H, 1, tq), F32),
                            pltpu.VMEM((H, dc + DSA_ONES_ROWS, tq), F32)],
            compiler_params=_cparams("parallel", vmem_limit=DSA_VMEM_LIMIT),
            name=f"dsa_select_attention_q{i}",
        )(qf, kvl, kvt, qir, kid, wit, wuv, o)
    return _outproj_ln(proj, 1, o.reshape(B * S, D), x, w_out, ln_g, ln_b, bf16_copy=False)


RWKV_HEADS = 16
RWKV_HEAD_DIM = 64
RWKV_GN_EPS = 64e-5
RWKV_CHUNK = 64
RWKV_SUB = 16


def _group_sum(x, gmat, split=True):
    outs = []
    for c in range(x.shape[1] // LANES):
        xc = x[:, c * LANES:(c + 1) * LANES]
        if split:
            hi, lo = _split2(xc)
            outs.append(_dot(hi, gmat) + _dot(lo, gmat))
        else:
            outs.append(_dot(xc.astype(BF16), gmat))
    return outs[0] if len(outs) == 1 else jnp.concatenate(outs, axis=1)


def _softplus(y):
    return jnp.maximum(y, 0.0) + jnp.log(1.0 + jnp.exp(-jnp.abs(y)))


def _rwkv_proj_kernel(x_ref, xprev_ref, mu_ref, wr_ref, wk_ref, wv_ref, wg_ref, wla_ref, wlb_ref,
                      ala_ref, alb_ref, w0_ref, a0_ref, kk_ref, ka_ref, rk_ref, gmat_ref, tril_ref,
                      r_ref, k_ref, v_ref, g_ref, kap_ref, b_ref, cum_ref, bonus_ref):
    x = x_ref[...]
    ts = x.shape[0]
    prev = jnp.where(pl.program_id(1) == 0, 0.0, xprev_ref[7:8, :])
    rowid = lax.broadcasted_iota(jnp.int32, (ts, 1), 0)
    xx = jnp.where(rowid == 0, prev, pltpu.roll(x, 1, 0)) - x

    def mixed(i):
        return (x + xx * mu_ref[i:i + 1, :]).astype(BF16)

    r = _dot(mixed(0), wr_ref[...])
    k = _dot(mixed(2), wk_ref[...])
    v = _dot(mixed(3), wv_ref[...])
    g_ref[...] = _dot(mixed(5), wg_ref[...]).astype(g_ref.dtype)
    lora_w = _dot(jnp.tanh(_dot(mixed(1), wla_ref[...])).astype(BF16), wlb_ref[...])
    lora_a = _dot(_dot(mixed(4), ala_ref[...]).astype(BF16), alb_ref[...])
    w_log = -_softplus(-(w0_ref[...] + lora_w)) - 0.5
    lw = -jnp.exp(w_log)
    tril = tril_ref[...]
    for c in range(ts // RWKV_CHUNK):
        rows = slice(c * RWKV_CHUNK, (c + 1) * RWKV_CHUNK)
        cum_ref[rows, :] = sum(_dot(tril, piece) for piece in _split2(lw[rows, :]))
    a = _sigmoid(a0_ref[...] + lora_a)
    gmat = gmat_ref[...]
    kk = k * kk_ref[...]
    kap = kk * lax.rsqrt(_group_sum(kk * kk, gmat, split=False) + 1e-12)
    k2 = k * (1.0 + (a - 1.0) * ka_ref[...])
    bonus_ref[...] = _group_sum(r * k2 * rk_ref[...], gmat, split=False) * v
    r_ref[...] = r.astype(r_ref.dtype)
    k_ref[...] = k2.astype(k_ref.dtype)
    v_ref[...] = v.astype(v_ref.dtype)
    kap_ref[...] = kap.astype(kap_ref.dtype)
    b_ref[...] = (kap * a).astype(b_ref.dtype)


def _bd(x, left):
    z = jnp.zeros_like(x)
    return jnp.concatenate([jnp.where(left, x, z), jnp.where(left, z, x)], axis=0)


def _unbd(x_bd):
    c = x_bd.shape[0] // 2
    return x_bd[:c] + x_bd[c:]


def _rwkv_chunk_pair(L, r, k, v, kap, b, masks):
    C = RWKV_CHUNK
    left, strict, lower, same_sub, eye, first_row = masks
    Lc = L[C - 1:C, :]
    L_excl = jnp.where(first_row, 0.0, pltpu.roll(L, 1, 0))
    e_l, e_lx, e_nl, e_r = jnp.exp(L), jnp.exp(L_excl), jnp.exp(-L), jnp.exp(Lc - L)
    at = _bd(-kap * e_lx, left).astype(BF16)
    rt = _bd(r * e_l, left)
    bt = _bd(b * e_nl, left).astype(BF16)
    kt = _bd(k * e_nl, left).astype(BF16)
    bh = _bd(b * e_r, left).astype(BF16)
    kh = _bd(k * e_r, left).astype(BF16)
    vb = _bd(v, left).astype(BF16)

    a1 = _dot_nt(jnp.concatenate([at, rt.astype(BF16)], axis=0), jnp.concatenate([bt, kt], axis=0))
    yield None
    n = jnp.where(strict, a1[:2 * C, :2 * C], 0.0)
    ak = jnp.where(strict, a1[:2 * C, 2 * C:], 0.0).astype(BF16)
    rb = jnp.where(lower, a1[2 * C:, :2 * C], 0.0).astype(BF16)
    rk = jnp.where(lower, a1[2 * C:, 2 * C:], 0.0).astype(BF16)

    W = 2 * C
    nd = jnp.where(same_sub, n, 0.0)
    no = (n - nd).astype(BF16)
    ndb = nd.astype(BF16)
    n2 = _dot(ndb, ndb)
    akv = _dot(ak, vb)
    yield None
    n2b = n2.astype(BF16)
    t0 = eye + nd
    r = _dot(n2b, jnp.concatenate([n2b, t0.astype(BF16)], axis=1))
    yield None
    n4b = r[:, :W].astype(BF16)
    t01 = t0 + r[:, W:]
    r = _dot(n4b, jnp.concatenate([n4b, t01.astype(BF16)], axis=1))
    yield None
    u = t01 + r[:, W:]
    td = u + _dot(r[:, :W].astype(BF16), u.astype(BF16))
    yield None
    tdb = td.astype(BF16)
    x1 = _dot(tdb, no)
    yield None
    x1b = x1.astype(BF16)
    r = _dot(x1b, jnp.concatenate([x1b, tdb], axis=1))
    yield None
    w = td + r[:, W:]
    t = (w + _dot(r[:, :W].astype(BF16), w.astype(BF16))).astype(BF16)
    yield None
    pq = _dot(t, jnp.concatenate([at, akv.astype(BF16)], axis=1)).astype(BF16)
    yield None
    z = jnp.concatenate([pq, jnp.concatenate([jnp.zeros_like(vb), vb], axis=1)], axis=0)
    ry = _dot(jnp.concatenate([rb, rk], axis=1), z)
    mg = _dot_tn(z, jnp.concatenate([bh, kh], axis=0))
    yield None
    rp = rt + ry[:, :2 * C]
    yl = ry[:, 2 * C:]
    yield _unbd(rp), _unbd(yl), _unbd(mg[:2 * C]), _unbd(mg[2 * C:]), jnp.exp(Lc)


def _rwkv_chunk_kernel(cum_ref, r_ref, k_ref, v_ref, kap_ref, b_ref,
                       rp_ref, yl_ref, mm_ref, gg_ref, gam_ref, *, pairs, cpb):
    C = RWKV_CHUNK
    lane = lax.broadcasted_iota(jnp.int32, (1, LANES), 1)
    left = lane < RWKV_HEAD_DIM
    ri = lax.broadcasted_iota(jnp.int32, (2 * C, 2 * C), 0)
    ci = lax.broadcasted_iota(jnp.int32, (2 * C, 2 * C), 1)
    same_head = (ri // C) == (ci // C)
    strict = same_head & ((ri % C) > (ci % C))
    lower = same_head & ((ri % C) >= (ci % C))
    same_sub = (ri // RWKV_SUB) == (ci // RWKV_SUB)
    eye = jnp.where(ri == ci, 1.0, 0.0).astype(F32)
    first_row = lax.broadcasted_iota(jnp.int32, (C, 1), 0) == 0
    masks = (left, strict, lower, same_sub, eye, first_row)
    jobs = [(c, slice(c * C, (c + 1) * C), slice(p * LANES, (p + 1) * LANES))
            for c in range(cpb) for p in range(pairs)]
    results = _round_robin([
        _rwkv_chunk_pair(cum_ref[rows, sl], r_ref[rows, sl].astype(F32), k_ref[rows, sl].astype(F32),
                         v_ref[rows, sl].astype(F32), kap_ref[rows, sl].astype(F32),
                         b_ref[rows, sl].astype(F32), masks)
        for _, rows, sl in jobs])
    for (c, rows, sl), (rp, yl, mm, gg, gam) in zip(jobs, results):
        rp_ref[rows, sl] = rp.astype(rp_ref.dtype)
        yl_ref[rows, sl] = yl
        mm_ref[c, :, sl] = mm.astype(mm_ref.dtype)
        gg_ref[c, :, sl] = gg
        gam_ref[c, :, sl] = gam


def _rwkv_seq_kernel(rp_ref, yl_ref, mm_ref, gg_ref, gam_ref, bonus_ref, gmat_ref, gng_ref, gnb_ref,
                     o_ref, s_sc, y_sc, *, pairs, cb):
    C = RWKV_CHUNK
    lane = lax.broadcasted_iota(jnp.int32, (1, LANES), 1)
    left = lane < RWKV_HEAD_DIM

    @pl.when(pl.program_id(2) == 0)
    def _():
        s_sc[...] = jnp.zeros_like(s_sc)

    states = [s_sc[p] for p in range(pairs)]
    for c in range(cb):
        rows = slice(c * C, (c + 1) * C)
        for p in range(pairs):
            sl = slice(p * LANES, (p + 1) * LANES)
            s = states[p]
            sb = s.astype(BF16)
            y_sc[rows, sl] = _dot_nt(rp_ref[rows, sl], sb) + yl_ref[rows, sl]
            mm = _bd(mm_ref[c, :, sl], left)
            gg = _bd(gg_ref[c, :, sl], left)
            states[p] = s * gam_ref[c, :, sl] + _dot(sb, mm) + gg
    for p in range(pairs):
        s_sc[p] = states[p]

    y = y_sc[...]
    gmat = gmat_ref[...]
    inv_n = 1.0 / RWKV_HEAD_DIM
    yc = y - _group_sum(y, gmat) * inv_n
    var = _group_sum(yc * yc, gmat) * inv_n
    yn = yc * lax.rsqrt(var + RWKV_GN_EPS) * gng_ref[...] + gnb_ref[...]
    o_ref[...] = (yn + bonus_ref[...]).astype(o_ref.dtype)


def _rwkv_layer(x, mu, w_in, w0, w_lora_a, w_lora_b, a0, a_lora_a, a_lora_b, k_k, k_a, r_k,
                gn_g, gn_b, w_out, ln_g, ln_b, B, S, ts=512, pairs=8, seq_pairs=8, chunks_per_step=2):
    D = D_MODEL
    C = RWKV_CHUNK
    nc = S // C
    ts = _pick_tile(S, ts)
    tril = jnp.asarray(np.tril(np.ones((C, C), np.float32)), BF16)
    w_r, w_k, w_v, w_g =[w.astype(BF16) for w in jnp.split(w_in, 4, axis=1)]
    gmat = jnp.asarray(np.kron(np.eye(2, dtype=np.float32), np.ones((RWKV_HEAD_DIM, RWKV_HEAD_DIM), np.float32)), BF16)
    row = lambda a: a.reshape(1, D)
    x3 = x.reshape(B, S, D)
    full = lambda shape: pl.BlockSpec(shape, lambda b, j: (0,) * len(shape))
    tile = lambda: pl.BlockSpec((None, ts, D), lambda b, j: (b, j, 0))
    lr = w_lora_a.shape[1]
    outs = pl.pallas_call(
        _rwkv_proj_kernel,
        grid=(B, S // ts),
        in_specs=[tile(),
                  pl.BlockSpec((None, 8, D), lambda b, j: (b, jnp.maximum(j * (ts // 8) - 1, 0), 0)),
                  full((6, D)), full((D, D)), full((D, D)), full((D, D)), full((D, D)),
                  full((D, lr)), full((lr, D)), full((D, lr)), full((lr, D)),
                  full((1, D)), full((1, D)), full((1, D)), full((1, D)), full((1, D)),
                  full((LANES, LANES)), full((C, C))],
        out_specs=[tile() for _ in range(8)],
        out_shape=[jax.ShapeDtypeStruct((B, S, D), dt) for dt in (BF16, BF16, BF16, BF16, BF16, BF16, F32, F32)],
        compiler_params=_cparams("parallel", "arbitrary"),
        name="rwkv_projections",
    )(x3, x3, mu, w_r, w_k, w_v, w_g, w_lora_a.astype(BF16), w_lora_b.astype(BF16),
      a_lora_a.astype(BF16), a_lora_b.astype(BF16), row(w0), row(a0), row(k_k), row(k_a), row(r_k), gmat, tril)
    r, k2, v, g, kap, bvec, cum, bonus = outs

    pw = pairs * LANES
    cpb = _pick_tile(nc, chunks_per_step)
    cblk = lambda: pl.BlockSpec((None, cpb * C, pw), lambda b, c, q: (b, c, q))
    sblk = lambda: pl.BlockSpec((None, cpb, C, pw), lambda b, c, q: (b, c, 0, q))
    rp, yl, mm, gg, gam = pl.pallas_call(
        functools.partial(_rwkv_chunk_kernel, pairs=pairs, cpb=cpb),
        grid=(B, nc // cpb, D // pw),
        in_specs=[cblk() for _ in range(6)],
        out_specs=[cblk(), cblk(), sblk(), sblk(),
                   pl.BlockSpec((None, cpb, 1, pw), lambda b, c, q: (b, c, 0, q))],
        out_shape=[jax.ShapeDtypeStruct((B, S, D), BF16), jax.ShapeDtypeStruct((B, S, D), F32),
                   jax.ShapeDtypeStruct((B, nc, C, D), BF16), jax.ShapeDtypeStruct((B, nc, C, D), F32),
                   jax.ShapeDtypeStruct((B, nc, 1, D), F32)],
        compiler_params=_cparams("parallel", "parallel", "parallel"),
        name="rwkv_chunk_summaries",
    )(cum, r, k2, v, kap, bvec)

    cb = _pick_tile(nc, 8)
    pairs = seq_pairs
    pw = pairs * LANES
    o = pl.pallas_call(
        functools.partial(_rwkv_seq_kernel, pairs=pairs, cb=cb),
        grid=(B, D // pw, nc // cb),
        in_specs=[pl.BlockSpec((None, cb * C, pw), lambda b, q, j: (b, j, q)),
                  pl.BlockSpec((None, cb * C, pw), lambda b, q, j: (b, j, q)),
                  pl.BlockSpec((None, cb, C, pw), lambda b, q, j: (b, j, 0, q)),
                  pl.BlockSpec((None, cb, C, pw), lambda b, q, j: (b, j, 0, q)),
                  pl.BlockSpec((None, cb, 1, pw), lambda b, q, j: (b, j, 0, q)),
                  pl.BlockSpec((None, cb * C, pw), lambda b, q, j: (b, j, q)),
                  pl.BlockSpec((LANES, LANES), lambda b, q, j: (0, 0)),
                  pl.BlockSpec((1, pw), lambda b, q, j: (0, q)),
                  pl.BlockSpec((1, pw), lambda b, q, j: (0, q))],
        out_specs=pl.BlockSpec((None, cb * C, pw), lambda b, q, j: (b, j, q)),
        out_shape=jax.ShapeDtypeStruct((B, S, D), BF16),
        scratch_shapes=[pltpu.VMEM((pairs, 2 * C, LANES), F32), pltpu.VMEM((cb * C, pw), F32)],
        compiler_params=_cparams("parallel", "parallel", "arbitrary"),
        name="rwkv_state_scan",
    )(rp, yl, mm, gg, gam, bonus, gmat, row(gn_g), row(gn_b))
    return _outproj_ln(g.reshape(B * S, D), 0, o.reshape(B * S, D), x, w_out, ln_g, ln_b)


def kernel(x, ln_g, ln_b, fox_w_in, fox_b_f, fox_w_out, dsa_w_in, dsa_kv_norm_g, dsa_w_uk, dsa_w_uv, dsa_w_out, rwkv_mu, rwkv_w_in, rwkv_w0, rwkv_w_lora_a, rwkv_w_lora_b, rwkv_a0, rwkv_a_lora_a, rwkv_a_lora_b, rwkv_k_k, rwkv_k_a, rwkv_r_k, rwkv_gn_g, rwkv_gn_b, rwkv_w_out, ret_w_in, ret_gn_g, ret_w_out):
    B, S, D = x.shape
    h = x.reshape(B * S, D)
    h, hb = _fox_layer(h, h, fox_w_in, fox_b_f, fox_w_out, ln_g[0], ln_b[0], B, S)
    h, hb = _dsa_layer(h, hb, dsa_w_in, dsa_kv_norm_g, dsa_w_uk, dsa_w_uv, dsa_w_out, ln_g[1], ln_b[1], B, S)
    h, hb = _rwkv_layer(h, rwkv_mu, rwkv_w_in, rwkv_w0, rwkv_w_lora_a, rwkv_w_lora_b, rwkv_a0,
                        rwkv_a_lora_a, rwkv_a_lora_b, rwkv_k_k, rwkv_k_a, rwkv_r_k, rwkv_gn_g, rwkv_gn_b,
                        rwkv_w_out, ln_g[2], ln_b[2], B, S)
    h, hb = _ret_layer(h, hb, ret_w_in, ret_gn_g, ret_w_out, ln_g[3], ln_b[3], B, S)
    return h.reshape(B, S, D)
```

```python
import functools
import math

import jax
import jax.numpy as jnp
import numpy as np
from jax import lax
from jax.experimental import pallas as pl
from jax.experimental.pallas import tpu as pltpu

F32 = jnp.float32
BF16 = jnp.bfloat16

D_MODEL = 1024
DEPTH = 4
LN_EPS = 1e-5
RMS_EPS = 1e-6
DN_ALPHA = (2 * DEPTH) ** 0.25
ROPE_THETA = 500000.0

FOX_HEADS = 8
FOX_HEAD_DIM = 128

RET_HEADS = 4
RET_HEAD_DIM = 256
RET_THETA = 10000.0

LANES = 128
VMEM_LIMIT = 48 * 1024 * 1024
NEG_BIG = -2.0 ** 100
LOG2E = 1.4426950408889634


def _cparams(*sem, vmem_limit=VMEM_LIMIT):
    return pltpu.CompilerParams(dimension_semantics=sem, vmem_limit_bytes=vmem_limit)


def _dot(a, b):
    return jnp.dot(a, b, preferred_element_type=F32)


def _dot_nt(a, b):
    return lax.dot_general(a, b, (((1,), (1,)), ((), ())), preferred_element_type=F32)


def _dot_tn(a, b):
    return lax.dot_general(a, b, (((0,), (0,)), ((), ())), preferred_element_type=F32)


def _split2(x):
    hi = x.astype(BF16)
    lo = (x - hi.astype(F32)).astype(BF16)
    return hi, lo


def _split3(x):
    p1 = x.astype(BF16)
    r1 = x - p1.astype(F32)
    p2 = r1.astype(BF16)
    p3 = (r1 - p2.astype(F32)).astype(BF16)
    return p1, p2, p3


def _sigmoid(x):
    return 1.0 / (1.0 + jnp.exp(-x))


def _round_robin(gens):
    results = [None] * len(gens)
    live = list(range(len(gens)))
    while live:
        still = []
        for i in live:
            try:
                out = next(gens[i])
            except StopIteration:
                continue
            if out is not None:
                results[i] = out
            still.append(i)
        live = still
    return results


def _pick_tile(n, pref):
    t = min(n, pref)
    while n % t:
        t //= 2
    return t


def _mm_kernel(a_ref, w_ref, o_ref):
    o_ref[...] = _dot(a_ref[...].astype(BF16), w_ref[...]).astype(o_ref.dtype)


def _matmul(a, w, out_dtype=BF16, tm=2048, tn=1024):
    M, K = a.shape
    N = w.shape[1]
    tm = _pick_tile(M, tm)
    if N % tn:
        tn = N
    return pl.pallas_call(
        _mm_kernel,
        grid=(M // tm, N // tn),
        in_specs=[pl.BlockSpec((tm, K), lambda i, j: (i, 0)),
                  pl.BlockSpec((K, tn), lambda i, j: (0, j))],
        out_specs=pl.BlockSpec((tm, tn), lambda i, j: (i, j)),
        out_shape=jax.ShapeDtypeStruct((M, N), out_dtype),
        compiler_params=_cparams("parallel", "arbitrary"),
        name="proj_matmul",
    )(a, w)


def _outproj_ln_kernel(g_ref, o_ref, x_ref, w_ref, lg_ref, lb_ref, xo_ref, *maybe_xb_ref):
    half_g = g_ref[...] * 0.5
    h = (half_g * o_ref[...]) * (1.0 + jnp.tanh(half_g))
    z = x_ref[...] + _dot(h.astype(BF16), w_ref[...])
    zc = z - jnp.mean(z, axis=-1, keepdims=True)
    var = jnp.mean(zc * zc, axis=-1, keepdims=True)
    out = zc * lax.rsqrt(var + LN_EPS / DN_ALPHA ** 2) * lg_ref[...] + lb_ref[...]
    xo_ref[...] = out
    for xb_ref in maybe_xb_ref:
        xb_ref[...] = out.astype(BF16)


def _outproj_ln(gate_arr, gate_col, o, x, w_out, ln_g, ln_b, tm=1024, bf16_copy=True):
    M, D = x.shape
    tm = _pick_tile(M, tm)
    n_out = 2 if bf16_copy else 1
    outs = pl.pallas_call(
        _outproj_ln_kernel,
        grid=(M // tm,),
        in_specs=[pl.BlockSpec((tm, D), lambda i: (i, gate_col)),
                  pl.BlockSpec((tm, D), lambda i: (i, 0)),
                  pl.BlockSpec((tm, D), lambda i: (i, 0)),
                  pl.BlockSpec((D, D), lambda i: (0, 0)),
                  pl.BlockSpec((1, D), lambda i: (0, 0)),
                  pl.BlockSpec((1, D), lambda i: (0, 0))],
        out_specs=[pl.BlockSpec((tm, D), lambda i: (i, 0)) for _ in range(n_out)],
        out_shape=[jax.ShapeDtypeStruct((M, D), F32), jax.ShapeDtypeStruct((M, D), BF16)][:n_out],
        compiler_params=_cparams("parallel"),
        name="outproj_layernorm",
    )(gate_arr, o, x, (w_out * (1.0 / DN_ALPHA)).astype(BF16), ln_g.reshape(1, D), ln_b.reshape(1, D))
    return (outs[0], outs[1]) if bf16_copy else (outs[0], None)


FOX_BIAS_PIECES = 3


def _fox_cum_kernel(x_ref, wh_ref, wl_ref, bf_ref, tril_ref, place_ref, pc_ref, carry_sc):
    @pl.when(pl.program_id(1) == 0)
    def _():
        carry_sc[...] = jnp.zeros_like(carry_sc)

    x_hi, x_lo = _split2(x_ref[...])
    z = _dot(x_hi, wh_ref[...]) + _dot(x_lo, wh_ref[...]) + _dot(x_hi, wl_ref[...]) + bf_ref[...]
    logf = jnp.minimum(z, 0.0) - jnp.log(1.0 + jnp.exp(-jnp.abs(z)))
    p1, p2, p3 = _split3(logf)
    tril = tril_ref[...]
    c = _dot(tril, p1) + _dot(tril, p2) + _dot(tril, p3) + carry_sc[...]
    carry_sc[...] = c[c.shape[0] - 1:, :]
    pieces = _split3(c * (-LOG2E))
    pc_ref[...] = sum(_dot(pieces[p], place_ref[p]) for p in range(FOX_BIAS_PIECES)).astype(pc_ref.dtype)


def _fox_cum(x3, w_f, b_f, ts=512):
    B, S, D = x3.shape
    H = w_f.shape[1]
    ts = _pick_tile(S, ts)
    w_pad = jnp.zeros((D, LANES), F32).at[:, :H].set(w_f)
    w_hi, w_lo = _split2(w_pad)
    b_pad = jnp.zeros((1, LANES), F32).at[0, :H].set(b_f)
    tril = jnp.asarray(np.tril(np.ones((ts, ts), np.float32)), BF16)
    place = np.zeros((FOX_BIAS_PIECES, LANES, LANES), np.float32)
    for p in range(FOX_BIAS_PIECES):
        for h in range(H):
            place[p, h, FOX_BIAS_PIECES * h + p] = 1.0
    return pl.pallas_call(
        _fox_cum_kernel,
        grid=(B, S // ts),
        in_specs=[pl.BlockSpec((None, ts, D), lambda b, j: (b, j, 0)),
                  pl.BlockSpec((D, LANES), lambda b, j: (0, 0)),
                  pl.BlockSpec((D, LANES), lambda b, j: (0, 0)),
                  pl.BlockSpec((1, LANES), lambda b, j: (0, 0)),
                  pl.BlockSpec((ts, ts), lambda b, j: (0, 0)),
                  pl.BlockSpec((FOX_BIAS_PIECES, LANES, LANES), lambda b, j: (0, 0, 0))],
        out_specs=pl.BlockSpec((None, ts, LANES), lambda b, j: (b, j, 0)),
        out_shape=jax.ShapeDtypeStruct((B, S, LANES), BF16),
        scratch_shapes=[pltpu.VMEM((1, LANES), F32)],
        compiler_params=_cparams("parallel", "arbitrary"),
        name="fox_decay_cumsum",
    )(x3, w_hi, w_lo, b_pad, tril, jnp.asarray(place, BF16))


ONES_ROWS = 16


def _fox_attn_kernel(q_ref, k_ref, v_ref, pc_ref, o_ref, kaug_sc, vt_sc, m_sc, acc_sc, *, tq, nsub, seq, unroll):
    dh = FOX_HEAD_DIM
    h = pl.program_id(1)
    g = pl.program_id(2)

    @pl.when(g == 0)
    def _():
        kaug_sc[:, :dh] = k_ref[...]
        kaug_sc[:, dh:] = pc_ref[...]
        for c in range(seq // tq):
            rows = slice(c * tq, (c + 1) * tq)
            vt_sc[:dh, rows] = v_ref[rows, :].astype(F32).T.astype(BF16)
        vt_sc[dh:, :] = jnp.ones((ONES_ROWS, seq), BF16)

    feat = lax.broadcasted_iota(jnp.int32, (LANES, tq), 0)
    bias_rows = (feat >= FOX_BIAS_PIECES * h) & (feat < FOX_BIAS_PIECES * (h + 1))
    ones_h = jnp.where(bias_rows, 1.0, 0.0).astype(BF16)
    q_aug = [jnp.concatenate([q_ref[a * tq:(a + 1) * tq, :].astype(F32).T.astype(BF16), ones_h], axis=0)
             for a in range(nsub)]
    m_sc[...] = jnp.full_like(m_sc, NEG_BIG)
    acc_sc[...] = jnp.zeros_like(acc_sc)
    causal = (lax.broadcasted_iota(jnp.int32, (tq, tq), 0) <= lax.broadcasted_iota(jnp.int32, (tq, tq), 1))
    first = g * nsub

    def chain(a, tiles, diag_last):
        offs = [pl.multiple_of(j * tq, tq) for j in tiles]
        scores = []
        for off in offs:
            scores.append(_dot(kaug_sc[pl.ds(off, tq), :], q_aug[a]))
            yield None
        for n, (off, s) in enumerate(zip(offs, scores)):
            if diag_last and n == len(offs) - 1:
                s = jnp.where(causal, s, NEG_BIG)
            m_old = m_sc[a]
            m_new = jnp.maximum(m_old, jnp.max(s, axis=0, keepdims=True))
            alpha = jnp.exp2(m_old - m_new)
            pv = _dot(vt_sc[:, pl.ds(off, tq)], jnp.exp2(s - m_new).astype(BF16))
            yield None
            acc_sc[a] = alpha * acc_sc[a] + pv
            m_sc[a] = m_new
        yield None

    def body(jj, c):
        _round_robin([chain(a, [jj * unroll + u for u in range(unroll)], False) for a in range(nsub)])
        return c

    lax.fori_loop(0, first // unroll, body, 0)
    _round_robin([chain(a, [first + t for t in range(a + 1)], True) for a in range(nsub)])
    for a in range(nsub):
        acc = acc_sc[a]
        o_t = acc[:dh] / acc[dh:dh + 1]
        o_ref[a * tq:(a + 1) * tq, :] = o_t.T.astype(o_ref.dtype)


def _fox_attention(proj3, pieces, tq=256, nsub=8):
    B, S, _ = proj3.shape
    H, dh = FOX_HEADS, FOX_HEAD_DIM
    tq = _pick_tile(S, tq)
    nsub = _pick_tile(S // tq, nsub)
    tg = tq * nsub
    return pl.pallas_call(
        functools.partial(_fox_attn_kernel, tq=tq, nsub=nsub, seq=S, unroll=min(nsub, 4)),
        grid=(B, H, S // tg),
        in_specs=[pl.BlockSpec((None, tg, dh), lambda b, h, g: (b, g, h)),
                  pl.BlockSpec((None, S, dh), lambda b, h, g: (b, 0, H + h)),
                  pl.BlockSpec((None, S, dh), lambda b, h, g: (b, 0, 2 * H + h)),
                  pl.BlockSpec((None, S, LANES), lambda b, h, g: (b, 0, 0))],
        out_specs=pl.BlockSpec((None, tg, dh), lambda b, h, g: (b, g, h)),
        out_shape=jax.ShapeDtypeStruct((B, S, H * dh), BF16),
        scratch_shapes=[pltpu.VMEM((S, dh + LANES), BF16),
                        pltpu.VMEM((dh + ONES_ROWS, S), BF16),
                        pltpu.VMEM((nsub, 1, tq), F32),
                        pltpu.VMEM((nsub, dh + ONES_ROWS, tq), F32)],
        compiler_params=_cparams("parallel", "parallel", "arbitrary"),
        name="fox_attention",
    )(proj3, proj3, proj3, pieces)


def _fox_layer(x, xb, w_in, b_f, w_out, ln_g, ln_b, B, S):
    D = D_MODEL
    H, dh = FOX_HEADS, FOX_HEAD_DIM
    scale = dh ** -0.5 * LOG2E
    w_q, w_k, w_v, w_f, w_g = jnp.split(w_in, [H * dh, 2 * H * dh, 3 * H * dh, 3 * H * dh + H], axis=1)
    w_main = jnp.concatenate([w_q * scale, w_k, w_v, w_g], axis=1).astype(BF16)
    proj = _matmul(xb, w_main)
    pieces = _fox_cum(x.reshape(B, S, D), w_f, b_f)
    o = _fox_attention(proj.reshape(B, S, 4 * D), pieces)
    return _outproj_ln(proj, 3, o.reshape(B * S, D), x, w_out, ln_g, ln_b)


def _ret_kernel(q_ref, k_ref, v_ref, cos_ref, sin_ref, dm_ref, xi_ref, zeta_ref, gc_ref, gn_ref,
                o_ref, r_sc):
    @pl.when(pl.program_id(1) == 0)
    def _():
        r_sc[...] = jnp.zeros_like(r_sc)

    dk = RET_HEAD_DIM
    half = dk // 2
    cos = cos_ref[...]
    sin = sin_ref[...]

    def rope(x):
        x1, x2 = x[:, :half], x[:, half:]
        return jnp.concatenate([x1 * cos - x2 * sin, x2 * cos + x1 * sin], axis=-1)

    def head_chain(h):
        cols = slice(h * dk, (h + 1) * dk)
        q = rope(q_ref[:, cols].astype(F32))
        k = rope(k_ref[:, cols].astype(F32)) * (dk ** -0.5)
        v = v_ref[:, cols]
        qb = q.astype(BF16)
        r_old = r_sc[h]
        scores = _dot_nt(qb, k.astype(BF16))
        cross = _dot(qb, r_old.astype(BF16))
        kz = (k * zeta_ref[h]).astype(BF16)
        r_new = _dot_tn(kz, v)
        yield None
        o = _dot((scores * dm_ref[h]).astype(BF16), v)
        r_sc[h] = r_old * gc_ref[h] + r_new
        yield None
        o = o + cross * xi_ref[h]
        o = o * lax.rsqrt(jnp.mean(o * o, axis=-1, keepdims=True) + RMS_EPS) * gn_ref[:, cols]
        o_ref[:, cols] = o.astype(o_ref.dtype)
        yield None

    _round_robin([head_chain(h) for h in range(RET_HEADS)])


def _ret_layer(x, xb, w_in, gn_g, w_out, ln_g, ln_b, B, S, chunk=512):
    D = D_MODEL
    H, dk = RET_HEADS, RET_HEAD_DIM
    C = _pick_tile(S, chunk)
    proj = _matmul(xb, w_in.astype(BF16))
    f32 = np.float32
    inv = (f32(1.0) / (f32(RET_THETA) ** (np.arange(0, dk, 2, dtype=f32) / f32(dk)))).astype(f32)
    ang = (np.arange(S, dtype=f32)[:, None] * inv[None, :]).astype(f32)
    cos, sin = jnp.asarray(np.cos(ang)), jnp.asarray(np.sin(ang))
    log_g = np.log1p(-(f32(2.0) ** (f32(-5.0) - np.arange(H, dtype=f32)))).astype(f32)
    pos = np.arange(C, dtype=f32)
    diff = pos[:, None] - pos[None, :]
    d_mask = jnp.asarray(np.where(diff[None] >= 0, np.exp(np.maximum(diff, 0.0)[None] * log_g[:, None, None]),
                                  0.0).astype(f32))
    xi = jnp.asarray(np.broadcast_to(np.exp((pos[None, :] + 1.0) * log_g[:, None])[:, :, None],
                                     (H, C, dk)).astype(f32))
    zeta = jnp.asarray(np.broadcast_to(np.exp((C - 1.0 - pos[None, :]) * log_g[:, None])[:, :, None],
                                       (H, C, dk)).astype(f32))
    g_c = jnp.asarray(np.broadcast_to(np.exp(f32(C) * log_g)[:, None, None], (H, 1, dk)).astype(f32))
    p3 = proj.reshape(B, S, 4 * D)
    o = pl.pallas_call(
        _ret_kernel,
        grid=(B, S // C),
        in_specs=[pl.BlockSpec((None, C, D), lambda b, c: (b, c, 0)),
                  pl.BlockSpec((None, C, D), lambda b, c: (b, c, 1)),
                  pl.BlockSpec((None, C, D), lambda b, c: (b, c, 2)),
                  pl.BlockSpec((C, dk // 2), lambda b, c: (c, 0)),
                  pl.BlockSpec((C, dk // 2), lambda b, c: (c, 0)),
                  pl.BlockSpec((H, C, C), lambda b, c: (0, 0, 0)),
                  pl.BlockSpec((H, C, dk), lambda b, c: (0, 0, 0)),
                  pl.BlockSpec((H, C, dk), lambda b, c: (0, 0, 0)),
                  pl.BlockSpec((H, 1, dk), lambda b, c: (0, 0, 0)),
                  pl.BlockSpec((1, D), lambda b, c: (0, 0))],
        out_specs=pl.BlockSpec((None, C, D), lambda b, c: (b, c, 0)),
        out_shape=jax.ShapeDtypeStruct((B, S, D), BF16),
        scratch_shapes=[pltpu.VMEM((H, dk, dk), F32)],
        compiler_params=_cparams("parallel", "arbitrary"),
        name="retnet_retention",
    )(p3, p3, p3, cos, sin, d_mask, xi, zeta, g_c, gn_g.reshape(1, D))
    return _outproj_ln(proj, 3, o.reshape(B * S, D), x, w_out, ln_g, ln_b, bf16_copy=False)


DSA_HEADS = 8
DSA_HEAD_DIM = 128
DSA_ROPE_DIM = 32
DSA_KV_RANK = 128
IDX_HEADS = 8
IDX_DIM = 64
IDX_ROPE_DIM = 16
TOPK_MAX = 256
HALF_MIN = -2 ** 15
HALF_ROWS = 16
DSA_VMEM_LIMIT = 56 * 1024 * 1024
DSA_ONES_ROWS = 16


def _rope_perm(width, groups):
    p = np.zeros((width, width), np.float32)
    for start, half in groups:
        for j in range(half):
            p[start + half + j, start + j] = 1.0
            p[start + j, start + half + j] = 1.0
    return p


def _rope_cs(S, width, groups, theta_dims):
    f32 = np.float32
    c = np.ones((S, width), f32)
    sg = np.zeros((S, width), f32)
    pos = np.arange(S, dtype=f32)[:, None]
    for (start, half), rot_dim in zip(groups, theta_dims):
        inv = (f32(1.0) / (f32(ROPE_THETA) ** (np.arange(0, rot_dim, 2, dtype=f32) / f32(rot_dim)))).astype(f32)
        ang = (pos * inv[None, :]).astype(f32)
        cos, sin = np.cos(ang), np.sin(ang)
        c[:, start:start + half] = cos
        c[:, start + half:start + 2 * half] = cos
        sg[:, start:start + half] = -sin
        sg[:, start + half:start + 2 * half] = sin
    return jnp.asarray(c), jnp.asarray(sg)


def _dsa_prep_kernel(q_ref, qi_ref, ckv_ref, misc_ref, cq_ref, sq_ref, ci_ref, si_ref, cm_ref, sm_ref,
                     pq_ref, pi_ref, pm_ref, selk_ref, selw_ref, wuk_ref, kvg_ref,
                     qf_ref, kvl_ref, kvt_ref, qir_ref, kid_ref, wit_ref):
    H, dh = DSA_HEADS, DSA_HEAD_DIM
    lane = lax.broadcasted_iota(jnp.int32, (1, LANES), 1)
    rope_lanes = lane < DSA_ROPE_DIM
    cq, sq = cq_ref[...], sq_ref[...]
    scale = dh ** -0.5 * LOG2E
    for h in range(H):
        qh = q_ref[:, h * dh:(h + 1) * dh]
        qr = qh.astype(F32) * cq + _dot(qh, pq_ref[...]) * sq
        q_lat = _dot(qr.astype(BF16), wuk_ref[h])
        qf_ref[h, :dh, :] = (q_lat * scale).T.astype(BF16)
        qf_ref[h, dh:, :] = jnp.where(rope_lanes, qr * scale, 0.0).T.astype(BF16)
    ci, si = ci_ref[...], si_ref[...]
    for g in range(IDX_HEADS * IDX_DIM // LANES):
        qg = qi_ref[:, g * LANES:(g + 1) * LANES]
        qr = qg.astype(F32) * ci + _dot(qg, pi_ref[...]) * si
        qir_ref[g * LANES:(g + 1) * LANES, :] = (qr * (IDX_DIM ** -0.5)).T.astype(BF16)
    ckv = ckv_ref[...].astype(F32)
    ckv = ckv * lax.rsqrt(jnp.mean(ckv * ckv, axis=-1, keepdims=True) + RMS_EPS) * kvg_ref[...]
    misc = misc_ref[...]
    mr = (misc.astype(F32) * cm_ref[...] + _dot(misc, pm_ref[...]) * sm_ref[...])
    kvl_ref[:, :DSA_KV_RANK] = ckv.astype(BF16)
    kvl_ref[:, DSA_KV_RANK:] = jnp.where(rope_lanes, mr, 0.0).astype(BF16)
    kvt_ref[:DSA_KV_RANK, :] = ckv.T.astype(BF16)
    kvt_ref[DSA_KV_RANK:, :] = jnp.ones((DSA_ONES_ROWS, ckv.shape[0]), BF16)
    kid_ref[...] = _dot(mr.astype(BF16), selk_ref[...]).astype(BF16)
    wi = _dot(misc, selw_ref[...]) * (IDX_HEADS ** -0.5)
    wit_ref[...] = wi.T[:IDX_HEADS, :]


def _sort_key(x):
    b = pltpu.bitcast(x, jnp.int32)
    return jnp.where(b < 0, b ^ 0x7FFFFFFF, b)


def _dsa_main_kernel(qf_ref, kvl_ref, kvt_ref, qi_ref, kid_ref, wit_ref, wuv_ref, obuf_ref, o_ref,
                     key_sc, hi_sc, lo_sc, low_sc, m_sc, acc_sc, *, tq, tk, k_sel, seq, qtile):
    del obuf_ref
    H = DSA_HEADS
    q0 = qtile * tq
    nj = (q0 + tq + tk - 1) // tk
    qpos = q0 + lax.broadcasted_iota(jnp.int32, (tk, tq), 1)
    kpos0 = lax.broadcasted_iota(jnp.int32, (tk, tq), 0)
    feat = lax.broadcasted_iota(jnp.int32, (LANES, 1), 0)
    neg_inf_key = _sort_key(jnp.full((1, 1), -jnp.inf, F32))

    wit = wit_ref[...]
    qi_heads = []
    for h in range(IDX_HEADS):
        g = qi_ref[(h // 2) * LANES:(h // 2 + 1) * LANES, :]
        keep = (feat >= IDX_DIM) if (h % 2) else (feat < IDX_DIM)
        qi_heads.append(jnp.where(keep, g, jnp.zeros_like(g)))

    def score_tile(j, c, causal):
        off = pl.multiple_of(j * tk, tk)
        ki = kid_ref[pl.ds(off, tk), :]
        scores = [_dot(ki, qi_heads[h]) for h in range(IDX_HEADS)]
        isc = jnp.zeros((tk, tq), F32)
        for h in range(IDX_HEADS):
            isc = isc + jnp.maximum(scores[h], 0.0) * wit[h:h + 1, :]
        isc = isc + 0.0
        if causal:
            isc = jnp.where(kpos0 + off <= qpos, isc, -jnp.inf)
        key = _sort_key(isc)
        key_sc[pl.ds(off, tk), :] = key
        hi_sc[pl.ds(off, tk), :] = (key >> 16).astype(jnp.int16)
        lo_sc[pl.ds(off, tk), :] = ((key & 0xFFFF) + HALF_MIN).astype(jnp.int16)
        return c

    n_clear = min(q0 // tk, nj)
    lax.fori_loop(0, n_clear, functools.partial(score_tile, causal=False), 0)
    lax.fori_loop(n_clear, nj, functools.partial(score_tile, causal=True), 0)

    def count(pred_fn):
        acc = jnp.zeros((8, tq), jnp.int32)
        for j in range(nj):
            hit = jnp.where(pred_fn(key_sc[j * tk:(j + 1) * tk, :], j * tk), 1, 0)
            acc = acc + jnp.sum(hit.reshape(tk // 8, 8, tq), axis=0)
        return jnp.sum(acc, axis=0, keepdims=True)

    rows16 = tk // HALF_ROWS

    def count16(ref, pred_fn):
        accs = [jnp.zeros((HALF_ROWS, tq), jnp.int16) for _ in range(2)]
        for j in range(nj):
            hit = jnp.where(pred_fn(ref[j * tk:(j + 1) * tk, :].reshape(rows16, HALF_ROWS, tq)),
                            jnp.int16(1), jnp.int16(0))
            for r in range(rows16):
                accs[r % 2] = accs[r % 2] + hit[r]
        return jnp.sum((accs[0] + accs[1]).astype(jnp.int32), axis=0, keepdims=True)

    def as_half(v):
        return jnp.broadcast_to(v.astype(jnp.int16), (HALF_ROWS, tq))[None]

    def search16(ref, base0, cnt0, rej0, bits, want):
        def bit_step(t, carry):
            base, cnt_b, cnt_r = carry
            cand = base + lax.shift_left(jnp.int32(1), bits - 1 - t)
            cand16 = as_half(cand)
            c = count16(ref, lambda kt: kt >= cand16)
            ok = c >= want
            return jnp.where(ok, cand, base), jnp.where(ok, c, cnt_b), jnp.where(ok, cnt_r, c)
        return lax.fori_loop(0, bits, bit_step, (base0, cnt0, rej0))

    zero16 = as_half(jnp.zeros((1, tq), jnp.int32))
    cnt_pos = count16(hi_sc, lambda kt: kt >= zero16)
    nonneg = cnt_pos >= k_sel
    t1, ge_hi, above = search16(hi_sc, jnp.where(nonneg, 0, HALF_MIN), jnp.where(nonneg, cnt_pos, nj * tk),
                                jnp.where(nonneg, 0, cnt_pos), 15, k_sel)
    t1_16 = as_half(t1)

    for j in range(nj):
        rows = slice(j * tk, (j + 1) * tk)
        hi = hi_sc[rows, :].reshape(rows16, HALF_ROWS, tq)
        lo = lo_sc[rows, :].reshape(rows16, HALF_ROWS, tq)
        low_sc[rows, :] = jnp.where(hi == t1_16, lo, jnp.int16(HALF_MIN)).reshape(tk, tq)
    t2, ge_low, gt_low = search16(low_sc, jnp.full((1, tq), HALF_MIN, jnp.int32), ge_hi - above,
                                  jnp.zeros((1, tq), jnp.int32), 16, k_sel - above)
    thr = lax.shift_left(t1, 16) | (t2 - HALF_MIN)
    n_ge = above + ge_low
    n_gt = above + gt_low
    need = k_sel - n_gt
    excess = ((n_ge - n_gt) > need) & (thr > neg_inf_key)
    any_excess = jnp.max(jnp.where(excess, 1, 0)) > 0

    def tie_cut():
        def step(t, lo):
            cand = lo + lax.shift_left(jnp.int32(1), int(math.log2(seq)) - t)
            c = count(lambda kt, off: (kt == thr) & (kpos0 + off < cand))
            return jnp.where(c < need, cand, lo)
        lo = lax.fori_loop(0, int(math.log2(seq)) + 1, step, jnp.zeros((1, tq), jnp.int32))
        return jnp.where(excess, lo, seq)

    cut = lax.cond(any_excess, tie_cut, lambda: jnp.full((1, tq), seq, jnp.int32))

    m_sc[...] = jnp.full_like(m_sc, NEG_BIG)
    acc_sc[...] = jnp.zeros_like(acc_sc)

    def attn_tile(j, c):
        off = pl.multiple_of(j * tk, tk)
        kt = key_sc[pl.ds(off, tk), :]
        kpos = kpos0 + off
        bias = jnp.where(kt > thr, 0.0, jnp.where(kt == thr, jnp.where(kpos <= cut, 0.0, NEG_BIG), NEG_BIG))
        bias = jnp.where(kpos <= qpos, bias, NEG_BIG)
        kv = kvl_ref[pl.ds(off, tk), :]
        kvt = kvt_ref[:, pl.ds(off, tk)]

        def head_step(h):
            s = _dot(kv, qf_ref[h]) + bias
            yield None
            m_old = m_sc[h]
            m_new = jnp.maximum(m_old, jnp.max(s, axis=0, keepdims=True))
            alpha = jnp.exp2(m_old - m_new)
            pv = _dot(kvt, jnp.exp2(s - m_new).astype(BF16))
            yield None
            acc_sc[h] = alpha * acc_sc[h] + pv
            m_sc[h] = m_new
            yield None

        _round_robin([head_step(h) for h in range(H)])
        return c

    lax.fori_loop(0, nj, attn_tile, 0)
    for h in range(H):
        acc = acc_sc[h]
        o_lat_t = (acc[:DSA_KV_RANK] / acc[DSA_KV_RANK:DSA_KV_RANK + 1]).astype(BF16)
        o_ref[:, h * DSA_HEAD_DIM:(h + 1) * DSA_HEAD_DIM] = _dot_tn(o_lat_t, wuv_ref[h]).astype(o_ref.dtype)


def _dsa_layer(x, xb, w_in, kv_norm_g, w_uk, w_uv, w_out, ln_g, ln_b, B, S, tq=512, tk=512):
    D = D_MODEL
    H, dh, dr, dc = DSA_HEADS, DSA_HEAD_DIM, DSA_ROPE_DIM, DSA_KV_RANK
    HI, di = IDX_HEADS, IDX_DIM
    w_q, w_ckv, w_kr, w_qi, w_ki, w_wi, w_g = jnp.split(
        w_in, np.cumsum([H * dh, dc, dr, HI * di, di, HI]).tolist(), axis=1)
    w_misc = jnp.concatenate([w_kr, w_ki, w_wi, jnp.zeros((D, LANES - dr - di - HI), F32)], axis=1)
    w_main = jnp.concatenate([w_q, w_g, w_qi, w_ckv, w_misc], axis=1).astype(BF16)
    n_main = w_main.shape[1]
    proj = _matmul(xb, w_main, tm=1024, tn=n_main)
    c_q, c_qi, c_ckv, c_misc = 0, 2 * D // LANES, (2 * D + HI * di) // LANES, (2 * D + HI * di + dc) // LANES

    q_groups = [(0, dr // 2)]
    i_groups = [(0, IDX_ROPE_DIM // 2), (di, IDX_ROPE_DIM // 2)]
    m_groups = [(0, dr // 2), (dr, IDX_ROPE_DIM // 2)]
    cq, sq = _rope_cs(S, LANES, q_groups, [dr])
    ci, si = _rope_cs(S, LANES, i_groups, [IDX_ROPE_DIM, IDX_ROPE_DIM])
    cm, sm = _rope_cs(S, LANES, m_groups, [dr, IDX_ROPE_DIM])
    pq = jnp.asarray(_rope_perm(LANES, q_groups), BF16)
    pi = jnp.asarray(_rope_perm(LANES, i_groups), BF16)
    pm = jnp.asarray(_rope_perm(LANES, m_groups), BF16)
    selk = np.zeros((LANES, LANES), np.float32)
    for j in range(di):
        selk[dr + j, j] = 1.0
        selk[dr + j, di + j] = 1.0
    selw = np.zeros((LANES, LANES), np.float32)
    for j in range(HI):
        selw[dr + di + j, j] = 1.0
    wuk = jnp.concatenate([jnp.zeros((H, dr, dc), F32), jnp.transpose(w_uk, (0, 2, 1))], axis=1).astype(BF16)

    ts = _pick_tile(S, 512)
    p3 = proj.reshape(B, S, n_main)
    tab = lambda: pl.BlockSpec((ts, LANES), lambda b, j: (j, 0))
    mat = lambda: pl.BlockSpec((LANES, LANES), lambda b, j: (0, 0))
    qf, kvl, kvt, qir, kid, wit = pl.pallas_call(
        _dsa_prep_kernel,
        grid=(B, S // ts),
        in_specs=[pl.BlockSpec((None, ts, H * dh), lambda b, j: (b, j, 0)),
                  pl.BlockSpec((None, ts, HI * di), lambda b, j: (b, j, c_qi * LANES // (HI * di))),
                  pl.BlockSpec((None, ts, dc), lambda b, j: (b, j, c_ckv)),
                  pl.BlockSpec((None, ts, LANES), lambda b, j: (b, j, c_misc)),
                  tab(), tab(), tab(), tab(), tab(), tab(),
                  mat(), mat(), mat(), mat(), mat(),
                  pl.BlockSpec((H, LANES, dc), lambda b, j: (0, 0, 0)),
                  pl.BlockSpec((1, dc), lambda b, j: (0, 0))],
        out_specs=[pl.BlockSpec((None, H, 2 * dc, ts), lambda b, j: (b, 0, 0, j)),
                   pl.BlockSpec((None, ts, 2 * dc), lambda b, j: (b, j, 0)),
                   pl.BlockSpec((None, dc + DSA_ONES_ROWS, ts), lambda b, j: (b, 0, j)),
                   pl.BlockSpec((None, HI * di, ts), lambda b, j: (b, 0, j)),
                   pl.BlockSpec((None, ts, LANES), lambda b, j: (b, j, 0)),
                   pl.BlockSpec((None, HI, ts), lambda b, j: (b, 0, j))],
        out_shape=[jax.ShapeDtypeStruct((B, H, 2 * dc, S), BF16),
                   jax.ShapeDtypeStruct((B, S, 2 * dc), BF16),
                   jax.ShapeDtypeStruct((B, dc + DSA_ONES_ROWS, S), BF16),
                   jax.ShapeDtypeStruct((B, HI * di, S), BF16),
                   jax.ShapeDtypeStruct((B, S, LANES), BF16),
                   jax.ShapeDtypeStruct((B, HI, S), F32)],
        compiler_params=_cparams("parallel", "parallel"),
        name="dsa_prep",
    )(p3, p3, p3, p3, cq, sq, ci, si, cm, sm, pq, pi, pm,
      jnp.asarray(selk, BF16), jnp.asarray(selw, BF16), wuk, kv_norm_g.reshape(1, dc))

    tq = _pick_tile(S, tq)
    tk = _pick_tile(S, tk)
    k_sel = min(TOPK_MAX, S // 4)
    wuv = w_uv.astype(BF16)
    o = jnp.zeros((B, S, H * dh), BF16)
    for i in range(S // tq):
        nkeys = -(-((i + 1) * tq) // tk) * tk
        o = pl.pallas_call(
            functools.partial(_dsa_main_kernel, tq=tq, tk=tk, k_sel=k_sel, seq=S, qtile=i),
            grid=(B,),
            in_specs=[pl.BlockSpec((None, H, 2 * dc, tq), lambda b, i=i: (b, 0, 0, i)),
                      pl.BlockSpec((None, nkeys, 2 * dc), lambda b: (b, 0, 0)),
                      pl.BlockSpec((None, dc + DSA_ONES_ROWS, nkeys), lambda b: (b, 0, 0)),
                      pl.BlockSpec((None, HI * di, tq), lambda b, i=i: (b, 0, i)),
                      pl.BlockSpec((None, nkeys, LANES), lambda b: (b, 0, 0)),
                      pl.BlockSpec((None, HI, tq), lambda b, i=i: (b, 0, i)),
                      pl.BlockSpec((H, dc, dh), lambda b: (0, 0, 0)),
                      pl.BlockSpec(memory_space=pl.ANY)],
            out_specs=pl.BlockSpec((None, tq, H * dh), lambda b, i=i: (b, i, 0)),
            out_shape=jax.ShapeDtypeStruct((B, S, H * dh), BF16),
            input_output_aliases={7: 0},
            scratch_shapes=[pltpu.VMEM((nkeys, tq), jnp.int32),
                            pltpu.VMEM((nkeys, tq), jnp.int16), pltpu.VMEM((nkeys, tq), jnp.int16),
                            pltpu.VMEM((nkeys, tq), jnp.int16),
                            pltpu.VMEM((H, 1, tq), F32),
                            pltpu.VMEM((H, dc + DSA_ONES_ROWS, tq), F32)],
            compiler_params=_cparams("parallel", vmem_limit=DSA_VMEM_LIMIT),
            name=f"dsa_select_attention_q{i}",
        )(qf, kvl, kvt, qir, kid, wit, wuv, o)
    return _outproj_ln(proj, 1, o.reshape(B * S, D), x, w_out, ln_g, ln_b, bf16_copy=False)


RWKV_HEADS = 16
RWKV_HEAD_DIM = 64
RWKV_GN_EPS = 64e-5
RWKV_CHUNK = 64
RWKV_SUB = 16


def _group_sum(x, gmat, split=True):
    outs = []
    for c in range(x.shape[1] // LANES):
        xc = x[:, c * LANES:(c + 1) * LANES]
        if split:
            hi, lo = _split2(xc)
            outs.append(_dot(hi, gmat) + _dot(lo, gmat))
        else:
            outs.append(_dot(xc.astype(BF16), gmat))
    return outs[0] if len(outs) == 1 else jnp.concatenate(outs, axis=1)


def _softplus(y):
    return jnp.maximum(y, 0.0) + jnp.log(1.0 + jnp.exp(-jnp.abs(y)))


def _rwkv_proj_kernel(x_ref, xprev_ref, mu_ref, wr_ref, wk_ref, wv_ref, wg_ref, wla_ref, wlb_ref,
                      ala_ref, alb_ref, w0_ref, a0_ref, kk_ref, ka_ref, rk_ref, gmat_ref, tril_ref,
                      r_ref, k_ref, v_ref, g_ref, kap_ref, b_ref, cum_ref, bonus_ref):
    x = x_ref[...]
    ts = x.shape[0]
    prev = jnp.where(pl.program_id(1) == 0, 0.0, xprev_ref[7:8, :])
    rowid = lax.broadcasted_iota(jnp.int32, (ts, 1), 0)
    xx = jnp.where(rowid == 0, prev, pltpu.roll(x, 1, 0)) - x

    def mixed(i):
        return (x + xx * mu_ref[i:i + 1, :]).astype(BF16)

    r = _dot(mixed(0), wr_ref[...])
    k = _dot(mixed(2), wk_ref[...])
    v = _dot(mixed(3), wv_ref[...])
    g_ref[...] = _dot(mixed(5), wg_ref[...]).astype(g_ref.dtype)
    lora_w = _dot(jnp.tanh(_dot(mixed(1), wla_ref[...])).astype(BF16), wlb_ref[...])
    lora_a = _dot(_dot(mixed(4), ala_ref[...]).astype(BF16), alb_ref[...])
    w_log = -_softplus(-(w0_ref[...] + lora_w)) - 0.5
    lw = -jnp.exp(w_log)
    tril = tril_ref[...]
    for c in range(ts // RWKV_CHUNK):
        rows = slice(c * RWKV_CHUNK, (c + 1) * RWKV_CHUNK)
        cum_ref[rows, :] = sum(_dot(tril, piece) for piece in _split2(lw[rows, :]))
    a = _sigmoid(a0_ref[...] + lora_a)
    gmat = gmat_ref[...]
    kk = k * kk_ref[...]
    kap = kk * lax.rsqrt(_group_sum(kk * kk, gmat, split=False) + 1e-12)
    k2 = k * (1.0 + (a - 1.0) * ka_ref[...])
    bonus_ref[...] = (_group_sum(r * k2 * rk_ref[...], gmat, split=False) * v).astype(bonus_ref.dtype)
    r_ref[...] = r.astype(r_ref.dtype)
    k_ref[...] = k2.astype(k_ref.dtype)
    v_ref[...] = v.astype(v_ref.dtype)
    kap_ref[...] = kap.astype(kap_ref.dtype)
    b_ref[...] = (kap * a).astype(b_ref.dtype)


def _bd(x, left):
    z = jnp.zeros_like(x)
    return jnp.concatenate([jnp.where(left, x, z), jnp.where(left, z, x)], axis=0)


def _unbd(x_bd):
    c = x_bd.shape[0] // 2
    return x_bd[:c] + x_bd[c:]


def _rwkv_chunk_pair(L, r, k, v, kap, b, masks):
    C = RWKV_CHUNK
    left, strict, lower, same_sub, eye, first_row = masks
    Lc = L[C - 1:C, :]
    L_excl = jnp.where(first_row, 0.0, pltpu.roll(L, 1, 0))
    e_l, e_lx, e_nl, e_r = jnp.exp(L), jnp.exp(L_excl), jnp.exp(-L), jnp.exp(Lc - L)
    at = _bd(-kap * e_lx, left).astype(BF16)
    rt = _bd(r * e_l, left)
    bt = _bd(b * e_nl, left).astype(BF16)
    kt = _bd(k * e_nl, left).astype(BF16)
    bh = _bd(b * e_r, left).astype(BF16)
    kh = _bd(k * e_r, left).astype(BF16)
    vb = _bd(v, left).astype(BF16)

    a1 = _dot_nt(jnp.concatenate([at, rt.astype(BF16)], axis=0), jnp.concatenate([bt, kt], axis=0))
    yield None
    n = jnp.where(strict, a1[:2 * C, :2 * C], 0.0)
    ak = jnp.where(strict, a1[:2 * C, 2 * C:], 0.0).astype(BF16)
    rb = jnp.where(lower, a1[2 * C:, :2 * C], 0.0).astype(BF16)
    rk = jnp.where(lower, a1[2 * C:, 2 * C:], 0.0).astype(BF16)

    W = 2 * C
    nd = jnp.where(same_sub, n, 0.0)
    no = (n - nd).astype(BF16)
    ndb = nd.astype(BF16)
    n2 = _dot(ndb, ndb)
    akv = _dot(ak, vb)
    yield None
    n2b = n2.astype(BF16)
    t0 = eye + nd
    r = _dot(n2b, jnp.concatenate([n2b, t0.astype(BF16)], axis=1))
    yield None
    n4b = r[:, :W].astype(BF16)
    t01 = t0 + r[:, W:]
    r = _dot(n4b, jnp.concatenate([n4b, t01.astype(BF16)], axis=1))
    yield None
    u = t01 + r[:, W:]
    td = u + _dot(r[:, :W].astype(BF16), u.astype(BF16))
    yield None
    tdb = td.astype(BF16)
    x1 = _dot(tdb, no)
    yield None
    x1b = x1.astype(BF16)
    r = _dot(x1b, jnp.concatenate([x1b, tdb], axis=1))
    yield None
    w = td + r[:, W:]
    t = (w + _dot(r[:, :W].astype(BF16), w.astype(BF16))).astype(BF16)
    yield None
    pq = _dot(t, jnp.concatenate([at, akv.astype(BF16)], axis=1)).astype(BF16)
    yield None
    z = jnp.concatenate([pq, jnp.concatenate([jnp.zeros_like(vb), vb], axis=1)], axis=0)
    ry = _dot(jnp.concatenate([rb, rk], axis=1), z)
    mg = _dot_tn(z, jnp.concatenate([bh, kh], axis=0))
    yield None
    rp = rt + ry[:, :2 * C]
    yl = ry[:, 2 * C:]
    yield _unbd(rp), _unbd(yl), _unbd(mg[:2 * C]), _unbd(mg[2 * C:]), jnp.exp(Lc)


def _rwkv_chunk_kernel(cum_ref, r_ref, k_ref, v_ref, kap_ref, b_ref,
                       rp_ref, yl_ref, mm_ref, gg_ref, gam_ref, *, pairs, cpb):
    C = RWKV_CHUNK
    lane = lax.broadcasted_iota(jnp.int32, (1, LANES), 1)
    left = lane < RWKV_HEAD_DIM
    ri = lax.broadcasted_iota(jnp.int32, (2 * C, 2 * C), 0)
    ci = lax.broadcasted_iota(jnp.int32, (2 * C, 2 * C), 1)
    same_head = (ri // C) == (ci // C)
    strict = same_head & ((ri % C) > (ci % C))
    lower = same_head & ((ri % C) >= (ci % C))
    same_sub = (ri // RWKV_SUB) == (ci // RWKV_SUB)
    eye = jnp.where(ri == ci, 1.0, 0.0).astype(F32)
    first_row = lax.broadcasted_iota(jnp.int32, (C, 1), 0) == 0
    masks = (left, strict, lower, same_sub, eye, first_row)
    jobs = [(c, slice(c * C, (c + 1) * C), slice(p * LANES, (p + 1) * LANES))
            for c in range(cpb) for p in range(pairs)]
    results = _round_robin([
        _rwkv_chunk_pair(cum_ref[rows, sl], r_ref[rows, sl].astype(F32), k_ref[rows, sl].astype(F32),
                         v_ref[rows, sl].astype(F32), kap_ref[rows, sl].astype(F32),
                         b_ref[rows, sl].astype(F32), masks)
        for _, rows, sl in jobs])
    for (c, rows, sl), (rp, yl, mm, gg, gam) in zip(jobs, results):
        rp_ref[rows, sl] = rp.astype(rp_ref.dtype)
        yl_ref[rows, sl] = yl
        mm_ref[c, :, sl] = mm.astype(mm_ref.dtype)
        gg_ref[c, :, sl] = gg
        gam_ref[c, :, sl] = gam


def _rwkv_seq_kernel(rp_ref, yl_ref, mm_ref, gg_ref, gam_ref, bonus_ref, gmat_ref, gng_ref, gnb_ref,
                     o_ref, s_sc, y_sc, *, pairs, cb):
    C = RWKV_CHUNK
    lane = lax.broadcasted_iota(jnp.int32, (1, LANES), 1)
    left = lane < RWKV_HEAD_DIM

    @pl.when(pl.program_id(2) == 0)
    def _():
        s_sc[...] = jnp.zeros_like(s_sc)

    states = [s_sc[p] for p in range(pairs)]
    for c in range(cb):
        rows = slice(c * C, (c + 1) * C)
        for p in range(pairs):
            sl = slice(p * LANES, (p + 1) * LANES)
            s = states[p]
            sb = s.astype(BF16)
            y_sc[rows, sl] = _dot_nt(rp_ref[rows, sl], sb) + yl_ref[rows, sl]
            mm = _bd(mm_ref[c, :, sl], left)
            gg = _bd(gg_ref[c, :, sl], left)
            states[p] = s * gam_ref[c, :, sl] + _dot(sb, mm) + gg
    for p in range(pairs):
        s_sc[p] = states[p]

    y = y_sc[...]
    gmat = gmat_ref[...]
    inv_n = 1.0 / RWKV_HEAD_DIM
    yc = y - _group_sum(y, gmat) * inv_n
    var = _group_sum(yc * yc, gmat) * inv_n
    yn = yc * lax.rsqrt(var + RWKV_GN_EPS) * gng_ref[...] + gnb_ref[...]
    o_ref[...] = (yn + bonus_ref[...].astype(F32)).astype(o_ref.dtype)


def _rwkv_layer(x, mu, w_in, w0, w_lora_a, w_lora_b, a0, a_lora_a, a_lora_b, k_k, k_a, r_k,
                gn_g, gn_b, w_out, ln_g, ln_b, B, S, ts=512, pairs=8, seq_pairs=8, chunks_per_step=2):
    D = D_MODEL
    C = RWKV_CHUNK
    nc = S // C
    ts = _pick_tile(S, ts)
    tril = jnp.asarray(np.tril(np.ones((C, C), np.float32)), BF16)
    w_r, w_k, w_v, w_g =[w.astype(BF16) for w in jnp.split(w_in, 4, axis=1)]
    gmat = jnp.asarray(np.kron(np.eye(2, dtype=np.float32), np.ones((RWKV_HEAD_DIM, RWKV_HEAD_DIM), np.float32)), BF16)
    row = lambda a: a.reshape(1, D)
    x3 = x.reshape(B, S, D)
    full = lambda shape: pl.BlockSpec(shape, lambda b, j: (0,) * len(shape))
    tile = lambda: pl.BlockSpec((None, ts, D), lambda b, j: (b, j, 0))
    lr = w_lora_a.shape[1]
    outs = pl.pallas_call(
        _rwkv_proj_kernel,
        grid=(B, S // ts),
        in_specs=[tile(),
                  pl.BlockSpec((None, 8, D), lambda b, j: (b, jnp.maximum(j * (ts // 8) - 1, 0), 0)),
                  full((6, D)), full((D, D)), full((D, D)), full((D, D)), full((D, D)),
                  full((D, lr)), full((lr, D)), full((D, lr)), full((lr, D)),
                  full((1, D)), full((1, D)), full((1, D)), full((1, D)), full((1, D)),
                  full((LANES, LANES)), full((C, C))],
        out_specs=[tile() for _ in range(8)],
        out_shape=[jax.ShapeDtypeStruct((B, S, D), dt) for dt in (BF16, BF16, BF16, BF16, BF16, BF16, F32, BF16)],
        compiler_params=_cparams("parallel", "arbitrary"),
        name="rwkv_projections",
    )(x3, x3, mu, w_r, w_k, w_v, w_g, w_lora_a.astype(BF16), w_lora_b.astype(BF16),
      a_lora_a.astype(BF16), a_lora_b.astype(BF16), row(w0), row(a0), row(k_k), row(k_a), row(r_k), gmat, tril)
    r, k2, v, g, kap, bvec, cum, bonus = outs

    pw = pairs * LANES
    cpb = _pick_tile(nc, chunks_per_step)
    cblk = lambda: pl.BlockSpec((None, cpb * C, pw), lambda b, c, q: (b, c, q))
    sblk = lambda: pl.BlockSpec((None, cpb, C, pw), lambda b, c, q: (b, c, 0, q))
    rp, yl, mm, gg, gam = pl.pallas_call(
        functools.partial(_rwkv_chunk_kernel, pairs=pairs, cpb=cpb),
        grid=(B, nc // cpb, D // pw),
        in_specs=[cblk() for _ in range(6)],
        out_specs=[cblk(), cblk(), sblk(), sblk(),
                   pl.BlockSpec((None, cpb, 1, pw), lambda b, c, q: (b, c, 0, q))],
        out_shape=[jax.ShapeDtypeStruct((B, S, D), BF16), jax.ShapeDtypeStruct((B, S, D), F32),
                   jax.ShapeDtypeStruct((B, nc, C, D), BF16), jax.ShapeDtypeStruct((B, nc, C, D), F32),
                   jax.ShapeDtypeStruct((B, nc, 1, D), F32)],
        compiler_params=_cparams("parallel", "parallel", "parallel"),
        name="rwkv_chunk_summaries",
    )(cum, r, k2, v, kap, bvec)

    cb = _pick_tile(nc, 8)
    pairs = seq_pairs
    pw = pairs * LANES
    o = pl.pallas_call(
        functools.partial(_rwkv_seq_kernel, pairs=pairs, cb=cb),
        grid=(B, D // pw, nc // cb),
        in_specs=[pl.BlockSpec((None, cb * C, pw), lambda b, q, j: (b, j, q)),
                  pl.BlockSpec((None, cb * C, pw), lambda b, q, j: (b, j, q)),
                  pl.BlockSpec((None, cb, C, pw), lambda b, q, j: (b, j, 0, q)),
                  pl.BlockSpec((None, cb, C, pw), lambda b, q, j: (b, j, 0, q)),
                  pl.BlockSpec((None, cb, 1, pw), lambda b, q, j: (b, j, 0, q)),
                  pl.BlockSpec((None, cb * C, pw), lambda b, q, j: (b, j, q)),
                  pl.BlockSpec((LANES, LANES), lambda b, q, j: (0, 0)),
                  pl.BlockSpec((1, pw), lambda b, q, j: (0, q)),
                  pl.BlockSpec((1, pw), lambda b, q, j: (0, q))],
        out_specs=pl.BlockSpec((None, cb * C, pw), lambda b, q, j: (b, j, q)),
        out_shape=jax.ShapeDtypeStruct((B, S, D), BF16),
        scratch_shapes=[pltpu.VMEM((pairs, 2 * C, LANES), F32), pltpu.VMEM((cb * C, pw), F32)],
        compiler_params=_cparams("parallel", "parallel", "arbitrary"),
        name="rwkv_state_scan",
    )(rp, yl, mm, gg, gam, bonus, gmat, row(gn_g), row(gn_b))
    return _outproj_ln(g.reshape(B * S, D), 0, o.reshape(B * S, D), x, w_out, ln_g, ln_b)


def kernel(x, ln_g, ln_b, fox_w_in, fox_b_f, fox_w_out, dsa_w_in, dsa_kv_norm_g, dsa_w_uk, dsa_w_uv, dsa_w_out, rwkv_mu, rwkv_w_in, rwkv_w0, rwkv_w_lora_a, rwkv_w_lora_b, rwkv_a0, rwkv_a_lora_a, rwkv_a_lora_b, rwkv_k_k, rwkv_k_a, rwkv_r_k, rwkv_gn_g, rwkv_gn_b, rwkv_w_out, ret_w_in, ret_gn_g, ret_w_out):
    B, S, D = x.shape
    h = x.reshape(B * S, D)
    h, hb = _fox_layer(h, h, fox_w_in, fox_b_f, fox_w_out, ln_g[0], ln_b[0], B, S)
    h, hb = _dsa_layer(h, hb, dsa_w_in, dsa_kv_norm_g, dsa_w_uk, dsa_w_uv, dsa_w_out, ln_g[1], ln_b[1], B, S)
    h, hb = _rwkv_layer(h, rwkv_mu, rwkv_w_in, rwkv_w0, rwkv_w_lora_a, rwkv_w_lora_b, rwkv_a0,
                        rwkv_a_lora_a, rwkv_a_lora_b, rwkv_k_k, rwkv_k_a, rwkv_r_k, rwkv_gn_g, rwkv_gn_b,
                        rwkv_w_out, ln_g[2], ln_b[2], B, S)
    h, hb = _ret_layer(h, hb, ret_w_in, ret_gn_g, ret_w_out, ln_g[3], ln_b[3], B, S)
    return h.reshape(B, S, D)
```

```python
import functools
import math

import jax
import jax.numpy as jnp
import numpy as np
from jax import lax
from jax.experimental import pallas as pl
from jax.experimental.pallas import tpu as pltpu

F32 = jnp.float32
BF16 = jnp.bfloat16

D_MODEL = 1024
DEPTH = 4
LN_EPS = 1e-5
RMS_EPS = 1e-6
DN_ALPHA = (2 * DEPTH) ** 0.25
ROPE_THETA = 500000.0

FOX_HEADS = 8
FOX_HEAD_DIM = 128

RET_HEADS = 4
RET_HEAD_DIM = 256
RET_THETA = 10000.0

LANES = 128
VMEM_LIMIT = 48 * 1024 * 1024
NEG_BIG = -2.0 ** 100
LOG2E = 1.4426950408889634


def _cparams(*sem, vmem_limit=VMEM_LIMIT):
    return pltpu.CompilerParams(dimension_semantics=sem, vmem_limit_bytes=vmem_limit)


def _dot(a, b):
    return jnp.dot(a, b, preferred_element_type=F32)


def _dot_nt(a, b):
    return lax.dot_general(a, b, (((1,), (1,)), ((), ())), preferred_element_type=F32)


def _dot_tn(a, b):
    return lax.dot_general(a, b, (((0,), (0,)), ((), ())), preferred_element_type=F32)


def _split2(x):
    hi = x.astype(BF16)
    lo = (x - hi.astype(F32)).astype(BF16)
    return hi, lo


def _split3(x):
    p1 = x.astype(BF16)
    r1 = x - p1.astype(F32)
    p2 = r1.astype(BF16)
    p3 = (r1 - p2.astype(F32)).astype(BF16)
    return p1, p2, p3


def _sigmoid(x):
    return 1.0 / (1.0 + jnp.exp(-x))


def _round_robin(gens):
    results = [None] * len(gens)
    live = list(range(len(gens)))
    while live:
        still = []
        for i in live:
            try:
                out = next(gens[i])
            except StopIteration:
                continue
            if out is not None:
                results[i] = out
            still.append(i)
        live = still
    return results


def _pick_tile(n, pref):
    t = min(n, pref)
    while n % t:
        t //= 2
    return t


def _mm_kernel(a_ref, w_ref, o_ref):
    o_ref[...] = _dot(a_ref[...].astype(BF16), w_ref[...]).astype(o_ref.dtype)


def _matmul(a, w, out_dtype=BF16, tm=2048, tn=1024):
    M, K = a.shape
    N = w.shape[1]
    tm = _pick_tile(M, tm)
    if N % tn:
        tn = N
    return pl.pallas_call(
        _mm_kernel,
        grid=(M // tm, N // tn),
        in_specs=[pl.BlockSpec((tm, K), lambda i, j: (i, 0)),
                  pl.BlockSpec((K, tn), lambda i, j: (0, j))],
        out_specs=pl.BlockSpec((tm, tn), lambda i, j: (i, j)),
        out_shape=jax.ShapeDtypeStruct((M, N), out_dtype),
        compiler_params=_cparams("parallel", "arbitrary"),
        name="proj_matmul",
    )(a, w)


def _outproj_ln_kernel(g_ref, o_ref, x_ref, w_ref, lg_ref, lb_ref, xo_ref, *maybe_xb_ref):
    half_g = g_ref[...] * 0.5
    h = (half_g * o_ref[...]) * (1.0 + jnp.tanh(half_g))
    z = x_ref[...] + _dot(h.astype(BF16), w_ref[...])
    zc = z - jnp.mean(z, axis=-1, keepdims=True)
    var = jnp.mean(zc * zc, axis=-1, keepdims=True)
    out = zc * lax.rsqrt(var + LN_EPS / DN_ALPHA ** 2) * lg_ref[...] + lb_ref[...]
    xo_ref[...] = out
    for xb_ref in maybe_xb_ref:
        xb_ref[...] = out.astype(BF16)


def _outproj_ln(gate_arr, gate_col, o, x, w_out, ln_g, ln_b, tm=1024, bf16_copy=True):
    M, D = x.shape
    tm = _pick_tile(M, tm)
    n_out = 2 if bf16_copy else 1
    outs = pl.pallas_call(
        _outproj_ln_kernel,
        grid=(M // tm,),
        in_specs=[pl.BlockSpec((tm, D), lambda i: (i, gate_col)),
                  pl.BlockSpec((tm, D), lambda i: (i, 0)),
                  pl.BlockSpec((tm, D), lambda i: (i, 0)),
                  pl.BlockSpec((D, D), lambda i: (0, 0)),
                  pl.BlockSpec((1, D), lambda i: (0, 0)),
                  pl.BlockSpec((1, D), lambda i: (0, 0))],
        out_specs=[pl.BlockSpec((tm, D), lambda i: (i, 0)) for _ in range(n_out)],
        out_shape=[jax.ShapeDtypeStruct((M, D), F32), jax.ShapeDtypeStruct((M, D), BF16)][:n_out],
        compiler_params=_cparams("parallel"),
        name="outproj_layernorm",
    )(gate_arr, o, x, (w_out * (1.0 / DN_ALPHA)).astype(BF16), ln_g.reshape(1, D), ln_b.reshape(1, D))
    return (outs[0], outs[1]) if bf16_copy else (outs[0], None)


FOX_BIAS_PIECES = 3


def _fox_cum_kernel(x_ref, wh_ref, wl_ref, bf_ref, tril_ref, place_ref, pc_ref, carry_sc):
    @pl.when(pl.program_id(1) == 0)
    def _():
        carry_sc[...] = jnp.zeros_like(carry_sc)

    x_hi, x_lo = _split2(x_ref[...])
    z = _dot(x_hi, wh_ref[...]) + _dot(x_lo, wh_ref[...]) + _dot(x_hi, wl_ref[...]) + bf_ref[...]
    logf = jnp.minimum(z, 0.0) - jnp.log(1.0 + jnp.exp(-jnp.abs(z)))
    p1, p2, p3 = _split3(logf)
    tril = tril_ref[...]
    c = _dot(tril, p1) + _dot(tril, p2) + _dot(tril, p3) + carry_sc[...]
    carry_sc[...] = c[c.shape[0] - 1:, :]
    pieces = _split3(c * (-LOG2E))
    pc_ref[...] = sum(_dot(pieces[p], place_ref[p]) for p in range(FOX_BIAS_PIECES)).astype(pc_ref.dtype)


def _fox_cum(x3, w_f, b_f, ts=512):
    B, S, D = x3.shape
    H = w_f.shape[1]
    ts = _pick_tile(S, ts)
    w_pad = jnp.zeros((D, LANES), F32).at[:, :H].set(w_f)
    w_hi, w_lo = _split2(w_pad)
    b_pad = jnp.zeros((1, LANES), F32).at[0, :H].set(b_f)
    tril = jnp.asarray(np.tril(np.ones((ts, ts), np.float32)), BF16)
    place = np.zeros((FOX_BIAS_PIECES, LANES, LANES), np.float32)
    for p in range(FOX_BIAS_PIECES):
        for h in range(H):
            place[p, h, FOX_BIAS_PIECES * h + p] = 1.0
    return pl.pallas_call(
        _fox_cum_kernel,
        grid=(B, S // ts),
        in_specs=[pl.BlockSpec((None, ts, D), lambda b, j: (b, j, 0)),
                  pl.BlockSpec((D, LANES), lambda b, j: (0, 0)),
                  pl.BlockSpec((D, LANES), lambda b, j: (0, 0)),
                  pl.BlockSpec((1, LANES), lambda b, j: (0, 0)),
                  pl.BlockSpec((ts, ts), lambda b, j: (0, 0)),
                  pl.BlockSpec((FOX_BIAS_PIECES, LANES, LANES), lambda b, j: (0, 0, 0))],
        out_specs=pl.BlockSpec((None, ts, LANES), lambda b, j: (b, j, 0)),
        out_shape=jax.ShapeDtypeStruct((B, S, LANES), BF16),
        scratch_shapes=[pltpu.VMEM((1, LANES), F32)],
        compiler_params=_cparams("parallel", "arbitrary"),
        name="fox_decay_cumsum",
    )(x3, w_hi, w_lo, b_pad, tril, jnp.asarray(place, BF16))


ONES_ROWS = 16


def _fox_attn_kernel(q_ref, k_ref, v_ref, pc_ref, o_ref, kaug_sc, vt_sc, m_sc, acc_sc, *, tq, nsub, seq, unroll):
    dh = FOX_HEAD_DIM
    h = pl.program_id(1)
    g = pl.program_id(2)

    @pl.when(g == 0)
    def _():
        kaug_sc[:, :dh] = k_ref[...]
        kaug_sc[:, dh:] = pc_ref[...]
        for c in range(seq // tq):
            rows = slice(c * tq, (c + 1) * tq)
            vt_sc[:dh, rows] = v_ref[rows, :].astype(F32).T.astype(BF16)
        vt_sc[dh:, :] = jnp.ones((ONES_ROWS, seq), BF16)

    feat = lax.broadcasted_iota(jnp.int32, (LANES, tq), 0)
    bias_rows = (feat >= FOX_BIAS_PIECES * h) & (feat < FOX_BIAS_PIECES * (h + 1))
    ones_h = jnp.where(bias_rows, 1.0, 0.0).astype(BF16)
    q_aug = [jnp.concatenate([q_ref[a * tq:(a + 1) * tq, :].astype(F32).T.astype(BF16), ones_h], axis=0)
             for a in range(nsub)]
    m_sc[...] = jnp.full_like(m_sc, NEG_BIG)
    acc_sc[...] = jnp.zeros_like(acc_sc)
    causal = (lax.broadcasted_iota(jnp.int32, (tq, tq), 0) <= lax.broadcasted_iota(jnp.int32, (tq, tq), 1))
    first = g * nsub

    def chain(a, tiles, diag_last):
        offs = [pl.multiple_of(j * tq, tq) for j in tiles]
        scores = []
        for off in offs:
            scores.append(_dot(kaug_sc[pl.ds(off, tq), :], q_aug[a]))
            yield None
        for n, (off, s) in enumerate(zip(offs, scores)):
            if diag_last and n == len(offs) - 1:
                s = jnp.where(causal, s, NEG_BIG)
            m_old = m_sc[a]
            m_new = jnp.maximum(m_old, jnp.max(s, axis=0, keepdims=True))
            alpha = jnp.exp2(m_old - m_new)
            pv = _dot(vt_sc[:, pl.ds(off, tq)], jnp.exp2(s - m_new).astype(BF16))
            yield None
            acc_sc[a] = alpha * acc_sc[a] + pv
            m_sc[a] = m_new
        yield None

    def body(jj, c):
        _round_robin([chain(a, [jj * unroll + u for u in range(unroll)], False) for a in range(nsub)])
        return c

    lax.fori_loop(0, first // unroll, body, 0)
    _round_robin([chain(a, [first + t for t in range(a + 1)], True) for a in range(nsub)])
    for a in range(nsub):
        acc = acc_sc[a]
        o_t = acc[:dh] / acc[dh:dh + 1]
        o_ref[a * tq:(a + 1) * tq, :] = o_t.T.astype(o_ref.dtype)


def _fox_attention(proj3, pieces, tq=256, nsub=8):
    B, S, _ = proj3.shape
    H, dh = FOX_HEADS, FOX_HEAD_DIM
    tq = _pick_tile(S, tq)
    nsub = _pick_tile(S // tq, nsub)
    tg = tq * nsub
    return pl.pallas_call(
        functools.partial(_fox_attn_kernel, tq=tq, nsub=nsub, seq=S, unroll=min(nsub, 4)),
        grid=(B, H, S // tg),
        in_specs=[pl.BlockSpec((None, tg, dh), lambda b, h, g: (b, g, h)),
                  pl.BlockSpec((None, S, dh), lambda b, h, g: (b, 0, H + h)),
                  pl.BlockSpec((None, S, dh), lambda b, h, g: (b, 0, 2 * H + h)),
                  pl.BlockSpec((None, S, LANES), lambda b, h, g: (b, 0, 0))],
        out_specs=pl.BlockSpec((None, tg, dh), lambda b, h, g: (b, g, h)),
        out_shape=jax.ShapeDtypeStruct((B, S, H * dh), BF16),
        scratch_shapes=[pltpu.VMEM((S, dh + LANES), BF16),
                        pltpu.VMEM((dh + ONES_ROWS, S), BF16),
                        pltpu.VMEM((nsub, 1, tq), F32),
                        pltpu.VMEM((nsub, dh + ONES_ROWS, tq), F32)],
        compiler_params=_cparams("parallel", "parallel", "arbitrary"),
        name="fox_attention",
    )(proj3, proj3, proj3, pieces)


def _fox_layer(x, xb, w_in, b_f, w_out, ln_g, ln_b, B, S):
    D = D_MODEL
    H, dh = FOX_HEADS, FOX_HEAD_DIM
    scale = dh ** -0.5 * LOG2E
    w_q, w_k, w_v, w_f, w_g = jnp.split(w_in, [H * dh, 2 * H * dh, 3 * H * dh, 3 * H * dh + H], axis=1)
    w_main = jnp.concatenate([w_q * scale, w_k, w_v, w_g], axis=1).astype(BF16)
    proj = _matmul(xb, w_main)
    pieces = _fox_cum(x.reshape(B, S, D), w_f, b_f)
    o = _fox_attention(proj.reshape(B, S, 4 * D), pieces)
    return _outproj_ln(proj, 3, o.reshape(B * S, D), x, w_out, ln_g, ln_b)


def _ret_kernel(q_ref, k_ref, v_ref, cos_ref, sin_ref, dm_ref, xi_ref, zeta_ref, gc_ref, gn_ref,
                o_ref, r_sc):
    @pl.when(pl.program_id(1) == 0)
    def _():
        r_sc[...] = jnp.zeros_like(r_sc)

    dk = RET_HEAD_DIM
    half = dk // 2
    cos = cos_ref[...]
    sin = sin_ref[...]

    def rope(x):
        x1, x2 = x[:, :half], x[:, half:]
        return jnp.concatenate([x1 * cos - x2 * sin, x2 * cos + x1 * sin], axis=-1)

    def head_chain(h):
        cols = slice(h * dk, (h + 1) * dk)
        q = rope(q_ref[:, cols].astype(F32))
        k = rope(k_ref[:, cols].astype(F32)) * (dk ** -0.5)
        v = v_ref[:, cols]
        qb = q.astype(BF16)
        r_old = r_sc[h]
        scores = _dot_nt(qb, k.astype(BF16))
        cross = _dot(qb, r_old.astype(BF16))
        kz = (k * zeta_ref[h]).astype(BF16)
        r_new = _dot_tn(kz, v)
        yield None
        o = _dot((scores * dm_ref[h]).astype(BF16), v)
        r_sc[h] = r_old * gc_ref[h] + r_new
        yield None
        o = o + cross * xi_ref[h]
        o = o * lax.rsqrt(jnp.mean(o * o, axis=-1, keepdims=True) + RMS_EPS) * gn_ref[:, cols]
        o_ref[:, cols] = o.astype(o_ref.dtype)
        yield None

    _round_robin([head_chain(h) for h in range(RET_HEADS)])


def _ret_layer(x, xb, w_in, gn_g, w_out, ln_g, ln_b, B, S, chunk=512):
    D = D_MODEL
    H, dk = RET_HEADS, RET_HEAD_DIM
    C = _pick_tile(S, chunk)
    proj = _matmul(xb, w_in.astype(BF16))
    f32 = np.float32
    inv = (f32(1.0) / (f32(RET_THETA) ** (np.arange(0, dk, 2, dtype=f32) / f32(dk)))).astype(f32)
    ang = (np.arange(S, dtype=f32)[:, None] * inv[None, :]).astype(f32)
    cos, sin = jnp.asarray(np.cos(ang)), jnp.asarray(np.sin(ang))
    log_g = np.log1p(-(f32(2.0) ** (f32(-5.0) - np.arange(H, dtype=f32)))).astype(f32)
    pos = np.arange(C, dtype=f32)
    diff = pos[:, None] - pos[None, :]
    d_mask = jnp.asarray(np.where(diff[None] >= 0, np.exp(np.maximum(diff, 0.0)[None] * log_g[:, None, None]),
                                  0.0).astype(f32))
    xi = jnp.asarray(np.broadcast_to(np.exp((pos[None, :] + 1.0) * log_g[:, None])[:, :, None],
                                     (H, C, dk)).astype(f32))
    zeta = jnp.asarray(np.broadcast_to(np.exp((C - 1.0 - pos[None, :]) * log_g[:, None])[:, :, None],
                                       (H, C, dk)).astype(f32))
    g_c = jnp.asarray(np.broadcast_to(np.exp(f32(C) * log_g)[:, None, None], (H, 1, dk)).astype(f32))
    p3 = proj.reshape(B, S, 4 * D)
    o = pl.pallas_call(
        _ret_kernel,
        grid=(B, S // C),
        in_specs=[pl.BlockSpec((None, C, D), lambda b, c: (b, c, 0)),
                  pl.BlockSpec((None, C, D), lambda b, c: (b, c, 1)),
                  pl.BlockSpec((None, C, D), lambda b, c: (b, c, 2)),
                  pl.BlockSpec((C, dk // 2), lambda b, c: (c, 0)),
                  pl.BlockSpec((C, dk // 2), lambda b, c: (c, 0)),
                  pl.BlockSpec((H, C, C), lambda b, c: (0, 0, 0)),
                  pl.BlockSpec((H, C, dk), lambda b, c: (0, 0, 0)),
                  pl.BlockSpec((H, C, dk), lambda b, c: (0, 0, 0)),
                  pl.BlockSpec((H, 1, dk), lambda b, c: (0, 0, 0)),
                  pl.BlockSpec((1, D), lambda b, c: (0, 0))],
        out_specs=pl.BlockSpec((None, C, D), lambda b, c: (b, c, 0)),
        out_shape=jax.ShapeDtypeStruct((B, S, D), BF16),
        scratch_shapes=[pltpu.VMEM((H, dk, dk), F32)],
        compiler_params=_cparams("parallel", "arbitrary"),
        name="retnet_retention",
    )(p3, p3, p3, cos, sin, d_mask, xi, zeta, g_c, gn_g.reshape(1, D))
    return _outproj_ln(proj, 3, o.reshape(B * S, D), x, w_out, ln_g, ln_b, bf16_copy=False)


DSA_HEADS = 8
DSA_HEAD_DIM = 128
DSA_ROPE_DIM = 32
DSA_KV_RANK = 128
IDX_HEADS = 8
IDX_DIM = 64
IDX_ROPE_DIM = 16
TOPK_MAX = 256
HALF_MIN = -2 ** 15
HALF_ROWS = 16
DSA_VMEM_LIMIT = 56 * 1024 * 1024
DSA_ONES_ROWS = 16


def _rope_perm(width, groups):
    p = np.zeros((width, width), np.float32)
    for start, half in groups:
        for j in range(half):
            p[start + half + j, start + j] = 1.0
            p[start + j, start + half + j] = 1.0
    return p


def _rope_cs(S, width, groups, theta_dims):
    f32 = np.float32
    c = np.ones((S, width), f32)
    sg = np.zeros((S, width), f32)
    pos = np.arange(S, dtype=f32)[:, None]
    for (start, half), rot_dim in zip(groups, theta_dims):
        inv = (f32(1.0) / (f32(ROPE_THETA) ** (np.arange(0, rot_dim, 2, dtype=f32) / f32(rot_dim)))).astype(f32)
        ang = (pos * inv[None, :]).astype(f32)
        cos, sin = np.cos(ang), np.sin(ang)
        c[:, start:start + half] = cos
        c[:, start + half:start + 2 * half] = cos
        sg[:, start:start + half] = -sin
        sg[:, start + half:start + 2 * half] = sin
    return jnp.asarray(c), jnp.asarray(sg)


def _dsa_prep_kernel(q_ref, qi_ref, ckv_ref, misc_ref, cq_ref, sq_ref, ci_ref, si_ref, cm_ref, sm_ref,
                     pq_ref, pi_ref, pm_ref, selk_ref, selw_ref, wuk_ref, kvg_ref,
                     qf_ref, kvl_ref, kvt_ref, qir_ref, kid_ref, wit_ref):
    H, dh = DSA_HEADS, DSA_HEAD_DIM
    lane = lax.broadcasted_iota(jnp.int32, (1, LANES), 1)
    rope_lanes = lane < DSA_ROPE_DIM
    cq, sq = cq_ref[...], sq_ref[...]
    scale = dh ** -0.5 * LOG2E
    for h in range(H):
        qh = q_ref[:, h * dh:(h + 1) * dh]
        qr = qh.astype(F32) * cq + _dot(qh, pq_ref[...]) * sq
        q_lat = _dot(qr.astype(BF16), wuk_ref[h])
        qf_ref[h, :dh, :] = (q_lat * scale).T.astype(BF16)
        qf_ref[h, dh:, :] = jnp.where(rope_lanes, qr * scale, 0.0).T.astype(BF16)
    ci, si = ci_ref[...], si_ref[...]
    for g in range(IDX_HEADS * IDX_DIM // LANES):
        qg = qi_ref[:, g * LANES:(g + 1) * LANES]
        qr = qg.astype(F32) * ci + _dot(qg, pi_ref[...]) * si
        qir_ref[g * LANES:(g + 1) * LANES, :] = (qr * (IDX_DIM ** -0.5)).T.astype(BF16)
    ckv = ckv_ref[...].astype(F32)
    ckv = ckv * lax.rsqrt(jnp.mean(ckv * ckv, axis=-1, keepdims=True) + RMS_EPS) * kvg_ref[...]
    misc = misc_ref[...]
    mr = (misc.astype(F32) * cm_ref[...] + _dot(misc, pm_ref[...]) * sm_ref[...])
    kvl_ref[:, :DSA_KV_RANK] = ckv.astype(BF16)
    kvl_ref[:, DSA_KV_RANK:] = jnp.where(rope_lanes, mr, 0.0).astype(BF16)
    kvt_ref[:DSA_KV_RANK, :] = ckv.T.astype(BF16)
    kvt_ref[DSA_KV_RANK:, :] = jnp.ones((DSA_ONES_ROWS, ckv.shape[0]), BF16)
    kid_ref[...] = _dot(mr.astype(BF16), selk_ref[...]).astype(BF16)
    wi = _dot(misc, selw_ref[...]) * (IDX_HEADS ** -0.5)
    wit_ref[...] = wi.T[:IDX_HEADS, :]


def _sort_key(x):
    b = pltpu.bitcast(x, jnp.int32)
    return jnp.where(b < 0, b ^ 0x7FFFFFFF, b)


def _dsa_main_kernel(qf_ref, kvl_ref, kvt_ref, qi_ref, kid_ref, wit_ref, wuv_ref, obuf_ref, o_ref,
                     key_sc, hi_sc, lo_sc, low_sc, m_sc, acc_sc, *, tq, tk, k_sel, seq, qtile):
    del obuf_ref
    H = DSA_HEADS
    q0 = qtile * tq
    nj = (q0 + tq + tk - 1) // tk
    qpos = q0 + lax.broadcasted_iota(jnp.int32, (tk, tq), 1)
    kpos0 = lax.broadcasted_iota(jnp.int32, (tk, tq), 0)
    feat = lax.broadcasted_iota(jnp.int32, (LANES, 1), 0)
    neg_inf_key = _sort_key(jnp.full((1, 1), -jnp.inf, F32))

    wit = wit_ref[...]
    qi_heads = []
    for h in range(IDX_HEADS):
        g = qi_ref[(h // 2) * LANES:(h // 2 + 1) * LANES, :]
        keep = (feat >= IDX_DIM) if (h % 2) else (feat < IDX_DIM)
        qi_heads.append(jnp.where(keep, g, jnp.zeros_like(g)))

    def score_tile(j, c, causal):
        off = pl.multiple_of(j * tk, tk)
        ki = kid_ref[pl.ds(off, tk), :]
        scores = [_dot(ki, qi_heads[h]) for h in range(IDX_HEADS)]
        isc = jnp.zeros((tk, tq), F32)
        for h in range(IDX_HEADS):
            isc = isc + jnp.maximum(scores[h], 0.0) * wit[h:h + 1, :]
        isc = isc + 0.0
        if causal:
            isc = jnp.where(kpos0 + off <= qpos, isc, -jnp.inf)
        key = _sort_key(isc)
        key_sc[pl.ds(off, tk), :] = key
        hi_sc[pl.ds(off, tk), :] = (key >> 16).astype(jnp.int16)
        lo_sc[pl.ds(off, tk), :] = ((key & 0xFFFF) + HALF_MIN).astype(jnp.int16)
        return c

    n_clear = min(q0 // tk, nj)
    lax.fori_loop(0, n_clear, functools.partial(score_tile, causal=False), 0)
    lax.fori_loop(n_clear, nj, functools.partial(score_tile, causal=True), 0)

    def count(pred_fn):
        acc = jnp.zeros((8, tq), jnp.int32)
        for j in range(nj):
            hit = jnp.where(pred_fn(key_sc[j * tk:(j + 1) * tk, :], j * tk), 1, 0)
            acc = acc + jnp.sum(hit.reshape(tk // 8, 8, tq), axis=0)
        return jnp.sum(acc, axis=0, keepdims=True)

    rows16 = tk // HALF_ROWS

    def count16(ref, pred_fn):
        accs = [jnp.zeros((HALF_ROWS, tq), jnp.int16) for _ in range(2)]
        for j in range(nj):
            hit = jnp.where(pred_fn(ref[j * tk:(j + 1) * tk, :].reshape(rows16, HALF_ROWS, tq)),
                            jnp.int16(1), jnp.int16(0))
            for r in range(rows16):
                accs[r % 2] = accs[r % 2] + hit[r]
        return jnp.sum((accs[0] + accs[1]).astype(jnp.int32), axis=0, keepdims=True)

    def as_half(v):
        return jnp.broadcast_to(v.astype(jnp.int16), (HALF_ROWS, tq))[None]

    def search16(ref, base0, cnt0, rej0, bits, want):
        def bit_step(t, carry):
            base, cnt_b, cnt_r = carry
            cand = base + lax.shift_left(jnp.int32(1), bits - 1 - t)
            cand16 = as_half(cand)
            c = count16(ref, lambda kt: kt >= cand16)
            ok = c >= want
            return jnp.where(ok, cand, base), jnp.where(ok, c, cnt_b), jnp.where(ok, cnt_r, c)
        return lax.fori_loop(0, bits, bit_step, (base0, cnt0, rej0))

    zero16 = as_half(jnp.zeros((1, tq), jnp.int32))
    cnt_pos = count16(hi_sc, lambda kt: kt >= zero16)
    nonneg = cnt_pos >= k_sel
    t1, ge_hi, above = search16(hi_sc, jnp.where(nonneg, 0, HALF_MIN), jnp.where(nonneg, cnt_pos, nj * tk),
                                jnp.where(nonneg, 0, cnt_pos), 15, k_sel)
    t1_16 = as_half(t1)

    for j in range(nj):
        rows = slice(j * tk, (j + 1) * tk)
        hi = hi_sc[rows, :].reshape(rows16, HALF_ROWS, tq)
        lo = lo_sc[rows, :].reshape(rows16, HALF_ROWS, tq)
        low_sc[rows, :] = jnp.where(hi == t1_16, lo, jnp.int16(HALF_MIN)).reshape(tk, tq)
    t2, ge_low, gt_low = search16(low_sc, jnp.full((1, tq), HALF_MIN, jnp.int32), ge_hi - above,
                                  jnp.zeros((1, tq), jnp.int32), 16, k_sel - above)
    thr = lax.shift_left(t1, 16) | (t2 - HALF_MIN)
    n_ge = above + ge_low
    n_gt = above + gt_low
    need = k_sel - n_gt
    excess = ((n_ge - n_gt) > need) & (thr > neg_inf_key)
    any_excess = jnp.max(jnp.where(excess, 1, 0)) > 0

    def tie_cut():
        def step(t, lo):
            cand = lo + lax.shift_left(jnp.int32(1), int(math.log2(seq)) - t)
            c = count(lambda kt, off: (kt == thr) & (kpos0 + off < cand))
            return jnp.where(c < need, cand, lo)
        lo = lax.fori_loop(0, int(math.log2(seq)) + 1, step, jnp.zeros((1, tq), jnp.int32))
        return jnp.where(excess, lo, seq)

    cut = lax.cond(any_excess, tie_cut, lambda: jnp.full((1, tq), seq, jnp.int32))

    m_sc[...] = jnp.full_like(m_sc, NEG_BIG)
    acc_sc[...] = jnp.zeros_like(acc_sc)

    def attn_tile(j, c):
        off = pl.multiple_of(j * tk, tk)
        kt = key_sc[pl.ds(off, tk), :]
        kpos = kpos0 + off
        bias = jnp.where(kt > thr, 0.0, jnp.where(kt == thr, jnp.where(kpos <= cut, 0.0, NEG_BIG), NEG_BIG))
        bias = jnp.where(kpos <= qpos, bias, NEG_BIG)
        kv = kvl_ref[pl.ds(off, tk), :]
        kvt = kvt_ref[:, pl.ds(off, tk)]

        def head_step(h, ql):
            s = _dot(kv, qf_ref[h, :, ql]) + bias[:, ql]
            yield None
            m_old = m_sc[h, :, ql]
            m_new = jnp.maximum(m_old, jnp.max(s, axis=0, keepdims=True))
            alpha = jnp.exp2(m_old - m_new)
            pv = _dot(kvt, jnp.exp2(s - m_new).astype(BF16))
            yield None
            acc_sc[h, :, ql] = alpha * acc_sc[h, :, ql] + pv
            m_sc[h, :, ql] = m_new
            yield None

        halves = [slice(0, tq // 2), slice(tq // 2, tq)]
        _round_robin([head_step(h, ql) for h in range(H) for ql in halves])
        return c

    lax.fori_loop(0, nj, attn_tile, 0)
    for h in range(H):
        acc = acc_sc[h]
        o_lat_t = (acc[:DSA_KV_RANK] / acc[DSA_KV_RANK:DSA_KV_RANK + 1]).astype(BF16)
        o_ref[:, h * DSA_HEAD_DIM:(h + 1) * DSA_HEAD_DIM] = _dot_tn(o_lat_t, wuv_ref[h]).astype(o_ref.dtype)


def _dsa_layer(x, xb, w_in, kv_norm_g, w_uk, w_uv, w_out, ln_g, ln_b, B, S, tq=512, tk=512):
    D = D_MODEL
    H, dh, dr, dc = DSA_HEADS, DSA_HEAD_DIM, DSA_ROPE_DIM, DSA_KV_RANK
    HI, di = IDX_HEADS, IDX_DIM
    w_q, w_ckv, w_kr, w_qi, w_ki, w_wi, w_g = jnp.split(
        w_in, np.cumsum([H * dh, dc, dr, HI * di, di, HI]).tolist(), axis=1)
    w_misc = jnp.concatenate([w_kr, w_ki, w_wi, jnp.zeros((D, LANES - dr - di - HI), F32)], axis=1)
    w_main = jnp.concatenate([w_q, w_g, w_qi, w_ckv, w_misc], axis=1).astype(BF16)
    n_main = w_main.shape[1]
    proj = _matmul(xb, w_main, tm=1024, tn=n_main)
    c_q, c_qi, c_ckv, c_misc = 0, 2 * D // LANES, (2 * D + HI * di) // LANES, (2 * D + HI * di + dc) // LANES

    q_groups = [(0, dr // 2)]
    i_groups = [(0, IDX_ROPE_DIM // 2), (di, IDX_ROPE_DIM // 2)]
    m_groups = [(0, dr // 2), (dr, IDX_ROPE_DIM // 2)]
    cq, sq = _rope_cs(S, LANES, q_groups, [dr])
    ci, si = _rope_cs(S, LANES, i_groups, [IDX_ROPE_DIM, IDX_ROPE_DIM])
    cm, sm = _rope_cs(S, LANES, m_groups, [dr, IDX_ROPE_DIM])
    pq = jnp.asarray(_rope_perm(LANES, q_groups), BF16)
    pi = jnp.asarray(_rope_perm(LANES, i_groups), BF16)
    pm = jnp.asarray(_rope_perm(LANES, m_groups), BF16)
    selk = np.zeros((LANES, LANES), np.float32)
    for j in range(di):
        selk[dr + j, j] = 1.0
        selk[dr + j, di + j] = 1.0
    selw = np.zeros((LANES, LANES), np.float32)
    for j in range(HI):
        selw[dr + di + j, j] = 1.0
    wuk = jnp.concatenate([jnp.zeros((H, dr, dc), F32), jnp.transpose(w_uk, (0, 2, 1))], axis=1).astype(BF16)

    ts = _pick_tile(S, 512)
    p3 = proj.reshape(B, S, n_main)
    tab = lambda: pl.BlockSpec((ts, LANES), lambda b, j: (j, 0))
    mat = lambda: pl.BlockSpec((LANES, LANES), lambda b, j: (0, 0))
    qf, kvl, kvt, qir, kid, wit = pl.pallas_call(
        _dsa_prep_kernel,
        grid=(B, S // ts),
        in_specs=[pl.BlockSpec((None, ts, H * dh), lambda b, j: (b, j, 0)),
                  pl.BlockSpec((None, ts, HI * di), lambda b, j: (b, j, c_qi * LANES // (HI * di))),
                  pl.BlockSpec((None, ts, dc), lambda b, j: (b, j, c_ckv)),
                  pl.BlockSpec((None, ts, LANES), lambda b, j: (b, j, c_misc)),
                  tab(), tab(), tab(), tab(), tab(), tab(),
                  mat(), mat(), mat(), mat(), mat(),
                  pl.BlockSpec((H, LANES, dc), lambda b, j: (0, 0, 0)),
                  pl.BlockSpec((1, dc), lambda b, j: (0, 0))],
        out_specs=[pl.BlockSpec((None, H, 2 * dc, ts), lambda b, j: (b, 0, 0, j)),
                   pl.BlockSpec((None, ts, 2 * dc), lambda b, j: (b, j, 0)),
                   pl.BlockSpec((None, dc + DSA_ONES_ROWS, ts), lambda b, j: (b, 0, j)),
                   pl.BlockSpec((None, HI * di, ts), lambda b, j: (b, 0, j)),
                   pl.BlockSpec((None, ts, LANES), lambda b, j: (b, j, 0)),
                   pl.BlockSpec((None, HI, ts), lambda b, j: (b, 0, j))],
        out_shape=[jax.ShapeDtypeStruct((B, H, 2 * dc, S), BF16),
                   jax.ShapeDtypeStruct((B, S, 2 * dc), BF16),
                   jax.ShapeDtypeStruct((B, dc + DSA_ONES_ROWS, S), BF16),
                   jax.ShapeDtypeStruct((B, HI * di, S), BF16),
                   jax.ShapeDtypeStruct((B, S, LANES), BF16),
                   jax.ShapeDtypeStruct((B, HI, S), F32)],
        compiler_params=_cparams("parallel", "parallel"),
        name="dsa_prep",
    )(p3, p3, p3, p3, cq, sq, ci, si, cm, sm, pq, pi, pm,
      jnp.asarray(selk, BF16), jnp.asarray(selw, BF16), wuk, kv_norm_g.reshape(1, dc))

    tq = _pick_tile(S, tq)
    tk = _pick_tile(S, tk)
    k_sel = min(TOPK_MAX, S // 4)
    wuv = w_uv.astype(BF16)
    o = jnp.zeros((B, S, H * dh), BF16)
    for i in range(S // tq):
        nkeys = -(-((i + 1) * tq) // tk) * tk
        o = pl.pallas_call(
            functools.partial(_dsa_main_kernel, tq=tq, tk=tk, k_sel=k_sel, seq=S, qtile=i),
            grid=(B,),
            in_specs=[pl.BlockSpec((None, H, 2 * dc, tq), lambda b, i=i: (b, 0, 0, i)),
                      pl.BlockSpec((None, nkeys, 2 * dc), lambda b: (b, 0, 0)),
                      pl.BlockSpec((None, dc + DSA_ONES_ROWS, nkeys), lambda b: (b, 0, 0)),
                      pl.BlockSpec((None, HI * di, tq), lambda b, i=i: (b, 0, i)),
                      pl.BlockSpec((None, nkeys, LANES), lambda b: (b, 0, 0)),
                      pl.BlockSpec((None, HI, tq), lambda b, i=i: (b, 0, i)),
                      pl.BlockSpec((H, dc, dh), lambda b: (0, 0, 0)),
                      pl.BlockSpec(memory_space=pl.ANY)],
            out_specs=pl.BlockSpec((None, tq, H * dh), lambda b, i=i: (b, i, 0)),
            out_shape=jax.ShapeDtypeStruct((B, S, H * dh), BF16),
            input_output_aliases={7: 0},
            scratch_shapes=[pltpu.VMEM((nkeys, tq), jnp.int32),
                            pltpu.VMEM((nkeys, tq), jnp.int16), pltpu.VMEM((nkeys, tq), jnp.int16),
                            pltpu.VMEM((nkeys, tq), jnp.int16),
                            pltpu.VMEM((H, 1, tq), F32),
                            pltpu.VMEM((H, dc + DSA_ONES_ROWS, tq), F32)],
            compiler_params=_cparams("parallel", vmem_limit=DSA_VMEM_LIMIT),
            name=f"dsa_select_attention_q{i}",
        )(qf, kvl, kvt, qir, kid, wit, wuv, o)
    return _outproj_ln(proj, 1, o.reshape(B * S, D), x, w_out, ln_g, ln_b, bf16_copy=False)


RWKV_HEADS = 16
RWKV_HEAD_DIM = 64
RWKV_GN_EPS = 64e-5
RWKV_CHUNK = 64
RWKV_SUB = 16


def _group_sum(x, gmat, split=True):
    outs = []
    for c in range(x.shape[1] // LANES):
        xc = x[:, c * LANES:(c + 1) * LANES]
        if split:
            hi, lo = _split2(xc)
            outs.append(_dot(hi, gmat) + _dot(lo, gmat))
        else:
            outs.append(_dot(xc.astype(BF16), gmat))
    return outs[0] if len(outs) == 1 else jnp.concatenate(outs, axis=1)


def _softplus(y):
    return jnp.maximum(y, 0.0) + jnp.log(1.0 + jnp.exp(-jnp.abs(y)))


def _rwkv_proj_kernel(x_ref, xprev_ref, mu_ref, wr_ref, wk_ref, wv_ref, wg_ref, wla_ref, wlb_ref,
                      ala_ref, alb_ref, w0_ref, a0_ref, kk_ref, ka_ref, rk_ref, gmat_ref, tril_ref,
                      r_ref, k_ref, v_ref, g_ref, kap_ref, b_ref, cum_ref, bonus_ref):
    x = x_ref[...]
    ts = x.shape[0]
    prev = jnp.where(pl.program_id(1) == 0, 0.0, xprev_ref[7:8, :])
    rowid = lax.broadcasted_iota(jnp.int32, (ts, 1), 0)
    xx = jnp.where(rowid == 0, prev, pltpu.roll(x, 1, 0)) - x

    def mixed(i):
        return (x + xx * mu_ref[i:i + 1, :]).astype(BF16)

    r = _dot(mixed(0), wr_ref[...])
    k = _dot(mixed(2), wk_ref[...])
    v = _dot(mixed(3), wv_ref[...])
    g_ref[...] = _dot(mixed(5), wg_ref[...]).astype(g_ref.dtype)
    lora_w = _dot(jnp.tanh(_dot(mixed(1), wla_ref[...])).astype(BF16), wlb_ref[...])
    lora_a = _dot(_dot(mixed(4), ala_ref[...]).astype(BF16), alb_ref[...])
    w_log = -_softplus(-(w0_ref[...] + lora_w)) - 0.5
    lw = -jnp.exp(w_log)
    tril = tril_ref[...]
    for c in range(ts // RWKV_CHUNK):
        rows = slice(c * RWKV_CHUNK, (c + 1) * RWKV_CHUNK)
        cum_ref[rows, :] = sum(_dot(tril, piece) for piece in _split2(lw[rows, :]))
    a = _sigmoid(a0_ref[...] + lora_a)
    gmat = gmat_ref[...]
    kk = k * kk_ref[...]
    kap = kk * lax.rsqrt(_group_sum(kk * kk, gmat, split=False) + 1e-12)
    k2 = k * (1.0 + (a - 1.0) * ka_ref[...])
    bonus_ref[...] = _group_sum(r * k2 * rk_ref[...], gmat, split=False) * v
    r_ref[...] = r.astype(r_ref.dtype)
    k_ref[...] = k2.astype(k_ref.dtype)
    v_ref[...] = v.astype(v_ref.dtype)
    kap_ref[...] = kap.astype(kap_ref.dtype)
    b_ref[...] = (kap * a).astype(b_ref.dtype)


def _bd(x, left):
    z = jnp.zeros_like(x)
    return jnp.concatenate([jnp.where(left, x, z), jnp.where(left, z, x)], axis=0)


def _unbd(x_bd):
    c = x_bd.shape[0] // 2
    return x_bd[:c] + x_bd[c:]


def _rwkv_chunk_pair(L, r, k, v, kap, b, masks):
    C = RWKV_CHUNK
    left, strict, lower, same_sub, eye, first_row = masks
    Lc = L[C - 1:C, :]
    L_excl = jnp.where(first_row, 0.0, pltpu.roll(L, 1, 0))
    e_l, e_lx, e_nl, e_r = jnp.exp(L), jnp.exp(L_excl), jnp.exp(-L), jnp.exp(Lc - L)
    at = _bd(-kap * e_lx, left).astype(BF16)
    rt = _bd(r * e_l, left)
    bt = _bd(b * e_nl, left).astype(BF16)
    kt = _bd(k * e_nl, left).astype(BF16)
    bh = _bd(b * e_r, left).astype(BF16)
    kh = _bd(k * e_r, left).astype(BF16)
    vb = _bd(v, left).astype(BF16)

    a1 = _dot_nt(jnp.concatenate([at, rt.astype(BF16)], axis=0), jnp.concatenate([bt, kt], axis=0))
    yield None
    n = jnp.where(strict, a1[:2 * C, :2 * C], 0.0)
    ak = jnp.where(strict, a1[:2 * C, 2 * C:], 0.0).astype(BF16)
    rb = jnp.where(lower, a1[2 * C:, :2 * C], 0.0).astype(BF16)
    rk = jnp.where(lower, a1[2 * C:, 2 * C:], 0.0).astype(BF16)

    W = 2 * C
    nd = jnp.where(same_sub, n, 0.0)
    no = (n - nd).astype(BF16)
    ndb = nd.astype(BF16)
    n2 = _dot(ndb, ndb)
    akv = _dot(ak, vb)
    yield None
    n2b = n2.astype(BF16)
    t0 = eye + nd
    r = _dot(n2b, jnp.concatenate([n2b, t0.astype(BF16)], axis=1))
    yield None
    n4b = r[:, :W].astype(BF16)
    t01 = t0 + r[:, W:]
    r = _dot(n4b, jnp.concatenate([n4b, t01.astype(BF16)], axis=1))
    yield None
    u = t01 + r[:, W:]
    td = u + _dot(r[:, :W].astype(BF16), u.astype(BF16))
    yield None
    tdb = td.astype(BF16)
    x1 = _dot(tdb, no)
    yield None
    x1b = x1.astype(BF16)
    r = _dot(x1b, jnp.concatenate([x1b, tdb], axis=1))
    yield None
    w = td + r[:, W:]
    t = (w + _dot(r[:, :W].astype(BF16), w.astype(BF16))).astype(BF16)
    yield None
    pq = _dot(t, jnp.concatenate([at, akv.astype(BF16)], axis=1)).astype(BF16)
    yield None
    z = jnp.concatenate([pq, jnp.concatenate([jnp.zeros_like(vb), vb], axis=1)], axis=0)
    ry = _dot(jnp.concatenate([rb, rk], axis=1), z)
    mg = _dot_tn(z, jnp.concatenate([bh, kh], axis=0))
    yield None
    rp = rt + ry[:, :2 * C]
    yl = ry[:, 2 * C:]
    yield _unbd(rp), _unbd(yl), _unbd(mg[:2 * C]), _unbd(mg[2 * C:]), jnp.exp(Lc)


def _rwkv_chunk_kernel(cum_ref, r_ref, k_ref, v_ref, kap_ref, b_ref,
                       rp_ref, yl_ref, mm_ref, gg_ref, gam_ref, *, pairs, cpb):
    C = RWKV_CHUNK
    lane = lax.broadcasted_iota(jnp.int32, (1, LANES), 1)
    left = lane < RWKV_HEAD_DIM
    ri = lax.broadcasted_iota(jnp.int32, (2 * C, 2 * C), 0)
    ci = lax.broadcasted_iota(jnp.int32, (2 * C, 2 * C), 1)
    same_head = (ri // C) == (ci // C)
    strict = same_head & ((ri % C) > (ci % C))
    lower = same_head & ((ri % C) >= (ci % C))
    same_sub = (ri // RWKV_SUB) == (ci // RWKV_SUB)
    eye = jnp.where(ri == ci, 1.0, 0.0).astype(F32)
    first_row = lax.broadcasted_iota(jnp.int32, (C, 1), 0) == 0
    masks = (left, strict, lower, same_sub, eye, first_row)
    jobs = [(c, slice(c * C, (c + 1) * C), slice(p * LANES, (p + 1) * LANES))
            for c in range(cpb) for p in range(pairs)]
    results = _round_robin([
        _rwkv_chunk_pair(cum_ref[rows, sl], r_ref[rows, sl].astype(F32), k_ref[rows, sl].astype(F32),
                         v_ref[rows, sl].astype(F32), kap_ref[rows, sl].astype(F32),
                         b_ref[rows, sl].astype(F32), masks)
        for _, rows, sl in jobs])
    for (c, rows, sl), (rp, yl, mm, gg, gam) in zip(jobs, results):
        rp_ref[rows, sl] = rp.astype(rp_ref.dtype)
        yl_ref[rows, sl] = yl
        mm_ref[c, :, sl] = mm.astype(mm_ref.dtype)
        gg_ref[c, :, sl] = gg
        gam_ref[c, :, sl] = gam


def _rwkv_seq_kernel(rp_ref, yl_ref, mm_ref, gg_ref, gam_ref, bonus_ref, gmat_ref, gng_ref, gnb_ref,
                     o_ref, s_sc, y_sc, *, pairs, cb):
    C = RWKV_CHUNK
    lane = lax.broadcasted_iota(jnp.int32, (1, LANES), 1)
    left = lane < RWKV_HEAD_DIM

    @pl.when(pl.program_id(2) == 0)
    def _():
        s_sc[...] = jnp.zeros_like(s_sc)

    states = [s_sc[p] for p in range(pairs)]
    for c in range(cb):
        rows = slice(c * C, (c + 1) * C)
        for p in range(pairs):
            sl = slice(p * LANES, (p + 1) * LANES)
            s = states[p]
            sb = s.astype(BF16)
            y_sc[rows, sl] = _dot_nt(rp_ref[rows, sl], sb) + yl_ref[rows, sl]
            mm = _bd(mm_ref[c, :, sl], left)
            gg = _bd(gg_ref[c, :, sl], left)
            states[p] = s * gam_ref[c, :, sl] + _dot(sb, mm) + gg
    for p in range(pairs):
        s_sc[p] = states[p]

    y = y_sc[...]
    gmat = gmat_ref[...]
    inv_n = 1.0 / RWKV_HEAD_DIM
    yc = y - _group_sum(y, gmat) * inv_n
    var = _group_sum(yc * yc, gmat) * inv_n
    yn = yc * lax.rsqrt(var + RWKV_GN_EPS) * gng_ref[...] + gnb_ref[...]
    o_ref[...] = (yn + bonus_ref[...]).astype(o_ref.dtype)


def _rwkv_layer(x, mu, w_in, w0, w_lora_a, w_lora_b, a0, a_lora_a, a_lora_b, k_k, k_a, r_k,
                gn_g, gn_b, w_out, ln_g, ln_b, B, S, ts=512, pairs=8, seq_pairs=8, chunks_per_step=2):
    D = D_MODEL
    C = RWKV_CHUNK
    nc = S // C
    ts = _pick_tile(S, ts)
    tril = jnp.asarray(np.tril(np.ones((C, C), np.float32)), BF16)
    w_r, w_k, w_v, w_g =[w.astype(BF16) for w in jnp.split(w_in, 4, axis=1)]
    gmat = jnp.asarray(np.kron(np.eye(2, dtype=np.float32), np.ones((RWKV_HEAD_DIM, RWKV_HEAD_DIM), np.float32)), BF16)
    row = lambda a: a.reshape(1, D)
    x3 = x.reshape(B, S, D)
    full = lambda shape: pl.BlockSpec(shape, lambda b, j: (0,) * len(shape))
    tile = lambda: pl.BlockSpec((None, ts, D), lambda b, j: (b, j, 0))
    lr = w_lora_a.shape[1]
    outs = pl.pallas_call(
        _rwkv_proj_kernel,
        grid=(B, S // ts),
        in_specs=[tile(),
                  pl.BlockSpec((None, 8, D), lambda b, j: (b, jnp.maximum(j * (ts // 8) - 1, 0), 0)),
                  full((6, D)), full((D, D)), full((D, D)), full((D, D)), full((D, D)),
                  full((D, lr)), full((lr, D)), full((D, lr)), full((lr, D)),
                  full((1, D)), full((1, D)), full((1, D)), full((1, D)), full((1, D)),
                  full((LANES, LANES)), full((C, C))],
        out_specs=[tile() for _ in range(8)],
        out_shape=[jax.ShapeDtypeStruct((B, S, D), dt) for dt in (BF16, BF16, BF16, BF16, BF16, BF16, F32, F32)],
        compiler_params=_cparams("parallel", "arbitrary"),
        name="rwkv_projections",
    )(x3, x3, mu, w_r, w_k, w_v, w_g, w_lora_a.astype(BF16), w_lora_b.astype(BF16),
      a_lora_a.astype(BF16), a_lora_b.astype(BF16), row(w0), row(a0), row(k_k), row(k_a), row(r_k), gmat, tril)
    r, k2, v, g, kap, bvec, cum, bonus = outs

    pw = pairs * LANES
    cpb = _pick_tile(nc, chunks_per_step)
    cblk = lambda: pl.BlockSpec((None, cpb * C, pw), lambda b, c, q: (b, c, q))
    sblk = lambda: pl.BlockSpec((None, cpb, C, pw), lambda b, c, q: (b, c, 0, q))
    rp, yl, mm, gg, gam = pl.pallas_call(
        functools.partial(_rwkv_chunk_kernel, pairs=pairs, cpb=cpb),
        grid=(B, nc // cpb, D // pw),
        in_specs=[cblk() for _ in range(6)],
        out_specs=[cblk(), cblk(), sblk(), sblk(),
                   pl.BlockSpec((None, cpb, 1, pw), lambda b, c, q: (b, c, 0, q))],
        out_shape=[jax.ShapeDtypeStruct((B, S, D), BF16), jax.ShapeDtypeStruct((B, S, D), F32),
                   jax.ShapeDtypeStruct((B, nc, C, D), BF16), jax.ShapeDtypeStruct((B, nc, C, D), F32),
                   jax.ShapeDtypeStruct((B, nc, 1, D), F32)],
        compiler_params=_cparams("parallel", "parallel", "parallel"),
        name="rwkv_chunk_summaries",
    )(cum, r, k2, v, kap, bvec)

    cb = _pick_tile(nc, 8)
    pairs = seq_pairs
    pw = pairs * LANES
    o = pl.pallas_call(
        functools.partial(_rwkv_seq_kernel, pairs=pairs, cb=cb),
        grid=(B, D // pw, nc // cb),
        in_specs=[pl.BlockSpec((None, cb * C, pw), lambda b, q, j: (b, j, q)),
                  pl.BlockSpec((None, cb * C, pw), lambda b, q, j: (b, j, q)),
                  pl.BlockSpec((None, cb, C, pw), lambda b, q, j: (b, j, 0, q)),
                  pl.BlockSpec((None, cb, C, pw), lambda b, q, j: (b, j, 0, q)),
                  pl.BlockSpec((None, cb, 1, pw), lambda b, q, j: (b, j, 0, q)),
                  pl.BlockSpec((None, cb * C, pw), lambda b, q, j: (b, j, q)),
                  pl.BlockSpec((LANES, LANES), lambda b, q, j: (0, 0)),
                  pl.BlockSpec((1, pw), lambda b, q, j: (0, q)),
                  pl.BlockSpec((1, pw), lambda b, q, j: (0, q))],
        out_specs=pl.BlockSpec((None, cb * C, pw), lambda b, q, j: (b, j, q)),
        out_shape=jax.ShapeDtypeStruct((B, S, D), BF16),
        scratch_shapes=[pltpu.VMEM((pairs, 2 * C, LANES), F32), pltpu.VMEM((cb * C, pw), F32)],
        compiler_params=_cparams("parallel", "parallel", "arbitrary"),
        name="rwkv_state_scan",
    )(rp, yl, mm, gg, gam, bonus, gmat, row(gn_g), row(gn_b))
    return _outproj_ln(g.reshape(B * S, D), 0, o.reshape(B * S, D), x, w_out, ln_g, ln_b)


def kernel(x, ln_g, ln_b, fox_w_in, fox_b_f, fox_w_out, dsa_w_in, dsa_kv_norm_g, dsa_w_uk, dsa_w_uv, dsa_w_out, rwkv_mu, rwkv_w_in, rwkv_w0, rwkv_w_lora_a, rwkv_w_lora_b, rwkv_a0, rwkv_a_lora_a, rwkv_a_lora_b, rwkv_k_k, rwkv_k_a, rwkv_r_k, rwkv_gn_g, rwkv_gn_b, rwkv_w_out, ret_w_in, ret_gn_g, ret_w_out):
    B, S, D = x.shape
    h = x.reshape(B * S, D)
    h, hb = _fox_layer(h, h, fox_w_in, fox_b_f, fox_w_out, ln_g[0], ln_b[0], B, S)
    h, hb = _dsa_layer(h, hb, dsa_w_in, dsa_kv_norm_g, dsa_w_uk, dsa_w_uv, dsa_w_out, ln_g[1], ln_b[1], B, S)
    h, hb = _rwkv_layer(h, rwkv_mu, rwkv_w_in, rwkv_w0, rwkv_w_lora_a, rwkv_w_lora_b, rwkv_a0,
                        rwkv_a_lora_a, rwkv_a_lora_b, rwkv_k_k, rwkv_k_a, rwkv_r_k, rwkv_gn_g, rwkv_gn_b,
                        rwkv_w_out, ln_g[2], ln_b[2], B, S)
    h, hb = _ret_layer(h, hb, ret_w_in, ret_gn_g, ret_w_out, ln_g[3], ln_b[3], B, S)
    return h.reshape(B, S, D)
```

```python
import functools
import math

import jax
import jax.numpy as jnp
import numpy as np
from jax import lax
from jax.experimental import pallas as pl
from jax.experimental.pallas import tpu as pltpu

F32 = jnp.float32
BF16 = jnp.bfloat16

D_MODEL = 1024
DEPTH = 4
LN_EPS = 1e-5
RMS_EPS = 1e-6
DN_ALPHA = (2 * DEPTH) ** 0.25
ROPE_THETA = 500000.0

FOX_HEADS = 8
FOX_HEAD_DIM = 128

RET_HEADS = 4
RET_HEAD_DIM = 256
RET_THETA = 10000.0

LANES = 128
VMEM_LIMIT = 48 * 1024 * 1024
NEG_BIG = -2.0 ** 100
LOG2E = 1.4426950408889634


def _cparams(*sem, vmem_limit=VMEM_LIMIT):
    return pltpu.CompilerParams(dimension_semantics=sem, vmem_limit_bytes=vmem_limit)


def _dot(a, b):
    return jnp.dot(a, b, preferred_element_type=F32)


def _dot_nt(a, b):
    return lax.dot_general(a, b, (((1,), (1,)), ((), ())), preferred_element_type=F32)


def _dot_tn(a, b):
    return lax.dot_general(a, b, (((0,), (0,)), ((), ())), preferred_element_type=F32)


def _split2(x):
    hi = x.astype(BF16)
    lo = (x - hi.astype(F32)).astype(BF16)
    return hi, lo


def _split3(x):
    p1 = x.astype(BF16)
    r1 = x - p1.astype(F32)
    p2 = r1.astype(BF16)
    p3 = (r1 - p2.astype(F32)).astype(BF16)
    return p1, p2, p3


def _sigmoid(x):
    return 1.0 / (1.0 + jnp.exp(-x))


def _round_robin(gens):
    results = [None] * len(gens)
    live = list(range(len(gens)))
    while live:
        still = []
        for i in live:
            try:
                out = next(gens[i])
            except StopIteration:
                continue
            if out is not None:
                results[i] = out
            still.append(i)
        live = still
    return results


def _pick_tile(n, pref):
    t = min(n, pref)
    while n % t:
        t //= 2
    return t


def _mm_kernel(a_ref, w_ref, o_ref):
    o_ref[...] = _dot(a_ref[...].astype(BF16), w_ref[...]).astype(o_ref.dtype)


def _matmul(a, w, out_dtype=BF16, tm=2048, tn=1024):
    M, K = a.shape
    N = w.shape[1]
    tm = _pick_tile(M, tm)
    if N % tn:
        tn = N
    return pl.pallas_call(
        _mm_kernel,
        grid=(M // tm, N // tn),
        in_specs=[pl.BlockSpec((tm, K), lambda i, j: (i, 0)),
                  pl.BlockSpec((K, tn), lambda i, j: (0, j))],
        out_specs=pl.BlockSpec((tm, tn), lambda i, j: (i, j)),
        out_shape=jax.ShapeDtypeStruct((M, N), out_dtype),
        compiler_params=_cparams("parallel", "arbitrary"),
        name="proj_matmul",
    )(a, w)


def _outproj_ln_kernel(g_ref, o_ref, x_ref, w_ref, lg_ref, lb_ref, xo_ref, *maybe_xb_ref):
    half_g = g_ref[...] * 0.5
    h = (half_g * o_ref[...]) * (1.0 + jnp.tanh(half_g))
    z = x_ref[...] + _dot(h.astype(BF16), w_ref[...])
    zc = z - jnp.mean(z, axis=-1, keepdims=True)
    var = jnp.mean(zc * zc, axis=-1, keepdims=True)
    out = zc * lax.rsqrt(var + LN_EPS / DN_ALPHA ** 2) * lg_ref[...] + lb_ref[...]
    xo_ref[...] = out
    for xb_ref in maybe_xb_ref:
        xb_ref[...] = out.astype(BF16)


def _outproj_ln(gate_arr, gate_col, o, x, w_out, ln_g, ln_b, tm=1024, bf16_copy=True):
    M, D = x.shape
    tm = _pick_tile(M, tm)
    n_out = 2 if bf16_copy else 1
    outs = pl.pallas_call(
        _outproj_ln_kernel,
        grid=(M // tm,),
        in_specs=[pl.BlockSpec((tm, D), lambda i: (i, gate_col)),
                  pl.BlockSpec((tm, D), lambda i: (i, 0)),
                  pl.BlockSpec((tm, D), lambda i: (i, 0)),
                  pl.BlockSpec((D, D), lambda i: (0, 0)),
                  pl.BlockSpec((1, D), lambda i: (0, 0)),
                  pl.BlockSpec((1, D), lambda i: (0, 0))],
        out_specs=[pl.BlockSpec((tm, D), lambda i: (i, 0)) for _ in range(n_out)],
        out_shape=[jax.ShapeDtypeStruct((M, D), F32), jax.ShapeDtypeStruct((M, D), BF16)][:n_out],
        compiler_params=_cparams("parallel"),
        name="outproj_layernorm",
    )(gate_arr, o, x, (w_out * (1.0 / DN_ALPHA)).astype(BF16), ln_g.reshape(1, D), ln_b.reshape(1, D))
    return (outs[0], outs[1]) if bf16_copy else (outs[0], None)


FOX_BIAS_PIECES = 3


def _fox_cum_kernel(x_ref, wh_ref, wl_ref, bf_ref, tril_ref, place_ref, pc_ref, carry_sc):
    @pl.when(pl.program_id(1) == 0)
    def _():
        carry_sc[...] = jnp.zeros_like(carry_sc)

    x_hi, x_lo = _split2(x_ref[...])
    z = _dot(x_hi, wh_ref[...]) + _dot(x_lo, wh_ref[...]) + _dot(x_hi, wl_ref[...]) + bf_ref[...]
    logf = jnp.minimum(z, 0.0) - jnp.log(1.0 + jnp.exp(-jnp.abs(z)))
    p1, p2, p3 = _split3(logf)
    tril = tril_ref[...]
    c = _dot(tril, p1) + _dot(tril, p2) + _dot(tril, p3) + carry_sc[...]
    carry_sc[...] = c[c.shape[0] - 1:, :]
    pieces = _split3(c * (-LOG2E))
    pc_ref[...] = sum(_dot(pieces[p], place_ref[p]) for p in range(FOX_BIAS_PIECES)).astype(pc_ref.dtype)


def _fox_cum(x3, w_f, b_f, ts=512):
    B, S, D = x3.shape
    H = w_f.shape[1]
    ts = _pick_tile(S, ts)
    w_pad = jnp.zeros((D, LANES), F32).at[:, :H].set(w_f)
    w_hi, w_lo = _split2(w_pad)
    b_pad = jnp.zeros((1, LANES), F32).at[0, :H].set(b_f)
    tril = jnp.asarray(np.tril(np.ones((ts, ts), np.float32)), BF16)
    place = np.zeros((FOX_BIAS_PIECES, LANES, LANES), np.float32)
    for p in range(FOX_BIAS_PIECES):
        for h in range(H):
            place[p, h, FOX_BIAS_PIECES * h + p] = 1.0
    return pl.pallas_call(
        _fox_cum_kernel,
        grid=(B, S // ts),
        in_specs=[pl.BlockSpec((None, ts, D), lambda b, j: (b, j, 0)),
                  pl.BlockSpec((D, LANES), lambda b, j: (0, 0)),
                  pl.BlockSpec((D, LANES), lambda b, j: (0, 0)),
                  pl.BlockSpec((1, LANES), lambda b, j: (0, 0)),
                  pl.BlockSpec((ts, ts), lambda b, j: (0, 0)),
                  pl.BlockSpec((FOX_BIAS_PIECES, LANES, LANES), lambda b, j: (0, 0, 0))],
        out_specs=pl.BlockSpec((None, ts, LANES), lambda b, j: (b, j, 0)),
        out_shape=jax.ShapeDtypeStruct((B, S, LANES), BF16),
        scratch_shapes=[pltpu.VMEM((1, LANES), F32)],
        compiler_params=_cparams("parallel", "arbitrary"),
        name="fox_decay_cumsum",
    )(x3, w_hi, w_lo, b_pad, tril, jnp.asarray(place, BF16))


ONES_ROWS = 16


def _fox_attn_kernel(q_ref, k_ref, v_ref, pc_ref, o_ref, kaug_sc, vt_sc, m_sc, acc_sc, *, tq, nsub, seq, unroll):
    dh = FOX_HEAD_DIM
    h = pl.program_id(1)
    g = pl.program_id(2)

    @pl.when(g == 0)
    def _():
        kaug_sc[:, :dh] = k_ref[...]
        kaug_sc[:, dh:] = pc_ref[...]
        for c in range(seq // tq):
            rows = slice(c * tq, (c + 1) * tq)
            vt_sc[:dh, rows] = v_ref[rows, :].astype(F32).T.astype(BF16)
        vt_sc[dh:, :] = jnp.ones((ONES_ROWS, seq), BF16)

    feat = lax.broadcasted_iota(jnp.int32, (LANES, tq), 0)
    bias_rows = (feat >= FOX_BIAS_PIECES * h) & (feat < FOX_BIAS_PIECES * (h + 1))
    ones_h = jnp.where(bias_rows, 1.0, 0.0).astype(BF16)
    q_aug = [jnp.concatenate([q_ref[a * tq:(a + 1) * tq, :].astype(F32).T.astype(BF16), ones_h], axis=0)
             for a in range(nsub)]
    m_sc[...] = jnp.full_like(m_sc, NEG_BIG)
    acc_sc[...] = jnp.zeros_like(acc_sc)
    causal = (lax.broadcasted_iota(jnp.int32, (tq, tq), 0) <= lax.broadcasted_iota(jnp.int32, (tq, tq), 1))
    first = g * nsub

    def chain(a, tiles, diag_last):
        offs = [pl.multiple_of(j * tq, tq) for j in tiles]
        scores = []
        for off in offs:
            scores.append(_dot(kaug_sc[pl.ds(off, tq), :], q_aug[a]))
            yield None
        for n, (off, s) in enumerate(zip(offs, scores)):
            if diag_last and n == len(offs) - 1:
                s = jnp.where(causal, s, NEG_BIG)
            m_old = m_sc[a]
            m_new = jnp.maximum(m_old, jnp.max(s, axis=0, keepdims=True))
            alpha = jnp.exp2(m_old - m_new)
            pv = _dot(vt_sc[:, pl.ds(off, tq)], jnp.exp2(s - m_new).astype(BF16))
            yield None
            acc_sc[a] = alpha * acc_sc[a] + pv
            m_sc[a] = m_new
        yield None

    def body(jj, c):
        _round_robin([chain(a, [jj * unroll + u for u in range(unroll)], False) for a in range(nsub)])
        return c

    lax.fori_loop(0, first // unroll, body, 0)
    _round_robin([chain(a, [first + t for t in range(a + 1)], True) for a in range(nsub)])
    for a in range(nsub):
        acc = acc_sc[a]
        o_t = acc[:dh] / acc[dh:dh + 1]
        o_ref[a * tq:(a + 1) * tq, :] = o_t.T.astype(o_ref.dtype)


def _fox_attention(proj3, pieces, tq=256, nsub=8):
    B, S, _ = proj3.shape
    H, dh = FOX_HEADS, FOX_HEAD_DIM
    tq = _pick_tile(S, tq)
    nsub = _pick_tile(S // tq, nsub)
    tg = tq * nsub
    return pl.pallas_call(
        functools.partial(_fox_attn_kernel, tq=tq, nsub=nsub, seq=S, unroll=min(nsub, 4)),
        grid=(B, H, S // tg),
        in_specs=[pl.BlockSpec((None, tg, dh), lambda b, h, g: (b, g, h)),
                  pl.BlockSpec((None, S, dh), lambda b, h, g: (b, 0, H + h)),
                  pl.BlockSpec((None, S, dh), lambda b, h, g: (b, 0, 2 * H + h)),
                  pl.BlockSpec((None, S, LANES), lambda b, h, g: (b, 0, 0))],
        out_specs=pl.BlockSpec((None, tg, dh), lambda b, h, g: (b, g, h)),
        out_shape=jax.ShapeDtypeStruct((B, S, H * dh), BF16),
        scratch_shapes=[pltpu.VMEM((S, dh + LANES), BF16),
                        pltpu.VMEM((dh + ONES_ROWS, S), BF16),
                        pltpu.VMEM((nsub, 1, tq), F32),
                        pltpu.VMEM((nsub, dh + ONES_ROWS, tq), F32)],
        compiler_params=_cparams("parallel", "parallel", "arbitrary"),
        name="fox_attention",
    )(proj3, proj3, proj3, pieces)


def _fox_layer(x, xb, w_in, b_f, w_out, ln_g, ln_b, B, S):
    D = D_MODEL
    H, dh = FOX_HEADS, FOX_HEAD_DIM
    scale = dh ** -0.5 * LOG2E
    w_q, w_k, w_v, w_f, w_g = jnp.split(w_in, [H * dh, 2 * H * dh, 3 * H * dh, 3 * H * dh + H], axis=1)
    w_main = jnp.concatenate([w_q * scale, w_k, w_v, w_g], axis=1).astype(BF16)
    proj = _matmul(xb, w_main)
    pieces = _fox_cum(x.reshape(B, S, D), w_f, b_f)
    o = _fox_attention(proj.reshape(B, S, 4 * D), pieces)
    return _outproj_ln(proj, 3, o.reshape(B * S, D), x, w_out, ln_g, ln_b)


def _ret_kernel(q_ref, k_ref, v_ref, cos_ref, sin_ref, dm_ref, xi_ref, zeta_ref, gc_ref, gn_ref,
                o_ref, r_sc):
    @pl.when(pl.program_id(1) == 0)
    def _():
        r_sc[...] = jnp.zeros_like(r_sc)

    dk = RET_HEAD_DIM
    half = dk // 2
    cos = cos_ref[...]
    sin = sin_ref[...]

    def rope(x):
        x1, x2 = x[:, :half], x[:, half:]
        return jnp.concatenate([x1 * cos - x2 * sin, x2 * cos + x1 * sin], axis=-1)

    def head_chain(h):
        cols = slice(h * dk, (h + 1) * dk)
        q = rope(q_ref[:, cols].astype(F32))
        k = rope(k_ref[:, cols].astype(F32)) * (dk ** -0.5)
        v = v_ref[:, cols]
        qb = q.astype(BF16)
        r_old = r_sc[h]
        scores = _dot_nt(qb, k.astype(BF16))
        cross = _dot(qb, r_old.astype(BF16))
        kz = (k * zeta_ref[h]).astype(BF16)
        r_new = _dot_tn(kz, v)
        yield None
        o = _dot((scores * dm_ref[h]).astype(BF16), v)
        r_sc[h] = r_old * gc_ref[h] + r_new
        yield None
        o = o + cross * xi_ref[h]
        o = o * lax.rsqrt(jnp.mean(o * o, axis=-1, keepdims=True) + RMS_EPS) * gn_ref[:, cols]
        o_ref[:, cols] = o.astype(o_ref.dtype)
        yield None

    _round_robin([head_chain(h) for h in range(RET_HEADS)])


def _ret_layer(x, xb, w_in, gn_g, w_out, ln_g, ln_b, B, S, chunk=512):
    D = D_MODEL
    H, dk = RET_HEADS, RET_HEAD_DIM
    C = _pick_tile(S, chunk)
    proj = _matmul(xb, w_in.astype(BF16))
    f32 = np.float32
    inv = (f32(1.0) / (f32(RET_THETA) ** (np.arange(0, dk, 2, dtype=f32) / f32(dk)))).astype(f32)
    ang = (np.arange(S, dtype=f32)[:, None] * inv[None, :]).astype(f32)
    cos, sin = jnp.asarray(np.cos(ang)), jnp.asarray(np.sin(ang))
    log_g = np.log1p(-(f32(2.0) ** (f32(-5.0) - np.arange(H, dtype=f32)))).astype(f32)
    pos = np.arange(C, dtype=f32)
    diff = pos[:, None] - pos[None, :]
    d_mask = jnp.asarray(np.where(diff[None] >= 0, np.exp(np.maximum(diff, 0.0)[None] * log_g[:, None, None]),
                                  0.0).astype(f32))
    xi = jnp.asarray(np.broadcast_to(np.exp((pos[None, :] + 1.0) * log_g[:, None])[:, :, None],
                                     (H, C, dk)).astype(f32))
    zeta = jnp.asarray(np.broadcast_to(np.exp((C - 1.0 - pos[None, :]) * log_g[:, None])[:, :, None],
                                       (H, C, dk)).astype(f32))
    g_c = jnp.asarray(np.broadcast_to(np.exp(f32(C) * log_g)[:, None, None], (H, 1, dk)).astype(f32))
    p3 = proj.reshape(B, S, 4 * D)
    o = pl.pallas_call(
        _ret_kernel,
        grid=(B, S // C),
        in_specs=[pl.BlockSpec((None, C, D), lambda b, c: (b, c, 0)),
                  pl.BlockSpec((None, C, D), lambda b, c: (b, c, 1)),
                  pl.BlockSpec((None, C, D), lambda b, c: (b, c, 2)),
                  pl.BlockSpec((C, dk // 2), lambda b, c: (c, 0)),
                  pl.BlockSpec((C, dk // 2), lambda b, c: (c, 0)),
                  pl.BlockSpec((H, C, C), lambda b, c: (0, 0, 0)),
                  pl.BlockSpec((H, C, dk), lambda b, c: (0, 0, 0)),
                  pl.BlockSpec((H, C, dk), lambda b, c: (0, 0, 0)),
                  pl.BlockSpec((H, 1, dk), lambda b, c: (0, 0, 0)),
                  pl.BlockSpec((1, D), lambda b, c: (0, 0))],
        out_specs=pl.BlockSpec((None, C, D), lambda b, c: (b, c, 0)),
        out_shape=jax.ShapeDtypeStruct((B, S, D), BF16),
        scratch_shapes=[pltpu.VMEM((H, dk, dk), F32)],
        compiler_params=_cparams("parallel", "arbitrary"),
        name="retnet_retention",
    )(p3, p3, p3, cos, sin, d_mask, xi, zeta, g_c, gn_g.reshape(1, D))
    return _outproj_ln(proj, 3, o.reshape(B * S, D), x, w_out, ln_g, ln_b, bf16_copy=False)


DSA_HEADS = 8
DSA_HEAD_DIM = 128
DSA_ROPE_DIM = 32
DSA_KV_RANK = 128
IDX_HEADS = 8
IDX_DIM = 64
IDX_ROPE_DIM = 16
TOPK_MAX = 256
HALF_MIN = -2 ** 15
HALF_ROWS = 16
DSA_VMEM_LIMIT = 56 * 1024 * 1024
DSA_ONES_ROWS = 16


def _rope_perm(width, groups):
    p = np.zeros((width, width), np.float32)
    for start, half in groups:
        for j in range(half):
            p[start + half + j, start + j] = 1.0
            p[start + j, start + half + j] = 1.0
    return p


def _rope_cs(S, width, groups, theta_dims):
    f32 = np.float32
    c = np.ones((S, width), f32)
    sg = np.zeros((S, width), f32)
    pos = np.arange(S, dtype=f32)[:, None]
    for (start, half), rot_dim in zip(groups, theta_dims):
        inv = (f32(1.0) / (f32(ROPE_THETA) ** (np.arange(0, rot_dim, 2, dtype=f32) / f32(rot_dim)))).astype(f32)
        ang = (pos * inv[None, :]).astype(f32)
        cos, sin = np.cos(ang), np.sin(ang)
        c[:, start:start + half] = cos
        c[:, start + half:start + 2 * half] = cos
        sg[:, start:start + half] = -sin
        sg[:, start + half:start + 2 * half] = sin
    return jnp.asarray(c), jnp.asarray(sg)


def _dsa_prep_kernel(q_ref, qi_ref, ckv_ref, misc_ref, cq_ref, sq_ref, ci_ref, si_ref, cm_ref, sm_ref,
                     pq_ref, pi_ref, pm_ref, selk_ref, selw_ref, wuk_ref, kvg_ref,
                     qf_ref, kvl_ref, kvt_ref, qir_ref, kid_ref, wit_ref):
    H, dh = DSA_HEADS, DSA_HEAD_DIM
    lane = lax.broadcasted_iota(jnp.int32, (1, LANES), 1)
    rope_lanes = lane < DSA_ROPE_DIM
    cq, sq = cq_ref[...], sq_ref[...]
    scale = dh ** -0.5 * LOG2E
    for h in range(H):
        qh = q_ref[:, h * dh:(h + 1) * dh]
        qr = qh.astype(F32) * cq + _dot(qh, pq_ref[...]) * sq
        q_lat = _dot(qr.astype(BF16), wuk_ref[h])
        qf_ref[h, :dh, :] = (q_lat * scale).T.astype(BF16)
        qf_ref[h, dh:, :] = jnp.where(rope_lanes, qr * scale, 0.0).T.astype(BF16)
    ci, si = ci_ref[...], si_ref[...]
    for g in range(IDX_HEADS * IDX_DIM // LANES):
        qg = qi_ref[:, g * LANES:(g + 1) * LANES]
        qr = qg.astype(F32) * ci + _dot(qg, pi_ref[...]) * si
        qir_ref[g * LANES:(g + 1) * LANES, :] = (qr * (IDX_DIM ** -0.5)).T.astype(BF16)
    ckv = ckv_ref[...].astype(F32)
    ckv = ckv * lax.rsqrt(jnp.mean(ckv * ckv, axis=-1, keepdims=True) + RMS_EPS) * kvg_ref[...]
    misc = misc_ref[...]
    mr = (misc.astype(F32) * cm_ref[...] + _dot(misc, pm_ref[...]) * sm_ref[...])
    kvl_ref[:, :DSA_KV_RANK] = ckv.astype(BF16)
    kvl_ref[:, DSA_KV_RANK:] = jnp.where(rope_lanes, mr, 0.0).astype(BF16)
    kvt_ref[:DSA_KV_RANK, :] = ckv.T.astype(BF16)
    kvt_ref[DSA_KV_RANK:, :] = jnp.ones((DSA_ONES_ROWS, ckv.shape[0]), BF16)
    kid_ref[...] = _dot(mr.astype(BF16), selk_ref[...]).astype(BF16)
    wi = _dot(misc, selw_ref[...]) * (IDX_HEADS ** -0.5)
    wit_ref[...] = wi.T[:IDX_HEADS, :]


def _sort_key(x):
    b = pltpu.bitcast(x, jnp.int32)
    return jnp.where(b < 0, b ^ 0x7FFFFFFF, b)


def _dsa_main_kernel(qf_ref, kvl_ref, kvt_ref, qi_ref, kid_ref, wit_ref, wuv_ref, obuf_ref, o_ref,
                     key_sc, hi_sc, lo_sc, low_sc, m_sc, acc_sc, *, tq, tk, k_sel, seq, qtile):
    del obuf_ref
    H = DSA_HEADS
    q0 = qtile * tq
    nj = (q0 + tq + tk - 1) // tk
    qpos = q0 + lax.broadcasted_iota(jnp.int32, (tk, tq), 1)
    kpos0 = lax.broadcasted_iota(jnp.int32, (tk, tq), 0)
    feat = lax.broadcasted_iota(jnp.int32, (LANES, 1), 0)
    neg_inf_key = _sort_key(jnp.full((1, 1), -jnp.inf, F32))

    wit = wit_ref[...]
    qi_heads = []
    for h in range(IDX_HEADS):
        g = qi_ref[(h // 2) * LANES:(h // 2 + 1) * LANES, :]
        keep = (feat >= IDX_DIM) if (h % 2) else (feat < IDX_DIM)
        qi_heads.append(jnp.where(keep, g, jnp.zeros_like(g)))

    def score_tile(j, c, causal):
        off = pl.multiple_of(j * tk, tk)
        ki = kid_ref[pl.ds(off, tk), :]
        scores = [_dot(ki, qi_heads[h]) for h in range(IDX_HEADS)]
        isc = jnp.zeros((tk, tq), F32)
        for h in range(IDX_HEADS):
            isc = isc + jnp.maximum(scores[h], 0.0) * wit[h:h + 1, :]
        isc = isc + 0.0
        if causal:
            isc = jnp.where(kpos0 + off <= qpos, isc, -jnp.inf)
        key = _sort_key(isc)
        key_sc[pl.ds(off, tk), :] = key
        hi_sc[pl.ds(off, tk), :] = (key >> 16).astype(jnp.int16)
        lo_sc[pl.ds(off, tk), :] = ((key & 0xFFFF) + HALF_MIN).astype(jnp.int16)
        return c

    n_clear = min(q0 // tk, nj)
    lax.fori_loop(0, n_clear, functools.partial(score_tile, causal=False), 0)
    lax.fori_loop(n_clear, nj, functools.partial(score_tile, causal=True), 0)

    def count(pred_fn):
        acc = jnp.zeros((8, tq), jnp.int32)
        for j in range(nj):
            hit = jnp.where(pred_fn(key_sc[j * tk:(j + 1) * tk, :], j * tk), 1, 0)
            acc = acc + jnp.sum(hit.reshape(tk // 8, 8, tq), axis=0)
        return jnp.sum(acc, axis=0, keepdims=True)

    rows16 = tk // HALF_ROWS

    def count16(ref, pred_fn):
        accs = [jnp.zeros((HALF_ROWS, tq), jnp.int16) for _ in range(2)]
        for j in range(nj):
            hit = jnp.where(pred_fn(ref[j * tk:(j + 1) * tk, :].reshape(rows16, HALF_ROWS, tq)),
                            jnp.int16(1), jnp.int16(0))
            for r in range(rows16):
                accs[r % 2] = accs[r % 2] + hit[r]
        return jnp.sum((accs[0] + accs[1]).astype(jnp.int32), axis=0, keepdims=True)

    def as_half(v):
        return jnp.broadcast_to(v.astype(jnp.int16), (HALF_ROWS, tq))[None]

    def search16(ref, base0, cnt0, rej0, bits, want):
        def bit_step(t, carry):
            base, cnt_b, cnt_r = carry
            cand = base + lax.shift_left(jnp.int32(1), bits - 1 - t)
            cand16 = as_half(cand)
            c = count16(ref, lambda kt: kt >= cand16)
            ok = c >= want
            return jnp.where(ok, cand, base), jnp.where(ok, c, cnt_b), jnp.where(ok, cnt_r, c)
        return lax.fori_loop(0, bits, bit_step, (base0, cnt0, rej0))

    zero16 = as_half(jnp.zeros((1, tq), jnp.int32))
    cnt_pos = count16(hi_sc, lambda kt: kt >= zero16)
    nonneg = cnt_pos >= k_sel
    t1, ge_hi, above = search16(hi_sc, jnp.where(nonneg, 0, HALF_MIN), jnp.where(nonneg, cnt_pos, nj * tk),
                                jnp.where(nonneg, 0, cnt_pos), 15, k_sel)
    t1_16 = as_half(t1)

    for j in range(nj):
        rows = slice(j * tk, (j + 1) * tk)
        hi = hi_sc[rows, :].reshape(rows16, HALF_ROWS, tq)
        lo = lo_sc[rows, :].reshape(rows16, HALF_ROWS, tq)
        low_sc[rows, :] = jnp.where(hi == t1_16, lo, jnp.int16(HALF_MIN)).reshape(tk, tq)
    t2, ge_low, gt_low = search16(low_sc, jnp.full((1, tq), HALF_MIN, jnp.int32), ge_hi - above,
                                  jnp.zeros((1, tq), jnp.int32), 16, k_sel - above)
    thr = lax.shift_left(t1, 16) | (t2 - HALF_MIN)
    n_ge = above + ge_low
    n_gt = above + gt_low
    need = k_sel - n_gt
    excess = ((n_ge - n_gt) > need) & (thr > neg_inf_key)
    any_excess = jnp.max(jnp.where(excess, 1, 0)) > 0

    def tie_cut():
        def step(t, lo):
            cand = lo + lax.shift_left(jnp.int32(1), int(math.log2(seq)) - t)
            c = count(lambda kt, off: (kt == thr) & (kpos0 + off < cand))
            return jnp.where(c < need, cand, lo)
        lo = lax.fori_loop(0, int(math.log2(seq)) + 1, step, jnp.zeros((1, tq), jnp.int32))
        return jnp.where(excess, lo, seq)

    cut = lax.cond(any_excess, tie_cut, lambda: jnp.full((1, tq), seq, jnp.int32))

    @pl.when(any_excess)
    def _():
        for j in range(nj):
            rows = slice(j * tk, (j + 1) * tk)
            kt = key_sc[rows, :]
            key_sc[rows, :] = jnp.where((kt == thr) & (kpos0 + j * tk > cut), thr - 1, kt)

    thr_sel = jnp.where(thr > neg_inf_key, thr, neg_inf_key + 1)

    m_sc[...] = jnp.full_like(m_sc, NEG_BIG)
    acc_sc[...] = jnp.zeros_like(acc_sc)

    def attn_tile(j, c):
        off = pl.multiple_of(j * tk, tk)
        bias = jnp.where(key_sc[pl.ds(off, tk), :] >= thr_sel, 0.0, NEG_BIG)
        kv = kvl_ref[pl.ds(off, tk), :]
        kvt = kvt_ref[:, pl.ds(off, tk)]

        def head_step(h, ql):
            s = _dot(kv, qf_ref[h, :, ql]) + bias[:, ql]
            yield None
            m_old = m_sc[h, :, ql]
            m_new = jnp.maximum(m_old, jnp.max(s, axis=0, keepdims=True))
            alpha = jnp.exp2(m_old - m_new)
            pv = _dot(kvt, jnp.exp2(s - m_new).astype(BF16))
            yield None
            acc_sc[h, :, ql] = alpha * acc_sc[h, :, ql] + pv
            m_sc[h, :, ql] = m_new
            yield None

        halves = [slice(0, tq // 2), slice(tq // 2, tq)]
        _round_robin([head_step(h, ql) for h in range(H) for ql in halves])
        return c

    lax.fori_loop(0, nj, attn_tile, 0)
    for h in range(H):
        acc = acc_sc[h]
        o_lat_t = (acc[:DSA_KV_RANK] / acc[DSA_KV_RANK:DSA_KV_RANK + 1]).astype(BF16)
        o_ref[:, h * DSA_HEAD_DIM:(h + 1) * DSA_HEAD_DIM] = _dot_tn(o_lat_t, wuv_ref[h]).astype(o_ref.dtype)


def _dsa_layer(x, xb, w_in, kv_norm_g, w_uk, w_uv, w_out, ln_g, ln_b, B, S, tq=512, tk=512):
    D = D_MODEL
    H, dh, dr, dc = DSA_HEADS, DSA_HEAD_DIM, DSA_ROPE_DIM, DSA_KV_RANK
    HI, di = IDX_HEADS, IDX_DIM
    w_q, w_ckv, w_kr, w_qi, w_ki, w_wi, w_g = jnp.split(
        w_in, np.cumsum([H * dh, dc, dr, HI * di, di, HI]).tolist(), axis=1)
    w_misc = jnp.concatenate([w_kr, w_ki, w_wi, jnp.zeros((D, LANES - dr - di - HI), F32)], axis=1)
    w_main = jnp.concatenate([w_q, w_g, w_qi, w_ckv, w_misc], axis=1).astype(BF16)
    n_main = w_main.shape[1]
    proj = _matmul(xb, w_main, tm=1024, tn=n_main)
    c_q, c_qi, c_ckv, c_misc = 0, 2 * D // LANES, (2 * D + HI * di) // LANES, (2 * D + HI * di + dc) // LANES

    q_groups = [(0, dr // 2)]
    i_groups = [(0, IDX_ROPE_DIM // 2), (di, IDX_ROPE_DIM // 2)]
    m_groups = [(0, dr // 2), (dr, IDX_ROPE_DIM // 2)]
    cq, sq = _rope_cs(S, LANES, q_groups, [dr])
    ci, si = _rope_cs(S, LANES, i_groups, [IDX_ROPE_DIM, IDX_ROPE_DIM])
    cm, sm = _rope_cs(S, LANES, m_groups, [dr, IDX_ROPE_DIM])
    pq = jnp.asarray(_rope_perm(LANES, q_groups), BF16)
    pi = jnp.asarray(_rope_perm(LANES, i_groups), BF16)
    pm = jnp.asarray(_rope_perm(LANES, m_groups), BF16)
    selk = np.zeros((LANES, LANES), np.float32)
    for j in range(di):
        selk[dr + j, j] = 1.0
        selk[dr + j, di + j] = 1.0
    selw = np.zeros((LANES, LANES), np.float32)
    for j in range(HI):
        selw[dr + di + j, j] = 1.0
    wuk = jnp.concatenate([jnp.zeros((H, dr, dc), F32), jnp.transpose(w_uk, (0, 2, 1))], axis=1).astype(BF16)

    ts = _pick_tile(S, 512)
    p3 = proj.reshape(B, S, n_main)
    tab = lambda: pl.BlockSpec((ts, LANES), lambda b, j: (j, 0))
    mat = lambda: pl.BlockSpec((LANES, LANES), lambda b, j: (0, 0))
    qf, kvl, kvt, qir, kid, wit = pl.pallas_call(
        _dsa_prep_kernel,
        grid=(B, S // ts),
        in_specs=[pl.BlockSpec((None, ts, H * dh), lambda b, j: (b, j, 0)),
                  pl.BlockSpec((None, ts, HI * di), lambda b, j: (b, j, c_qi * LANES // (HI * di))),
                  pl.BlockSpec((None, ts, dc), lambda b, j: (b, j, c_ckv)),
                  pl.BlockSpec((None, ts, LANES), lambda b, j: (b, j, c_misc)),
                  tab(), tab(), tab(), tab(), tab(), tab(),
                  mat(), mat(), mat(), mat(), mat(),
                  pl.BlockSpec((H, LANES, dc), lambda b, j: (0, 0, 0)),
                  pl.BlockSpec((1, dc), lambda b, j: (0, 0))],
        out_specs=[pl.BlockSpec((None, H, 2 * dc, ts), lambda b, j: (b, 0, 0, j)),
                   pl.BlockSpec((None, ts, 2 * dc), lambda b, j: (b, j, 0)),
                   pl.BlockSpec((None, dc + DSA_ONES_ROWS, ts), lambda b, j: (b, 0, j)),
                   pl.BlockSpec((None, HI * di, ts), lambda b, j: (b, 0, j)),
                   pl.BlockSpec((None, ts, LANES), lambda b, j: (b, j, 0)),
                   pl.BlockSpec((None, HI, ts), lambda b, j: (b, 0, j))],
        out_shape=[jax.ShapeDtypeStruct((B, H, 2 * dc, S), BF16),
                   jax.ShapeDtypeStruct((B, S, 2 * dc), BF16),
                   jax.ShapeDtypeStruct((B, dc + DSA_ONES_ROWS, S), BF16),
                   jax.ShapeDtypeStruct((B, HI * di, S), BF16),
                   jax.ShapeDtypeStruct((B, S, LANES), BF16),
                   jax.ShapeDtypeStruct((B, HI, S), F32)],
        compiler_params=_cparams("parallel", "parallel"),
        name="dsa_prep",
    )(p3, p3, p3, p3, cq, sq, ci, si, cm, sm, pq, pi, pm,
      jnp.asarray(selk, BF16), jnp.asarray(selw, BF16), wuk, kv_norm_g.reshape(1, dc))

    tq = _pick_tile(S, tq)
    tk = _pick_tile(S, tk)
    k_sel = min(TOPK_MAX, S // 4)
    wuv = w_uv.astype(BF16)
    o = jnp.zeros((B, S, H * dh), BF16)
    for i in range(S // tq):
        nkeys = -(-((i + 1) * tq) // tk) * tk
        o = pl.pallas_call(
            functools.partial(_dsa_main_kernel, tq=tq, tk=tk, k_sel=k_sel, seq=S, qtile=i),
            grid=(B,),
            in_specs=[pl.BlockSpec((None, H, 2 * dc, tq), lambda b, i=i: (b, 0, 0, i)),
                      pl.BlockSpec((None, nkeys, 2 * dc), lambda b: (b, 0, 0)),
                      pl.BlockSpec((None, dc + DSA_ONES_ROWS, nkeys), lambda b: (b, 0, 0)),
                      pl.BlockSpec((None, HI * di, tq), lambda b, i=i: (b, 0, i)),
                      pl.BlockSpec((None, nkeys, LANES), lambda b: (b, 0, 0)),
                      pl.BlockSpec((None, HI, tq), lambda b, i=i: (b, 0, i)),
                      pl.BlockSpec((H, dc, dh), lambda b: (0, 0, 0)),
                      pl.BlockSpec(memory_space=pl.ANY)],
            out_specs=pl.BlockSpec((None, tq, H * dh), lambda b, i=i: (b, i, 0)),
            out_shape=jax.ShapeDtypeStruct((B, S, H * dh), BF16),
            input_output_aliases={7: 0},
            scratch_shapes=[pltpu.VMEM((nkeys, tq), jnp.int32),
                            pltpu.VMEM((nkeys, tq), jnp.int16), pltpu.VMEM((nkeys, tq), jnp.int16),
                            pltpu.VMEM((nkeys, tq), jnp.int16),
                            pltpu.VMEM((H, 1, tq), F32),
                            pltpu.VMEM((H, dc + DSA_ONES_ROWS, tq), F32)],
            compiler_params=_cparams("parallel", vmem_limit=DSA_VMEM_LIMIT),
            name=f"dsa_select_attention_q{i}",
        )(qf, kvl, kvt, qir, kid, wit, wuv, o)
    return _outproj_ln(proj, 1, o.reshape(B * S, D), x, w_out, ln_g, ln_b, bf16_copy=False)


RWKV_HEADS = 16
RWKV_HEAD_DIM = 64
RWKV_GN_EPS = 64e-5
RWKV_CHUNK = 64
RWKV_SUB = 16


def _group_sum(x, gmat, split=True):
    outs = []
    for c in range(x.shape[1] // LANES):
        xc = x[:, c * LANES:(c + 1) * LANES]
        if split:
            hi, lo = _split2(xc)
            outs.append(_dot(hi, gmat) + _dot(lo, gmat))
        else:
            outs.append(_dot(xc.astype(BF16), gmat))
    return outs[0] if len(outs) == 1 else jnp.concatenate(outs, axis=1)


def _softplus(y):
    return jnp.maximum(y, 0.0) + jnp.log(1.0 + jnp.exp(-jnp.abs(y)))


def _rwkv_proj_kernel(x_ref, xprev_ref, mu_ref, wr_ref, wk_ref, wv_ref, wg_ref, wla_ref, wlb_ref,
                      ala_ref, alb_ref, w0_ref, a0_ref, kk_ref, ka_ref, rk_ref, gmat_ref, tril_ref,
                      r_ref, k_ref, v_ref, g_ref, kap_ref, b_ref, cum_ref, bonus_ref):
    x = x_ref[...]
    ts = x.shape[0]
    prev = jnp.where(pl.program_id(1) == 0, 0.0, xprev_ref[7:8, :])
    rowid = lax.broadcasted_iota(jnp.int32, (ts, 1), 0)
    xx = jnp.where(rowid == 0, prev, pltpu.roll(x, 1, 0)) - x

    def mixed(i):
        return (x + xx * mu_ref[i:i + 1, :]).astype(BF16)

    r = _dot(mixed(0), wr_ref[...])
    k = _dot(mixed(2), wk_ref[...])
    v = _dot(mixed(3), wv_ref[...])
    g_ref[...] = _dot(mixed(5), wg_ref[...]).astype(g_ref.dtype)
    lora_w = _dot(jnp.tanh(_dot(mixed(1), wla_ref[...])).astype(BF16), wlb_ref[...])
    lora_a = _dot(_dot(mixed(4), ala_ref[...]).astype(BF16), alb_ref[...])
    w_log = -_softplus(-(w0_ref[...] + lora_w)) - 0.5
    lw = -jnp.exp(w_log)
    tril = tril_ref[...]
    for c in range(ts // RWKV_CHUNK):
        rows = slice(c * RWKV_CHUNK, (c + 1) * RWKV_CHUNK)
        cum_ref[rows, :] = sum(_dot(tril, piece) for piece in _split2(lw[rows, :]))
    a = _sigmoid(a0_ref[...] + lora_a)
    gmat = gmat_ref[...]
    kk = k * kk_ref[...]
    kap = kk * lax.rsqrt(_group_sum(kk * kk, gmat, split=False) + 1e-12)
    k2 = k * (1.0 + (a - 1.0) * ka_ref[...])
    bonus_ref[...] = _group_sum(r * k2 * rk_ref[...], gmat, split=False) * v
    r_ref[...] = r.astype(r_ref.dtype)
    k_ref[...] = k2.astype(k_ref.dtype)
    v_ref[...] = v.astype(v_ref.dtype)
    kap_ref[...] = kap.astype(kap_ref.dtype)
    b_ref[...] = (kap * a).astype(b_ref.dtype)


def _bd(x, left):
    z = jnp.zeros_like(x)
    return jnp.concatenate([jnp.where(left, x, z), jnp.where(left, z, x)], axis=0)


def _unbd(x_bd):
    c = x_bd.shape[0] // 2
    return x_bd[:c] + x_bd[c:]


def _rwkv_chunk_pair(L, r, k, v, kap, b, masks):
    C = RWKV_CHUNK
    left, strict, lower, same_sub, eye, first_row = masks
    Lc = L[C - 1:C, :]
    L_excl = jnp.where(first_row, 0.0, pltpu.roll(L, 1, 0))
    e_l, e_lx, e_nl, e_r = jnp.exp(L), jnp.exp(L_excl), jnp.exp(-L), jnp.exp(Lc - L)
    at = _bd(-kap * e_lx, left).astype(BF16)
    rt = _bd(r * e_l, left)
    bt = _bd(b * e_nl, left).astype(BF16)
    kt = _bd(k * e_nl, left).astype(BF16)
    bh = _bd(b * e_r, left).astype(BF16)
    kh = _bd(k * e_r, left).astype(BF16)
    vb = _bd(v, left).astype(BF16)

    a1 = _dot_nt(jnp.concatenate([at, rt.astype(BF16)], axis=0), jnp.concatenate([bt, kt], axis=0))
    yield None
    n = jnp.where(strict, a1[:2 * C, :2 * C], 0.0)
    ak = jnp.where(strict, a1[:2 * C, 2 * C:], 0.0).astype(BF16)
    rb = jnp.where(lower, a1[2 * C:, :2 * C], 0.0).astype(BF16)
    rk = jnp.where(lower, a1[2 * C:, 2 * C:], 0.0).astype(BF16)

    W = 2 * C
    nd = jnp.where(same_sub, n, 0.0)
    no = (n - nd).astype(BF16)
    ndb = nd.astype(BF16)
    n2 = _dot(ndb, ndb)
    akv = _dot(ak, vb)
    yield None
    n2b = n2.astype(BF16)
    t0 = eye + nd
    r = _dot(n2b, jnp.concatenate([n2b, t0.astype(BF16)], axis=1))
    yield None
    n4b = r[:, :W].astype(BF16)
    t01 = t0 + r[:, W:]
    r = _dot(n4b, jnp.concatenate([n4b, t01.astype(BF16)], axis=1))
    yield None
    u = t01 + r[:, W:]
    td = u + _dot(r[:, :W].astype(BF16), u.astype(BF16))
    yield None
    tdb = td.astype(BF16)
    x1 = _dot(tdb, no)
    yield None
    x1b = x1.astype(BF16)
    r = _dot(x1b, jnp.concatenate([x1b, tdb], axis=1))
    yield None
    w = td + r[:, W:]
    t = (w + _dot(r[:, :W].astype(BF16), w.astype(BF16))).astype(BF16)
    yield None
    pq = _dot(t, jnp.concatenate([at, akv.astype(BF16)], axis=1)).astype(BF16)
    yield None
    z = jnp.concatenate([pq, jnp.concatenate([jnp.zeros_like(vb), vb], axis=1)], axis=0)
    ry = _dot(jnp.concatenate([rb, rk], axis=1), z)
    mg = _dot_tn(z, jnp.concatenate([bh, kh], axis=0))
    yield None
    rp = rt + ry[:, :2 * C]
    yl = ry[:, 2 * C:]
    yield _unbd(rp), _unbd(yl), _unbd(mg[:2 * C]), _unbd(mg[2 * C:]), jnp.exp(Lc)


def _rwkv_chunk_kernel(cum_ref, r_ref, k_ref, v_ref, kap_ref, b_ref,
                       rp_ref, yl_ref, mm_ref, gg_ref, gam_ref, *, pairs, cpb):
    C = RWKV_CHUNK
    lane = lax.broadcasted_iota(jnp.int32, (1, LANES), 1)
    left = lane < RWKV_HEAD_DIM
    ri = lax.broadcasted_iota(jnp.int32, (2 * C, 2 * C), 0)
    ci = lax.broadcasted_iota(jnp.int32, (2 * C, 2 * C), 1)
    same_head = (ri // C) == (ci // C)
    strict = same_head & ((ri % C) > (ci % C))
    lower = same_head & ((ri % C) >= (ci % C))
    same_sub = (ri // RWKV_SUB) == (ci // RWKV_SUB)
    eye = jnp.where(ri == ci, 1.0, 0.0).astype(F32)
    first_row = lax.broadcasted_iota(jnp.int32, (C, 1), 0) == 0
    masks = (left, strict, lower, same_sub, eye, first_row)
    jobs = [(c, slice(c * C, (c + 1) * C), slice(p * LANES, (p + 1) * LANES))
            for c in range(cpb) for p in range(pairs)]
    results = _round_robin([
        _rwkv_chunk_pair(cum_ref[rows, sl], r_ref[rows, sl].astype(F32), k_ref[rows, sl].astype(F32),
                         v_ref[rows, sl].astype(F32), kap_ref[rows, sl].astype(F32),
                         b_ref[rows, sl].astype(F32), masks)
        for _, rows, sl in jobs])
    for (c, rows, sl), (rp, yl, mm, gg, gam) in zip(jobs, results):
        rp_ref[rows, sl] = rp.astype(rp_ref.dtype)
        yl_ref[rows, sl] = yl
        mm_ref[c, :, sl] = mm.astype(mm_ref.dtype)
        gg_ref[c, :, sl] = gg
        gam_ref[c, :, sl] = gam


def _rwkv_seq_kernel(rp_ref, yl_ref, mm_ref, gg_ref, gam_ref, bonus_ref, gmat_ref, gng_ref, gnb_ref,
                     o_ref, s_sc, y_sc, *, pairs, cb):
    C = RWKV_CHUNK
    lane = lax.broadcasted_iota(jnp.int32, (1, LANES), 1)
    left = lane < RWKV_HEAD_DIM

    @pl.when(pl.program_id(2) == 0)
    def _():
        s_sc[...] = jnp.zeros_like(s_sc)

    states = [s_sc[p] for p in range(pairs)]
    for c in range(cb):
        rows = slice(c * C, (c + 1) * C)
        for p in range(pairs):
            sl = slice(p * LANES, (p + 1) * LANES)
            s = states[p]
            sb = s.astype(BF16)
            y_sc[rows, sl] = _dot_nt(rp_ref[rows, sl], sb) + yl_ref[rows, sl]
            mm = _bd(mm_ref[c, :, sl], left)
            gg = _bd(gg_ref[c, :, sl], left)
            states[p] = s * gam_ref[c, :, sl] + _dot(sb, mm) + gg
    for p in range(pairs):
        s_sc[p] = states[p]

    y = y_sc[...]
    gmat = gmat_ref[...]
    inv_n = 1.0 / RWKV_HEAD_DIM
    yc = y - _group_sum(y, gmat) * inv_n
    var = _group_sum(yc * yc, gmat) * inv_n
    yn = yc * lax.rsqrt(var + RWKV_GN_EPS) * gng_ref[...] + gnb_ref[...]
    o_ref[...] = (yn + bonus_ref[...]).astype(o_ref.dtype)


def _rwkv_layer(x, mu, w_in, w0, w_lora_a, w_lora_b, a0, a_lora_a, a_lora_b, k_k, k_a, r_k,
                gn_g, gn_b, w_out, ln_g, ln_b, B, S, ts=512, pairs=8, seq_pairs=8, chunks_per_step=2):
    D = D_MODEL
    C = RWKV_CHUNK
    nc = S // C
    ts = _pick_tile(S, ts)
    tril = jnp.asarray(np.tril(np.ones((C, C), np.float32)), BF16)
    w_r, w_k, w_v, w_g =[w.astype(BF16) for w in jnp.split(w_in, 4, axis=1)]
    gmat = jnp.asarray(np.kron(np.eye(2, dtype=np.float32), np.ones((RWKV_HEAD_DIM, RWKV_HEAD_DIM), np.float32)), BF16)
    row = lambda a: a.reshape(1, D)
    x3 = x.reshape(B, S, D)
    full = lambda shape: pl.BlockSpec(shape, lambda b, j: (0,) * len(shape))
    tile = lambda: pl.BlockSpec((None, ts, D), lambda b, j: (b, j, 0))
    lr = w_lora_a.shape[1]
    outs = pl.pallas_call(
        _rwkv_proj_kernel,
        grid=(B, S // ts),
        in_specs=[tile(),
                  pl.BlockSpec((None, 8, D), lambda b, j: (b, jnp.maximum(j * (ts // 8) - 1, 0), 0)),
                  full((6, D)), full((D, D)), full((D, D)), full((D, D)), full((D, D)),
                  full((D, lr)), full((lr, D)), full((D, lr)), full((lr, D)),
                  full((1, D)), full((1, D)), full((1, D)), full((1, D)), full((1, D)),
                  full((LANES, LANES)), full((C, C))],
        out_specs=[tile() for _ in range(8)],
        out_shape=[jax.ShapeDtypeStruct((B, S, D), dt) for dt in (BF16, BF16, BF16, BF16, BF16, BF16, F32, F32)],
        compiler_params=_cparams("parallel", "arbitrary"),
        name="rwkv_projections",
    )(x3, x3, mu, w_r, w_k, w_v, w_g, w_lora_a.astype(BF16), w_lora_b.astype(BF16),
      a_lora_a.astype(BF16), a_lora_b.astype(BF16), row(w0), row(a0), row(k_k), row(k_a), row(r_k), gmat, tril)
    r, k2, v, g, kap, bvec, cum, bonus = outs

    pw = pairs * LANES
    cpb = _pick_tile(nc, chunks_per_step)
    cblk = lambda: pl.BlockSpec((None, cpb * C, pw), lambda b, c, q: (b, c, q))
    sblk = lambda: pl.BlockSpec((None, cpb, C, pw), lambda b, c, q: (b, c, 0, q))
    rp, yl, mm, gg, gam = pl.pallas_call(
        functools.partial(_rwkv_chunk_kernel, pairs=pairs, cpb=cpb),
        grid=(B, nc // cpb, D // pw),
        in_specs=[cblk() for _ in range(6)],
        out_specs=[cblk(), cblk(), sblk(), sblk(),
                   pl.BlockSpec((None, cpb, 1, pw), lambda b, c, q: (b, c, 0, q))],
        out_shape=[jax.ShapeDtypeStruct((B, S, D), BF16), jax.ShapeDtypeStruct((B, S, D), F32),
                   jax.ShapeDtypeStruct((B, nc, C, D), BF16), jax.ShapeDtypeStruct((B, nc, C, D), F32),
                   jax.ShapeDtypeStruct((B, nc, 1, D), F32)],
        compiler_params=_cparams("parallel", "parallel", "parallel"),
        name="rwkv_chunk_summaries",
    )(cum, r, k2, v, kap, bvec)

    cb = _pick_tile(nc, 8)
    pairs = seq_pairs
    pw = pairs * LANES
    o = pl.pallas_call(
        functools.partial(_rwkv_seq_kernel, pairs=pairs, cb=cb),
        grid=(B, D // pw, nc // cb),
        in_specs=[pl.BlockSpec((None, cb * C, pw), lambda b, q, j: (b, j, q)),
                  pl.BlockSpec((None, cb * C, pw), lambda b, q, j: (b, j, q)),
                  pl.BlockSpec((None, cb, C, pw), lambda b, q, j: (b, j, 0, q)),
                  pl.BlockSpec((None, cb, C, pw), lambda b, q, j: (b, j, 0, q)),
                  pl.BlockSpec((None, cb, 1, pw), lambda b, q, j: (b, j, 0, q)),
                  pl.BlockSpec((None, cb * C, pw), lambda b, q, j: (b, j, q)),
                  pl.BlockSpec((LANES, LANES), lambda b, q, j: (0, 0)),
                  pl.BlockSpec((1, pw), lambda b, q, j: (0, q)),
                  pl.BlockSpec((1, pw), lambda b, q, j: (0, q))],
        out_specs=pl.BlockSpec((None, cb * C, pw), lambda b, q, j: (b, j, q)),
        out_shape=jax.ShapeDtypeStruct((B, S, D), BF16),
        scratch_shapes=[pltpu.VMEM((pairs, 2 * C, LANES), F32), pltpu.VMEM((cb * C, pw), F32)],
        compiler_params=_cparams("parallel", "parallel", "arbitrary"),
        name="rwkv_state_scan",
    )(rp, yl, mm, gg, gam, bonus, gmat, row(gn_g), row(gn_b))
    return _outproj_ln(g.reshape(B * S, D), 0, o.reshape(B * S, D), x, w_out, ln_g, ln_b)


def kernel(x, ln_g, ln_b, fox_w_in, fox_b_f, fox_w_out, dsa_w_in, dsa_kv_norm_g, dsa_w_uk, dsa_w_uv, dsa_w_out, rwkv_mu, rwkv_w_in, rwkv_w0, rwkv_w_lora_a, rwkv_w_lora_b, rwkv_a0, rwkv_a_lora_a, rwkv_a_lora_b, rwkv_k_k, rwkv_k_a, rwkv_r_k, rwkv_gn_g, rwkv_gn_b, rwkv_w_out, ret_w_in, ret_gn_g, ret_w_out):
    B, S, D = x.shape
    h = x.reshape(B * S, D)
    h, hb = _fox_layer(h, h, fox_w_in, fox_b_f, fox_w_out, ln_g[0], ln_b[0], B, S)
    h, hb = _dsa_layer(h, hb, dsa_w_in, dsa_kv_norm_g, dsa_w_uk, dsa_w_uv, dsa_w_out, ln_g[1], ln_b[1], B, S)
    h, hb = _rwkv_layer(h, rwkv_mu, rwkv_w_in, rwkv_w0, rwkv_w_lora_a, rwkv_w_lora_b, rwkv_a0,
                        rwkv_a_lora_a, rwkv_a_lora_b, rwkv_k_k, rwkv_k_a, rwkv_r_k, rwkv_gn_g, rwkv_gn_b,
                        rwkv_w_out, ln_g[2], ln_b[2], B, S)
    h, hb = _ret_layer(h, hb, ret_w_in, ret_gn_g, ret_w_out, ln_g[3], ln_b[3], B, S)
    return h.reshape(B, S, D)
```
